```python
import math
import jax, jax.numpy as jnp
from jax import lax
import numpy as np

D_MODEL = 2048
BATCH = 2
SEQ = 8192
DEPTH = 4

HEAD_DIM = 64
BLK = 128
NORM_EPS = 1e-5
N_BRANCH = 3
A_WINDOW = 128
A_HQ = 8
A_HKV = 2
A_G = A_HQ // A_HKV
B_HEADS = 12
B_WIDTH = B_HEADS * HEAD_DIM
LORA_DECAY = 96
LORA_ICLR = 96
LORA_GATE = 256
B_GN_EPS = 64e-5
C_GROUPS = ((128, 1), (512, 4), (2048, 16))
C_HG = 4
C_HEADS = C_HG * len(C_GROUPS)
C_WIDTH = C_HEADS * HEAD_DIM
C_OUT = C_HG * HEAD_DIM
N_BUCKETS = 32
MAX_EXACT = 16
REL_MAX_DIST = 2048
N_BIAS_HEADS = A_HQ + C_HEADS
D_FF = 5504
CONV_W = 3
B_COL_SIZES = (B_WIDTH, B_WIDTH, B_WIDTH, LORA_DECAY, LORA_ICLR, LORA_GATE)
B_COLS = sum(B_COL_SIZES)
IN_COL_SIZES = (A_HQ * HEAD_DIM, A_HKV * HEAD_DIM, A_HKV * HEAD_DIM, B_COLS,
                C_WIDTH, C_WIDTH, C_WIDTH, N_BRANCH * D_MODEL)
D_IN = sum(IN_COL_SIZES)

kernel_name = 'hybrid_swa_rwkv7_dilated_gated_trunk'


def _split(t, sizes):
    return jnp.split(t, [int(c) for c in np.cumsum(sizes)[:-1]], axis=-1)


def _rmsnorm(x, g):
    xf = x.astype(jnp.float32)
    y = xf * lax.rsqrt(jnp.mean(xf * xf, axis=-1, keepdims=True) + NORM_EPS)
    return (y * g.astype(jnp.float32)).astype(x.dtype)


def _shift(t):
    return jnp.pad(t, ((0, 0), (1, 0), (0, 0)))[:, :-1]


def _band_offsets():
    i = jnp.arange(BLK)[:, None]
    j = jnp.arange(2 * BLK)[None, :]
    return i + BLK - j


def _t5_bucket(dist):
    small = dist < MAX_EXACT
    nf = jnp.maximum(dist, 1).astype(jnp.float32)
    large = MAX_EXACT + (jnp.log(nf / MAX_EXACT) / math.log(REL_MAX_DIST / MAX_EXACT)
                         * (N_BUCKETS - MAX_EXACT)).astype(jnp.int32)
    return jnp.where(small, dist, jnp.minimum(large, N_BUCKETS - 1))


def _rel_bias(table, dilation):
    dist = jnp.maximum(_band_offsets(), 0) * dilation
    return jnp.transpose(table[_t5_bucket(dist)], (2, 0, 1))


def _banded_attention(q, k, v, max_steps, bias, sink):
    n, L, hkv, g, hd = q.shape
    nb = -(-L // BLK)
    pad = nb * BLK - L
    q = jnp.pad(q, ((0, 0), (0, pad), (0, 0), (0, 0), (0, 0)))
    k = jnp.pad(k, ((0, 0), (0, pad), (0, 0), (0, 0)))
    v = jnp.pad(v, ((0, 0), (0, pad), (0, 0), (0, 0)))
    qb = q.reshape(n, nb, BLK, hkv, g, hd).astype(jnp.float32)

    def window(t):
        tb = t.reshape(n, nb, BLK, hkv, hd)
        prev = jnp.pad(tb, ((0, 0), (1, 0), (0, 0), (0, 0), (0, 0)))[:, :-1]
        return jnp.concatenate([prev, tb], axis=2).astype(jnp.float32)

    kw, vw = window(k), window(v)
    s = jnp.einsum('nbqhgd,nbkhd->nbhgqk', qb, kw) * (hd ** -0.5) + bias.astype(jnp.float32)
    dist = _band_offsets()
    kpos = jnp.arange(nb)[:, None, None] * BLK - BLK + jnp.arange(2 * BLK)[None, None, :]
    valid = (dist >= 0) & (dist <= max_steps) & (kpos >= 0)
    s = jnp.where(valid[None, :, None, None], s, -jnp.inf)
    m = jnp.max(s, axis=-1)
    if sink is not None:
        sk = sink.astype(jnp.float32)[:, :, None]
        m = jnp.maximum(m, sk)
    p = jnp.exp(s - m[..., None])
    l = jnp.sum(p, axis=-1)
    denom = l + jnp.exp(sk - m) if sink is not None else l
    o = jnp.einsum('nbhgqk,nbkhd->nbqhgd', p, vw) / jnp.moveaxis(denom, -1, 2)[..., None]
    lse = jnp.moveaxis(m + jnp.log(l), -1, 2)
    o = o.reshape(n, nb * BLK, hkv, g, hd)[:, :L].astype(q.dtype)
    lse = lse.reshape(n, nb * BLK, hkv, g)[:, :L]
    return o, lse


def _swa_sink_attention(q, k, v, bias, sink):
    B, S, _ = q.shape
    q = q.reshape(B, S, A_HKV, A_G, HEAD_DIM)
    k = k.reshape(B, S, A_HKV, HEAD_DIM)
    v = v.reshape(B, S, A_HKV, HEAD_DIM)
    o, _ = _banded_attention(q, k, v, A_WINDOW - 1, bias, sink.reshape(A_HKV, A_G))
    return o.reshape(B, S, A_HQ * HEAD_DIM)


def _fold(t, dil):
    B, S = t.shape[:2]
    rest = t.shape[2:]
    return jnp.swapaxes(t.reshape(B, S // dil, dil, *rest), 1, 2).reshape(B * dil, S // dil, *rest)


def _unfold(t, B, dil):
    L = t.shape[1]
    rest = t.shape[2:]
    return jnp.swapaxes(t.reshape(B, dil, L, *rest), 1, 2).reshape(B, L * dil, *rest)


def _dilated_mixture(q, k, v, biases):
    B, S, _ = q.shape
    shp = (B, S, len(C_GROUPS), C_HG, HEAD_DIM)
    q, k, v = q.reshape(shp), k.reshape(shp), v.reshape(shp)
    outs, lses = [], []
    for gi, ((win, dil), bias) in enumerate(zip(C_GROUPS, biases)):
        o, lse = _banded_attention(_fold(q[:, :, gi], dil)[:, :, :, None], _fold(k[:, :, gi], dil),
                                   _fold(v[:, :, gi], dil), win // dil, bias, None)
        outs.append(_unfold(o[:, :, :, 0], B, dil).astype(jnp.float32))
        lses.append(_unfold(lse[..., 0], B, dil))
    wts = jax.nn.softmax(jnp.stack(lses), axis=0)
    y = jnp.sum(wts[..., None] * jnp.stack(outs), axis=0)
    return y.reshape(B, S, C_OUT).astype(q.dtype)


def _wkv7_scan(r, w, k, v, a, b):
    B, S, H, N = r.shape

    def step(state, inp):
        r_t, w_t, k_t, v_t, a_t, b_t = inp
        sa = jnp.einsum('bhvk,bhk->bhv', state, a_t)
        state = state * w_t[:, :, None, :] + sa[..., None] * b_t[:, :, None, :] + v_t[..., None] * k_t[:, :, None, :]
        return state, jnp.einsum('bhvk,bhk->bhv', state, r_t)

    xs = tuple(jnp.swapaxes(t, 0, 1) for t in (r, w, k, v, a, b))
    _, ys = lax.scan(step, jnp.zeros((B, H, N, N), jnp.float32), xs)
    return jnp.swapaxes(ys, 0, 1)


def _rwkv7_time_mix(pb, mu, w0, w_up, a0, a_up, g_up, k_k, k_a, r_k, lnx_g, lnx_b):
    B, S, _ = pb.shape
    pf = pb.astype(jnp.float32)
    pf = pf + (_shift(pf) - pf) * mu
    r, k, v, wd, ad, gd = _split(pf, B_COL_SIZES)
    w = -jax.nn.softplus(-(w0 + jnp.tanh(wd) @ w_up)) - 0.5
    a = jax.nn.sigmoid(a0 + ad @ a_up)
    g = jax.nn.sigmoid(gd) @ g_up
    heads = lambda t: t.reshape(B, S, B_HEADS, HEAD_DIM)
    kk = heads(k * k_k)
    kk = kk / jnp.maximum(jnp.sqrt(jnp.sum(kk * kk, axis=-1, keepdims=True)), 1e-12)
    k = k * (1.0 + (a - 1.0) * k_a)
    r, k, v, a = heads(r), heads(k), heads(v), heads(a)
    y = _wkv7_scan(r, jnp.exp(-jnp.exp(heads(w))), k, v, -kk, kk * a)
    mean = jnp.mean(y, axis=-1, keepdims=True)
    var = jnp.mean(jnp.square(y - mean), axis=-1, keepdims=True)
    y = ((y - mean) * lax.rsqrt(var + B_GN_EPS)).reshape(B, S, B_WIDTH) * lnx_g + lnx_b
    y = y + (jnp.sum(r * k * r_k, axis=-1, keepdims=True) * v).reshape(B, S, B_WIDTH)
    return (y * g).astype(pb.dtype)


def _conv_ffn(h, w_up, conv_w, w_down):
    u = h @ w_up
    gate, val = jnp.split(u, 2, axis=-1)
    gp = jnp.pad(gate, ((0, 0), (CONV_W - 1, 0), (0, 0)))
    S = gate.shape[1]
    gate = conv_w[0] * gp[:, 0:S] + conv_w[1] * gp[:, 1:S + 1] + conv_w[2] * gp[:, 2:S + 2]
    return (jax.nn.silu(gate) * val) @ w_down


def setup_inputs(seed: int = 0) -> dict:
    key = jax.random.key(seed)
    ks = jax.random.split(key, 26)
    nrm = lambda kk, shape, scale: jax.random.normal(kk, shape, jnp.float32) * scale
    L = DEPTH
    return {
        'x': nrm(ks[0], (BATCH, SEQ, D_MODEL), 1.0),
        'rel_bias': nrm(ks[1], (N_BUCKETS, N_BIAS_HEADS), 0.3),
        'norm1_g': 1.0 + nrm(ks[2], (L, D_MODEL), 0.02),
        'w_in': nrm(ks[3], (L, D_MODEL, D_IN), D_MODEL ** -0.5),
        'attn_sinks': nrm(ks[4], (L, A_HQ), 0.5),
        'rwkv_mu': jax.random.uniform(ks[5], (L, B_COLS), jnp.float32),
        'rwkv_w0': -0.5 + nrm(ks[6], (L, B_WIDTH), 0.5),
        'rwkv_w_up': nrm(ks[7], (L, LORA_DECAY, B_WIDTH), 0.5 * LORA_DECAY ** -0.5),
        'rwkv_a0': nrm(ks[8], (L, B_WIDTH), 0.1),
        'rwkv_a_up': nrm(ks[9], (L, LORA_ICLR, B_WIDTH), 0.5 * LORA_ICLR ** -0.5),
        'rwkv_g_up': nrm(ks[10], (L, LORA_GATE, B_WIDTH), 2.0 * LORA_GATE ** -0.5),
        'rwkv_k_k': 0.85 + nrm(ks[11], (L, B_WIDTH), 0.05),
        'rwkv_k_a': 1.0 + nrm(ks[12], (L, B_WIDTH), 0.05),
        'rwkv_r_k': nrm(ks[13], (L, B_HEADS, HEAD_DIM), 0.1),
        'rwkv_lnx_g': 1.0 + nrm(ks[14], (L, B_WIDTH), 0.02),
        'rwkv_lnx_b': nrm(ks[15], (L, B_WIDTH), 0.02),
        'proj_a': nrm(ks[16], (L, A_HQ * HEAD_DIM, D_MODEL), (A_HQ * HEAD_DIM) ** -0.5),
        'proj_b': nrm(ks[17], (L, B_WIDTH, D_MODEL), B_WIDTH ** -0.5),
        'proj_c': nrm(ks[18], (L, C_OUT, D_MODEL), C_OUT ** -0.5),
        'w_out': nrm(ks[19], (L, D_MODEL, D_MODEL), D_MODEL ** -0.5),
        'norm2_g': 1.0 + nrm(ks[20], (L, D_MODEL), 0.02),
        'ffn_up': nrm(ks[21], (L, D_MODEL, 2 * D_FF), D_MODEL ** -0.5),
        'ffn_conv': nrm(ks[22], (L, CONV_W, D_FF), 0.6),
        'ffn_down': nrm(ks[23], (L, D_FF, D_MODEL), D_FF ** -0.5),
        'final_g': 1.0 + nrm(ks[24], (D_MODEL,), 0.02),
    }


def reference(x, rel_bias, norm1_g, w_in, attn_sinks, rwkv_mu, rwkv_w0, rwkv_w_up, rwkv_a0, rwkv_a_up,
              rwkv_g_up, rwkv_k_k, rwkv_k_a, rwkv_r_k, rwkv_lnx_g, rwkv_lnx_b, proj_a, proj_b, proj_c,
              w_out, norm2_g, ffn_up, ffn_conv, ffn_down, final_g):
    bias_a = _rel_bias(rel_bias[:, :A_HQ], 1).reshape(A_HKV, A_G, BLK, 2 * BLK)
    bias_c = [_rel_bias(rel_bias[:, A_HQ + gi * C_HG:A_HQ + (gi + 1) * C_HG], dil)[:, None]
              for gi, (_, dil) in enumerate(C_GROUPS)]
    for l in range(DEPTH):
        h = _rmsnorm(x, norm1_g[l])
        aq, ak, av, pb, cq, ck, cv, gates = _split(h @ w_in[l], IN_COL_SIZES)
        y_a = _swa_sink_attention(aq, ak, av, bias_a, attn_sinks[l])
        y_b = _rwkv7_time_mix(pb, rwkv_mu[l], rwkv_w0[l], rwkv_w_up[l], rwkv_a0[l], rwkv_a_up[l],
                              rwkv_g_up[l], rwkv_k_k[l], rwkv_k_a[l], rwkv_r_k[l], rwkv_lnx_g[l], rwkv_lnx_b[l])
        y_c = _dilated_mixture(cq, ck, cv, bias_c)
        g_a, g_b, g_c = jnp.split(jax.nn.sigmoid(gates.astype(jnp.float32)).astype(h.dtype), N_BRANCH, axis=-1)
        merged = g_a * (y_a @ proj_a[l]) + g_b * (y_b @ proj_b[l]) + g_c * (y_c @ proj_c[l])
        x = x + merged @ w_out[l]
        x = x + _conv_ffn(_rmsnorm(x, norm2_g[l]), ffn_up[l], ffn_conv[l], ffn_down[l])
    return _rmsnorm(x, final_g)
```

```python
import functools
import math

import jax
import jax.numpy as jnp
import numpy as np
from jax import lax
from jax.experimental import pallas as pl
from jax.experimental.pallas import tpu as pltpu

F32 = jnp.float32
BF16 = jnp.bfloat16

HEAD_DIM = 64
BLK = 128
NORM_EPS = 1e-5
A_HQ, A_HKV = 8, 2
B_HEADS = 12
B_WIDTH = B_HEADS * HEAD_DIM
LORA_DECAY, LORA_ICLR, LORA_GATE = 96, 96, 256
LORA_PAD = 128
B_GN_EPS = 64e-5
C_GROUPS = ((128, 1), (512, 4), (2048, 16))
C_HG = 4
C_WIDTH = C_HG * len(C_GROUPS) * HEAD_DIM
C_OUT = C_HG * HEAD_DIM
N_BUCKETS, MAX_EXACT, REL_MAX_DIST = 32, 16, 2048
D_FF_PAD_TO = 512
NEG_BIG = -1e30

N_ATTN = A_HQ * HEAD_DIM + 2 * A_HKV * HEAD_DIM + 3 * C_WIDTH
N_RWKV = 3 * B_WIDTH + 2 * LORA_PAD + LORA_GATE + 256
OFF_WD = 3 * B_WIDTH
OFF_AD = OFF_WD + LORA_PAD
OFF_GD = OFF_AD + LORA_PAD
WKV_CHUNK = 64
HALO = 16

VMEM_LIMIT = 56 * 1024 * 1024


def _cparams(sem):
    return pltpu.CompilerParams(dimension_semantics=sem, vmem_limit_bytes=VMEM_LIMIT)


def _dot(a, b):
    return jnp.dot(a, b, preferred_element_type=F32)


def _dot_nt(a, b):
    return lax.dot_general(a, b, (((1,), (1,)), ((), ())), preferred_element_type=F32)


def _split2(x):
    hi = x.astype(BF16)
    lo = (x - hi.astype(F32)).astype(BF16)
    return hi, lo


def _split3(x):
    hi = x.astype(BF16)
    r1 = x - hi.astype(F32)
    mid = r1.astype(BF16)
    lo = (r1 - mid.astype(F32)).astype(BF16)
    return hi, mid, lo


def _dot_exact_rhs(a, b_bf16):
    h, m, l = _split3(a)
    return _dot(h, b_bf16) + _dot(m, b_bf16) + _dot(l, b_bf16)


def _dot_exact_lhs(a_bf16, b):
    h, m, l = _split3(b)
    return _dot(a_bf16, h) + _dot(a_bf16, m) + _dot(a_bf16, l)


def _dot3(a, b):
    ah, al = _split2(a)
    bh, bl = _split2(b)
    return _dot(ah, bh) + _dot(ah, bl) + _dot(al, bh)


def _rms(x, g):
    ms = jnp.mean(x * x, axis=-1, keepdims=True)
    return x * lax.rsqrt(ms + NORM_EPS) * g


def _norm_matmul_kernel(x_ref, g_ref, w_ref, o_ref, h_ref, *, sigmoid):
    @pl.when(pl.program_id(1) == 0)
    def _():
        h_ref[...] = _rms(x_ref[...], g_ref[...]).astype(BF16)

    acc = _dot(h_ref[...], w_ref[...])
    if sigmoid:
        acc = jax.nn.sigmoid(acc)
    o_ref[...] = acc.astype(o_ref.dtype)


def _norm_matmul(x2, g, w, out_dtype, sigmoid, tm, tn, name):
    m, d = x2.shape
    n = w.shape[1]
    return pl.pallas_call(
        functools.partial(_norm_matmul_kernel, sigmoid=sigmoid),
        grid=(m // tm, n // tn),
        in_specs=[
            pl.BlockSpec((tm, d), lambda i, j: (i, 0)),
            pl.BlockSpec((1, d), lambda i, j: (0, 0)),
            pl.BlockSpec((d, tn), lambda i, j: (0, j)),
        ],
        out_specs=pl.BlockSpec((tm, tn), lambda i, j: (i, j)),
        out_shape=jax.ShapeDtypeStruct((m, n), out_dtype),
        scratch_shapes=[pltpu.VMEM((tm, d), BF16)],
        compiler_params=_cparams(("parallel", "arbitrary")),
        name=name,
    )(x2, g.reshape(1, d), w)


def _band_attn_kernel(*refs, nq, nkv, has_sink, want_lse):
    it = iter(refs)
    q_ref, kp_ref, kc_ref, vp_ref, vc_ref, bias_ref = (next(it) for _ in range(6))
    sink_ref = next(it) if has_sink else None
    o_ref = next(it)
    lse_ref = next(it) if want_lse else None

    first = pl.program_id(2) == 0
    q = q_ref[...] * jnp.asarray(HEAD_DIM ** -0.5, BF16)
    k = jnp.concatenate([kp_ref[...], kc_ref[...]], axis=0)
    v = jnp.concatenate([vp_ref[...], vc_ref[...]], axis=0)
    col = lax.broadcasted_iota(jnp.int32, (BLK, 2 * BLK), 1)
    edge = jnp.where(col < BLK, jnp.where(first, NEG_BIG, 0.0), 0.0)
    rep = nq // nkv
    outs, lses = [], []
    for h in range(nq):
        g = h // rep
        qh = q[:, h * HEAD_DIM:(h + 1) * HEAD_DIM]
        kh = k[:, g * HEAD_DIM:(g + 1) * HEAD_DIM]
        vh = v[:, g * HEAD_DIM:(g + 1) * HEAD_DIM]
        s = _dot_nt(qh, kh) + bias_ref[h] + edge
        m = jnp.max(s, axis=-1, keepdims=True)
        if has_sink:
            sk = sink_ref[h]
            m = jnp.maximum(m, sk)
        p = jnp.exp(s - m)
        l = jnp.sum(p, axis=-1, keepdims=True)
        denom = l + jnp.exp(sk - m) if has_sink else l
        o = _dot(p.astype(BF16), vh) / denom
        outs.append(o)
        if want_lse:
            lses.append(jnp.broadcast_to(m + jnp.log(l), (BLK, HEAD_DIM)))
    o_ref[...] = jnp.concatenate(outs, axis=-1).astype(o_ref.dtype)
    if want_lse:
        lse_ref[...] = jnp.concatenate(lses, axis=-1)


def _band_attention(p_attn, bias, sink, *, dil, q_blk, k_blk, v_blk, q_w, kv_w, nq, nkv, want_lse,
                    out_dtype, name):
    b, s, n = p_attn.shape
    lf = s // dil
    nb = lf // BLK
    pv = p_attn.reshape(b, lf, dil * n)
    qpr, kpr = n // q_w, n // kv_w
    has_sink = sink is not None

    in_specs = [
        pl.BlockSpec((None, BLK, q_w), lambda bi, r, j: (bi, j, r * qpr + q_blk)),
        pl.BlockSpec((None, BLK, kv_w), lambda bi, r, j: (bi, jnp.maximum(j - 1, 0), r * kpr + k_blk)),
        pl.BlockSpec((None, BLK, kv_w), lambda bi, r, j: (bi, j, r * kpr + k_blk)),
        pl.BlockSpec((None, BLK, kv_w), lambda bi, r, j: (bi, jnp.maximum(j - 1, 0), r * kpr + v_blk)),
        pl.BlockSpec((None, BLK, kv_w), lambda bi, r, j: (bi, j, r * kpr + v_blk)),
        pl.BlockSpec((nq, BLK, 2 * BLK), lambda bi, r, j: (0, 0, 0)),
    ]
    args = [pv, pv, pv, pv, pv, bias]
    if has_sink:
        in_specs.append(pl.BlockSpec(memory_space=pltpu.SMEM))
        args.append(sink)
    ow = nq * HEAD_DIM
    out_spec = pl.BlockSpec((None, BLK, ow), lambda bi, r, j: (bi, j, r))
    out_shape = jax.ShapeDtypeStruct((b, lf, dil * ow), out_dtype)
    if want_lse:
        out_specs = [out_spec, out_spec]
        out_shapes = [out_shape, jax.ShapeDtypeStruct((b, lf, dil * ow), F32)]
    else:
        out_specs, out_shapes = out_spec, out_shape
    res = pl.pallas_call(
        functools.partial(_band_attn_kernel, nq=nq, nkv=nkv, has_sink=has_sink, want_lse=want_lse),
        grid=(b, dil, nb),
        in_specs=in_specs,
        out_specs=out_specs,
        out_shape=out_shapes,
        compiler_params=_cparams(("parallel", "parallel", "arbitrary")),
        name=name,
    )(*args)
    if want_lse:
        return res[0].reshape(b, s, ow), res[1].reshape(b, s, ow)
    return res.reshape(b, s, ow)


def _rwkv_prep_kernel(ph_ref, p_ref, mu_ref, w0_ref, wup_ref, a0_ref, aup_ref, gup_ref, kk_ref, ka_ref,
                      rk_ref, e_ref, r_o, lw_o, k_o, v_o, kk_o, b_o, g_o, bonus_o, *, tiles_per_seq):
    tm = p_ref.shape[0]
    p = p_ref[...]
    seq_start = pl.program_id(0) % tiles_per_seq == 0
    last = ph_ref[...][HALO - 1:HALO, :]
    last = jnp.where(seq_start, 0.0, last)
    row = lax.broadcasted_iota(jnp.int32, (tm, 1), 0)
    prev = jnp.where(row == 0, last, pltpu.roll(p, 1, 0))
    pf = p + (prev - p) * mu_ref[...]

    r = pf[:, 0:B_WIDTH]
    k = pf[:, B_WIDTH:2 * B_WIDTH]
    v = pf[:, 2 * B_WIDTH:3 * B_WIDTH]
    wd = pf[:, OFF_WD:OFF_WD + LORA_PAD]
    ad = pf[:, OFF_AD:OFF_AD + LORA_PAD]
    gd = pf[:, OFF_GD:OFF_GD + LORA_GATE]

    z = w0_ref[...] + _dot3(jnp.tanh(wd), wup_ref[...])
    nz = -z
    softplus = jnp.maximum(nz, 0.0) + jnp.log(1.0 + jnp.exp(-jnp.abs(nz)))
    w = -softplus - 0.5
    lw_o[...] = -jnp.exp(w)
    a = jax.nn.sigmoid(a0_ref[...] + _dot3(ad, aup_ref[...]))
    g_o[...] = _dot3(jax.nn.sigmoid(gd), gup_ref[...])

    e = e_ref[...]
    kk = k * kk_ref[...]
    nrm = jnp.sqrt(_dot_exact_rhs(kk * kk, e))
    kk = kk / jnp.maximum(nrm, 1e-12)
    k2 = k * (1.0 + (a - 1.0) * ka_ref[...])
    r_o[...] = r
    k_o[...] = k2
    v_o[...] = v
    kk_o[...] = kk
    b_o[...] = kk * a
    bonus_o[...] = _dot_exact_rhs(r * k2 * rk_ref[...], e) * v


def _rwkv_prep(pb2, seq, mu, w0, wup, a0, aup, gup, k_k, k_a, r_k, e_bd, tm):
    m, n = pb2.shape
    tps = seq // tm
    row = lambda a: a.reshape(1, -1)
    full = lambda a: pl.BlockSpec(a.shape, lambda i: (0,) * a.ndim)
    args = [pb2, pb2, row(mu), row(w0), wup, row(a0), aup, gup, row(k_k), row(k_a), row(r_k), e_bd]
    in_specs = [
        pl.BlockSpec((HALO, n), lambda i: (jnp.maximum(i * (tm // HALO) - 1, 0), 0)),
        pl.BlockSpec((tm, n), lambda i: (i, 0)),
    ] + [full(a) for a in args[2:]]
    o_spec = pl.BlockSpec((tm, B_WIDTH), lambda i: (i, 0))
    o_shape = jax.ShapeDtypeStruct((m, B_WIDTH), F32)
    return pl.pallas_call(
        functools.partial(_rwkv_prep_kernel, tiles_per_seq=tps),
        grid=(m // tm,),
        in_specs=in_specs,
        out_specs=[o_spec] * 8,
        out_shape=[o_shape] * 8,
        compiler_params=_cparams(("parallel",)),
        name="rwkv_prep",
    )(*args)


def _wkv_kernel(r_ref, lw_ref, k_ref, v_ref, kk_ref, b_ref, y_ref, st_ref):
    c = WKV_CHUNK
    n_chunks = r_ref.shape[0] // c

    @pl.when(pl.program_id(1) == 0)
    def _():
        st_ref[...] = jnp.zeros_like(st_ref)

    rowi = lax.broadcasted_iota(jnp.int32, (c, c), 0)
    coli = lax.broadcasted_iota(jnp.int32, (c, c), 1)
    incl = rowi >= coli
    strict = rowi > coli
    tri = jnp.where(incl, 1.0, 0.0).astype(BF16)
    eye = jnp.where(rowi == coli, 1.0, 0.0)

    def chunk(ci, carry):
        sl = pl.ds(pl.multiple_of(ci * c, c), c)
        lw = lw_ref[sl, :]
        cum = _dot_exact_lhs(tri, lw)
        tot = cum[c - 1:c, :]
        p_in = jnp.exp(cum)
        p_ex = jnp.exp(cum - lw)
        p_inv = jnp.exp(-cum)
        p_rest = jnp.exp(tot - cum)
        p_tot = jnp.exp(tot)
        rh_all = (r_ref[sl, :] * p_in).astype(BF16)
        ah_all = (-kk_ref[sl, :] * p_ex).astype(BF16)
        b_all = b_ref[sl, :]
        k_all = k_ref[sl, :]
        bh_all = (b_all * p_inv).astype(BF16)
        kh_all = (k_all * p_inv).astype(BF16)
        bt_all = (b_all * p_rest).astype(BF16)
        kt_all = (k_all * p_rest).astype(BF16)
        v_all = v_ref[sl, :]
        ys = []
        for h in range(B_HEADS):
            hs = slice(h * HEAD_DIM, (h + 1) * HEAD_DIM)
            ah, rh, bh, kh = ah_all[:, hs], rh_all[:, hs], bh_all[:, hs], kh_all[:, hs]
            vf = v_all[:, hs]
            vb = vf.astype(BF16)
            ar = jnp.concatenate([ah, rh], axis=0)
            gb = _dot_nt(ar, bh)
            gk = _dot_nt(ar, kh)
            a_ab = jnp.where(strict, gb[:c], 0.0)
            a_rb = jnp.where(incl, gb[c:], 0.0).astype(BF16)
            a_ak = jnp.where(strict, gk[:c], 0.0).astype(BF16)
            a_rk = jnp.where(incl, gk[c:], 0.0).astype(BF16)
            x = a_ab
            t = eye + x
            for _ in range(int(math.log2(c)) - 1):
                xb = x.astype(BF16)
                x = _dot(xb, xb)
                t = t + _dot(t.astype(BF16), x.astype(BF16))
            tb = t.astype(BF16)
            w1 = _dot(a_ak, vb)
            a2 = _dot(tb, ah)
            u1 = _dot(tb, w1.astype(BF16))
            y2 = _dot(a_rk, vb)
            n2 = _dot(vb.T, kt_all[:, hs])
            s0 = st_ref[h]
            s0b = s0.astype(BF16)
            ut = _dot_nt(s0b, a2.astype(BF16)) + u1.T
            utb = ut.astype(BF16)
            ys.append(_dot_nt(rh, s0b) + _dot_nt(a_rb, utb) + y2)
            st_ref[h] = s0 * p_tot[:, hs] + _dot(utb, bt_all[:, hs]) + n2
        y_ref[sl, :] = jnp.concatenate(ys, axis=-1)
        return carry

    lax.fori_loop(0, n_chunks, chunk, 0)


def _wkv_scan(r, lw, k, v, kk, b, ts):
    bsz, s, w = r.shape
    spec = pl.BlockSpec((None, ts, w), lambda bi, j: (bi, j, 0))
    return pl.pallas_call(
        _wkv_kernel,
        grid=(bsz, s // ts),
        in_specs=[spec] * 6,
        out_specs=spec,
        out_shape=jax.ShapeDtypeStruct((bsz, s, w), F32),
        scratch_shapes=[pltpu.VMEM((B_HEADS, HEAD_DIM, HEAD_DIM), F32)],
        compiler_params=_cparams(("parallel", "arbitrary")),
        name="wkv_scan",
    )(r, lw, k, v, kk, b)


def _merge_kernel(x_ref, ya_ref, o1_ref, o2_ref, o3_ref, l1_ref, l2_ref, l3_ref, yr_ref, g_ref, bonus_ref,
                  gates_ref, lng_ref, lnb_ref, e_ref, pa_ref, pb_ref, pc_ref, wo_ref, out_ref):
    d = x_ref.shape[1]
    l1, l2, l3 = l1_ref[...], l2_ref[...], l3_ref[...]
    m = jnp.maximum(jnp.maximum(l1, l2), l3)
    e1, e2, e3 = jnp.exp(l1 - m), jnp.exp(l2 - m), jnp.exp(l3 - m)
    yc = (e1 * o1_ref[...] + e2 * o2_ref[...] + e3 * o3_ref[...]) / (e1 + e2 + e3)

    e = e_ref[...]
    y = yr_ref[...]
    mean = _dot_exact_rhs(y, e) * (1.0 / HEAD_DIM)
    dv = y - mean
    var = _dot_exact_rhs(dv * dv, e) * (1.0 / HEAD_DIM)
    yb = dv * lax.rsqrt(var + B_GN_EPS) * lng_ref[...] + lnb_ref[...] + bonus_ref[...]
    yb = yb * g_ref[...]

    gates = gates_ref[...]
    merged = (gates[:, 0:d].astype(F32) * _dot(ya_ref[...], pa_ref[...])
              + gates[:, d:2 * d].astype(F32) * _dot(yb.astype(BF16), pb_ref[...])
              + gates[:, 2 * d:3 * d].astype(F32) * _dot(yc.astype(BF16), pc_ref[...]))
    out_ref[...] = x_ref[...] + _dot(merged.astype(BF16), wo_ref[...])


def _merge(x2, ya, o1, o2, o3, l1, l2, l3, yr, g, bonus, gates, lng, lnb, e_bd, pa, pb, pc, wo, tm):
    m, d = x2.shape
    tok = lambda a: pl.BlockSpec((tm, a.shape[1]), lambda i: (i, 0))
    full = lambda a: pl.BlockSpec(a.shape, lambda i: (0,) * a.ndim, pipeline_mode=pl.Buffered(1))
    toks = [x2, ya, o1, o2, o3, l1, l2, l3, yr, g, bonus, gates]
    consts = [lng.reshape(1, -1), lnb.reshape(1, -1), e_bd, pa, pb, pc, wo]
    return pl.pallas_call(
        _merge_kernel,
        grid=(m // tm,),
        in_specs=[tok(a) for a in toks] + [full(a) for a in consts],
        out_specs=pl.BlockSpec((tm, d), lambda i: (i, 0)),
        out_shape=jax.ShapeDtypeStruct((m, d), F32),
        compiler_params=_cparams(("parallel",)),
        name="merge",
    )(*toks, *consts)


def _ffn_kernel(xh_ref, x_ref, g_ref, wg_ref, wv_ref, cw_ref, wd_ref, fg_ref, o_ref, h_ref, acc_ref, *,
                tiles_per_seq, final_norm):
    j = pl.program_id(1)
    tm = x_ref.shape[0]

    @pl.when(j == 0)
    def _():
        x = x_ref[...]
        g = g_ref[...]
        h_ref[HALO:, :] = _rms(x, g).astype(BF16)
        seq_start = pl.program_id(0) % tiles_per_seq == 0
        halo = _rms(xh_ref[...], g)
        h_ref[0:HALO, :] = jnp.where(seq_start, 0.0, halo).astype(BF16)
        acc_ref[...] = x

    h = h_ref[...]
    gext = _dot(h, wg_ref[...])
    val = _dot(h[HALO:], wv_ref[...])
    cw = cw_ref[...]
    gate = (cw[0:1, :] * pltpu.roll(gext, 2, 0)[HALO:]
            + cw[1:2, :] * pltpu.roll(gext, 1, 0)[HALO:]
            + cw[2:3, :] * gext[HALO:])
    hid = gate * jax.nn.sigmoid(gate) * val
    acc_ref[...] += _dot(hid.astype(BF16), wd_ref[...])

    @pl.when(j == pl.num_programs(1) - 1)
    def _():
        out = acc_ref[...]
        if final_norm:
            out = _rms(out, fg_ref[...])
        o_ref[...] = out


def _ffn(x2, seq, g, wg, wv, cw, wd, fg, final_norm, tm, tf):
    m, d = x2.shape
    ffp = wg.shape[1]
    tps = seq // tm
    return pl.pallas_call(
        functools.partial(_ffn_kernel, tiles_per_seq=tps, final_norm=final_norm),
        grid=(m // tm, ffp // tf),
        in_specs=[
            pl.BlockSpec((HALO, d), lambda i, j: (jnp.maximum(i * (tm // HALO) - 1, 0), 0)),
            pl.BlockSpec((tm, d), lambda i, j: (i, 0)),
            pl.BlockSpec((1, d), lambda i, j: (0, 0)),
            pl.BlockSpec((d, tf), lambda i, j: (0, j)),
            pl.BlockSpec((d, tf), lambda i, j: (0, j)),
            pl.BlockSpec((3, tf), lambda i, j: (0, j)),
            pl.BlockSpec((tf, d), lambda i, j: (j, 0)),
            pl.BlockSpec((1, d), lambda i, j: (0, 0)),
        ],
        out_specs=pl.BlockSpec((tm, d), lambda i, j: (i, 0)),
        out_shape=jax.ShapeDtypeStruct((m, d), F32),
        scratch_shapes=[pltpu.VMEM((HALO + tm, d), BF16), pltpu.VMEM((tm, d), F32)],
        compiler_params=_cparams(("parallel", "arbitrary")),
        name="conv_ffn",
    )(x2, x2, g.reshape(1, d), wg, wv, cw, wd, fg.reshape(1, d))


def _t5_bucket(dist):
    small = dist < MAX_EXACT
    nf = jnp.maximum(dist, 1).astype(F32)
    large = MAX_EXACT + (jnp.log(nf / MAX_EXACT) / math.log(REL_MAX_DIST / MAX_EXACT)
                         * (N_BUCKETS - MAX_EXACT)).astype(jnp.int32)
    return jnp.where(small, dist, jnp.minimum(large, N_BUCKETS - 1))


def _band_bias(table, dilation, max_steps):
    i = jnp.arange(BLK)[:, None]
    j = jnp.arange(2 * BLK)[None, :]
    off = i + BLK - j
    bias = jnp.transpose(table[_t5_bucket(jnp.maximum(off, 0) * dilation)], (2, 0, 1)).astype(F32)
    valid = (off >= 0) & (off <= max_steps)
    return jnp.where(valid[None], bias, NEG_BIG)


def _pad_cols(w, n):
    return jnp.pad(w, [(0, 0)] * (w.ndim - 1) + [(0, n - w.shape[-1])])


def _pad_rows(w, n):
    return jnp.pad(w, [(0, 0)] * (w.ndim - 2) + [(0, n - w.shape[-2]), (0, 0)])


def kernel(x, rel_bias, norm1_g, w_in, attn_sinks, rwkv_mu, rwkv_w0, rwkv_w_up, rwkv_a0, rwkv_a_up, rwkv_g_up,
           rwkv_k_k, rwkv_k_a, rwkv_r_k, rwkv_lnx_g, rwkv_lnx_b, proj_a, proj_b, proj_c, w_out, norm2_g,
           ffn_up, ffn_conv, ffn_down, final_g):
    bsz, seq, d = x.shape
    depth = w_in.shape[0]
    m = bsz * seq
    d_ff = ffn_conv.shape[-1]
    ffp = -(-d_ff // D_FF_PAD_TO) * D_FF_PAD_TO

    tm_proj = min(1024, seq)
    tm_tok = min(256, seq)
    tm_ffn = min(512, seq)
    ts_wkv = min(512, seq)

    na = A_HQ
    bias_a = _band_bias(rel_bias[:, :na], 1, BLK - 1)
    bias_c = [_band_bias(rel_bias[:, na + gi * C_HG:na + (gi + 1) * C_HG], dil, win // dil)
              for gi, (win, dil) in enumerate(C_GROUPS)]

    q_end = (A_HQ + 2 * A_HKV) * HEAD_DIM
    b0 = q_end
    b_r_end = b0 + 3 * B_WIDTH
    b_wd_end = b_r_end + LORA_DECAY
    b_ad_end = b_wd_end + LORA_ICLR
    b_end = b_ad_end + LORA_GATE
    c_end = b_end + 3 * C_WIDTH

    def rwkv_cols(t):
        parts = [t[..., b0:b_r_end], _pad_cols(t[..., b_r_end:b_wd_end], LORA_PAD),
                 _pad_cols(t[..., b_wd_end:b_ad_end], LORA_PAD), t[..., b_ad_end:b_end]]
        return _pad_cols(jnp.concatenate(parts, axis=-1), N_RWKV)

    w_attn = jnp.concatenate([w_in[..., :q_end], w_in[..., b_end:c_end]], axis=-1).astype(BF16)
    w_rwkv = rwkv_cols(w_in).astype(BF16)
    w_gate = w_in[..., c_end:].astype(BF16)
    mu_pad = rwkv_cols(jnp.pad(rwkv_mu, ((0, 0), (b0, 0))))
    wup_pad = _pad_rows(rwkv_w_up, LORA_PAD)
    aup_pad = _pad_rows(rwkv_a_up, LORA_PAD)
    r_k = rwkv_r_k.reshape(depth, B_WIDTH)

    head_id = jnp.arange(B_WIDTH) // HEAD_DIM
    e_bd = (head_id[:, None] == head_id[None, :]).astype(BF16)

    pa, pb, pc, wo = (t.astype(BF16) for t in (proj_a, proj_b, proj_c, w_out))
    wg = _pad_cols(ffn_up[..., :d_ff], ffp).astype(BF16)
    wv = _pad_cols(ffn_up[..., d_ff:], ffp).astype(BF16)
    cw = _pad_cols(ffn_conv, ffp)
    wd = _pad_rows(ffn_down, ffp).astype(BF16)

    x2 = x.reshape(m, d)
    for l in range(depth):
        p_attn = _norm_matmul(x2, norm1_g[l], w_attn[l], BF16, False, tm_proj, 512, "proj_attn")
        p_rwkv = _norm_matmul(x2, norm1_g[l], w_rwkv[l], F32, False, tm_proj, 512, "proj_rwkv")
        gates = _norm_matmul(x2, norm1_g[l], w_gate[l], BF16, True, tm_proj, 512, "proj_gates")

        pa3 = p_attn.reshape(bsz, seq, N_ATTN)
        kvw = A_HKV * HEAD_DIM
        y_a = _band_attention(pa3, bias_a, attn_sinks[l], dil=1, q_blk=0, k_blk=(A_HQ * HEAD_DIM) // kvw,
                              v_blk=(A_HQ * HEAD_DIM) // kvw + 1, q_w=A_HQ * HEAD_DIM, kv_w=kvw,
                              nq=A_HQ, nkv=A_HKV, want_lse=False, out_dtype=BF16, name="attn_swa")
        oc, lc = [], []
        cq0 = q_end // C_OUT
        for gi, (win, dil) in enumerate(C_GROUPS):
            o, ls = _band_attention(pa3, bias_c[gi], None, dil=dil, q_blk=cq0 + gi, k_blk=cq0 + 3 + gi,
                                    v_blk=cq0 + 6 + gi, q_w=C_OUT, kv_w=C_OUT, nq=C_HG, nkv=C_HG,
                                    want_lse=True, out_dtype=F32, name=f"attn_dil{dil}")
            oc.append(o.reshape(m, C_OUT))
            lc.append(ls.reshape(m, C_OUT))

        r, lw, k2, v, kk, bb, g, bonus = _rwkv_prep(
            p_rwkv, seq, mu_pad[l], rwkv_w0[l], wup_pad[l], rwkv_a0[l], aup_pad[l], rwkv_g_up[l],
            rwkv_k_k[l], rwkv_k_a[l], r_k[l], e_bd, tm_tok)
        sh = lambda t: t.reshape(bsz, seq, B_WIDTH)
        y_raw = _wkv_scan(sh(r), sh(lw), sh(k2), sh(v), sh(kk), sh(bb), ts_wkv).reshape(m, B_WIDTH)

        x2 = _merge(x2, y_a.reshape(m, A_HQ * HEAD_DIM), oc[0], oc[1], oc[2], lc[0], lc[1], lc[2], y_raw, g,
                    bonus, gates, rwkv_lnx_g[l], rwkv_lnx_b[l], e_bd, pa[l], pb[l], pc[l], wo[l], tm_tok)
        x2 = _ffn(x2, seq, norm2_g[l], wg[l], wv[l], cw[l], wd[l], final_g, l == depth - 1, tm_ffn, 512)
    return x2.reshape(bsz, seq, d)
```

```python
import functools
import math

import jax
import jax.numpy as jnp
import numpy as np
from jax import lax
from jax.experimental import pallas as pl
from jax.experimental.pallas import tpu as pltpu

F32 = jnp.float32
BF16 = jnp.bfloat16

HEAD_DIM = 64
BLK = 128
NORM_EPS = 1e-5
A_HQ, A_HKV = 8, 2
B_HEADS = 12
B_WIDTH = B_HEADS * HEAD_DIM
LORA_DECAY, LORA_ICLR, LORA_GATE = 96, 96, 256
LORA_PAD = 128
B_GN_EPS = 64e-5
C_GROUPS = ((128, 1), (512, 4), (2048, 16))
C_HG = 4
C_WIDTH = C_HG * len(C_GROUPS) * HEAD_DIM
C_OUT = C_HG * HEAD_DIM
N_BUCKETS, MAX_EXACT, REL_MAX_DIST = 32, 16, 2048
D_FF_PAD_TO = 512
NEG_BIG = -1e30

N_ATTN = A_HQ * HEAD_DIM + 2 * A_HKV * HEAD_DIM + 3 * C_WIDTH
N_RWKV = 3 * B_WIDTH + 2 * LORA_PAD + LORA_GATE + 256
OFF_WD = 3 * B_WIDTH
OFF_AD = OFF_WD + LORA_PAD
OFF_GD = OFF_AD + LORA_PAD
WKV_CHUNK = 64
HALO = 16

VMEM_LIMIT = 56 * 1024 * 1024


def _cparams(sem):
    return pltpu.CompilerParams(dimension_semantics=sem, vmem_limit_bytes=VMEM_LIMIT)


def _dot(a, b):
    return jnp.dot(a, b, preferred_element_type=F32)


def _dot_nt(a, b):
    return lax.dot_general(a, b, (((1,), (1,)), ((), ())), preferred_element_type=F32)


def _split2(x):
    hi = x.astype(BF16)
    lo = (x - hi.astype(F32)).astype(BF16)
    return hi, lo


def _split3(x):
    hi = x.astype(BF16)
    r1 = x - hi.astype(F32)
    mid = r1.astype(BF16)
    lo = (r1 - mid.astype(F32)).astype(BF16)
    return hi, mid, lo


def _dot_exact_rhs(a, b_bf16):
    h, m, l = _split3(a)
    return _dot(h, b_bf16) + _dot(m, b_bf16) + _dot(l, b_bf16)


def _dot_exact_lhs(a_bf16, b):
    h, m, l = _split3(b)
    return _dot(a_bf16, h) + _dot(a_bf16, m) + _dot(a_bf16, l)


def _dot3(a, b):
    ah, al = _split2(a)
    bh, bl = _split2(b)
    return _dot(ah, bh) + _dot(ah, bl) + _dot(al, bh)


def _rms(x, g):
    ms = jnp.mean(x * x, axis=-1, keepdims=True)
    return x * lax.rsqrt(ms + NORM_EPS) * g


def _norm_matmul_kernel(x_ref, g_ref, w_ref, o_ref, h_ref, *, sigmoid):
    @pl.when(pl.program_id(1) == 0)
    def _():
        h_ref[...] = _rms(x_ref[...], g_ref[...]).astype(BF16)

    acc = _dot(h_ref[...], w_ref[...])
    if sigmoid:
        acc = jax.nn.sigmoid(acc)
    o_ref[...] = acc.astype(o_ref.dtype)


def _norm_matmul(x2, g, w, out_dtype, sigmoid, tm, tn, name):
    m, d = x2.shape
    n = w.shape[1]
    return pl.pallas_call(
        functools.partial(_norm_matmul_kernel, sigmoid=sigmoid),
        grid=(m // tm, n // tn),
        in_specs=[
            pl.BlockSpec((tm, d), lambda i, j: (i, 0)),
            pl.BlockSpec((1, d), lambda i, j: (0, 0)),
            pl.BlockSpec((d, tn), lambda i, j: (0, j)),
        ],
        out_specs=pl.BlockSpec((tm, tn), lambda i, j: (i, j)),
        out_shape=jax.ShapeDtypeStruct((m, n), out_dtype),
        scratch_shapes=[pltpu.VMEM((tm, d), BF16)],
        compiler_params=_cparams(("parallel", "arbitrary")),
        name=name,
    )(x2, g.reshape(1, d), w)


def _band_attn_kernel(*refs, nq, nkv, has_sink, want_lse):
    it = iter(refs)
    q_ref, kp_ref, kc_ref, vp_ref, vc_ref, bias_ref = (next(it) for _ in range(6))
    sink_ref = next(it) if has_sink else None
    o_ref = next(it)
    lse_ref = next(it) if want_lse else None

    first = pl.program_id(2) == 0
    q = q_ref[...] * jnp.asarray(HEAD_DIM ** -0.5, BF16)
    k = jnp.concatenate([kp_ref[...], kc_ref[...]], axis=0)
    v = jnp.concatenate([vp_ref[...], vc_ref[...]], axis=0)
    col = lax.broadcasted_iota(jnp.int32, (BLK, 2 * BLK), 1)
    edge = jnp.where(col < BLK, jnp.where(first, NEG_BIG, 0.0), 0.0)
    rep = nq // nkv
    outs, lses = [], []
    for h in range(nq):
        g = h // rep
        qh = q[:, h * HEAD_DIM:(h + 1) * HEAD_DIM]
        kh = k[:, g * HEAD_DIM:(g + 1) * HEAD_DIM]
        vh = v[:, g * HEAD_DIM:(g + 1) * HEAD_DIM]
        s = _dot_nt(qh, kh) + bias_ref[h] + edge
        m = jnp.max(s, axis=-1, keepdims=True)
        if has_sink:
            sk = sink_ref[h]
            m = jnp.maximum(m, sk)
        p = jnp.exp(s - m)
        l = jnp.sum(p, axis=-1, keepdims=True)
        denom = l + jnp.exp(sk - m) if has_sink else l
        o = _dot(p.astype(BF16), vh) / denom
        outs.append(o)
        if want_lse:
            lses.append(jnp.broadcast_to(m + jnp.log(l), (BLK, HEAD_DIM)))
    o_ref[...] = jnp.concatenate(outs, axis=-1).astype(o_ref.dtype)
    if want_lse:
        lse_ref[...] = jnp.concatenate(lses, axis=-1)


def _band_attention(p_attn, bias, sink, *, dil, q_blk, k_blk, v_blk, q_w, kv_w, nq, nkv, want_lse,
                    out_dtype, name):
    b, s, n = p_attn.shape
    lf = s // dil
    nb = lf // BLK
    pv = p_attn.reshape(b, lf, dil * n)
    qpr, kpr = n // q_w, n // kv_w
    has_sink = sink is not None

    in_specs = [
        pl.BlockSpec((None, BLK, q_w), lambda bi, r, j: (bi, j, r * qpr + q_blk)),
        pl.BlockSpec((None, BLK, kv_w), lambda bi, r, j: (bi, jnp.maximum(j - 1, 0), r * kpr + k_blk)),
        pl.BlockSpec((None, BLK, kv_w), lambda bi, r, j: (bi, j, r * kpr + k_blk)),
        pl.BlockSpec((None, BLK, kv_w), lambda bi, r, j: (bi, jnp.maximum(j - 1, 0), r * kpr + v_blk)),
        pl.BlockSpec((None, BLK, kv_w), lambda bi, r, j: (bi, j, r * kpr + v_blk)),
        pl.BlockSpec((nq, BLK, 2 * BLK), lambda bi, r, j: (0, 0, 0)),
    ]
    args = [pv, pv, pv, pv, pv, bias]
    if has_sink:
        in_specs.append(pl.BlockSpec(memory_space=pltpu.SMEM))
        args.append(sink)
    ow = nq * HEAD_DIM
    out_spec = pl.BlockSpec((None, BLK, ow), lambda bi, r, j: (bi, j, r))
    out_shape = jax.ShapeDtypeStruct((b, lf, dil * ow), out_dtype)
    if want_lse:
        out_specs = [out_spec, out_spec]
        out_shapes = [out_shape, jax.ShapeDtypeStruct((b, lf, dil * ow), F32)]
    else:
        out_specs, out_shapes = out_spec, out_shape
    res = pl.pallas_call(
        functools.partial(_band_attn_kernel, nq=nq, nkv=nkv, has_sink=has_sink, want_lse=want_lse),
        grid=(b, dil, nb),
        in_specs=in_specs,
        out_specs=out_specs,
        out_shape=out_shapes,
        compiler_params=_cparams(("parallel", "parallel", "arbitrary")),
        name=name,
    )(*args)
    if want_lse:
        return res[0].reshape(b, s, ow), res[1].reshape(b, s, ow)
    return res.reshape(b, s, ow)


def _rwkv_prep_kernel(ph_ref, p_ref, mu_ref, w0_ref, wup_ref, a0_ref, aup_ref, gup_ref, kk_ref, ka_ref,
                      rk_ref, e_ref, r_o, lw_o, k_o, v_o, kk_o, b_o, g_o, bonus_o, *, tiles_per_seq):
    tm = p_ref.shape[0]
    p = p_ref[...]
    seq_start = pl.program_id(0) % tiles_per_seq == 0
    last = ph_ref[...][HALO - 1:HALO, :]
    last = jnp.where(seq_start, 0.0, last)
    row = lax.broadcasted_iota(jnp.int32, (tm, 1), 0)
    prev = jnp.where(row == 0, last, pltpu.roll(p, 1, 0))
    pf = p + (prev - p) * mu_ref[...]

    r = pf[:, 0:B_WIDTH]
    k = pf[:, B_WIDTH:2 * B_WIDTH]
    v = pf[:, 2 * B_WIDTH:3 * B_WIDTH]
    wd = pf[:, OFF_WD:OFF_WD + LORA_PAD]
    ad = pf[:, OFF_AD:OFF_AD + LORA_PAD]
    gd = pf[:, OFF_GD:OFF_GD + LORA_GATE]

    z = w0_ref[...] + _dot3(jnp.tanh(wd), wup_ref[...])
    nz = -z
    softplus = jnp.maximum(nz, 0.0) + jnp.log(1.0 + jnp.exp(-jnp.abs(nz)))
    w = -softplus - 0.5
    lw_o[...] = -jnp.exp(w)
    a = jax.nn.sigmoid(a0_ref[...] + _dot3(ad, aup_ref[...]))
    g_o[...] = _dot3(jax.nn.sigmoid(gd), gup_ref[...])

    e = e_ref[...]
    kk = k * kk_ref[...]
    nrm = jnp.sqrt(_dot_exact_rhs(kk * kk, e))
    kk = kk / jnp.maximum(nrm, 1e-12)
    k2 = k * (1.0 + (a - 1.0) * ka_ref[...])
    r_o[...] = r
    k_o[...] = k2
    v_o[...] = v
    kk_o[...] = kk
    b_o[...] = kk * a
    bonus_o[...] = _dot_exact_rhs(r * k2 * rk_ref[...], e) * v


def _rwkv_prep(pb2, seq, mu, w0, wup, a0, aup, gup, k_k, k_a, r_k, e_bd, tm):
    m, n = pb2.shape
    tps = seq // tm
    row = lambda a: a.reshape(1, -1)
    full = lambda a: pl.BlockSpec(a.shape, lambda i: (0,) * a.ndim)
    args = [pb2, pb2, row(mu), row(w0), wup, row(a0), aup, gup, row(k_k), row(k_a), row(r_k), e_bd]
    in_specs = [
        pl.BlockSpec((HALO, n), lambda i: (jnp.maximum(i * (tm // HALO) - 1, 0), 0)),
        pl.BlockSpec((tm, n), lambda i: (i, 0)),
    ] + [full(a) for a in args[2:]]
    o_spec = pl.BlockSpec((tm, B_WIDTH), lambda i: (i, 0))
    o_shape = jax.ShapeDtypeStruct((m, B_WIDTH), F32)
    return pl.pallas_call(
        functools.partial(_rwkv_prep_kernel, tiles_per_seq=tps),
        grid=(m // tm,),
        in_specs=in_specs,
        out_specs=[o_spec] * 8,
        out_shape=[o_shape] * 8,
        compiler_params=_cparams(("parallel",)),
        name="rwkv_prep",
    )(*args)


def _wkv_kernel(r_ref, lw_ref, k_ref, v_ref, kk_ref, b_ref, y_ref, st_ref):
    c = WKV_CHUNK
    n_chunks = r_ref.shape[0] // c

    @pl.when(pl.program_id(1) == 0)
    def _():
        st_ref[...] = jnp.zeros_like(st_ref)

    rowi = lax.broadcasted_iota(jnp.int32, (c, c), 0)
    coli = lax.broadcasted_iota(jnp.int32, (c, c), 1)
    incl = rowi >= coli
    strict = rowi > coli
    tri = jnp.where(incl, 1.0, 0.0).astype(BF16)
    eye = jnp.where(rowi == coli, 1.0, 0.0)

    def chunk(ci, carry):
        sl = pl.ds(pl.multiple_of(ci * c, c), c)
        lw = lw_ref[sl, :]
        cum = _dot_exact_lhs(tri, lw)
        tot = cum[c - 1:c, :]
        p_in = jnp.exp(cum)
        p_ex = jnp.exp(cum - lw)
        p_inv = jnp.exp(-cum)
        p_rest = jnp.exp(tot - cum)
        p_tot = jnp.exp(tot)
        rh_all = (r_ref[sl, :] * p_in).astype(BF16)
        ah_all = (-kk_ref[sl, :] * p_ex).astype(BF16)
        b_all = b_ref[sl, :]
        k_all = k_ref[sl, :]
        bh_all = (b_all * p_inv).astype(BF16)
        kh_all = (k_all * p_inv).astype(BF16)
        bt_all = (b_all * p_rest).astype(BF16)
        kt_all = (k_all * p_rest).astype(BF16)
        v_all = v_ref[sl, :]
        heads = range(B_HEADS)
        hsl = [slice(h * HEAD_DIM, (h + 1) * HEAD_DIM) for h in heads]
        ah = [ah_all[:, s] for s in hsl]
        rh = [rh_all[:, s] for s in hsl]
        vf = [v_all[:, s] for s in hsl]
        vb = [t.astype(BF16) for t in vf]
        ar = [jnp.concatenate([ah[h], rh[h]], axis=0) for h in heads]
        gb = [_dot_nt(ar[h], bh_all[:, hsl[h]]) for h in heads]
        gk = [_dot_nt(ar[h], kh_all[:, hsl[h]]) for h in heads]
        a_rb = [jnp.where(incl, t[c:], 0.0).astype(BF16) for t in gb]
        a_ak = [jnp.where(strict, t[:c], 0.0).astype(BF16) for t in gk]
        a_rk = [jnp.where(incl, t[c:], 0.0).astype(BF16) for t in gk]
        x = [jnp.where(strict, t[:c], 0.0) for t in gb]
        t = [eye + xi for xi in x]
        for _ in range(int(math.log2(c)) - 1):
            xb = [xi.astype(BF16) for xi in x]
            x = [_dot(xi, xi) for xi in xb]
            t = [t[h] + _dot(t[h].astype(BF16), x[h].astype(BF16)) for h in heads]
        tb = [ti.astype(BF16) for ti in t]
        w1 = [_dot(a_ak[h], vb[h]) for h in heads]
        y2 = [_dot(a_rk[h], vb[h]) for h in heads]
        n2 = [_dot(vf[h].T.astype(BF16), kt_all[:, hsl[h]]) for h in heads]
        a2 = [_dot(tb[h], ah[h]).astype(BF16) for h in heads]
        u1 = [_dot(tb[h], w1[h].astype(BF16)) for h in heads]
        st_all = st_ref[...]
        s0b = [st_all[h].astype(BF16) for h in heads]
        utb = [(_dot_nt(s0b[h], a2[h]) + u1[h].T).astype(BF16) for h in heads]
        ys = [_dot_nt(rh[h], s0b[h]) + _dot_nt(a_rb[h], utb[h]) + y2[h] for h in heads]
        sts = [st_all[h] * p_tot[:, hsl[h]] + _dot(utb[h], bt_all[:, hsl[h]]) + n2[h] for h in heads]
        st_ref[...] = jnp.stack(sts, axis=0)
        y_ref[sl, :] = jnp.concatenate(ys, axis=-1)
        return carry

    lax.fori_loop(0, n_chunks, chunk, 0)


def _wkv_scan(r, lw, k, v, kk, b, ts):
    bsz, s, w = r.shape
    spec = pl.BlockSpec((None, ts, w), lambda bi, j: (bi, j, 0))
    return pl.pallas_call(
        _wkv_kernel,
        grid=(bsz, s // ts),
        in_specs=[spec] * 6,
        out_specs=spec,
        out_shape=jax.ShapeDtypeStruct((bsz, s, w), F32),
        scratch_shapes=[pltpu.VMEM((B_HEADS, HEAD_DIM, HEAD_DIM), F32)],
        compiler_params=_cparams(("parallel", "arbitrary")),
        name="wkv_scan",
    )(r, lw, k, v, kk, b)


def _merge_kernel(x_ref, ya_ref, o1_ref, o2_ref, o3_ref, l1_ref, l2_ref, l3_ref, yr_ref, g_ref, bonus_ref,
                  gates_ref, lng_ref, lnb_ref, e_ref, pa_ref, pb_ref, pc_ref, wo_ref, out_ref):
    d = x_ref.shape[1]
    l1, l2, l3 = l1_ref[...], l2_ref[...], l3_ref[...]
    m = jnp.maximum(jnp.maximum(l1, l2), l3)
    e1, e2, e3 = jnp.exp(l1 - m), jnp.exp(l2 - m), jnp.exp(l3 - m)
    yc = (e1 * o1_ref[...] + e2 * o2_ref[...] + e3 * o3_ref[...]) / (e1 + e2 + e3)

    e = e_ref[...]
    y = yr_ref[...]
    mean = _dot_exact_rhs(y, e) * (1.0 / HEAD_DIM)
    dv = y - mean
    var = _dot_exact_rhs(dv * dv, e) * (1.0 / HEAD_DIM)
    yb = dv * lax.rsqrt(var + B_GN_EPS) * lng_ref[...] + lnb_ref[...] + bonus_ref[...]
    yb = yb * g_ref[...]

    gates = gates_ref[...]
    merged = (gates[:, 0:d].astype(F32) * _dot(ya_ref[...], pa_ref[...])
              + gates[:, d:2 * d].astype(F32) * _dot(yb.astype(BF16), pb_ref[...])
              + gates[:, 2 * d:3 * d].astype(F32) * _dot(yc.astype(BF16), pc_ref[...]))
    out_ref[...] = x_ref[...] + _dot(merged.astype(BF16), wo_ref[...])


def _merge(x2, ya, o1, o2, o3, l1, l2, l3, yr, g, bonus, gates, lng, lnb, e_bd, pa, pb, pc, wo, tm):
    m, d = x2.shape
    tok = lambda a: pl.BlockSpec((tm, a.shape[1]), lambda i: (i, 0))
    full = lambda a: pl.BlockSpec(a.shape, lambda i: (0,) * a.ndim, pipeline_mode=pl.Buffered(1))
    toks = [x2, ya, o1, o2, o3, l1, l2, l3, yr, g, bonus, gates]
    consts = [lng.reshape(1, -1), lnb.reshape(1, -1), e_bd, pa, pb, pc, wo]
    return pl.pallas_call(
        _merge_kernel,
        grid=(m // tm,),
        in_specs=[tok(a) for a in toks] + [full(a) for a in consts],
        out_specs=pl.BlockSpec((tm, d), lambda i: (i, 0)),
        out_shape=jax.ShapeDtypeStruct((m, d), F32),
        compiler_params=_cparams(("parallel",)),
        name="merge",
    )(*toks, *consts)


def _ffn_kernel(xh_ref, x_ref, g_ref, wg_ref, wv_ref, cw_ref, wd_ref, fg_ref, o_ref, h_ref, acc_ref, *,
                tiles_per_seq, final_norm):
    j = pl.program_id(1)
    tm = x_ref.shape[0]

    @pl.when(j == 0)
    def _():
        x = x_ref[...]
        g = g_ref[...]
        h_ref[HALO:, :] = _rms(x, g).astype(BF16)
        seq_start = pl.program_id(0) % tiles_per_seq == 0
        halo = _rms(xh_ref[...], g)
        h_ref[0:HALO, :] = jnp.where(seq_start, 0.0, halo).astype(BF16)
        acc_ref[...] = x

    h = h_ref[...]
    gext = _dot(h, wg_ref[...])
    val = _dot(h[HALO:], wv_ref[...])
    cw = cw_ref[...]
    gate = (cw[0:1, :] * pltpu.roll(gext, 2, 0)[HALO:]
            + cw[1:2, :] * pltpu.roll(gext, 1, 0)[HALO:]
            + cw[2:3, :] * gext[HALO:])
    hid = gate * jax.nn.sigmoid(gate) * val
    acc_ref[...] += _dot(hid.astype(BF16), wd_ref[...])

    @pl.when(j == pl.num_programs(1) - 1)
    def _():
        out = acc_ref[...]
        if final_norm:
            out = _rms(out, fg_ref[...])
        o_ref[...] = out


def _ffn(x2, seq, g, wg, wv, cw, wd, fg, final_norm, tm, tf):
    m, d = x2.shape
    ffp = wg.shape[1]
    tps = seq // tm
    return pl.pallas_call(
        functools.partial(_ffn_kernel, tiles_per_seq=tps, final_norm=final_norm),
        grid=(m // tm, ffp // tf),
        in_specs=[
            pl.BlockSpec((HALO, d), lambda i, j: (jnp.maximum(i * (tm // HALO) - 1, 0), 0)),
            pl.BlockSpec((tm, d), lambda i, j: (i, 0)),
            pl.BlockSpec((1, d), lambda i, j: (0, 0)),
            pl.BlockSpec((d, tf), lambda i, j: (0, j)),
            pl.BlockSpec((d, tf), lambda i, j: (0, j)),
            pl.BlockSpec((3, tf), lambda i, j: (0, j)),
            pl.BlockSpec((tf, d), lambda i, j: (j, 0)),
            pl.BlockSpec((1, d), lambda i, j: (0, 0)),
        ],
        out_specs=pl.BlockSpec((tm, d), lambda i, j: (i, 0)),
        out_shape=jax.ShapeDtypeStruct((m, d), F32),
        scratch_shapes=[pltpu.VMEM((HALO + tm, d), BF16), pltpu.VMEM((tm, d), F32)],
        compiler_params=_cparams(("parallel", "arbitrary")),
        name="conv_ffn",
    )(x2, x2, g.reshape(1, d), wg, wv, cw, wd, fg.reshape(1, d))


def _t5_bucket(dist):
    small = dist < MAX_EXACT
    nf = jnp.maximum(dist, 1).astype(F32)
    large = MAX_EXACT + (jnp.log(nf / MAX_EXACT) / math.log(REL_MAX_DIST / MAX_EXACT)
                         * (N_BUCKETS - MAX_EXACT)).astype(jnp.int32)
    return jnp.where(small, dist, jnp.minimum(large, N_BUCKETS - 1))


def _band_bias(table, dilation, max_steps):
    i = jnp.arange(BLK)[:, None]
    j = jnp.arange(2 * BLK)[None, :]
    off = i + BLK - j
    bias = jnp.transpose(table[_t5_bucket(jnp.maximum(off, 0) * dilation)], (2, 0, 1)).astype(F32)
    valid = (off >= 0) & (off <= max_steps)
    return jnp.where(valid[None], bias, NEG_BIG)


def _pad_cols(w, n):
    return jnp.pad(w, [(0, 0)] * (w.ndim - 1) + [(0, n - w.shape[-1])])


def _pad_rows(w, n):
    return jnp.pad(w, [(0, 0)] * (w.ndim - 2) + [(0, n - w.shape[-2]), (0, 0)])


def kernel(x, rel_bias, norm1_g, w_in, attn_sinks, rwkv_mu, rwkv_w0, rwkv_w_up, rwkv_a0, rwkv_a_up, rwkv_g_up,
           rwkv_k_k, rwkv_k_a, rwkv_r_k, rwkv_lnx_g, rwkv_lnx_b, proj_a, proj_b, proj_c, w_out, norm2_g,
           ffn_up, ffn_conv, ffn_down, final_g):
    bsz, seq, d = x.shape
    depth = w_in.shape[0]
    m = bsz * seq
    d_ff = ffn_conv.shape[-1]
    ffp = -(-d_ff // D_FF_PAD_TO) * D_FF_PAD_TO

    tm_proj = min(1024, seq)
    tm_tok = min(256, seq)
    tm_ffn = min(512, seq)
    ts_wkv = min(512, seq)

    na = A_HQ
    bias_a = _band_bias(rel_bias[:, :na], 1, BLK - 1)
    bias_c = [_band_bias(rel_bias[:, na + gi * C_HG:na + (gi + 1) * C_HG], dil, win // dil)
              for gi, (win, dil) in enumerate(C_GROUPS)]

    q_end = (A_HQ + 2 * A_HKV) * HEAD_DIM
    b0 = q_end
    b_r_end = b0 + 3 * B_WIDTH
    b_wd_end = b_r_end + LORA_DECAY
    b_ad_end = b_wd_end + LORA_ICLR
    b_end = b_ad_end + LORA_GATE
    c_end = b_end + 3 * C_WIDTH

    def rwkv_cols(t):
        parts = [t[..., b0:b_r_end], _pad_cols(t[..., b_r_end:b_wd_end], LORA_PAD),
                 _pad_cols(t[..., b_wd_end:b_ad_end], LORA_PAD), t[..., b_ad_end:b_end]]
        return _pad_cols(jnp.concatenate(parts, axis=-1), N_RWKV)

    w_attn = jnp.concatenate([w_in[..., :q_end], w_in[..., b_end:c_end]], axis=-1).astype(BF16)
    w_rwkv = rwkv_cols(w_in).astype(BF16)
    w_gate = w_in[..., c_end:].astype(BF16)
    mu_pad = rwkv_cols(jnp.pad(rwkv_mu, ((0, 0), (b0, 0))))
    wup_pad = _pad_rows(rwkv_w_up, LORA_PAD)
    aup_pad = _pad_rows(rwkv_a_up, LORA_PAD)
    r_k = rwkv_r_k.reshape(depth, B_WIDTH)

    head_id = jnp.arange(B_WIDTH) // HEAD_DIM
    e_bd = (head_id[:, None] == head_id[None, :]).astype(BF16)

    pa, pb, pc, wo = (t.astype(BF16) for t in (proj_a, proj_b, proj_c, w_out))
    wg = _pad_cols(ffn_up[..., :d_ff], ffp).astype(BF16)
    wv = _pad_cols(ffn_up[..., d_ff:], ffp).astype(BF16)
    cw = _pad_cols(ffn_conv, ffp)
    wd = _pad_rows(ffn_down, ffp).astype(BF16)

    x2 = x.reshape(m, d)
    for l in range(depth):
        p_attn = _norm_matmul(x2, norm1_g[l], w_attn[l], BF16, False, tm_proj, 512, "proj_attn")
        p_rwkv = _norm_matmul(x2, norm1_g[l], w_rwkv[l], F32, False, tm_proj, 512, "proj_rwkv")
        gates = _norm_matmul(x2, norm1_g[l], w_gate[l], BF16, True, tm_proj, 512, "proj_gates")

        pa3 = p_attn.reshape(bsz, seq, N_ATTN)
        kvw = A_HKV * HEAD_DIM
        y_a = _band_attention(pa3, bias_a, attn_sinks[l], dil=1, q_blk=0, k_blk=(A_HQ * HEAD_DIM) // kvw,
                              v_blk=(A_HQ * HEAD_DIM) // kvw + 1, q_w=A_HQ * HEAD_DIM, kv_w=kvw,
                              nq=A_HQ, nkv=A_HKV, want_lse=False, out_dtype=BF16, name="attn_swa")
        oc, lc = [], []
        cq0 = q_end // C_OUT
        for gi, (win, dil) in enumerate(C_GROUPS):
            o, ls = _band_attention(pa3, bias_c[gi], None, dil=dil, q_blk=cq0 + gi, k_blk=cq0 + 3 + gi,
                                    v_blk=cq0 + 6 + gi, q_w=C_OUT, kv_w=C_OUT, nq=C_HG, nkv=C_HG,
                                    want_lse=True, out_dtype=F32, name=f"attn_dil{dil}")
            oc.append(o.reshape(m, C_OUT))
            lc.append(ls.reshape(m, C_OUT))

        r, lw, k2, v, kk, bb, g, bonus = _rwkv_prep(
            p_rwkv, seq, mu_pad[l], rwkv_w0[l], wup_pad[l], rwkv_a0[l], aup_pad[l], rwkv_g_up[l],
            rwkv_k_k[l], rwkv_k_a[l], r_k[l], e_bd, tm_tok)
        sh = lambda t: t.reshape(bsz, seq, B_WIDTH)
        y_raw = _wkv_scan(sh(r), sh(lw), sh(k2), sh(v), sh(kk), sh(bb), ts_wkv).reshape(m, B_WIDTH)

        x2 = _merge(x2, y_a.reshape(m, A_HQ * HEAD_DIM), oc[0], oc[1], oc[2], lc[0], lc[1], lc[2], y_raw, g,
                    bonus, gates, rwkv_lnx_g[l], rwkv_lnx_b[l], e_bd, pa[l], pb[l], pc[l], wo[l], tm_tok)
        x2 = _ffn(x2, seq, norm2_g[l], wg[l], wv[l], cw[l], wd[l], final_g, l == depth - 1, tm_ffn, 512)
    return x2.reshape(bsz, seq, d)
```

```python
import functools
import math

import jax
import jax.numpy as jnp
import numpy as np
from jax import lax
from jax.experimental import pallas as pl
from jax.experimental.pallas import tpu as pltpu

F32 = jnp.float32
BF16 = jnp.bfloat16

HEAD_DIM = 64
LANES = 128
BLK = 128
NORM_EPS = 1e-5
A_HQ, A_HKV = 8, 2
B_HEADS = 12
B_WIDTH = B_HEADS * HEAD_DIM
LORA_DECAY, LORA_ICLR, LORA_GATE = 96, 96, 256
LORA_PAD = 128
B_GN_EPS = 64e-5
C_GROUPS = ((128, 1), (512, 4), (2048, 16))
C_HG = 4
C_WIDTH = C_HG * len(C_GROUPS) * HEAD_DIM
C_OUT = C_HG * HEAD_DIM
N_BUCKETS, MAX_EXACT, REL_MAX_DIST = 32, 16, 2048
D_FF_PAD_TO = 512
NEG_BIG = -1e30

N_ATTN = A_HQ * HEAD_DIM + 2 * A_HKV * HEAD_DIM + 3 * C_WIDTH
N_RWKV = 3 * B_WIDTH + 2 * LORA_PAD + LORA_GATE + 256
OFF_WD = 3 * B_WIDTH
OFF_AD = OFF_WD + LORA_PAD
OFF_GD = OFF_AD + LORA_PAD
WKV_CHUNK = 64
HALO = 16

VMEM_LIMIT = 56 * 1024 * 1024


def _cparams(sem):
    return pltpu.CompilerParams(dimension_semantics=sem, vmem_limit_bytes=VMEM_LIMIT)


def _dot(a, b):
    return jnp.dot(a, b, preferred_element_type=F32)


def _dot_nt(a, b):
    return lax.dot_general(a, b, (((1,), (1,)), ((), ())), preferred_element_type=F32)


def _split2(x):
    hi = x.astype(BF16)
    lo = (x - hi.astype(F32)).astype(BF16)
    return hi, lo


def _split3(x):
    hi = x.astype(BF16)
    r1 = x - hi.astype(F32)
    mid = r1.astype(BF16)
    lo = (r1 - mid.astype(F32)).astype(BF16)
    return hi, mid, lo


def _dot_exact_rhs(a, b_bf16):
    h, m, l = _split3(a)
    return _dot(h, b_bf16) + _dot(m, b_bf16) + _dot(l, b_bf16)


def _dot_exact_lhs(a_bf16, b):
    h, m, l = _split3(b)
    return _dot(a_bf16, h) + _dot(a_bf16, m) + _dot(a_bf16, l)


def _dot3(a, b):
    ah, al = _split2(a)
    bh, bl = _split2(b)
    return _dot(ah, bh) + _dot(ah, bl) + _dot(al, bh)


def _rms(x, g):
    ms = jnp.mean(x * x, axis=-1, keepdims=True)
    return x * lax.rsqrt(ms + NORM_EPS) * g


def _norm_matmul_kernel(x_ref, g_ref, w_ref, o_ref, h_ref, *, sigmoid):
    @pl.when(pl.program_id(1) == 0)
    def _():
        h_ref[...] = _rms(x_ref[...], g_ref[...]).astype(BF16)

    acc = _dot(h_ref[...], w_ref[...])
    if sigmoid:
        acc = jax.nn.sigmoid(acc)
    o_ref[...] = acc.astype(o_ref.dtype)


def _norm_matmul(x2, g, w, out_dtype, sigmoid, tm, tn, name):
    m, d = x2.shape
    n = w.shape[1]
    return pl.pallas_call(
        functools.partial(_norm_matmul_kernel, sigmoid=sigmoid),
        grid=(m // tm, n // tn),
        in_specs=[
            pl.BlockSpec((tm, d), lambda i, j: (i, 0)),
            pl.BlockSpec((1, d), lambda i, j: (0, 0)),
            pl.BlockSpec((d, tn), lambda i, j: (0, j)),
        ],
        out_specs=pl.BlockSpec((tm, tn), lambda i, j: (i, j)),
        out_shape=jax.ShapeDtypeStruct((m, n), out_dtype),
        scratch_shapes=[pltpu.VMEM((tm, d), BF16)],
        compiler_params=_cparams(("parallel", "arbitrary")),
        name=name,
    )(x2, g.reshape(1, d), w)


def _proj_attn_kernel(x_ref, g_ref, w_ref, pa_ref, p1_ref, p4_ref, p16_ref, h_ref, acc_ref):
    j = pl.program_id(1)
    tm = x_ref.shape[0]
    w = w_ref.shape[1]

    @pl.when(j == 0)
    def _():
        h_ref[...] = _rms(x_ref[...], g_ref[...]).astype(BF16)

    acc = _dot(h_ref[...], w_ref[...])

    @pl.when(j == 0)
    def _():
        pa_ref[...] = acc.astype(BF16)

    @pl.when(j == 1)
    def _():
        p1_ref[...] = acc.astype(BF16)

    for jj, dil, ref in ((2, C_GROUPS[1][1], p4_ref), (3, C_GROUPS[2][1], p16_ref)):
        @pl.when(j == jj)
        def _(dil=dil, ref=ref):
            for c in range(w // LANES):
                acc_ref[c] = acc[:, c * LANES:(c + 1) * LANES]
            for r in range(dil):
                for c in range(w // LANES):
                    ref[:, r * w + c * LANES:r * w + (c + 1) * LANES] = (
                        acc_ref[c, pl.ds(r, tm // dil, stride=dil), :].astype(BF16))


def _proj_attn(x2, g, w, tm):
    m, d = x2.shape
    wq = w.shape[1] // 4
    d4, d16 = C_GROUPS[1][1], C_GROUPS[2][1]
    return pl.pallas_call(
        _proj_attn_kernel,
        grid=(m // tm, 4),
        in_specs=[
            pl.BlockSpec((tm, d), lambda i, j: (i, 0)),
            pl.BlockSpec((1, d), lambda i, j: (0, 0)),
            pl.BlockSpec((d, wq), lambda i, j: (0, j)),
        ],
        out_specs=[
            pl.BlockSpec((tm, wq), lambda i, j: (i, 0)),
            pl.BlockSpec((tm, wq), lambda i, j: (i, 0)),
            pl.BlockSpec((tm // d4, d4 * wq), lambda i, j: (i, 0)),
            pl.BlockSpec((tm // d16, d16 * wq), lambda i, j: (i, 0)),
        ],
        out_shape=[
            jax.ShapeDtypeStruct((m, wq), BF16),
            jax.ShapeDtypeStruct((m, wq), BF16),
            jax.ShapeDtypeStruct((m // d4, d4 * wq), BF16),
            jax.ShapeDtypeStruct((m // d16, d16 * wq), BF16),
        ],
        scratch_shapes=[pltpu.VMEM((tm, d), BF16), pltpu.VMEM((wq // LANES, tm, LANES), F32)],
        compiler_params=_cparams(("parallel", "arbitrary")),
        name="proj_attn",
    )(x2, g.reshape(1, d), w)


def _band_attn_kernel(*refs, nq, nkv, has_sink, want_lse):
    it = iter(refs)
    q_ref, kp_ref, kc_ref, vp_ref, vc_ref, bias_ref = (next(it) for _ in range(6))
    sink_ref = next(it) if has_sink else None
    o_ref = next(it)
    lse_ref = next(it) if want_lse else None

    first = pl.program_id(2) == 0
    q = q_ref[...] * jnp.asarray(HEAD_DIM ** -0.5, BF16)
    k = jnp.concatenate([kp_ref[...], kc_ref[...]], axis=0)
    v = jnp.concatenate([vp_ref[...], vc_ref[...]], axis=0)
    col = lax.broadcasted_iota(jnp.int32, (BLK, 2 * BLK), 1)
    edge = jnp.where(col < BLK, jnp.where(first, NEG_BIG, 0.0), 0.0)
    rep = nq // nkv
    heads = range(nq)
    hd = lambda t, i: t[:, i * HEAD_DIM:(i + 1) * HEAD_DIM]
    s = [_dot_nt(hd(q, h), hd(k, h // rep)) + bias_ref[h] + edge for h in heads]
    m = [jnp.max(t, axis=-1, keepdims=True) for t in s]
    if has_sink:
        m = [jnp.maximum(m[h], sink_ref[h]) for h in heads]
    p = [jnp.exp(s[h] - m[h]) for h in heads]
    l = [jnp.sum(t, axis=-1, keepdims=True) for t in p]
    denom = [l[h] + jnp.exp(sink_ref[h] - m[h]) for h in heads] if has_sink else l
    o = [_dot(p[h].astype(BF16), hd(v, h // rep)) / denom[h] for h in heads]
    o_ref[...] = jnp.concatenate(o, axis=-1).astype(o_ref.dtype)
    if want_lse:
        lse_ref[...] = jnp.concatenate(
            [jnp.broadcast_to(m[h] + jnp.log(l[h]), (BLK, HEAD_DIM)) for h in heads], axis=-1)


def _band_attention(pv, bias, sink, *, dil, q_blk, k_blk, v_blk, q_w, kv_w, nq, nkv, want_lse, out_dtype, name):
    b, lf, nd = pv.shape
    n = nd // dil
    nb = lf // BLK
    qpr, kpr = n // q_w, n // kv_w
    has_sink = sink is not None

    in_specs = [
        pl.BlockSpec((None, BLK, q_w), lambda bi, r, j: (bi, j, r * qpr + q_blk)),
        pl.BlockSpec((None, BLK, kv_w), lambda bi, r, j: (bi, jnp.maximum(j - 1, 0), r * kpr + k_blk)),
        pl.BlockSpec((None, BLK, kv_w), lambda bi, r, j: (bi, j, r * kpr + k_blk)),
        pl.BlockSpec((None, BLK, kv_w), lambda bi, r, j: (bi, jnp.maximum(j - 1, 0), r * kpr + v_blk)),
        pl.BlockSpec((None, BLK, kv_w), lambda bi, r, j: (bi, j, r * kpr + v_blk)),
        pl.BlockSpec((nq, BLK, 2 * BLK), lambda bi, r, j: (0, 0, 0)),
    ]
    args = [pv, pv, pv, pv, pv, bias]
    if has_sink:
        in_specs.append(pl.BlockSpec(memory_space=pltpu.SMEM))
        args.append(sink)
    ow = nq * HEAD_DIM
    out_spec = pl.BlockSpec((None, BLK, ow), lambda bi, r, j: (bi, j, r))
    out_shape = jax.ShapeDtypeStruct((b, lf, dil * ow), out_dtype)
    if want_lse:
        out_specs = [out_spec, out_spec]
        out_shapes = [out_shape, jax.ShapeDtypeStruct((b, lf, dil * ow), F32)]
    else:
        out_specs, out_shapes = out_spec, out_shape
    return pl.pallas_call(
        functools.partial(_band_attn_kernel, nq=nq, nkv=nkv, has_sink=has_sink, want_lse=want_lse),
        grid=(b, dil, nb),
        in_specs=in_specs,
        out_specs=out_specs,
        out_shape=out_shapes,
        compiler_params=_cparams(("parallel", "parallel", "arbitrary")),
        name=name,
    )(*args)


def _rwkv_prep_kernel(ph_ref, p_ref, mu_ref, w0_ref, wup_ref, a0_ref, aup_ref, gup_ref, kk_ref, ka_ref,
                      rk_ref, e_ref, r_o, lw_o, k_o, v_o, kk_o, b_o, g_o, bonus_o, *, tiles_per_seq):
    tm = p_ref.shape[0]
    p = p_ref[...]
    seq_start = pl.program_id(0) % tiles_per_seq == 0
    last = ph_ref[...][HALO - 1:HALO, :]
    last = jnp.where(seq_start, 0.0, last)
    row = lax.broadcasted_iota(jnp.int32, (tm, 1), 0)
    prev = jnp.where(row == 0, last, pltpu.roll(p, 1, 0))
    pf = p + (prev - p) * mu_ref[...]

    r = pf[:, 0:B_WIDTH]
    k = pf[:, B_WIDTH:2 * B_WIDTH]
    v = pf[:, 2 * B_WIDTH:3 * B_WIDTH]
    wd = pf[:, OFF_WD:OFF_WD + LORA_PAD]
    ad = pf[:, OFF_AD:OFF_AD + LORA_PAD]
    gd = pf[:, OFF_GD:OFF_GD + LORA_GATE]

    z = w0_ref[...] + _dot3(jnp.tanh(wd), wup_ref[...])
    nz = -z
    softplus = jnp.maximum(nz, 0.0) + jnp.log(1.0 + jnp.exp(-jnp.abs(nz)))
    w = -softplus - 0.5
    lw_o[...] = -jnp.exp(w)
    a = jax.nn.sigmoid(a0_ref[...] + _dot3(ad, aup_ref[...]))
    g_o[...] = _dot3(jax.nn.sigmoid(gd), gup_ref[...])

    e = e_ref[...]
    kk = k * kk_ref[...]
    nrm = jnp.sqrt(_dot_exact_rhs(kk * kk, e))
    kk = kk / jnp.maximum(nrm, 1e-12)
    k2 = k * (1.0 + (a - 1.0) * ka_ref[...])
    r_o[...] = r
    k_o[...] = k2
    v_o[...] = v
    kk_o[...] = kk
    b_o[...] = kk * a
    bonus_o[...] = _dot_exact_rhs(r * k2 * rk_ref[...], e) * v


def _rwkv_prep(pb2, seq, mu, w0, wup, a0, aup, gup, k_k, k_a, r_k, e_bd, tm):
    m, n = pb2.shape
    tps = seq // tm
    row = lambda a: a.reshape(1, -1)
    full = lambda a: pl.BlockSpec(a.shape, lambda i: (0,) * a.ndim)
    args = [pb2, pb2, row(mu), row(w0), wup, row(a0), aup, gup, row(k_k), row(k_a), row(r_k), e_bd]
    in_specs = [
        pl.BlockSpec((HALO, n), lambda i: (jnp.maximum(i * (tm // HALO) - 1, 0), 0)),
        pl.BlockSpec((tm, n), lambda i: (i, 0)),
    ] + [full(a) for a in args[2:]]
    o_spec = pl.BlockSpec((tm, B_WIDTH), lambda i: (i, 0))
    o_shape = jax.ShapeDtypeStruct((m, B_WIDTH), F32)
    return pl.pallas_call(
        functools.partial(_rwkv_prep_kernel, tiles_per_seq=tps),
        grid=(m // tm,),
        in_specs=in_specs,
        out_specs=[o_spec] * 8,
        out_shape=[o_shape] * 8,
        compiler_params=_cparams(("parallel",)),
        name="rwkv_prep",
    )(*args)


def _wkv_kernel(r_ref, lw_ref, k_ref, v_ref, kk_ref, b_ref, y_ref, st_ref):
    c = WKV_CHUNK
    n_chunks = r_ref.shape[0] // c

    @pl.when(pl.program_id(1) == 0)
    def _():
        st_ref[...] = jnp.zeros_like(st_ref)

    rowi = lax.broadcasted_iota(jnp.int32, (c, c), 0)
    coli = lax.broadcasted_iota(jnp.int32, (c, c), 1)
    incl = rowi >= coli
    strict = rowi > coli
    tri = jnp.where(incl, 1.0, 0.0).astype(BF16)
    eye = jnp.where(rowi == coli, 1.0, 0.0)

    def chunk(ci, carry):
        sl = pl.ds(pl.multiple_of(ci * c, c), c)
        lw = lw_ref[sl, :]
        cum = _dot_exact_lhs(tri, lw)
        tot = cum[c - 1:c, :]
        p_in = jnp.exp(cum)
        p_ex = jnp.exp(cum - lw)
        p_inv = jnp.exp(-cum)
        p_rest = jnp.exp(tot - cum)
        p_tot = jnp.exp(tot)
        rh_all = (r_ref[sl, :] * p_in).astype(BF16)
        ah_all = (-kk_ref[sl, :] * p_ex).astype(BF16)
        b_all = b_ref[sl, :]
        k_all = k_ref[sl, :]
        bh_all = (b_all * p_inv).astype(BF16)
        kh_all = (k_all * p_inv).astype(BF16)
        bt_all = (b_all * p_rest).astype(BF16)
        kt_all = (k_all * p_rest).astype(BF16)
        v_all = v_ref[sl, :]
        heads = range(B_HEADS)
        hsl = [slice(h * HEAD_DIM, (h + 1) * HEAD_DIM) for h in heads]
        ah = [ah_all[:, s] for s in hsl]
        rh = [rh_all[:, s] for s in hsl]
        vf = [v_all[:, s] for s in hsl]
        vb = [t.astype(BF16) for t in vf]
        ar = [jnp.concatenate([ah[h], rh[h]], axis=0) for h in heads]
        gb = [_dot_nt(ar[h], bh_all[:, hsl[h]]) for h in heads]
        gk = [_dot_nt(ar[h], kh_all[:, hsl[h]]) for h in heads]
        a_rb = [jnp.where(incl, t[c:], 0.0).astype(BF16) for t in gb]
        a_ak = [jnp.where(strict, t[:c], 0.0).astype(BF16) for t in gk]
        a_rk = [jnp.where(incl, t[c:], 0.0).astype(BF16) for t in gk]
        x = [jnp.where(strict, t[:c], 0.0) for t in gb]
        t = [eye + xi for xi in x]
        for _ in range(int(math.log2(c)) - 1):
            xb = [xi.astype(BF16) for xi in x]
            x = [_dot(xi, xi) for xi in xb]
            t = [t[h] + _dot(t[h].astype(BF16), x[h].astype(BF16)) for h in heads]
        tb = [ti.astype(BF16) for ti in t]
        w1 = [_dot(a_ak[h], vb[h]) for h in heads]
        y2 = [_dot(a_rk[h], vb[h]) for h in heads]
        n2 = [_dot(vf[h].T.astype(BF16), kt_all[:, hsl[h]]) for h in heads]
        a2 = [_dot(tb[h], ah[h]).astype(BF16) for h in heads]
        u1 = [_dot(tb[h], w1[h].astype(BF16)) for h in heads]
        st_all = st_ref[...]
        s0b = [st_all[h].astype(BF16) for h in heads]
        utb = [(_dot_nt(s0b[h], a2[h]) + u1[h].T).astype(BF16) for h in heads]
        ys = [_dot_nt(rh[h], s0b[h]) + _dot_nt(a_rb[h], utb[h]) + y2[h] for h in heads]
        sts = [st_all[h] * p_tot[:, hsl[h]] + _dot(utb[h], bt_all[:, hsl[h]]) + n2[h] for h in heads]
        st_ref[...] = jnp.stack(sts, axis=0)
        y_ref[sl, :] = jnp.concatenate(ys, axis=-1)
        return carry

    lax.fori_loop(0, n_chunks, chunk, 0)


def _wkv_scan(r, lw, k, v, kk, b, ts):
    bsz, s, w = r.shape
    spec = pl.BlockSpec((None, ts, w), lambda bi, j: (bi, j, 0))
    return pl.pallas_call(
        _wkv_kernel,
        grid=(bsz, s // ts),
        in_specs=[spec] * 6,
        out_specs=spec,
        out_shape=jax.ShapeDtypeStruct((bsz, s, w), F32),
        scratch_shapes=[pltpu.VMEM((B_HEADS, HEAD_DIM, HEAD_DIM), F32)],
        compiler_params=_cparams(("parallel", "arbitrary")),
        name="wkv_scan",
    )(r, lw, k, v, kk, b)


def _merge_kernel(x_ref, ya_ref, o1_ref, o2_ref, o3_ref, l1_ref, l2_ref, l3_ref, yr_ref, g_ref, bonus_ref,
                  gates_ref, lng_ref, lnb_ref, e_ref, pa_ref, pb_ref, pc_ref, wo_ref, out_ref, unf_ref):
    tm, d = x_ref.shape

    def unfold(ref, slot, dil):
        nc = C_OUT // LANES
        for r in range(dil):
            for c in range(nc):
                unf_ref[slot * nc + c, pl.ds(r, tm // dil, stride=dil), :] = (
                    ref[:, r * C_OUT + c * LANES:r * C_OUT + (c + 1) * LANES])
        return jnp.concatenate([unf_ref[slot * nc + c] for c in range(nc)], axis=-1)

    d2, d3 = C_GROUPS[1][1], C_GROUPS[2][1]
    o1, l1 = o1_ref[...], l1_ref[...]
    o2, l2 = unfold(o2_ref, 0, d2), unfold(l2_ref, 1, d2)
    o3, l3 = unfold(o3_ref, 2, d3), unfold(l3_ref, 3, d3)
    m = jnp.maximum(jnp.maximum(l1, l2), l3)
    e1, e2, e3 = jnp.exp(l1 - m), jnp.exp(l2 - m), jnp.exp(l3 - m)
    yc = (e1 * o1 + e2 * o2 + e3 * o3) / (e1 + e2 + e3)

    e = e_ref[...]
    y = yr_ref[...]
    mean = _dot_exact_rhs(y, e) * (1.0 / HEAD_DIM)
    dv = y - mean
    var = _dot_exact_rhs(dv * dv, e) * (1.0 / HEAD_DIM)
    yb = dv * lax.rsqrt(var + B_GN_EPS) * lng_ref[...] + lnb_ref[...] + bonus_ref[...]
    yb = yb * g_ref[...]

    gates = gates_ref[...]
    merged = (gates[:, 0:d].astype(F32) * _dot(ya_ref[...], pa_ref[...])
              + gates[:, d:2 * d].astype(F32) * _dot(yb.astype(BF16), pb_ref[...])
              + gates[:, 2 * d:3 * d].astype(F32) * _dot(yc.astype(BF16), pc_ref[...]))
    out_ref[...] = x_ref[...] + _dot(merged.astype(BF16), wo_ref[...])


def _merge(x2, ya, o1, o2, o3, l1, l2, l3, yr, g, bonus, gates, lng, lnb, e_bd, pa, pb, pc, wo, tm):
    m, d = x2.shape
    tok = lambda a: pl.BlockSpec((tm * a.shape[0] // m, a.shape[1]), lambda i: (i, 0))
    full = lambda a: pl.BlockSpec(a.shape, lambda i: (0,) * a.ndim, pipeline_mode=pl.Buffered(1))
    toks = [x2, ya, o1, o2, o3, l1, l2, l3, yr, g, bonus, gates]
    consts = [lng.reshape(1, -1), lnb.reshape(1, -1), e_bd, pa, pb, pc, wo]
    return pl.pallas_call(
        _merge_kernel,
        grid=(m // tm,),
        in_specs=[tok(a) for a in toks] + [full(a) for a in consts],
        out_specs=pl.BlockSpec((tm, d), lambda i: (i, 0)),
        out_shape=jax.ShapeDtypeStruct((m, d), F32),
        scratch_shapes=[pltpu.VMEM((4 * C_OUT // LANES, tm, LANES), F32)],
        compiler_params=_cparams(("parallel",)),
        name="merge",
    )(*toks, *consts)


def _ffn_kernel(xh_ref, x_ref, g_ref, wg_ref, wv_ref, cw_ref, wd_ref, fg_ref, o_ref, h_ref, acc_ref, *,
                tiles_per_seq, final_norm):
    j = pl.program_id(1)
    tm = x_ref.shape[0]

    @pl.when(j == 0)
    def _():
        x = x_ref[...]
        g = g_ref[...]
        h_ref[HALO:, :] = _rms(x, g).astype(BF16)
        seq_start = pl.program_id(0) % tiles_per_seq == 0
        halo = _rms(xh_ref[...], g)
        h_ref[0:HALO, :] = jnp.where(seq_start, 0.0, halo).astype(BF16)
        acc_ref[...] = x

    h = h_ref[...]
    gext = _dot(h, wg_ref[...])
    val = _dot(h[HALO:], wv_ref[...])
    cw = cw_ref[...]
    gate = (cw[0:1, :] * pltpu.roll(gext, 2, 0)[HALO:]
            + cw[1:2, :] * pltpu.roll(gext, 1, 0)[HALO:]
            + cw[2:3, :] * gext[HALO:])
    hid = gate * jax.nn.sigmoid(gate) * val
    acc_ref[...] += _dot(hid.astype(BF16), wd_ref[...])

    @pl.when(j == pl.num_programs(1) - 1)
    def _():
        out = acc_ref[...]
        if final_norm:
            out = _rms(out, fg_ref[...])
        o_ref[...] = out


def _ffn(x2, seq, g, wg, wv, cw, wd, fg, final_norm, tm, tf):
    m, d = x2.shape
    ffp = wg.shape[1]
    tps = seq // tm
    return pl.pallas_call(
        functools.partial(_ffn_kernel, tiles_per_seq=tps, final_norm=final_norm),
        grid=(m // tm, ffp // tf),
        in_specs=[
            pl.BlockSpec((HALO, d), lambda i, j: (jnp.maximum(i * (tm // HALO) - 1, 0), 0)),
            pl.BlockSpec((tm, d), lambda i, j: (i, 0)),
            pl.BlockSpec((1, d), lambda i, j: (0, 0)),
            pl.BlockSpec((d, tf), lambda i, j: (0, j)),
            pl.BlockSpec((d, tf), lambda i, j: (0, j)),
            pl.BlockSpec((3, tf), lambda i, j: (0, j)),
            pl.BlockSpec((tf, d), lambda i, j: (j, 0)),
            pl.BlockSpec((1, d), lambda i, j: (0, 0)),
        ],
        out_specs=pl.BlockSpec((tm, d), lambda i, j: (i, 0)),
        out_shape=jax.ShapeDtypeStruct((m, d), F32),
        scratch_shapes=[pltpu.VMEM((HALO + tm, d), BF16), pltpu.VMEM((tm, d), F32)],
        compiler_params=_cparams(("parallel", "arbitrary")),
        name="conv_ffn",
    )(x2, x2, g.reshape(1, d), wg, wv, cw, wd, fg.reshape(1, d))


def _t5_bucket(dist):
    small = dist < MAX_EXACT
    nf = jnp.maximum(dist, 1).astype(F32)
    large = MAX_EXACT + (jnp.log(nf / MAX_EXACT) / math.log(REL_MAX_DIST / MAX_EXACT)
                         * (N_BUCKETS - MAX_EXACT)).astype(jnp.int32)
    return jnp.where(small, dist, jnp.minimum(large, N_BUCKETS - 1))


def _band_bias(table, dilation, max_steps):
    i = jnp.arange(BLK)[:, None]
    j = jnp.arange(2 * BLK)[None, :]
    off = i + BLK - j
    bias = jnp.transpose(table[_t5_bucket(jnp.maximum(off, 0) * dilation)], (2, 0, 1)).astype(F32)
    valid = (off >= 0) & (off <= max_steps)
    return jnp.where(valid[None], bias, NEG_BIG)


def _pad_cols(w, n):
    return jnp.pad(w, [(0, 0)] * (w.ndim - 1) + [(0, n - w.shape[-1])])


def _pad_rows(w, n):
    return jnp.pad(w, [(0, 0)] * (w.ndim - 2) + [(0, n - w.shape[-2]), (0, 0)])


def kernel(x, rel_bias, norm1_g, w_in, attn_sinks, rwkv_mu, rwkv_w0, rwkv_w_up, rwkv_a0, rwkv_a_up, rwkv_g_up,
           rwkv_k_k, rwkv_k_a, rwkv_r_k, rwkv_lnx_g, rwkv_lnx_b, proj_a, proj_b, proj_c, w_out, norm2_g,
           ffn_up, ffn_conv, ffn_down, final_g):
    bsz, seq, d = x.shape
    depth = w_in.shape[0]
    m = bsz * seq
    d_ff = ffn_conv.shape[-1]
    ffp = -(-d_ff // D_FF_PAD_TO) * D_FF_PAD_TO

    tm_proj = min(1024, seq)
    tm_tok = min(256, seq)
    tm_ffn = min(512, seq)
    ts_wkv = min(512, seq)

    na = A_HQ
    bias_a = _band_bias(rel_bias[:, :na], 1, BLK - 1)
    bias_c = [_band_bias(rel_bias[:, na + gi * C_HG:na + (gi + 1) * C_HG], dil, win // dil)
              for gi, (win, dil) in enumerate(C_GROUPS)]

    q_end = (A_HQ + 2 * A_HKV) * HEAD_DIM
    b0 = q_end
    b_r_end = b0 + 3 * B_WIDTH
    b_wd_end = b_r_end + LORA_DECAY
    b_ad_end = b_wd_end + LORA_ICLR
    b_end = b_ad_end + LORA_GATE
    c_end = b_end + 3 * C_WIDTH

    def rwkv_cols(t):
        parts = [t[..., b0:b_r_end], _pad_cols(t[..., b_r_end:b_wd_end], LORA_PAD),
                 _pad_cols(t[..., b_wd_end:b_ad_end], LORA_PAD), t[..., b_ad_end:b_end]]
        return _pad_cols(jnp.concatenate(parts, axis=-1), N_RWKV)

    qkv_c = [w_in[..., b_end + t * C_WIDTH + gi * C_OUT:b_end + t * C_WIDTH + (gi + 1) * C_OUT]
             for gi in range(len(C_GROUPS)) for t in range(3)]
    w_attn = jnp.concatenate([w_in[..., :q_end]] + qkv_c, axis=-1).astype(BF16)
    w_rwkv = rwkv_cols(w_in).astype(BF16)
    w_gate = w_in[..., c_end:].astype(BF16)
    mu_pad = rwkv_cols(jnp.pad(rwkv_mu, ((0, 0), (b0, 0))))
    wup_pad = _pad_rows(rwkv_w_up, LORA_PAD)
    aup_pad = _pad_rows(rwkv_a_up, LORA_PAD)
    r_k = rwkv_r_k.reshape(depth, B_WIDTH)

    head_id = jnp.arange(B_WIDTH) // HEAD_DIM
    e_bd = (head_id[:, None] == head_id[None, :]).astype(BF16)

    pa, pb, pc, wo = (t.astype(BF16) for t in (proj_a, proj_b, proj_c, w_out))
    wg = _pad_cols(ffn_up[..., :d_ff], ffp).astype(BF16)
    wv = _pad_cols(ffn_up[..., d_ff:], ffp).astype(BF16)
    cw = _pad_cols(ffn_conv, ffp)
    wd = _pad_rows(ffn_down, ffp).astype(BF16)

    x2 = x.reshape(m, d)
    for l in range(depth):
        p_swa, *p_dil = _proj_attn(x2, norm1_g[l], w_attn[l], tm_proj)
        p_rwkv = _norm_matmul(x2, norm1_g[l], w_rwkv[l], F32, False, tm_proj, 512, "proj_rwkv")
        gates = _norm_matmul(x2, norm1_g[l], w_gate[l], BF16, True, tm_proj, 512, "proj_gates")

        kvw = A_HKV * HEAD_DIM
        y_a = _band_attention(p_swa.reshape(bsz, seq, -1), bias_a, attn_sinks[l], dil=1, q_blk=0,
                              k_blk=(A_HQ * HEAD_DIM) // kvw, v_blk=(A_HQ * HEAD_DIM) // kvw + 1,
                              q_w=A_HQ * HEAD_DIM, kv_w=kvw, nq=A_HQ, nkv=A_HKV, want_lse=False,
                              out_dtype=BF16, name="attn_swa")
        oc, lc = [], []
        for gi, (win, dil) in enumerate(C_GROUPS):
            pv = p_dil[gi].reshape(bsz, seq // dil, -1)
            o, ls = _band_attention(pv, bias_c[gi], None, dil=dil, q_blk=0, k_blk=1, v_blk=2, q_w=C_OUT,
                                    kv_w=C_OUT, nq=C_HG, nkv=C_HG, want_lse=True, out_dtype=F32,
                                    name=f"attn_dil{dil}")
            oc.append(o.reshape(m // dil, dil * C_OUT))
            lc.append(ls.reshape(m // dil, dil * C_OUT))

        r, lw, k2, v, kk, bb, g, bonus = _rwkv_prep(
            p_rwkv, seq, mu_pad[l], rwkv_w0[l], wup_pad[l], rwkv_a0[l], aup_pad[l], rwkv_g_up[l],
            rwkv_k_k[l], rwkv_k_a[l], r_k[l], e_bd, tm_tok)
        sh = lambda t: t.reshape(bsz, seq, B_WIDTH)
        y_raw = _wkv_scan(sh(r), sh(lw), sh(k2), sh(v), sh(kk), sh(bb), ts_wkv).reshape(m, B_WIDTH)

        x2 = _merge(x2, y_a.reshape(m, A_HQ * HEAD_DIM), oc[0], oc[1], oc[2], lc[0], lc[1], lc[2], y_raw, g,
                    bonus, gates, rwkv_lnx_g[l], rwkv_lnx_b[l], e_bd, pa[l], pb[l], pc[l], wo[l], tm_tok)
        x2 = _ffn(x2, seq, norm2_g[l], wg[l], wv[l], cw[l], wd[l], final_g, l == depth - 1, tm_ffn, 512)
    return x2.reshape(bsz, seq, d)
```

```python
import functools
import math

import jax
import jax.numpy as jnp
import numpy as np
from jax import lax
from jax.experimental import pallas as pl
from jax.experimental.pallas import tpu as pltpu

F32 = jnp.float32
BF16 = jnp.bfloat16

HEAD_DIM = 64
LANES = 128
BLK = 128
NORM_EPS = 1e-5
A_HQ, A_HKV = 8, 2
B_HEADS = 12
B_WIDTH = B_HEADS * HEAD_DIM
LORA_DECAY, LORA_ICLR, LORA_GATE = 96, 96, 256
LORA_PAD = 128
B_GN_EPS = 64e-5
C_GROUPS = ((128, 1), (512, 4), (2048, 16))
C_HG = 4
C_WIDTH = C_HG * len(C_GROUPS) * HEAD_DIM
C_OUT = C_HG * HEAD_DIM
N_BUCKETS, MAX_EXACT, REL_MAX_DIST = 32, 16, 2048
D_FF_PAD_TO = 512
NEG_BIG = -1e30

N_ATTN = A_HQ * HEAD_DIM + 2 * A_HKV * HEAD_DIM + 3 * C_WIDTH
N_RWKV = 3 * B_WIDTH + 2 * LORA_PAD + LORA_GATE + 256
OFF_WD = 3 * B_WIDTH
OFF_AD = OFF_WD + LORA_PAD
OFF_GD = OFF_AD + LORA_PAD
WKV_CHUNK = 64
HALO = 16

VMEM_LIMIT = 56 * 1024 * 1024


def _cparams(sem):
    return pltpu.CompilerParams(dimension_semantics=sem, vmem_limit_bytes=VMEM_LIMIT)


def _dot(a, b):
    return jnp.dot(a, b, preferred_element_type=F32)


def _dot_nt(a, b):
    return lax.dot_general(a, b, (((1,), (1,)), ((), ())), preferred_element_type=F32)


def _split2(x):
    hi = x.astype(BF16)
    lo = (x - hi.astype(F32)).astype(BF16)
    return hi, lo


def _dot_exact_rhs(a, b_bf16):
    h, l = _split2(a)
    return _dot(h, b_bf16) + _dot(l, b_bf16)


def _dot_exact_lhs(a_bf16, b):
    h, l = _split2(b)
    return _dot(a_bf16, h) + _dot(a_bf16, l)


def _dot3(a, b):
    ah, al = _split2(a)
    bh, bl = _split2(b)
    return _dot(ah, bh) + _dot(ah, bl) + _dot(al, bh)


def _rms(x, g):
    ms = jnp.mean(x * x, axis=-1, keepdims=True)
    return x * lax.rsqrt(ms + NORM_EPS) * g


def _norm_matmul_kernel(x_ref, g_ref, w_ref, o_ref, h_ref, *, sigmoid):
    @pl.when(pl.program_id(1) == 0)
    def _():
        h_ref[...] = _rms(x_ref[...], g_ref[...]).astype(BF16)

    acc = _dot(h_ref[...], w_ref[...])
    if sigmoid:
        acc = jax.nn.sigmoid(acc)
    o_ref[...] = acc.astype(o_ref.dtype)


def _norm_matmul(x2, g, w, out_dtype, sigmoid, tm, tn, name):
    m, d = x2.shape
    n = w.shape[1]
    return pl.pallas_call(
        functools.partial(_norm_matmul_kernel, sigmoid=sigmoid),
        grid=(m // tm, n // tn),
        in_specs=[
            pl.BlockSpec((tm, d), lambda i, j: (i, 0)),
            pl.BlockSpec((1, d), lambda i, j: (0, 0)),
            pl.BlockSpec((d, tn), lambda i, j: (0, j)),
        ],
        out_specs=pl.BlockSpec((tm, tn), lambda i, j: (i, j)),
        out_shape=jax.ShapeDtypeStruct((m, n), out_dtype),
        scratch_shapes=[pltpu.VMEM((tm, d), BF16)],
        compiler_params=_cparams(("parallel", "arbitrary")),
        name=name,
    )(x2, g.reshape(1, d), w)


def _proj_attn_kernel(x_ref, g_ref, w_ref, pa_ref, p1_ref, p4_ref, p16_ref, h_ref, acc_ref):
    j = pl.program_id(1)
    tm = x_ref.shape[0]
    w = w_ref.shape[1]

    @pl.when(j == 0)
    def _():
        h_ref[...] = _rms(x_ref[...], g_ref[...]).astype(BF16)

    acc = _dot(h_ref[...], w_ref[...])

    @pl.when(j == 0)
    def _():
        pa_ref[...] = acc.astype(BF16)

    @pl.when(j == 1)
    def _():
        p1_ref[...] = acc.astype(BF16)

    for jj, dil, ref in ((2, C_GROUPS[1][1], p4_ref), (3, C_GROUPS[2][1], p16_ref)):
        @pl.when(j == jj)
        def _(dil=dil, ref=ref):
            for c in range(w // LANES):
                acc_ref[c] = acc[:, c * LANES:(c + 1) * LANES]
            for r in range(dil):
                for c in range(w // LANES):
                    ref[:, r * w + c * LANES:r * w + (c + 1) * LANES] = (
                        acc_ref[c, pl.ds(r, tm // dil, stride=dil), :].astype(BF16))


def _proj_attn(x2, g, w, tm):
    m, d = x2.shape
    wq = w.shape[1] // 4
    d4, d16 = C_GROUPS[1][1], C_GROUPS[2][1]
    return pl.pallas_call(
        _proj_attn_kernel,
        grid=(m // tm, 4),
        in_specs=[
            pl.BlockSpec((tm, d), lambda i, j: (i, 0)),
            pl.BlockSpec((1, d), lambda i, j: (0, 0)),
            pl.BlockSpec((d, wq), lambda i, j: (0, j)),
        ],
        out_specs=[
            pl.BlockSpec((tm, wq), lambda i, j: (i, 0)),
            pl.BlockSpec((tm, wq), lambda i, j: (i, 0)),
            pl.BlockSpec((tm // d4, d4 * wq), lambda i, j: (i, 0)),
            pl.BlockSpec((tm // d16, d16 * wq), lambda i, j: (i, 0)),
        ],
        out_shape=[
            jax.ShapeDtypeStruct((m, wq), BF16),
            jax.ShapeDtypeStruct((m, wq), BF16),
            jax.ShapeDtypeStruct((m // d4, d4 * wq), BF16),
            jax.ShapeDtypeStruct((m // d16, d16 * wq), BF16),
        ],
        scratch_shapes=[pltpu.VMEM((tm, d), BF16), pltpu.VMEM((wq // LANES, tm, LANES), F32)],
        compiler_params=_cparams(("parallel", "arbitrary")),
        name="proj_attn",
    )(x2, g.reshape(1, d), w)


def _band_attn_kernel(*refs, nq, nkv, has_sink, want_lse):
    it = iter(refs)
    q_ref, kp_ref, kc_ref, vp_ref, vc_ref, bias_ref = (next(it) for _ in range(6))
    sink_ref = next(it) if has_sink else None
    o_ref = next(it)
    lse_ref = next(it) if want_lse else None

    nqb = q_ref.shape[0] // BLK
    first = pl.program_id(2) == 0
    q = q_ref[...] * jnp.asarray(HEAD_DIM ** -0.5, BF16)
    k = jnp.concatenate([kp_ref[...], kc_ref[...]], axis=0)
    v = jnp.concatenate([vp_ref[...], vc_ref[...]], axis=0)
    col = lax.broadcasted_iota(jnp.int32, (BLK, 2 * BLK), 1)
    edge = jnp.where(col < BLK, jnp.where(first, NEG_BIG, 0.0), 0.0)
    rep = nq // nkv
    units = [(i, h) for i in range(nqb) for h in range(nq)]
    hd = lambda t, i: t[:, i * HEAD_DIM:(i + 1) * HEAD_DIM]
    qs = lambda i, h: hd(q[i * BLK:(i + 1) * BLK], h)
    win = lambda t, i, h: hd(t[i * BLK:(i + 2) * BLK], h // rep)
    s = [_dot_nt(qs(i, h), win(k, i, h)) + bias_ref[h] + (edge if i == 0 else 0.0) for i, h in units]
    m = [jnp.max(t, axis=-1, keepdims=True) for t in s]
    if has_sink:
        m = [jnp.maximum(m[u], sink_ref[h]) for u, (i, h) in enumerate(units)]
    p = [jnp.exp(s[u] - m[u]) for u in range(len(units))]
    l = [jnp.sum(t, axis=-1, keepdims=True) for t in p]
    denom = [l[u] + jnp.exp(sink_ref[h] - m[u]) for u, (i, h) in enumerate(units)] if has_sink else l
    o = [_dot(p[u].astype(BF16), win(v, i, h)) / denom[u] for u, (i, h) in enumerate(units)]
    rows = lambda parts: jnp.concatenate(
        [jnp.concatenate(parts[i * nq:(i + 1) * nq], axis=-1) for i in range(nqb)], axis=0)
    o_ref[...] = rows(o).astype(o_ref.dtype)
    if want_lse:
        lse_ref[...] = rows([jnp.broadcast_to(m[u] + jnp.log(l[u]), (BLK, HEAD_DIM)) for u in range(len(units))])


def _band_attention(pv, bias, sink, *, dil, q_blk, k_blk, v_blk, q_w, kv_w, nq, nkv, want_lse, out_dtype, name,
                    nqb):
    b, lf, nd = pv.shape
    n = nd // dil
    nb = lf // (BLK * nqb)
    qpr, kpr = n // q_w, n // kv_w
    has_sink = sink is not None
    prev = lambda j: jnp.maximum(j * nqb - 1, 0)

    in_specs = [
        pl.BlockSpec((None, nqb * BLK, q_w), lambda bi, r, j: (bi, j, r * qpr + q_blk)),
        pl.BlockSpec((None, BLK, kv_w), lambda bi, r, j: (bi, prev(j), r * kpr + k_blk)),
        pl.BlockSpec((None, nqb * BLK, kv_w), lambda bi, r, j: (bi, j, r * kpr + k_blk)),
        pl.BlockSpec((None, BLK, kv_w), lambda bi, r, j: (bi, prev(j), r * kpr + v_blk)),
        pl.BlockSpec((None, nqb * BLK, kv_w), lambda bi, r, j: (bi, j, r * kpr + v_blk)),
        pl.BlockSpec((nq, BLK, 2 * BLK), lambda bi, r, j: (0, 0, 0)),
    ]
    args = [pv, pv, pv, pv, pv, bias]
    if has_sink:
        in_specs.append(pl.BlockSpec(memory_space=pltpu.SMEM))
        args.append(sink)
    ow = nq * HEAD_DIM
    out_spec = pl.BlockSpec((None, nqb * BLK, ow), lambda bi, r, j: (bi, j, r))
    out_shape = jax.ShapeDtypeStruct((b, lf, dil * ow), out_dtype)
    if want_lse:
        out_specs = [out_spec, out_spec]
        out_shapes = [out_shape, jax.ShapeDtypeStruct((b, lf, dil * ow), F32)]
    else:
        out_specs, out_shapes = out_spec, out_shape
    return pl.pallas_call(
        functools.partial(_band_attn_kernel, nq=nq, nkv=nkv, has_sink=has_sink, want_lse=want_lse),
        grid=(b, dil, nb),
        in_specs=in_specs,
        out_specs=out_specs,
        out_shape=out_shapes,
        compiler_params=_cparams(("parallel", "parallel", "arbitrary")),
        name=name,
    )(*args)


def _rwkv_prep_kernel(ph_ref, p_ref, mu_ref, w0_ref, wup_ref, a0_ref, aup_ref, gup_ref, kk_ref, ka_ref,
                      rk_ref, e_ref, r_o, lw_o, k_o, v_o, kk_o, b_o, g_o, bonus_o, *, tiles_per_seq):
    tm = p_ref.shape[0]
    p = p_ref[...]
    seq_start = pl.program_id(0) % tiles_per_seq == 0
    last = ph_ref[...][HALO - 1:HALO, :]
    last = jnp.where(seq_start, 0.0, last)
    row = lax.broadcasted_iota(jnp.int32, (tm, 1), 0)
    prev = jnp.where(row == 0, last, pltpu.roll(p, 1, 0))
    pf = p + (prev - p) * mu_ref[...]

    r = pf[:, 0:B_WIDTH]
    k = pf[:, B_WIDTH:2 * B_WIDTH]
    v = pf[:, 2 * B_WIDTH:3 * B_WIDTH]
    wd = pf[:, OFF_WD:OFF_WD + LORA_PAD]
    ad = pf[:, OFF_AD:OFF_AD + LORA_PAD]
    gd = pf[:, OFF_GD:OFF_GD + LORA_GATE]

    z = w0_ref[...] + _dot3(jnp.tanh(wd), wup_ref[...])
    nz = -z
    softplus = jnp.maximum(nz, 0.0) + jnp.log(1.0 + jnp.exp(-jnp.abs(nz)))
    w = -softplus - 0.5
    lw_o[...] = -jnp.exp(w)
    a = jax.nn.sigmoid(a0_ref[...] + _dot3(ad, aup_ref[...]))
    g_o[...] = _dot3(jax.nn.sigmoid(gd), gup_ref[...])

    e = e_ref[...]
    kk = k * kk_ref[...]
    nrm = jnp.sqrt(_dot_exact_rhs(kk * kk, e))
    kk = kk / jnp.maximum(nrm, 1e-12)
    k2 = k * (1.0 + (a - 1.0) * ka_ref[...])
    r_o[...] = r
    k_o[...] = k2
    v_o[...] = v
    kk_o[...] = kk
    b_o[...] = kk * a
    bonus_o[...] = _dot_exact_rhs(r * k2 * rk_ref[...], e) * v


def _rwkv_prep(pb2, seq, mu, w0, wup, a0, aup, gup, k_k, k_a, r_k, e_bd, tm):
    m, n = pb2.shape
    tps = seq // tm
    row = lambda a: a.reshape(1, -1)
    full = lambda a: pl.BlockSpec(a.shape, lambda i: (0,) * a.ndim)
    args = [pb2, pb2, row(mu), row(w0), wup, row(a0), aup, gup, row(k_k), row(k_a), row(r_k), e_bd]
    in_specs = [
        pl.BlockSpec((HALO, n), lambda i: (jnp.maximum(i * (tm // HALO) - 1, 0), 0)),
        pl.BlockSpec((tm, n), lambda i: (i, 0)),
    ] + [full(a) for a in args[2:]]
    o_spec = pl.BlockSpec((tm, B_WIDTH), lambda i: (i, 0))
    o_shape = jax.ShapeDtypeStruct((m, B_WIDTH), F32)
    return pl.pallas_call(
        functools.partial(_rwkv_prep_kernel, tiles_per_seq=tps),
        grid=(m // tm,),
        in_specs=in_specs,
        out_specs=[o_spec] * 8,
        out_shape=[o_shape] * 8,
        compiler_params=_cparams(("parallel",)),
        name="rwkv_prep",
    )(*args)


def _wkv_kernel(r_ref, lw_ref, k_ref, v_ref, kk_ref, b_ref, y_ref, st_ref):
    c = WKV_CHUNK
    n_chunks = r_ref.shape[0] // c

    @pl.when(pl.program_id(1) == 0)
    def _():
        st_ref[...] = jnp.zeros_like(st_ref)

    rowi = lax.broadcasted_iota(jnp.int32, (c, c), 0)
    coli = lax.broadcasted_iota(jnp.int32, (c, c), 1)
    incl = rowi >= coli
    strict = rowi > coli
    row2 = lax.broadcasted_iota(jnp.int32, (2 * c, c), 0)
    col2 = lax.broadcasted_iota(jnp.int32, (2 * c, c), 1)
    tri2 = row2 - jnp.where(row2 < c, 1, c) >= col2
    tri = jnp.where(incl, 1.0, 0.0).astype(BF16)
    eye = jnp.where(rowi == coli, 1.0, 0.0)

    def chunk(ci, carry):
        sl = pl.ds(pl.multiple_of(ci * c, c), c)
        lw = lw_ref[sl, :]
        cum = _dot_exact_lhs(tri, lw)
        tot = cum[c - 1:c, :]
        p_in = jnp.exp(cum)
        p_ex = jnp.exp(cum - lw)
        p_inv = jnp.exp(-cum)
        p_rest = jnp.exp(tot - cum)
        p_tot = jnp.exp(tot)
        rh_all = (r_ref[sl, :] * p_in).astype(BF16)
        ah_all = (-kk_ref[sl, :] * p_ex).astype(BF16)
        b_all = b_ref[sl, :]
        k_all = k_ref[sl, :]
        bh_all = (b_all * p_inv).astype(BF16)
        kh_all = (k_all * p_inv).astype(BF16)
        bt_all = (b_all * p_rest).astype(BF16)
        kt_all = (k_all * p_rest).astype(BF16)
        v_all = v_ref[sl, :]
        heads = range(B_HEADS)
        hsl = [slice(h * HEAD_DIM, (h + 1) * HEAD_DIM) for h in heads]
        ah = [ah_all[:, s] for s in hsl]
        rh = [rh_all[:, s] for s in hsl]
        vf = [v_all[:, s] for s in hsl]
        vb = [t.astype(BF16) for t in vf]
        ar = [jnp.concatenate([ah[h], rh[h]], axis=0) for h in heads]
        gb = [_dot_nt(ar[h], bh_all[:, hsl[h]]) for h in heads]
        gk = [_dot_nt(ar[h], kh_all[:, hsl[h]]) for h in heads]
        a_rb = [jnp.where(incl, t[c:], 0.0).astype(BF16) for t in gb]
        akrk = [jnp.where(tri2, t, 0.0).astype(BF16) for t in gk]
        x0 = [jnp.where(strict, t[:c], 0.0) for t in gb]
        t = [eye + xi for xi in x0]
        xb = [xi.astype(BF16) for xi in x0]
        x = [_dot(xi, xi) for xi in xb]
        for _ in range(int(math.log2(c)) - 2):
            xt = [_dot(jnp.concatenate([x[h], t[h]], axis=0).astype(BF16), x[h].astype(BF16)) for h in heads]
            x = [p[:c] for p in xt]
            t = [t[h] + xt[h][c:] for h in heads]
        t = [t[h] + _dot(t[h].astype(BF16), x[h].astype(BF16)) for h in heads]
        tb = [ti.astype(BF16) for ti in t]
        w1y2 = [_dot(akrk[h], vb[h]) for h in heads]
        w1 = [p[:c] for p in w1y2]
        y2 = [p[c:] for p in w1y2]
        n2 = [_dot(vf[h].T.astype(BF16), kt_all[:, hsl[h]]) for h in heads]
        a2 = [_dot(tb[h], ah[h]).astype(BF16) for h in heads]
        u1 = [_dot(tb[h], w1[h].astype(BF16)) for h in heads]
        st_all = st_ref[...]
        s0b = [st_all[h].astype(BF16) for h in heads]
        utb = [(_dot_nt(s0b[h], a2[h]) + u1[h].T).astype(BF16) for h in heads]
        ys = [_dot_nt(rh[h], s0b[h]) + _dot_nt(a_rb[h], utb[h]) + y2[h] for h in heads]
        sts = [st_all[h] * p_tot[:, hsl[h]] + _dot(utb[h], bt_all[:, hsl[h]]) + n2[h] for h in heads]
        st_ref[...] = jnp.stack(sts, axis=0)
        y_ref[sl, :] = jnp.concatenate(ys, axis=-1)
        return carry

    lax.fori_loop(0, n_chunks, chunk, 0)


def _wkv_scan(r, lw, k, v, kk, b, ts):
    bsz, s, w = r.shape
    spec = pl.BlockSpec((None, ts, w), lambda bi, j: (bi, j, 0))
    return pl.pallas_call(
        _wkv_kernel,
        grid=(bsz, s // ts),
        in_specs=[spec] * 6,
        out_specs=spec,
        out_shape=jax.ShapeDtypeStruct((bsz, s, w), F32),
        scratch_shapes=[pltpu.VMEM((B_HEADS, HEAD_DIM, HEAD_DIM), F32)],
        compiler_params=_cparams(("parallel", "arbitrary")),
        name="wkv_scan",
    )(r, lw, k, v, kk, b)


def _merge_kernel(x_ref, ya_ref, o1_ref, o2_ref, o3_ref, l1_ref, l2_ref, l3_ref, yr_ref, g_ref, bonus_ref,
                  gates_ref, lng_ref, lnb_ref, e_ref, pa_ref, pb_ref, pc_ref, wo_ref, out_ref, unf_ref):
    tm, d = x_ref.shape

    def unfold(ref, slot, dil):
        nc = C_OUT // LANES
        for r in range(dil):
            for c in range(nc):
                unf_ref[slot * nc + c, pl.ds(r, tm // dil, stride=dil), :] = (
                    ref[:, r * C_OUT + c * LANES:r * C_OUT + (c + 1) * LANES])
        return jnp.concatenate([unf_ref[slot * nc + c] for c in range(nc)], axis=-1)

    d2, d3 = C_GROUPS[1][1], C_GROUPS[2][1]
    o1, l1 = o1_ref[...], l1_ref[...]
    o2, l2 = unfold(o2_ref, 0, d2), unfold(l2_ref, 1, d2)
    o3, l3 = unfold(o3_ref, 2, d3), unfold(l3_ref, 3, d3)
    m = jnp.maximum(jnp.maximum(l1, l2), l3)
    e1, e2, e3 = jnp.exp(l1 - m), jnp.exp(l2 - m), jnp.exp(l3 - m)
    yc = (e1 * o1 + e2 * o2 + e3 * o3) / (e1 + e2 + e3)

    e = e_ref[...]
    y = yr_ref[...]
    mean = _dot_exact_rhs(y, e) * (1.0 / HEAD_DIM)
    dv = y - mean
    var = _dot_exact_rhs(dv * dv, e) * (1.0 / HEAD_DIM)
    yb = dv * lax.rsqrt(var + B_GN_EPS) * lng_ref[...] + lnb_ref[...] + bonus_ref[...]
    yb = yb * g_ref[...]

    gates = gates_ref[...]
    merged = (gates[:, 0:d].astype(F32) * _dot(ya_ref[...], pa_ref[...])
              + gates[:, d:2 * d].astype(F32) * _dot(yb.astype(BF16), pb_ref[...])
              + gates[:, 2 * d:3 * d].astype(F32) * _dot(yc.astype(BF16), pc_ref[...]))
    out_ref[...] = x_ref[...] + _dot(merged.astype(BF16), wo_ref[...])


def _merge(x2, ya, o1, o2, o3, l1, l2, l3, yr, g, bonus, gates, lng, lnb, e_bd, pa, pb, pc, wo, tm):
    m, d = x2.shape
    tok = lambda a: pl.BlockSpec((tm * a.shape[0] // m, a.shape[1]), lambda i: (i, 0))
    full = lambda a: pl.BlockSpec(a.shape, lambda i: (0,) * a.ndim, pipeline_mode=pl.Buffered(1))
    toks = [x2, ya, o1, o2, o3, l1, l2, l3, yr, g, bonus, gates]
    consts = [lng.reshape(1, -1), lnb.reshape(1, -1), e_bd, pa, pb, pc, wo]
    return pl.pallas_call(
        _merge_kernel,
        grid=(m // tm,),
        in_specs=[tok(a) for a in toks] + [full(a) for a in consts],
        out_specs=pl.BlockSpec((tm, d), lambda i: (i, 0)),
        out_shape=jax.ShapeDtypeStruct((m, d), F32),
        scratch_shapes=[pltpu.VMEM((4 * C_OUT // LANES, tm, LANES), F32)],
        compiler_params=_cparams(("parallel",)),
        name="merge",
    )(*toks, *consts)


def _ffn_kernel(xh_ref, x_ref, g_ref, wg_ref, wv_ref, cw_ref, wd_ref, fg_ref, o_ref, h_ref, acc_ref, *,
                tiles_per_seq, final_norm):
    j = pl.program_id(1)
    tm = x_ref.shape[0]

    @pl.when(j == 0)
    def _():
        x = x_ref[...]
        g = g_ref[...]
        h_ref[HALO:, :] = _rms(x, g).astype(BF16)
        seq_start = pl.program_id(0) % tiles_per_seq == 0
        halo = _rms(xh_ref[...], g)
        h_ref[0:HALO, :] = jnp.where(seq_start, 0.0, halo).astype(BF16)
        acc_ref[...] = x

    h = h_ref[...]
    gext = _dot(h, wg_ref[...])
    val = _dot(h[HALO:], wv_ref[...])
    cw = cw_ref[...]
    gate = (cw[0:1, :] * pltpu.roll(gext, 2, 0)[HALO:]
            + cw[1:2, :] * pltpu.roll(gext, 1, 0)[HALO:]
            + cw[2:3, :] * gext[HALO:])
    hid = gate * jax.nn.sigmoid(gate) * val
    acc_ref[...] += _dot(hid.astype(BF16), wd_ref[...])

    @pl.when(j == pl.num_programs(1) - 1)
    def _():
        out = acc_ref[...]
        if final_norm:
            out = _rms(out, fg_ref[...])
        o_ref[...] = out


def _ffn(x2, seq, g, wg, wv, cw, wd, fg, final_norm, tm, tf):
    m, d = x2.shape
    ffp = wg.shape[1]
    tps = seq // tm
    return pl.pallas_call(
        functools.partial(_ffn_kernel, tiles_per_seq=tps, final_norm=final_norm),
        grid=(m // tm, ffp // tf),
        in_specs=[
            pl.BlockSpec((HALO, d), lambda i, j: (jnp.maximum(i * (tm // HALO) - 1, 0), 0)),
            pl.BlockSpec((tm, d), lambda i, j: (i, 0)),
            pl.BlockSpec((1, d), lambda i, j: (0, 0)),
            pl.BlockSpec((d, tf), lambda i, j: (0, j)),
            pl.BlockSpec((d, tf), lambda i, j: (0, j)),
            pl.BlockSpec((3, tf), lambda i, j: (0, j)),
            pl.BlockSpec((tf, d), lambda i, j: (j, 0)),
            pl.BlockSpec((1, d), lambda i, j: (0, 0)),
        ],
        out_specs=pl.BlockSpec((tm, d), lambda i, j: (i, 0)),
        out_shape=jax.ShapeDtypeStruct((m, d), F32),
        scratch_shapes=[pltpu.VMEM((HALO + tm, d), BF16), pltpu.VMEM((tm, d), F32)],
        compiler_params=_cparams(("parallel", "arbitrary")),
        name="conv_ffn",
    )(x2, x2, g.reshape(1, d), wg, wv, cw, wd, fg.reshape(1, d))


def _t5_bucket(dist):
    small = dist < MAX_EXACT
    nf = jnp.maximum(dist, 1).astype(F32)
    large = MAX_EXACT + (jnp.log(nf / MAX_EXACT) / math.log(REL_MAX_DIST / MAX_EXACT)
                         * (N_BUCKETS - MAX_EXACT)).astype(jnp.int32)
    return jnp.where(small, dist, jnp.minimum(large, N_BUCKETS - 1))


def _band_bias(table, dilation, max_steps):
    i = jnp.arange(BLK)[:, None]
    j = jnp.arange(2 * BLK)[None, :]
    off = i + BLK - j
    onehot = jax.nn.one_hot(_t5_bucket(jnp.maximum(off, 0) * dilation), N_BUCKETS, dtype=F32)
    bias = jnp.einsum("ijb,bh->hij", onehot, table.astype(F32), precision=lax.Precision.HIGHEST)
    valid = (off >= 0) & (off <= max_steps)
    return jnp.where(valid[None], bias, NEG_BIG)


def _pad_cols(w, n):
    return jnp.pad(w, [(0, 0)] * (w.ndim - 1) + [(0, n - w.shape[-1])])


def _pad_rows(w, n):
    return jnp.pad(w, [(0, 0)] * (w.ndim - 2) + [(0, n - w.shape[-2]), (0, 0)])


def kernel(x, rel_bias, norm1_g, w_in, attn_sinks, rwkv_mu, rwkv_w0, rwkv_w_up, rwkv_a0, rwkv_a_up, rwkv_g_up,
           rwkv_k_k, rwkv_k_a, rwkv_r_k, rwkv_lnx_g, rwkv_lnx_b, proj_a, proj_b, proj_c, w_out, norm2_g,
           ffn_up, ffn_conv, ffn_down, final_g):
    bsz, seq, d = x.shape
    depth = w_in.shape[0]
    m = bsz * seq
    d_ff = ffn_conv.shape[-1]
    ffp = -(-d_ff // D_FF_PAD_TO) * D_FF_PAD_TO

    tm_proj = min(1024, seq)
    tm_tok = min(256, seq)
    tm_ffn = min(512, seq)
    ts_wkv = min(512, seq)

    na = A_HQ
    bias_a = _band_bias(rel_bias[:, :na], 1, BLK - 1)
    bias_c = [_band_bias(rel_bias[:, na + gi * C_HG:na + (gi + 1) * C_HG], dil, win // dil)
              for gi, (win, dil) in enumerate(C_GROUPS)]

    q_end = (A_HQ + 2 * A_HKV) * HEAD_DIM
    b0 = q_end
    b_r_end = b0 + 3 * B_WIDTH
    b_wd_end = b_r_end + LORA_DECAY
    b_ad_end = b_wd_end + LORA_ICLR
    b_end = b_ad_end + LORA_GATE
    c_end = b_end + 3 * C_WIDTH

    def rwkv_cols(t):
        parts = [t[..., b0:b_r_end], _pad_cols(t[..., b_r_end:b_wd_end], LORA_PAD),
                 _pad_cols(t[..., b_wd_end:b_ad_end], LORA_PAD), t[..., b_ad_end:b_end]]
        return _pad_cols(jnp.concatenate(parts, axis=-1), N_RWKV)

    qkv_c = [w_in[..., b_end + t * C_WIDTH + gi * C_OUT:b_end + t * C_WIDTH + (gi + 1) * C_OUT]
             for gi in range(len(C_GROUPS)) for t in range(3)]
    w_attn = jnp.concatenate([w_in[..., :q_end]] + qkv_c, axis=-1).astype(BF16)
    w_rwkv = rwkv_cols(w_in).astype(BF16)
    w_gate = w_in[..., c_end:].astype(BF16)
    mu_pad = rwkv_cols(jnp.pad(rwkv_mu, ((0, 0), (b0, 0))))
    wup_pad = _pad_rows(rwkv_w_up, LORA_PAD)
    aup_pad = _pad_rows(rwkv_a_up, LORA_PAD)
    r_k = rwkv_r_k.reshape(depth, B_WIDTH)

    head_id = jnp.arange(B_WIDTH) // HEAD_DIM
    e_bd = (head_id[:, None] == head_id[None, :]).astype(BF16)

    pa, pb, pc, wo = (t.astype(BF16) for t in (proj_a, proj_b, proj_c, w_out))
    wg = _pad_cols(ffn_up[..., :d_ff], ffp).astype(BF16)
    wv = _pad_cols(ffn_up[..., d_ff:], ffp).astype(BF16)
    cw = _pad_cols(ffn_conv, ffp)
    wd = _pad_rows(ffn_down, ffp).astype(BF16)

    x2 = x.reshape(m, d)
    for l in range(depth):
        p_swa, *p_dil = _proj_attn(x2, norm1_g[l], w_attn[l], tm_proj)
        p_rwkv = _norm_matmul(x2, norm1_g[l], w_rwkv[l], F32, False, tm_proj, 512, "proj_rwkv")
        gates = _norm_matmul(x2, norm1_g[l], w_gate[l], BF16, True, tm_proj, 512, "proj_gates")

        kvw = A_HKV * HEAD_DIM
        y_a = _band_attention(p_swa.reshape(bsz, seq, -1), bias_a, attn_sinks[l], dil=1, q_blk=0,
                              k_blk=(A_HQ * HEAD_DIM) // kvw, v_blk=(A_HQ * HEAD_DIM) // kvw + 1,
                              q_w=A_HQ * HEAD_DIM, kv_w=kvw, nq=A_HQ, nkv=A_HKV, want_lse=False,
                              out_dtype=BF16, name="attn_swa", nqb=2)
        oc, lc = [], []
        for gi, (win, dil) in enumerate(C_GROUPS):
            pv = p_dil[gi].reshape(bsz, seq // dil, -1)
            o, ls = _band_attention(pv, bias_c[gi], None, dil=dil, q_blk=0, k_blk=1, v_blk=2, q_w=C_OUT,
                                    kv_w=C_OUT, nq=C_HG, nkv=C_HG, want_lse=True, out_dtype=F32,
                                    name=f"attn_dil{dil}", nqb=min(4, seq // dil // BLK))
            oc.append(o.reshape(m // dil, dil * C_OUT))
            lc.append(ls.reshape(m // dil, dil * C_OUT))

        r, lw, k2, v, kk, bb, g, bonus = _rwkv_prep(
            p_rwkv, seq, mu_pad[l], rwkv_w0[l], wup_pad[l], rwkv_a0[l], aup_pad[l], rwkv_g_up[l],
            rwkv_k_k[l], rwkv_k_a[l], r_k[l], e_bd, tm_tok)
        sh = lambda t: t.reshape(bsz, seq, B_WIDTH)
        y_raw = _wkv_scan(sh(r), sh(lw), sh(k2), sh(v), sh(kk), sh(bb), ts_wkv).reshape(m, B_WIDTH)

        x2 = _merge(x2, y_a.reshape(m, A_HQ * HEAD_DIM), oc[0], oc[1], oc[2], lc[0], lc[1], lc[2], y_raw, g,
                    bonus, gates, rwkv_lnx_g[l], rwkv_lnx_b[l], e_bd, pa[l], pb[l], pc[l], wo[l], tm_tok)
        x2 = _ffn(x2, seq, norm2_g[l], wg[l], wv[l], cw[l], wd[l], final_g, l == depth - 1, tm_ffn, 512)
    return x2.reshape(bsz, seq, d)
```

```python
import functools
import math

import jax
import jax.numpy as jnp
import numpy as np
from jax import lax
from jax.experimental import pallas as pl
from jax.experimental.pallas import tpu as pltpu

F32 = jnp.float32
BF16 = jnp.bfloat16

HEAD_DIM = 64
LANES = 128
BLK = 128
NORM_EPS = 1e-5
A_HQ, A_HKV = 8, 2
B_HEADS = 12
B_WIDTH = B_HEADS * HEAD_DIM
LORA_DECAY, LORA_ICLR, LORA_GATE = 96, 96, 256
LORA_PAD = 128
B_GN_EPS = 64e-5
C_GROUPS = ((128, 1), (512, 4), (2048, 16))
C_HG = 4
C_WIDTH = C_HG * len(C_GROUPS) * HEAD_DIM
C_OUT = C_HG * HEAD_DIM
N_BUCKETS, MAX_EXACT, REL_MAX_DIST = 32, 16, 2048
D_FF_PAD_TO = 512
NEG_BIG = -1e30

N_ATTN = A_HQ * HEAD_DIM + 2 * A_HKV * HEAD_DIM + 3 * C_WIDTH
N_RWKV = 3 * B_WIDTH + 2 * LORA_PAD + LORA_GATE + 256
OFF_WD = 3 * B_WIDTH
OFF_AD = OFF_WD + LORA_PAD
OFF_GD = OFF_AD + LORA_PAD
WKV_CHUNK = 64
HALO = 16

VMEM_LIMIT = 56 * 1024 * 1024


def _cparams(sem):
    return pltpu.CompilerParams(dimension_semantics=sem, vmem_limit_bytes=VMEM_LIMIT)


def _dot(a, b):
    return jnp.dot(a, b, preferred_element_type=F32)


def _dot_nt(a, b):
    return lax.dot_general(a, b, (((1,), (1,)), ((), ())), preferred_element_type=F32)


def _split2(x):
    hi = x.astype(BF16)
    lo = (x - hi.astype(F32)).astype(BF16)
    return hi, lo


def _dot_exact_rhs(a, b_bf16):
    h, l = _split2(a)
    return _dot(h, b_bf16) + _dot(l, b_bf16)


def _dot_exact_lhs(a_bf16, b):
    h, l = _split2(b)
    return _dot(a_bf16, h) + _dot(a_bf16, l)


def _dot3(a, b):
    ah, al = _split2(a)
    bh, bl = _split2(b)
    return _dot(ah, bh) + _dot(ah, bl) + _dot(al, bh)


def _rms(x, g):
    ms = jnp.mean(x * x, axis=-1, keepdims=True)
    return x * lax.rsqrt(ms + NORM_EPS) * g


def _norm_matmul_kernel(x_ref, g_ref, w_ref, o_ref, h_ref, *, sigmoid):
    @pl.when(pl.program_id(1) == 0)
    def _():
        h_ref[...] = _rms(x_ref[...], g_ref[...]).astype(BF16)

    acc = _dot(h_ref[...], w_ref[...])
    if sigmoid:
        acc = jax.nn.sigmoid(acc)
    o_ref[...] = acc.astype(o_ref.dtype)


def _norm_matmul(x2, g, w, out_dtype, sigmoid, tm, tn, name):
    m, d = x2.shape
    n = w.shape[1]
    return pl.pallas_call(
        functools.partial(_norm_matmul_kernel, sigmoid=sigmoid),
        grid=(m // tm, n // tn),
        in_specs=[
            pl.BlockSpec((tm, d), lambda i, j: (i, 0)),
            pl.BlockSpec((1, d), lambda i, j: (0, 0)),
            pl.BlockSpec((d, tn), lambda i, j: (0, j)),
        ],
        out_specs=pl.BlockSpec((tm, tn), lambda i, j: (i, j)),
        out_shape=jax.ShapeDtypeStruct((m, n), out_dtype),
        scratch_shapes=[pltpu.VMEM((tm, d), BF16)],
        compiler_params=_cparams(("parallel", "arbitrary")),
        name=name,
    )(x2, g.reshape(1, d), w)


def _proj_attn_kernel(x_ref, g_ref, w_ref, pa_ref, p1_ref, p4_ref, p16_ref, h_ref, acc_ref):
    j = pl.program_id(1)
    tm = x_ref.shape[0]
    w = w_ref.shape[1]

    @pl.when(j == 0)
    def _():
        h_ref[...] = _rms(x_ref[...], g_ref[...]).astype(BF16)

    acc = _dot(h_ref[...], w_ref[...])

    @pl.when(j == 0)
    def _():
        pa_ref[...] = acc.astype(BF16)

    @pl.when(j == 1)
    def _():
        p1_ref[...] = acc.astype(BF16)

    for jj, dil, ref in ((2, C_GROUPS[1][1], p4_ref), (3, C_GROUPS[2][1], p16_ref)):
        @pl.when(j == jj)
        def _(dil=dil, ref=ref):
            for c in range(w // LANES):
                acc_ref[c] = acc[:, c * LANES:(c + 1) * LANES]
            for r in range(dil):
                for c in range(w // LANES):
                    ref[:, r * w + c * LANES:r * w + (c + 1) * LANES] = (
                        acc_ref[c, pl.ds(r, tm // dil, stride=dil), :].astype(BF16))


def _proj_attn(x2, g, w, tm):
    m, d = x2.shape
    wq = w.shape[1] // 4
    d4, d16 = C_GROUPS[1][1], C_GROUPS[2][1]
    return pl.pallas_call(
        _proj_attn_kernel,
        grid=(m // tm, 4),
        in_specs=[
            pl.BlockSpec((tm, d), lambda i, j: (i, 0)),
            pl.BlockSpec((1, d), lambda i, j: (0, 0)),
            pl.BlockSpec((d, wq), lambda i, j: (0, j)),
        ],
        out_specs=[
            pl.BlockSpec((tm, wq), lambda i, j: (i, 0)),
            pl.BlockSpec((tm, wq), lambda i, j: (i, 0)),
            pl.BlockSpec((tm // d4, d4 * wq), lambda i, j: (i, 0)),
            pl.BlockSpec((tm // d16, d16 * wq), lambda i, j: (i, 0)),
        ],
        out_shape=[
            jax.ShapeDtypeStruct((m, wq), BF16),
            jax.ShapeDtypeStruct((m, wq), BF16),
            jax.ShapeDtypeStruct((m // d4, d4 * wq), BF16),
            jax.ShapeDtypeStruct((m // d16, d16 * wq), BF16),
        ],
        scratch_shapes=[pltpu.VMEM((tm, d), BF16), pltpu.VMEM((wq // LANES, tm, LANES), F32)],
        compiler_params=_cparams(("parallel", "arbitrary")),
        name="proj_attn",
    )(x2, g.reshape(1, d), w)


def _band_attn_kernel(*refs, nq, nkv, has_sink, want_lse):
    it = iter(refs)
    q_ref, kp_ref, kc_ref, vp_ref, vc_ref, bias_ref = (next(it) for _ in range(6))
    sink_ref = next(it) if has_sink else None
    o_ref = next(it)
    lse_ref = next(it) if want_lse else None

    nqb = q_ref.shape[0] // BLK
    first = pl.program_id(2) == 0
    q = q_ref[...] * jnp.asarray(HEAD_DIM ** -0.5, BF16)
    k = jnp.concatenate([kp_ref[...], kc_ref[...]], axis=0)
    v = jnp.concatenate([vp_ref[...], vc_ref[...]], axis=0)
    col = lax.broadcasted_iota(jnp.int32, (BLK, 2 * BLK), 1)
    edge = jnp.where(col < BLK, jnp.where(first, NEG_BIG, 0.0), 0.0)
    rep = nq // nkv
    units = [(i, h) for i in range(nqb) for h in range(nq)]
    hd = lambda t, i: t[:, i * HEAD_DIM:(i + 1) * HEAD_DIM]
    qs = lambda i, h: hd(q[i * BLK:(i + 1) * BLK], h)
    win = lambda t, i, h: hd(t[i * BLK:(i + 2) * BLK], h // rep)
    s = [_dot_nt(qs(i, h), win(k, i, h)) + bias_ref[h] + (edge if i == 0 else 0.0) for i, h in units]
    m = [jnp.max(t, axis=-1, keepdims=True) for t in s]
    if has_sink:
        m = [jnp.maximum(m[u], sink_ref[h]) for u, (i, h) in enumerate(units)]
    p = [jnp.exp(s[u] - m[u]) for u in range(len(units))]
    l = [jnp.sum(t, axis=-1, keepdims=True) for t in p]
    denom = [l[u] + jnp.exp(sink_ref[h] - m[u]) for u, (i, h) in enumerate(units)] if has_sink else l
    o = [_dot(p[u].astype(BF16), win(v, i, h)) / denom[u] for u, (i, h) in enumerate(units)]
    rows = lambda parts: jnp.concatenate(
        [jnp.concatenate(parts[i * nq:(i + 1) * nq], axis=-1) for i in range(nqb)], axis=0)
    o_ref[...] = rows(o).astype(o_ref.dtype)
    if want_lse:
        lse_ref[...] = rows([jnp.broadcast_to(m[u] + jnp.log(l[u]), (BLK, HEAD_DIM)) for u in range(len(units))])


def _band_attention(pv, bias, sink, *, dil, q_blk, k_blk, v_blk, q_w, kv_w, nq, nkv, want_lse, out_dtype, name,
                    nqb):
    b, lf, nd = pv.shape
    n = nd // dil
    nb = lf // (BLK * nqb)
    qpr, kpr = n // q_w, n // kv_w
    has_sink = sink is not None
    prev = lambda j: jnp.maximum(j * nqb - 1, 0)

    in_specs = [
        pl.BlockSpec((None, nqb * BLK, q_w), lambda bi, r, j: (bi, j, r * qpr + q_blk)),
        pl.BlockSpec((None, BLK, kv_w), lambda bi, r, j: (bi, prev(j), r * kpr + k_blk)),
        pl.BlockSpec((None, nqb * BLK, kv_w), lambda bi, r, j: (bi, j, r * kpr + k_blk)),
        pl.BlockSpec((None, BLK, kv_w), lambda bi, r, j: (bi, prev(j), r * kpr + v_blk)),
        pl.BlockSpec((None, nqb * BLK, kv_w), lambda bi, r, j: (bi, j, r * kpr + v_blk)),
        pl.BlockSpec((nq, BLK, 2 * BLK), lambda bi, r, j: (0, 0, 0)),
    ]
    args = [pv, pv, pv, pv, pv, bias]
    if has_sink:
        in_specs.append(pl.BlockSpec(memory_space=pltpu.SMEM))
        args.append(sink)
    ow = nq * HEAD_DIM
    out_spec = pl.BlockSpec((None, nqb * BLK, ow), lambda bi, r, j: (bi, j, r))
    out_shape = jax.ShapeDtypeStruct((b, lf, dil * ow), out_dtype)
    if want_lse:
        out_specs = [out_spec, out_spec]
        out_shapes = [out_shape, jax.ShapeDtypeStruct((b, lf, dil * ow), F32)]
    else:
        out_specs, out_shapes = out_spec, out_shape
    return pl.pallas_call(
        functools.partial(_band_attn_kernel, nq=nq, nkv=nkv, has_sink=has_sink, want_lse=want_lse),
        grid=(b, dil, nb),
        in_specs=in_specs,
        out_specs=out_specs,
        out_shape=out_shapes,
        compiler_params=_cparams(("parallel", "parallel", "arbitrary")),
        name=name,
    )(*args)


def _rwkv_prep_kernel(ph_ref, p_ref, mu_ref, w0_ref, wup_ref, a0_ref, aup_ref, gup_ref, kk_ref, ka_ref,
                      rk_ref, e_ref, r_o, lw_o, k_o, v_o, kk_o, b_o, g_o, bonus_o, *, tiles_per_seq):
    tm = p_ref.shape[0]
    p = p_ref[...]
    seq_start = pl.program_id(0) % tiles_per_seq == 0
    last = ph_ref[...][HALO - 1:HALO, :]
    last = jnp.where(seq_start, 0.0, last)
    row = lax.broadcasted_iota(jnp.int32, (tm, 1), 0)
    prev = jnp.where(row == 0, last, pltpu.roll(p, 1, 0))
    pf = p + (prev - p) * mu_ref[...]

    r = pf[:, 0:B_WIDTH]
    k = pf[:, B_WIDTH:2 * B_WIDTH]
    v = pf[:, 2 * B_WIDTH:3 * B_WIDTH]
    wd = pf[:, OFF_WD:OFF_WD + LORA_PAD]
    ad = pf[:, OFF_AD:OFF_AD + LORA_PAD]
    gd = pf[:, OFF_GD:OFF_GD + LORA_GATE]

    z = w0_ref[...] + _dot3(jnp.tanh(wd), wup_ref[...])
    nz = -z
    softplus = jnp.maximum(nz, 0.0) + jnp.log(1.0 + jnp.exp(-jnp.abs(nz)))
    w = -softplus - 0.5
    lw_o[...] = -jnp.exp(w)
    a = jax.nn.sigmoid(a0_ref[...] + _dot3(ad, aup_ref[...]))
    g_o[...] = _dot3(jax.nn.sigmoid(gd), gup_ref[...])

    e = e_ref[...]
    kk = k * kk_ref[...]
    nrm = jnp.sqrt(_dot_exact_rhs(kk * kk, e))
    kk = kk / jnp.maximum(nrm, 1e-12)
    k2 = k * (1.0 + (a - 1.0) * ka_ref[...])
    r_o[...] = r
    k_o[...] = k2
    v_o[...] = v
    kk_o[...] = kk
    b_o[...] = kk * a
    bonus_o[...] = _dot_exact_rhs(r * k2 * rk_ref[...], e) * v


def _rwkv_prep(pb2, seq, mu, w0, wup, a0, aup, gup, k_k, k_a, r_k, e_bd, tm):
    m, n = pb2.shape
    tps = seq // tm
    row = lambda a: a.reshape(1, -1)
    full = lambda a: pl.BlockSpec(a.shape, lambda i: (0,) * a.ndim)
    args = [pb2, pb2, row(mu), row(w0), wup, row(a0), aup, gup, row(k_k), row(k_a), row(r_k), e_bd]
    in_specs = [
        pl.BlockSpec((HALO, n), lambda i: (jnp.maximum(i * (tm // HALO) - 1, 0), 0)),
        pl.BlockSpec((tm, n), lambda i: (i, 0)),
    ] + [full(a) for a in args[2:]]
    o_spec = pl.BlockSpec((tm, B_WIDTH), lambda i: (i, 0))
    o_shape = jax.ShapeDtypeStruct((m, B_WIDTH), F32)
    return pl.pallas_call(
        functools.partial(_rwkv_prep_kernel, tiles_per_seq=tps),
        grid=(m // tm,),
        in_specs=in_specs,
        out_specs=[o_spec] * 8,
        out_shape=[o_shape] * 8,
        compiler_params=_cparams(("parallel",)),
        name="rwkv_prep",
    )(*args)


def _wkv_kernel(r_ref, lw_ref, k_ref, v_ref, kk_ref, b_ref, y_ref, st_ref):
    c = WKV_CHUNK
    n_chunks = r_ref.shape[0] // c

    @pl.when(pl.program_id(1) == 0)
    def _():
        st_ref[...] = jnp.zeros_like(st_ref)

    rowi = lax.broadcasted_iota(jnp.int32, (c, c), 0)
    coli = lax.broadcasted_iota(jnp.int32, (c, c), 1)
    incl = rowi >= coli
    strict = rowi > coli
    row2 = lax.broadcasted_iota(jnp.int32, (2 * c, c), 0)
    col2 = lax.broadcasted_iota(jnp.int32, (2 * c, c), 1)
    tri2 = row2 - jnp.where(row2 < c, 1, c) >= col2
    tri = jnp.where(incl, 1.0, 0.0).astype(BF16)
    eye = jnp.where(rowi == coli, 1.0, 0.0)

    def chunk(ci, carry):
        sl = pl.ds(pl.multiple_of(ci * c, c), c)
        lw = lw_ref[sl, :]
        cum = _dot_exact_lhs(tri, lw)
        tot = cum[c - 1:c, :]
        p_in = jnp.exp(cum)
        p_ex = jnp.exp(cum - lw)
        p_inv = jnp.exp(-cum)
        p_rest = jnp.exp(tot - cum)
        p_tot = jnp.exp(tot)
        rh_all = (r_ref[sl, :] * p_in).astype(BF16)
        ah_all = (-kk_ref[sl, :] * p_ex).astype(BF16)
        b_all = b_ref[sl, :]
        k_all = k_ref[sl, :]
        bh_all = (b_all * p_inv).astype(BF16)
        kh_all = (k_all * p_inv).astype(BF16)
        bt_all = (b_all * p_rest).astype(BF16)
        kt_all = (k_all * p_rest).astype(BF16)
        v_all = v_ref[sl, :]
        heads = range(B_HEADS)
        hsl = [slice(h * HEAD_DIM, (h + 1) * HEAD_DIM) for h in heads]
        ah = [ah_all[:, s] for s in hsl]
        rh = [rh_all[:, s] for s in hsl]
        vf = [v_all[:, s] for s in hsl]
        vb = [t.astype(BF16) for t in vf]
        ar = [jnp.concatenate([ah[h], rh[h]], axis=0) for h in heads]
        gb = [_dot_nt(ar[h], bh_all[:, hsl[h]]) for h in heads]
        gk = [_dot_nt(ar[h], kh_all[:, hsl[h]]) for h in heads]
        a_rb = [jnp.where(incl, t[c:], 0.0).astype(BF16) for t in gb]
        akrk = [jnp.where(tri2, t, 0.0).astype(BF16) for t in gk]
        x0 = [jnp.where(strict, t[:c], 0.0) for t in gb]
        t = [eye + xi for xi in x0]
        xb = [xi.astype(BF16) for xi in x0]
        x = [_dot(xi, xi) for xi in xb]
        for _ in range(int(math.log2(c)) - 2):
            xt = [_dot(jnp.concatenate([x[h], t[h]], axis=0).astype(BF16), x[h].astype(BF16)) for h in heads]
            x = [p[:c] for p in xt]
            t = [t[h] + xt[h][c:] for h in heads]
        t = [t[h] + _dot(t[h].astype(BF16), x[h].astype(BF16)) for h in heads]
        tb = [ti.astype(BF16) for ti in t]
        w1y2 = [_dot(akrk[h], vb[h]) for h in heads]
        w1 = [p[:c] for p in w1y2]
        y2 = [p[c:] for p in w1y2]
        n2 = [_dot(vf[h].T.astype(BF16), kt_all[:, hsl[h]]) for h in heads]
        a2 = [_dot(tb[h], ah[h]).astype(BF16) for h in heads]
        u1 = [_dot(tb[h], w1[h].astype(BF16)) for h in heads]
        st_all = st_ref[...]
        s0b = [st_all[h].astype(BF16) for h in heads]
        utb = [(_dot_nt(s0b[h], a2[h]) + u1[h].T).astype(BF16) for h in heads]
        ys = [_dot_nt(rh[h], s0b[h]) + _dot_nt(a_rb[h], utb[h]) + y2[h] for h in heads]
        sts = [st_all[h] * p_tot[:, hsl[h]] + _dot(utb[h], bt_all[:, hsl[h]]) + n2[h] for h in heads]
        st_ref[...] = jnp.stack(sts, axis=0)
        y_ref[sl, :] = jnp.concatenate(ys, axis=-1)
        return carry

    lax.fori_loop(0, n_chunks, chunk, 0)


def _wkv_scan(r, lw, k, v, kk, b, ts):
    bsz, s, w = r.shape
    spec = pl.BlockSpec((None, ts, w), lambda bi, j: (bi, j, 0))
    return pl.pallas_call(
        _wkv_kernel,
        grid=(bsz, s // ts),
        in_specs=[spec] * 6,
        out_specs=spec,
        out_shape=jax.ShapeDtypeStruct((bsz, s, w), F32),
        scratch_shapes=[pltpu.VMEM((B_HEADS, HEAD_DIM, HEAD_DIM), F32)],
        compiler_params=_cparams(("parallel", "arbitrary")),
        name="wkv_scan",
    )(r, lw, k, v, kk, b)


def _merge_kernel(ya_ref, o1_ref, o2_ref, o3_ref, l1_ref, l2_ref, l3_ref, yr_ref, g_ref, bonus_ref,
                  gates_ref, lng_ref, lnb_ref, e_ref, pa_ref, pb_ref, pc_ref, out_ref, unf_ref):
    tm, d = out_ref.shape

    def unfold(ref, slot, dil):
        nc = C_OUT // LANES
        for r in range(dil):
            for c in range(nc):
                unf_ref[slot * nc + c, pl.ds(r, tm // dil, stride=dil), :] = (
                    ref[:, r * C_OUT + c * LANES:r * C_OUT + (c + 1) * LANES])
        return jnp.concatenate([unf_ref[slot * nc + c] for c in range(nc)], axis=-1)

    d2, d3 = C_GROUPS[1][1], C_GROUPS[2][1]
    o1, l1 = o1_ref[...], l1_ref[...]
    o2, l2 = unfold(o2_ref, 0, d2), unfold(l2_ref, 1, d2)
    o3, l3 = unfold(o3_ref, 2, d3), unfold(l3_ref, 3, d3)
    m = jnp.maximum(jnp.maximum(l1, l2), l3)
    e1, e2, e3 = jnp.exp(l1 - m), jnp.exp(l2 - m), jnp.exp(l3 - m)
    yc = (e1 * o1 + e2 * o2 + e3 * o3) / (e1 + e2 + e3)

    e = e_ref[...]
    y = yr_ref[...]
    mean = _dot_exact_rhs(y, e) * (1.0 / HEAD_DIM)
    dv = y - mean
    var = _dot_exact_rhs(dv * dv, e) * (1.0 / HEAD_DIM)
    yb = dv * lax.rsqrt(var + B_GN_EPS) * lng_ref[...] + lnb_ref[...] + bonus_ref[...]
    yb = yb * g_ref[...]

    gates = gates_ref[...]
    merged = (gates[:, 0:d].astype(F32) * _dot(ya_ref[...], pa_ref[...])
              + gates[:, d:2 * d].astype(F32) * _dot(yb.astype(BF16), pb_ref[...])
              + gates[:, 2 * d:3 * d].astype(F32) * _dot(yc.astype(BF16), pc_ref[...]))
    out_ref[...] = merged.astype(BF16)


def _merge(ya, o1, o2, o3, l1, l2, l3, yr, g, bonus, gates, lng, lnb, e_bd, pa, pb, pc, tm):
    m, d = ya.shape[0], pa.shape[1]
    tok = lambda a: pl.BlockSpec((tm * a.shape[0] // m, a.shape[1]), lambda i: (i, 0))
    full = lambda a: pl.BlockSpec(a.shape, lambda i: (0,) * a.ndim, pipeline_mode=pl.Buffered(1))
    toks = [ya, o1, o2, o3, l1, l2, l3, yr, g, bonus, gates]
    consts = [lng.reshape(1, -1), lnb.reshape(1, -1), e_bd, pa, pb, pc]
    return pl.pallas_call(
        _merge_kernel,
        grid=(m // tm,),
        in_specs=[tok(a) for a in toks] + [full(a) for a in consts],
        out_specs=pl.BlockSpec((tm, d), lambda i: (i, 0)),
        out_shape=jax.ShapeDtypeStruct((m, d), BF16),
        scratch_shapes=[pltpu.VMEM((4 * C_OUT // LANES, tm, LANES), F32)],
        compiler_params=_cparams(("parallel",)),
        name="merge",
    )(*toks, *consts)


def _out_proj_kernel(x_ref, m_ref, w_ref, o_ref):
    o_ref[...] = x_ref[...] + _dot(m_ref[...], w_ref[...])


def _out_proj(x2, merged, wo, tm):
    m, d = x2.shape
    tok = pl.BlockSpec((tm, d), lambda i: (i, 0))
    return pl.pallas_call(
        _out_proj_kernel,
        grid=(m // tm,),
        in_specs=[tok, tok, pl.BlockSpec(wo.shape, lambda i: (0, 0), pipeline_mode=pl.Buffered(1))],
        out_specs=tok,
        out_shape=jax.ShapeDtypeStruct((m, d), F32),
        compiler_params=_cparams(("parallel",)),
        name="out_proj",
    )(x2, merged, wo)


def _ffn_kernel(xh_ref, x_ref, g_ref, wg_ref, wv_ref, cw_ref, wd_ref, fg_ref, o_ref, h_ref, *,
                tiles_per_seq, final_norm):
    j = pl.program_id(1)
    tm = x_ref.shape[0]

    @pl.when(j == 0)
    def _():
        x = x_ref[...]
        g = g_ref[...]
        h_ref[HALO:, :] = _rms(x, g).astype(BF16)
        seq_start = pl.program_id(0) % tiles_per_seq == 0
        halo = _rms(xh_ref[...], g)
        h_ref[0:HALO, :] = jnp.where(seq_start, 0.0, halo).astype(BF16)
        o_ref[...] = x

    h = h_ref[...]
    gext = _dot(h, wg_ref[...])
    val = _dot(h[HALO:], wv_ref[...])
    cw = cw_ref[...]
    gate = (cw[0:1, :] * pltpu.roll(gext, 2, 0)[HALO:]
            + cw[1:2, :] * pltpu.roll(gext, 1, 0)[HALO:]
            + cw[2:3, :] * gext[HALO:])
    hid = gate * jax.nn.sigmoid(gate) * val
    o_ref[...] += _dot(hid.astype(BF16), wd_ref[...])

    if final_norm:
        @pl.when(j == pl.num_programs(1) - 1)
        def _():
            o_ref[...] = _rms(o_ref[...], fg_ref[...])


def _ffn(x2, seq, g, wg, wv, cw, wd, fg, final_norm, tm, tf):
    m, d = x2.shape
    ffp = wg.shape[1]
    tps = seq // tm
    return pl.pallas_call(
        functools.partial(_ffn_kernel, tiles_per_seq=tps, final_norm=final_norm),
        grid=(m // tm, ffp // tf),
        in_specs=[
            pl.BlockSpec((HALO, d), lambda i, j: (jnp.maximum(i * (tm // HALO) - 1, 0), 0)),
            pl.BlockSpec((tm, d), lambda i, j: (i, 0)),
            pl.BlockSpec((1, d), lambda i, j: (0, 0)),
            pl.BlockSpec((d, tf), lambda i, j: (0, j)),
            pl.BlockSpec((d, tf), lambda i, j: (0, j)),
            pl.BlockSpec((3, tf), lambda i, j: (0, j)),
            pl.BlockSpec((tf, d), lambda i, j: (j, 0)),
            pl.BlockSpec((1, d), lambda i, j: (0, 0)),
        ],
        out_specs=pl.BlockSpec((tm, d), lambda i, j: (i, 0)),
        out_shape=jax.ShapeDtypeStruct((m, d), F32),
        scratch_shapes=[pltpu.VMEM((HALO + tm, d), BF16)],
        compiler_params=_cparams(("parallel", "arbitrary")),
        name="conv_ffn",
    )(x2, x2, g.reshape(1, d), wg, wv, cw, wd, fg.reshape(1, d))


def _t5_bucket(dist):
    small = dist < MAX_EXACT
    nf = jnp.maximum(dist, 1).astype(F32)
    large = MAX_EXACT + (jnp.log(nf / MAX_EXACT) / math.log(REL_MAX_DIST / MAX_EXACT)
                         * (N_BUCKETS - MAX_EXACT)).astype(jnp.int32)
    return jnp.where(small, dist, jnp.minimum(large, N_BUCKETS - 1))


def _band_bias(table, dilation, max_steps):
    i = jnp.arange(BLK)[:, None]
    j = jnp.arange(2 * BLK)[None, :]
    off = i + BLK - j
    onehot = jax.nn.one_hot(_t5_bucket(jnp.maximum(off, 0) * dilation), N_BUCKETS, dtype=F32)
    bias = jnp.einsum("ijb,bh->hij", onehot, table.astype(F32), precision=lax.Precision.HIGHEST)
    valid = (off >= 0) & (off <= max_steps)
    return jnp.where(valid[None], bias, NEG_BIG)


def _pad_cols(w, n):
    return jnp.pad(w, [(0, 0)] * (w.ndim - 1) + [(0, n - w.shape[-1])])


def _pad_rows(w, n):
    return jnp.pad(w, [(0, 0)] * (w.ndim - 2) + [(0, n - w.shape[-2]), (0, 0)])


def kernel(x, rel_bias, norm1_g, w_in, attn_sinks, rwkv_mu, rwkv_w0, rwkv_w_up, rwkv_a0, rwkv_a_up, rwkv_g_up,
           rwkv_k_k, rwkv_k_a, rwkv_r_k, rwkv_lnx_g, rwkv_lnx_b, proj_a, proj_b, proj_c, w_out, norm2_g,
           ffn_up, ffn_conv, ffn_down, final_g):
    bsz, seq, d = x.shape
    depth = w_in.shape[0]
    m = bsz * seq
    d_ff = ffn_conv.shape[-1]
    ffp = -(-d_ff // D_FF_PAD_TO) * D_FF_PAD_TO

    tm_proj = min(1024, seq)
    tm_tok = min(512, seq)
    tm_ffn = min(1024, seq)
    ts_wkv = min(512, seq)

    na = A_HQ
    bias_a = _band_bias(rel_bias[:, :na], 1, BLK - 1)
    bias_c = [_band_bias(rel_bias[:, na + gi * C_HG:na + (gi + 1) * C_HG], dil, win // dil)
              for gi, (win, dil) in enumerate(C_GROUPS)]

    q_end = (A_HQ + 2 * A_HKV) * HEAD_DIM
    b0 = q_end
    b_r_end = b0 + 3 * B_WIDTH
    b_wd_end = b_r_end + LORA_DECAY
    b_ad_end = b_wd_end + LORA_ICLR
    b_end = b_ad_end + LORA_GATE
    c_end = b_end + 3 * C_WIDTH

    def rwkv_cols(t):
        parts = [t[..., b0:b_r_end], _pad_cols(t[..., b_r_end:b_wd_end], LORA_PAD),
                 _pad_cols(t[..., b_wd_end:b_ad_end], LORA_PAD), t[..., b_ad_end:b_end]]
        return _pad_cols(jnp.concatenate(parts, axis=-1), N_RWKV)

    qkv_c = [w_in[..., b_end + t * C_WIDTH + gi * C_OUT:b_end + t * C_WIDTH + (gi + 1) * C_OUT]
             for gi in range(len(C_GROUPS)) for t in range(3)]
    w_attn = jnp.concatenate([w_in[..., :q_end]] + qkv_c, axis=-1).astype(BF16)
    w_rwkv = rwkv_cols(w_in).astype(BF16)
    w_gate = w_in[..., c_end:].astype(BF16)
    mu_pad = rwkv_cols(jnp.pad(rwkv_mu, ((0, 0), (b0, 0))))
    wup_pad = _pad_rows(rwkv_w_up, LORA_PAD)
    aup_pad = _pad_rows(rwkv_a_up, LORA_PAD)
    r_k = rwkv_r_k.reshape(depth, B_WIDTH)

    head_id = jnp.arange(B_WIDTH) // HEAD_DIM
    e_bd = (head_id[:, None] == head_id[None, :]).astype(BF16)

    pa, pb, pc, wo = (t.astype(BF16) for t in (proj_a, proj_b, proj_c, w_out))
    wg = _pad_cols(ffn_up[..., :d_ff], ffp).astype(BF16)
    wv = _pad_cols(ffn_up[..., d_ff:], ffp).astype(BF16)
    cw = _pad_cols(ffn_conv, ffp)
    wd = _pad_rows(ffn_down, ffp).astype(BF16)

    x2 = x.reshape(m, d)
    for l in range(depth):
        p_swa, *p_dil = _proj_attn(x2, norm1_g[l], w_attn[l], tm_proj)
        p_rwkv = _norm_matmul(x2, norm1_g[l], w_rwkv[l], F32, False, tm_proj, 1024, "proj_rwkv")
        gates = _norm_matmul(x2, norm1_g[l], w_gate[l], BF16, True, tm_proj, 1024, "proj_gates")

        kvw = A_HKV * HEAD_DIM
        y_a = _band_attention(p_swa.reshape(bsz, seq, -1), bias_a, attn_sinks[l], dil=1, q_blk=0,
                              k_blk=(A_HQ * HEAD_DIM) // kvw, v_blk=(A_HQ * HEAD_DIM) // kvw + 1,
                              q_w=A_HQ * HEAD_DIM, kv_w=kvw, nq=A_HQ, nkv=A_HKV, want_lse=False,
                              out_dtype=BF16, name="attn_swa", nqb=2)
        oc, lc = [], []
        for gi, (win, dil) in enumerate(C_GROUPS):
            pv = p_dil[gi].reshape(bsz, seq // dil, -1)
            o, ls = _band_attention(pv, bias_c[gi], None, dil=dil, q_blk=0, k_blk=1, v_blk=2, q_w=C_OUT,
                                    kv_w=C_OUT, nq=C_HG, nkv=C_HG, want_lse=True, out_dtype=F32,
                                    name=f"attn_dil{dil}", nqb=min(4, seq // dil // BLK))
            oc.append(o.reshape(m // dil, dil * C_OUT))
            lc.append(ls.reshape(m // dil, dil * C_OUT))

        r, lw, k2, v, kk, bb, g, bonus = _rwkv_prep(
            p_rwkv, seq, mu_pad[l], rwkv_w0[l], wup_pad[l], rwkv_a0[l], aup_pad[l], rwkv_g_up[l],
            rwkv_k_k[l], rwkv_k_a[l], r_k[l], e_bd, tm_tok)
        sh = lambda t: t.reshape(bsz, seq, B_WIDTH)
        y_raw = _wkv_scan(sh(r), sh(lw), sh(k2), sh(v), sh(kk), sh(bb), ts_wkv).reshape(m, B_WIDTH)

        merged = _merge(y_a.reshape(m, A_HQ * HEAD_DIM), oc[0], oc[1], oc[2], lc[0], lc[1], lc[2], y_raw, g,
                        bonus, gates, rwkv_lnx_g[l], rwkv_lnx_b[l], e_bd, pa[l], pb[l], pc[l], tm_tok)
        x2 = _out_proj(x2, merged, wo[l], tm_proj)
        x2 = _ffn(x2, seq, norm2_g[l], wg[l], wv[l], cw[l], wd[l], final_g, l == depth - 1, tm_ffn, 512)
    return x2.reshape(bsz, seq, d)
```

```python
import functools
import math

import jax
import jax.numpy as jnp
import numpy as np
from jax import lax
from jax.experimental import pallas as pl
from jax.experimental.pallas import tpu as pltpu

F32 = jnp.float32
BF16 = jnp.bfloat16

HEAD_DIM = 64
LANES = 128
BLK = 128
NORM_EPS = 1e-5
A_HQ, A_HKV = 8, 2
B_HEADS = 12
B_WIDTH = B_HEADS * HEAD_DIM
LORA_DECAY, LORA_ICLR, LORA_GATE = 96, 96, 256
LORA_PAD = 128
B_GN_EPS = 64e-5
C_GROUPS = ((128, 1), (512, 4), (2048, 16))
C_HG = 4
C_WIDTH = C_HG * len(C_GROUPS) * HEAD_DIM
C_OUT = C_HG * HEAD_DIM
N_BUCKETS, MAX_EXACT, REL_MAX_DIST = 32, 16, 2048
D_FF_PAD_TO = 512
NEG_BIG = -1e30

N_ATTN = A_HQ * HEAD_DIM + 2 * A_HKV * HEAD_DIM + 3 * C_WIDTH
N_RWKV = 3 * B_WIDTH + 2 * LORA_PAD + LORA_GATE + 256
OFF_WD = 3 * B_WIDTH
OFF_AD = OFF_WD + LORA_PAD
OFF_GD = OFF_AD + LORA_PAD
WKV_CHUNK = 64
WKV_UNROLL = 4
HALO = 16

VMEM_LIMIT = 56 * 1024 * 1024


def _cparams(sem):
    return pltpu.CompilerParams(dimension_semantics=sem, vmem_limit_bytes=VMEM_LIMIT)


def _dot(a, b):
    return jnp.dot(a, b, preferred_element_type=F32)


def _dot_nt(a, b):
    return lax.dot_general(a, b, (((1,), (1,)), ((), ())), preferred_element_type=F32)


def _split2(x):
    hi = x.astype(BF16)
    lo = (x - hi.astype(F32)).astype(BF16)
    return hi, lo


def _dot_exact_rhs(a, b_bf16):
    h, l = _split2(a)
    return _dot(h, b_bf16) + _dot(l, b_bf16)


def _dot_exact_lhs(a_bf16, b):
    h, l = _split2(b)
    return _dot(a_bf16, h) + _dot(a_bf16, l)


def _dot3(a, b):
    ah, al = _split2(a)
    bh, bl = _split2(b)
    return _dot(ah, bh) + _dot(ah, bl) + _dot(al, bh)


def _rms(x, g):
    ms = jnp.mean(x * x, axis=-1, keepdims=True)
    return x * lax.rsqrt(ms + NORM_EPS) * g


def _proj_kernel(x_ref, g_ref, w_ref, pa_ref, p1_ref, p4_ref, p16_ref, pr_ref, pg_ref, h_ref, acc_ref, *,
                 n_rwkv):
    j = pl.program_id(1)
    tm = x_ref.shape[0]
    w = w_ref.shape[1]
    n_attn = 1 + len(C_GROUPS)

    @pl.when(j == 0)
    def _():
        h_ref[...] = _rms(x_ref[...], g_ref[...]).astype(BF16)

    acc = _dot(h_ref[...], w_ref[...])

    @pl.when(j == 0)
    def _():
        pa_ref[...] = acc.astype(BF16)

    @pl.when(j == 1)
    def _():
        p1_ref[...] = acc.astype(BF16)

    for jj, dil, ref in ((2, C_GROUPS[1][1], p4_ref), (3, C_GROUPS[2][1], p16_ref)):
        @pl.when(j == jj)
        def _(dil=dil, ref=ref):
            for c in range(w // LANES):
                acc_ref[c] = acc[:, c * LANES:(c + 1) * LANES]
            for r in range(dil):
                for c in range(w // LANES):
                    ref[:, r * w + c * LANES:r * w + (c + 1) * LANES] = (
                        acc_ref[c, pl.ds(r, tm // dil, stride=dil), :].astype(BF16))

    @pl.when(jnp.logical_and(j >= n_attn, j < n_attn + n_rwkv))
    def _():
        pr_ref[...] = acc

    @pl.when(j >= n_attn + n_rwkv)
    def _():
        pg_ref[...] = jax.nn.sigmoid(acc).astype(BF16)


def _proj(x2, g, w, n_rwkv, n_gate, tm):
    m, d = x2.shape
    n_attn = 1 + len(C_GROUPS)
    nblk = n_attn + n_rwkv + n_gate
    wq = w.shape[1] // nblk
    d4, d16 = C_GROUPS[1][1], C_GROUPS[2][1]
    return pl.pallas_call(
        functools.partial(_proj_kernel, n_rwkv=n_rwkv),
        grid=(m // tm, nblk),
        in_specs=[
            pl.BlockSpec((tm, d), lambda i, j: (i, 0)),
            pl.BlockSpec((1, d), lambda i, j: (0, 0)),
            pl.BlockSpec((d, wq), lambda i, j: (0, j)),
        ],
        out_specs=[
            pl.BlockSpec((tm, wq), lambda i, j: (i, 0)),
            pl.BlockSpec((tm, wq), lambda i, j: (i, 0)),
            pl.BlockSpec((tm // d4, d4 * wq), lambda i, j: (i, 0)),
            pl.BlockSpec((tm // d16, d16 * wq), lambda i, j: (i, 0)),
            pl.BlockSpec((tm, wq), lambda i, j: (i, jnp.clip(j - n_attn, 0, n_rwkv - 1))),
            pl.BlockSpec((tm, wq), lambda i, j: (i, jnp.clip(j - n_attn - n_rwkv, 0, n_gate - 1))),
        ],
        out_shape=[
            jax.ShapeDtypeStruct((m, wq), BF16),
            jax.ShapeDtypeStruct((m, wq), BF16),
            jax.ShapeDtypeStruct((m // d4, d4 * wq), BF16),
            jax.ShapeDtypeStruct((m // d16, d16 * wq), BF16),
            jax.ShapeDtypeStruct((m, n_rwkv * wq), F32),
            jax.ShapeDtypeStruct((m, n_gate * wq), BF16),
        ],
        scratch_shapes=[pltpu.VMEM((tm, d), BF16), pltpu.VMEM((wq // LANES, tm, LANES), F32)],
        compiler_params=_cparams(("parallel", "arbitrary")),
        name="proj",
    )(x2, g.reshape(1, d), w)


def _band_attn_kernel(*refs, nq, nkv, has_sink, want_lse):
    it = iter(refs)
    q_ref, kp_ref, kc_ref, vp_ref, vc_ref, bias_ref = (next(it) for _ in range(6))
    sink_ref = next(it) if has_sink else None
    o_ref = next(it)
    lse_ref = next(it) if want_lse else None

    nqb = q_ref.shape[0] // BLK
    first = pl.program_id(2) == 0
    q = q_ref[...] * jnp.asarray(HEAD_DIM ** -0.5, BF16)
    k = jnp.concatenate([kp_ref[...], kc_ref[...]], axis=0)
    v = jnp.concatenate([vp_ref[...], vc_ref[...]], axis=0)
    col = lax.broadcasted_iota(jnp.int32, (BLK, 2 * BLK), 1)
    edge = jnp.where(col < BLK, jnp.where(first, NEG_BIG, 0.0), 0.0)
    rep = nq // nkv
    units = [(i, h) for i in range(nqb) for h in range(nq)]
    hd = lambda t, i: t[:, i * HEAD_DIM:(i + 1) * HEAD_DIM]
    qs = lambda i, h: hd(q[i * BLK:(i + 1) * BLK], h)
    win = lambda t, i, h: hd(t[i * BLK:(i + 2) * BLK], h // rep)
    s = [_dot_nt(qs(i, h), win(k, i, h)) + bias_ref[h] + (edge if i == 0 else 0.0) for i, h in units]
    m = [jnp.max(t, axis=-1, keepdims=True) for t in s]
    if has_sink:
        m = [jnp.maximum(m[u], sink_ref[h]) for u, (i, h) in enumerate(units)]
    p = [jnp.exp(s[u] - m[u]) for u in range(len(units))]
    l = [jnp.sum(t, axis=-1, keepdims=True) for t in p]
    denom = [l[u] + jnp.exp(sink_ref[h] - m[u]) for u, (i, h) in enumerate(units)] if has_sink else l
    o = [_dot(p[u].astype(BF16), win(v, i, h)) / denom[u] for u, (i, h) in enumerate(units)]
    rows = lambda parts: jnp.concatenate(
        [jnp.concatenate(parts[i * nq:(i + 1) * nq], axis=-1) for i in range(nqb)], axis=0)
    o_ref[...] = rows(o).astype(o_ref.dtype)
    if want_lse:
        lse_ref[...] = rows([jnp.broadcast_to(m[u] + jnp.log(l[u]), (BLK, HEAD_DIM)) for u in range(len(units))])


def _band_attention(pv, bias, sink, *, dil, q_blk, k_blk, v_blk, q_w, kv_w, nq, nkv, want_lse, out_dtype, name,
                    nqb):
    b, lf, nd = pv.shape
    n = nd // dil
    nb = lf // (BLK * nqb)
    qpr, kpr = n // q_w, n // kv_w
    has_sink = sink is not None
    prev = lambda j: jnp.maximum(j * nqb - 1, 0)

    in_specs = [
        pl.BlockSpec((None, nqb * BLK, q_w), lambda bi, r, j: (bi, j, r * qpr + q_blk)),
        pl.BlockSpec((None, BLK, kv_w), lambda bi, r, j: (bi, prev(j), r * kpr + k_blk)),
        pl.BlockSpec((None, nqb * BLK, kv_w), lambda bi, r, j: (bi, j, r * kpr + k_blk)),
        pl.BlockSpec((None, BLK, kv_w), lambda bi, r, j: (bi, prev(j), r * kpr + v_blk)),
        pl.BlockSpec((None, nqb * BLK, kv_w), lambda bi, r, j: (bi, j, r * kpr + v_blk)),
        pl.BlockSpec((nq, BLK, 2 * BLK), lambda bi, r, j: (0, 0, 0)),
    ]
    args = [pv, pv, pv, pv, pv, bias]
    if has_sink:
        in_specs.append(pl.BlockSpec(memory_space=pltpu.SMEM))
        args.append(sink)
    ow = nq * HEAD_DIM
    out_spec = pl.BlockSpec((None, nqb * BLK, ow), lambda bi, r, j: (bi, j, r))
    out_shape = jax.ShapeDtypeStruct((b, lf, dil * ow), out_dtype)
    if want_lse:
        out_specs = [out_spec, out_spec]
        out_shapes = [out_shape, jax.ShapeDtypeStruct((b, lf, dil * ow), F32)]
    else:
        out_specs, out_shapes = out_spec, out_shape
    return pl.pallas_call(
        functools.partial(_band_attn_kernel, nq=nq, nkv=nkv, has_sink=has_sink, want_lse=want_lse),
        grid=(b, dil, nb),
        in_specs=in_specs,
        out_specs=out_specs,
        out_shape=out_shapes,
        compiler_params=_cparams(("parallel", "parallel", "arbitrary")),
        name=name,
    )(*args)


def _rwkv_prep_kernel(ph_ref, p_ref, mu_ref, w0_ref, wup_ref, a0_ref, aup_ref, gup_ref, kk_ref, ka_ref,
                      rk_ref, e_ref, r_o, lw_o, k_o, v_o, kk_o, b_o, g_o, bonus_o, *, tiles_per_seq):
    tm = p_ref.shape[0]
    p = p_ref[...]
    seq_start = pl.program_id(0) % tiles_per_seq == 0
    last = ph_ref[...][HALO - 1:HALO, :]
    last = jnp.where(seq_start, 0.0, last)
    row = lax.broadcasted_iota(jnp.int32, (tm, 1), 0)
    prev = jnp.where(row == 0, last, pltpu.roll(p, 1, 0))
    pf = p + (prev - p) * mu_ref[...]

    r = pf[:, 0:B_WIDTH]
    k = pf[:, B_WIDTH:2 * B_WIDTH]
    v = pf[:, 2 * B_WIDTH:3 * B_WIDTH]
    wd = pf[:, OFF_WD:OFF_WD + LORA_PAD]
    ad = pf[:, OFF_AD:OFF_AD + LORA_PAD]
    gd = pf[:, OFF_GD:OFF_GD + LORA_GATE]

    z = w0_ref[...] + _dot3(jnp.tanh(wd), wup_ref[...])
    nz = -z
    softplus = jnp.maximum(nz, 0.0) + jnp.log(1.0 + jnp.exp(-jnp.abs(nz)))
    w = -softplus - 0.5
    lw_o[...] = -jnp.exp(w)
    a = jax.nn.sigmoid(a0_ref[...] + _dot3(ad, aup_ref[...]))
    g_o[...] = _dot3(jax.nn.sigmoid(gd), gup_ref[...])

    e = e_ref[...]
    kk = k * kk_ref[...]
    nrm = jnp.sqrt(_dot_exact_rhs(kk * kk, e))
    kk = kk / jnp.maximum(nrm, 1e-12)
    k2 = k * (1.0 + (a - 1.0) * ka_ref[...])
    r_o[...] = r
    k_o[...] = k2
    v_o[...] = v
    kk_o[...] = kk
    b_o[...] = kk * a
    bonus_o[...] = _dot_exact_rhs(r * k2 * rk_ref[...], e) * v


def _rwkv_prep(pb2, seq, mu, w0, wup, a0, aup, gup, k_k, k_a, r_k, e_bd, tm):
    m, n = pb2.shape
    tps = seq // tm
    row = lambda a: a.reshape(1, -1)
    full = lambda a: pl.BlockSpec(a.shape, lambda i: (0,) * a.ndim)
    args = [pb2, pb2, row(mu), row(w0), wup, row(a0), aup, gup, row(k_k), row(k_a), row(r_k), e_bd]
    in_specs = [
        pl.BlockSpec((HALO, n), lambda i: (jnp.maximum(i * (tm // HALO) - 1, 0), 0)),
        pl.BlockSpec((tm, n), lambda i: (i, 0)),
    ] + [full(a) for a in args[2:]]
    o_spec = pl.BlockSpec((tm, B_WIDTH), lambda i: (i, 0))
    o_shape = jax.ShapeDtypeStruct((m, B_WIDTH), F32)
    return pl.pallas_call(
        functools.partial(_rwkv_prep_kernel, tiles_per_seq=tps),
        grid=(m // tm,),
        in_specs=in_specs,
        out_specs=[o_spec] * 8,
        out_shape=[o_shape] * 8,
        compiler_params=_cparams(("parallel",)),
        name="rwkv_prep",
    )(*args)


def _wkv_kernel(r_ref, lw_ref, k_ref, v_ref, kk_ref, b_ref, y_ref, st_ref):
    c = WKV_CHUNK
    n_chunks = r_ref.shape[0] // c

    @pl.when(pl.program_id(1) == 0)
    def _():
        st_ref[...] = jnp.zeros_like(st_ref)

    rowi = lax.broadcasted_iota(jnp.int32, (c, c), 0)
    coli = lax.broadcasted_iota(jnp.int32, (c, c), 1)
    incl = rowi >= coli
    strict = rowi > coli
    row2 = lax.broadcasted_iota(jnp.int32, (2 * c, c), 0)
    col2 = lax.broadcasted_iota(jnp.int32, (2 * c, c), 1)
    tri2 = row2 - jnp.where(row2 < c, 1, c) >= col2
    tri = jnp.where(incl, 1.0, 0.0).astype(BF16)
    eye = jnp.where(rowi == coli, 1.0, 0.0)

    nu = WKV_UNROLL
    heads = range(B_HEADS)
    hsl = [slice(h * HEAD_DIM, (h + 1) * HEAD_DIM) for h in heads]
    units = [(u, h) for u in range(nu) for h in heads]

    def chunks(ci, carry):
        sls = [pl.ds(pl.multiple_of((ci * nu + u) * c, c), c) for u in range(nu)]
        lw = [lw_ref[sl, :] for sl in sls]
        cum = [_dot_exact_lhs(tri, t) for t in lw]
        tot = [t[c - 1:c, :] for t in cum]
        p_inv = [jnp.exp(-t) for t in cum]
        p_rest = [jnp.exp(tot[u] - cum[u]) for u in range(nu)]
        p_tot = [jnp.exp(t) for t in tot]
        rh_all = [(r_ref[sls[u], :] * jnp.exp(cum[u])).astype(BF16) for u in range(nu)]
        ah_all = [(-kk_ref[sls[u], :] * jnp.exp(cum[u] - lw[u])).astype(BF16) for u in range(nu)]
        b_all = [b_ref[sl, :] for sl in sls]
        k_all = [k_ref[sl, :] for sl in sls]
        bh_all = [(b_all[u] * p_inv[u]).astype(BF16) for u in range(nu)]
        kh_all = [(k_all[u] * p_inv[u]).astype(BF16) for u in range(nu)]
        bt_all = [(b_all[u] * p_rest[u]).astype(BF16) for u in range(nu)]
        kt_all = [(k_all[u] * p_rest[u]).astype(BF16) for u in range(nu)]
        v_all = [v_ref[sl, :] for sl in sls]

        ah = [ah_all[u][:, hsl[h]] for u, h in units]
        rh = [rh_all[u][:, hsl[h]] for u, h in units]
        vf = [v_all[u][:, hsl[h]] for u, h in units]
        vb = [t.astype(BF16) for t in vf]
        n = range(len(units))
        ar = [jnp.concatenate([ah[i], rh[i]], axis=0) for i in n]
        gb = [_dot_nt(ar[i], bh_all[u][:, hsl[h]]) for i, (u, h) in enumerate(units)]
        gk = [_dot_nt(ar[i], kh_all[u][:, hsl[h]]) for i, (u, h) in enumerate(units)]
        a_rb = [jnp.where(incl, t[c:], 0.0).astype(BF16) for t in gb]
        akrk = [jnp.where(tri2, t, 0.0).astype(BF16) for t in gk]
        x0 = [jnp.where(strict, t[:c], 0.0) for t in gb]
        t = [eye + xi for xi in x0]
        xb = [xi.astype(BF16) for xi in x0]
        x = [_dot(xi, xi) for xi in xb]
        for _ in range(int(math.log2(c)) - 2):
            xt = [_dot(jnp.concatenate([x[i], t[i]], axis=0).astype(BF16), x[i].astype(BF16)) for i in n]
            x = [p[:c] for p in xt]
            t = [t[i] + xt[i][c:] for i in n]
        t = [t[i] + _dot(t[i].astype(BF16), x[i].astype(BF16)) for i in n]
        tb = [ti.astype(BF16) for ti in t]
        w1y2 = [_dot(akrk[i], vb[i]) for i in n]
        w1 = [p[:c] for p in w1y2]
        y2 = [p[c:] for p in w1y2]
        n2 = [_dot(vf[i].T.astype(BF16), kt_all[u][:, hsl[h]]) for i, (u, h) in enumerate(units)]
        a2 = [_dot(tb[i], ah[i]).astype(BF16) for i in n]
        u1t = [_dot(tb[i], w1[i].astype(BF16)).T for i in n]
        st = [st_ref[h] for h in heads]
        for u in range(nu):
            o = u * B_HEADS
            s0b = [st[h].astype(BF16) for h in heads]
            utb = [(_dot_nt(s0b[h], a2[o + h]) + u1t[o + h]).astype(BF16) for h in heads]
            ys = [_dot_nt(rh[o + h], s0b[h]) + _dot_nt(a_rb[o + h], utb[h]) + y2[o + h] for h in heads]
            st = [st[h] * p_tot[u][:, hsl[h]] + _dot(utb[h], bt_all[u][:, hsl[h]]) + n2[o + h] for h in heads]
            y_ref[sls[u], :] = jnp.concatenate(ys, axis=-1)
        st_ref[...] = jnp.stack(st, axis=0)
        return carry

    lax.fori_loop(0, n_chunks // nu, chunks, 0)


def _wkv_scan(r, lw, k, v, kk, b, ts):
    bsz, s, w = r.shape
    spec = pl.BlockSpec((None, ts, w), lambda bi, j: (bi, j, 0))
    return pl.pallas_call(
        _wkv_kernel,
        grid=(bsz, s // ts),
        in_specs=[spec] * 6,
        out_specs=spec,
        out_shape=jax.ShapeDtypeStruct((bsz, s, w), F32),
        scratch_shapes=[pltpu.VMEM((B_HEADS, HEAD_DIM, HEAD_DIM), F32)],
        compiler_params=_cparams(("parallel", "arbitrary")),
        name="wkv_scan",
    )(r, lw, k, v, kk, b)


def _merge_kernel(ya_ref, o1_ref, o2_ref, o3_ref, l1_ref, l2_ref, l3_ref, yr_ref, g_ref, bonus_ref,
                  gates_ref, lng_ref, lnb_ref, e_ref, pa_ref, pb_ref, pc_ref, out_ref, unf_ref):
    tm, d = out_ref.shape

    def unfold(ref, slot, dil):
        nc = C_OUT // LANES
        for r in range(dil):
            for c in range(nc):
                unf_ref[slot * nc + c, pl.ds(r, tm // dil, stride=dil), :] = (
                    ref[:, r * C_OUT + c * LANES:r * C_OUT + (c + 1) * LANES])
        return jnp.concatenate([unf_ref[slot * nc + c] for c in range(nc)], axis=-1)

    d2, d3 = C_GROUPS[1][1], C_GROUPS[2][1]
    o1, l1 = o1_ref[...], l1_ref[...]
    o2, l2 = unfold(o2_ref, 0, d2), unfold(l2_ref, 1, d2)
    o3, l3 = unfold(o3_ref, 2, d3), unfold(l3_ref, 3, d3)
    m = jnp.maximum(jnp.maximum(l1, l2), l3)
    e1, e2, e3 = jnp.exp(l1 - m), jnp.exp(l2 - m), jnp.exp(l3 - m)
    yc = (e1 * o1 + e2 * o2 + e3 * o3) / (e1 + e2 + e3)

    e = e_ref[...]
    y = yr_ref[...]
    mean = _dot_exact_rhs(y, e) * (1.0 / HEAD_DIM)
    dv = y - mean
    var = _dot_exact_rhs(dv * dv, e) * (1.0 / HEAD_DIM)
    yb = dv * lax.rsqrt(var + B_GN_EPS) * lng_ref[...] + lnb_ref[...] + bonus_ref[...]
    yb = yb * g_ref[...]

    gates = gates_ref[...]
    merged = (gates[:, 0:d].astype(F32) * _dot(ya_ref[...], pa_ref[...])
              + gates[:, d:2 * d].astype(F32) * _dot(yb.astype(BF16), pb_ref[...])
              + gates[:, 2 * d:3 * d].astype(F32) * _dot(yc.astype(BF16), pc_ref[...]))
    out_ref[...] = merged.astype(BF16)


def _merge(ya, o1, o2, o3, l1, l2, l3, yr, g, bonus, gates, lng, lnb, e_bd, pa, pb, pc, tm):
    m, d = ya.shape[0], pa.shape[1]
    tok = lambda a: pl.BlockSpec((tm * a.shape[0] // m, a.shape[1]), lambda i: (i, 0))
    full = lambda a: pl.BlockSpec(a.shape, lambda i: (0,) * a.ndim, pipeline_mode=pl.Buffered(1))
    toks = [ya, o1, o2, o3, l1, l2, l3, yr, g, bonus, gates]
    consts = [lng.reshape(1, -1), lnb.reshape(1, -1), e_bd, pa, pb, pc]
    return pl.pallas_call(
        _merge_kernel,
        grid=(m // tm,),
        in_specs=[tok(a) for a in toks] + [full(a) for a in consts],
        out_specs=pl.BlockSpec((tm, d), lambda i: (i, 0)),
        out_shape=jax.ShapeDtypeStruct((m, d), BF16),
        scratch_shapes=[pltpu.VMEM((4 * C_OUT // LANES, tm, LANES), F32)],
        compiler_params=_cparams(("parallel",)),
        name="merge",
    )(*toks, *consts)


def _out_proj_kernel(x_ref, m_ref, w_ref, o_ref):
    o_ref[...] = x_ref[...] + _dot(m_ref[...], w_ref[...])


def _out_proj(x2, merged, wo, tm):
    m, d = x2.shape
    tok = pl.BlockSpec((tm, d), lambda i: (i, 0))
    return pl.pallas_call(
        _out_proj_kernel,
        grid=(m // tm,),
        in_specs=[tok, tok, pl.BlockSpec(wo.shape, lambda i: (0, 0), pipeline_mode=pl.Buffered(1))],
        out_specs=tok,
        out_shape=jax.ShapeDtypeStruct((m, d), F32),
        compiler_params=_cparams(("parallel",)),
        name="out_proj",
    )(x2, merged, wo)


def _ffn_kernel(xh_ref, x_ref, g_ref, wg_ref, wv_ref, cw_ref, wd_ref, fg_ref, o_ref, h_ref, *,
                tiles_per_seq, final_norm):
    j = pl.program_id(1)
    tm = x_ref.shape[0]

    @pl.when(j == 0)
    def _():
        x = x_ref[...]
        g = g_ref[...]
        h_ref[HALO:, :] = _rms(x, g).astype(BF16)
        seq_start = pl.program_id(0) % tiles_per_seq == 0
        halo = _rms(xh_ref[...], g)
        h_ref[0:HALO, :] = jnp.where(seq_start, 0.0, halo).astype(BF16)
        o_ref[...] = x

    h = h_ref[...]
    gext = _dot(h, wg_ref[...])
    val = _dot(h[HALO:], wv_ref[...])
    cw = cw_ref[...]
    gate = (cw[0:1, :] * pltpu.roll(gext, 2, 0)[HALO:]
            + cw[1:2, :] * pltpu.roll(gext, 1, 0)[HALO:]
            + cw[2:3, :] * gext[HALO:])
    hid = gate * jax.nn.sigmoid(gate) * val
    o_ref[...] += _dot(hid.astype(BF16), wd_ref[...])

    if final_norm:
        @pl.when(j == pl.num_programs(1) - 1)
        def _():
            o_ref[...] = _rms(o_ref[...], fg_ref[...])


def _ffn(x2, seq, g, wg, wv, cw, wd, fg, final_norm, tm, tf):
    m, d = x2.shape
    ffp = wg.shape[1]
    tps = seq // tm
    return pl.pallas_call(
        functools.partial(_ffn_kernel, tiles_per_seq=tps, final_norm=final_norm),
        grid=(m // tm, ffp // tf),
        in_specs=[
            pl.BlockSpec((HALO, d), lambda i, j: (jnp.maximum(i * (tm // HALO) - 1, 0), 0)),
            pl.BlockSpec((tm, d), lambda i, j: (i, 0)),
            pl.BlockSpec((1, d), lambda i, j: (0, 0)),
            pl.BlockSpec((d, tf), lambda i, j: (0, j)),
            pl.BlockSpec((d, tf), lambda i, j: (0, j)),
            pl.BlockSpec((3, tf), lambda i, j: (0, j)),
            pl.BlockSpec((tf, d), lambda i, j: (j, 0)),
            pl.BlockSpec((1, d), lambda i, j: (0, 0)),
        ],
        out_specs=pl.BlockSpec((tm, d), lambda i, j: (i, 0)),
        out_shape=jax.ShapeDtypeStruct((m, d), F32),
        scratch_shapes=[pltpu.VMEM((HALO + tm, d), BF16)],
        compiler_params=_cparams(("parallel", "arbitrary")),
        name="conv_ffn",
    )(x2, x2, g.reshape(1, d), wg, wv, cw, wd, fg.reshape(1, d))


def _t5_bucket(dist):
    small = dist < MAX_EXACT
    nf = jnp.maximum(dist, 1).astype(F32)
    large = MAX_EXACT + (jnp.log(nf / MAX_EXACT) / math.log(REL_MAX_DIST / MAX_EXACT)
                         * (N_BUCKETS - MAX_EXACT)).astype(jnp.int32)
    return jnp.where(small, dist, jnp.minimum(large, N_BUCKETS - 1))


def _band_bias(table, dilation, max_steps):
    i = jnp.arange(BLK)[:, None]
    j = jnp.arange(2 * BLK)[None, :]
    off = i + BLK - j
    onehot = jax.nn.one_hot(_t5_bucket(jnp.maximum(off, 0) * dilation), N_BUCKETS, dtype=F32)
    bias = jnp.einsum("ijb,bh->hij", onehot, table.astype(F32), precision=lax.Precision.HIGHEST)
    valid = (off >= 0) & (off <= max_steps)
    return jnp.where(valid[None], bias, NEG_BIG)


def _pad_cols(w, n):
    return jnp.pad(w, [(0, 0)] * (w.ndim - 1) + [(0, n - w.shape[-1])])


def _pad_rows(w, n):
    return jnp.pad(w, [(0, 0)] * (w.ndim - 2) + [(0, n - w.shape[-2]), (0, 0)])


def kernel(x, rel_bias, norm1_g, w_in, attn_sinks, rwkv_mu, rwkv_w0, rwkv_w_up, rwkv_a0, rwkv_a_up, rwkv_g_up,
           rwkv_k_k, rwkv_k_a, rwkv_r_k, rwkv_lnx_g, rwkv_lnx_b, proj_a, proj_b, proj_c, w_out, norm2_g,
           ffn_up, ffn_conv, ffn_down, final_g):
    bsz, seq, d = x.shape
    depth = w_in.shape[0]
    m = bsz * seq
    d_ff = ffn_conv.shape[-1]
    ffp = -(-d_ff // D_FF_PAD_TO) * D_FF_PAD_TO

    tm_proj = min(1024, seq)
    tm_tok = min(512, seq)
    tm_ffn = min(1024, seq)
    ts_wkv = min(512, seq)

    na = A_HQ
    bias_a = _band_bias(rel_bias[:, :na], 1, BLK - 1)
    bias_c = [_band_bias(rel_bias[:, na + gi * C_HG:na + (gi + 1) * C_HG], dil, win // dil)
              for gi, (win, dil) in enumerate(C_GROUPS)]

    q_end = (A_HQ + 2 * A_HKV) * HEAD_DIM
    b0 = q_end
    b_r_end = b0 + 3 * B_WIDTH
    b_wd_end = b_r_end + LORA_DECAY
    b_ad_end = b_wd_end + LORA_ICLR
    b_end = b_ad_end + LORA_GATE
    c_end = b_end + 3 * C_WIDTH

    def rwkv_cols(t):
        parts = [t[..., b0:b_r_end], _pad_cols(t[..., b_r_end:b_wd_end], LORA_PAD),
                 _pad_cols(t[..., b_wd_end:b_ad_end], LORA_PAD), t[..., b_ad_end:b_end]]
        return _pad_cols(jnp.concatenate(parts, axis=-1), N_RWKV)

    qkv_c = [w_in[..., b_end + t * C_WIDTH + gi * C_OUT:b_end + t * C_WIDTH + (gi + 1) * C_OUT]
             for gi in range(len(C_GROUPS)) for t in range(3)]
    w_all = jnp.concatenate([w_in[..., :q_end]] + qkv_c + [rwkv_cols(w_in), w_in[..., c_end:]],
                            axis=-1).astype(BF16)
    proj_blk = q_end
    n_rwkv_blk = N_RWKV // proj_blk
    n_gate_blk = (w_in.shape[-1] - c_end) // proj_blk
    mu_pad = rwkv_cols(jnp.pad(rwkv_mu, ((0, 0), (b0, 0))))
    wup_pad = _pad_rows(rwkv_w_up, LORA_PAD)
    aup_pad = _pad_rows(rwkv_a_up, LORA_PAD)
    r_k = rwkv_r_k.reshape(depth, B_WIDTH)

    head_id = jnp.arange(B_WIDTH) // HEAD_DIM
    e_bd = (head_id[:, None] == head_id[None, :]).astype(BF16)

    pa, pb, pc, wo = (t.astype(BF16) for t in (proj_a, proj_b, proj_c, w_out))
    wg = _pad_cols(ffn_up[..., :d_ff], ffp).astype(BF16)
    wv = _pad_cols(ffn_up[..., d_ff:], ffp).astype(BF16)
    cw = _pad_cols(ffn_conv, ffp)
    wd = _pad_rows(ffn_down, ffp).astype(BF16)

    x2 = x.reshape(m, d)
    for l in range(depth):
        p_swa, *p_dil, p_rwkv, gates = _proj(x2, norm1_g[l], w_all[l], n_rwkv_blk, n_gate_blk, tm_proj)

        kvw = A_HKV * HEAD_DIM
        y_a = _band_attention(p_swa.reshape(bsz, seq, -1), bias_a, attn_sinks[l], dil=1, q_blk=0,
                              k_blk=(A_HQ * HEAD_DIM) // kvw, v_blk=(A_HQ * HEAD_DIM) // kvw + 1,
                              q_w=A_HQ * HEAD_DIM, kv_w=kvw, nq=A_HQ, nkv=A_HKV, want_lse=False,
                              out_dtype=BF16, name="attn_swa", nqb=2)
        oc, lc = [], []
        for gi, (win, dil) in enumerate(C_GROUPS):
            pv = p_dil[gi].reshape(bsz, seq // dil, -1)
            o, ls = _band_attention(pv, bias_c[gi], None, dil=dil, q_blk=0, k_blk=1, v_blk=2, q_w=C_OUT,
                                    kv_w=C_OUT, nq=C_HG, nkv=C_HG, want_lse=True, out_dtype=F32,
                                    name=f"attn_dil{dil}", nqb=min(4, seq // dil // BLK))
            oc.append(o.reshape(m // dil, dil * C_OUT))
            lc.append(ls.reshape(m // dil, dil * C_OUT))

        r, lw, k2, v, kk, bb, g, bonus = _rwkv_prep(
            p_rwkv, seq, mu_pad[l], rwkv_w0[l], wup_pad[l], rwkv_a0[l], aup_pad[l], rwkv_g_up[l],
            rwkv_k_k[l], rwkv_k_a[l], r_k[l], e_bd, tm_tok)
        sh = lambda t: t.reshape(bsz, seq, B_WIDTH)
        y_raw = _wkv_scan(sh(r), sh(lw), sh(k2), sh(v), sh(kk), sh(bb), ts_wkv).reshape(m, B_WIDTH)

        merged = _merge(y_a.reshape(m, A_HQ * HEAD_DIM), oc[0], oc[1], oc[2], lc[0], lc[1], lc[2], y_raw, g,
                        bonus, gates, rwkv_lnx_g[l], rwkv_lnx_b[l], e_bd, pa[l], pb[l], pc[l], tm_tok)
        x2 = _out_proj(x2, merged, wo[l], tm_proj)
        x2 = _ffn(x2, seq, norm2_g[l], wg[l], wv[l], cw[l], wd[l], final_g, l == depth - 1, tm_ffn, 512)
    return x2.reshape(bsz, seq, d)
```

```python
import functools
import math

import jax
import jax.numpy as jnp
import numpy as np
from jax import lax
from jax.experimental import pallas as pl
from jax.experimental.pallas import tpu as pltpu

F32 = jnp.float32
BF16 = jnp.bfloat16

HEAD_DIM = 64
LANES = 128
BLK = 128
NORM_EPS = 1e-5
A_HQ, A_HKV = 8, 2
B_HEADS = 12
B_WIDTH = B_HEADS * HEAD_DIM
LORA_DECAY, LORA_ICLR, LORA_GATE = 96, 96, 256
LORA_PAD = 128
B_GN_EPS = 64e-5
C_GROUPS = ((128, 1), (512, 4), (2048, 16))
C_HG = 4
C_WIDTH = C_HG * len(C_GROUPS) * HEAD_DIM
C_OUT = C_HG * HEAD_DIM
N_BUCKETS, MAX_EXACT, REL_MAX_DIST = 32, 16, 2048
D_FF_PAD_TO = 512
NEG_BIG = -1e30

N_ATTN = A_HQ * HEAD_DIM + 2 * A_HKV * HEAD_DIM + 3 * C_WIDTH
N_RWKV = 3 * B_WIDTH + 2 * LORA_PAD + LORA_GATE + 256
OFF_WD = 3 * B_WIDTH
OFF_AD = OFF_WD + LORA_PAD
OFF_GD = OFF_AD + LORA_PAD
WKV_CHUNK = 64
WKV_UNROLL = 4
HALO = 16

VMEM_LIMIT = 56 * 1024 * 1024


def _cparams(sem):
    return pltpu.CompilerParams(dimension_semantics=sem, vmem_limit_bytes=VMEM_LIMIT)


def _dot(a, b):
    return jnp.dot(a, b, preferred_element_type=F32)


def _dot_nt(a, b):
    return lax.dot_general(a, b, (((1,), (1,)), ((), ())), preferred_element_type=F32)


def _split2(x):
    hi = x.astype(BF16)
    lo = (x - hi.astype(F32)).astype(BF16)
    return hi, lo


def _dot_exact_rhs(a, b_bf16):
    h, l = _split2(a)
    return _dot(h, b_bf16) + _dot(l, b_bf16)


def _dot_exact_lhs(a_bf16, b):
    h, l = _split2(b)
    return _dot(a_bf16, h) + _dot(a_bf16, l)


def _dot3(a, b):
    ah, al = _split2(a)
    bh, bl = _split2(b)
    return _dot(ah, bh) + _dot(ah, bl) + _dot(al, bh)


def _rms(x, g):
    ms = jnp.mean(x * x, axis=-1, keepdims=True)
    return x * lax.rsqrt(ms + NORM_EPS) * g


def _proj_kernel(x_ref, g_ref, w_ref, pa_ref, p1_ref, p4_ref, p16_ref, pr_ref, pg_ref, h_ref, acc_ref, *,
                 n_rwkv):
    j = pl.program_id(1)
    tm = x_ref.shape[0]
    w = w_ref.shape[1]
    n_attn = 1 + len(C_GROUPS)

    @pl.when(j == 0)
    def _():
        h_ref[...] = _rms(x_ref[...], g_ref[...]).astype(BF16)

    def acc():
        return _dot(h_ref[...], w_ref[...])

    @pl.when(j == 0)
    def _():
        pa_ref[...] = acc().astype(BF16)

    @pl.when(j == 1)
    def _():
        p1_ref[...] = acc().astype(BF16)

    for jj, dil, ref in ((2, C_GROUPS[1][1], p4_ref), (3, C_GROUPS[2][1], p16_ref)):
        @pl.when(j == jj)
        def _(dil=dil, ref=ref):
            a = acc()
            for c in range(w // LANES):
                acc_ref[c] = a[:, c * LANES:(c + 1) * LANES]
            for r in range(dil):
                for c in range(w // LANES):
                    ref[:, r * w + c * LANES:r * w + (c + 1) * LANES] = (
                        acc_ref[c, pl.ds(r, tm // dil, stride=dil), :].astype(BF16))

    @pl.when(jnp.logical_and(j >= n_attn, j < n_attn + n_rwkv))
    def _():
        pr_ref[...] = acc().astype(BF16)

    @pl.when(j >= n_attn + n_rwkv)
    def _():
        pg_ref[...] = jax.nn.sigmoid(acc()).astype(BF16)


def _proj(x2, g, w, n_rwkv, n_gate, tm):
    m, d = x2.shape
    n_attn = 1 + len(C_GROUPS)
    nblk = n_attn + n_rwkv + n_gate
    wq = w.shape[1] // nblk
    d4, d16 = C_GROUPS[1][1], C_GROUPS[2][1]
    return pl.pallas_call(
        functools.partial(_proj_kernel, n_rwkv=n_rwkv),
        grid=(m // tm, nblk),
        in_specs=[
            pl.BlockSpec((tm, d), lambda i, j: (i, 0)),
            pl.BlockSpec((1, d), lambda i, j: (0, 0)),
            pl.BlockSpec((d, wq), lambda i, j: (0, j)),
        ],
        out_specs=[
            pl.BlockSpec((tm, wq), lambda i, j: (i, 0)),
            pl.BlockSpec((tm, wq), lambda i, j: (i, 0)),
            pl.BlockSpec((tm // d4, d4 * wq), lambda i, j: (i, 0)),
            pl.BlockSpec((tm // d16, d16 * wq), lambda i, j: (i, 0)),
            pl.BlockSpec((tm, wq), lambda i, j: (i, jnp.clip(j - n_attn, 0, n_rwkv - 1))),
            pl.BlockSpec((tm, wq), lambda i, j: (i, jnp.clip(j - n_attn - n_rwkv, 0, n_gate - 1))),
        ],
        out_shape=[
            jax.ShapeDtypeStruct((m, wq), BF16),
            jax.ShapeDtypeStruct((m, wq), BF16),
            jax.ShapeDtypeStruct((m // d4, d4 * wq), BF16),
            jax.ShapeDtypeStruct((m // d16, d16 * wq), BF16),
            jax.ShapeDtypeStruct((m, n_rwkv * wq), BF16),
            jax.ShapeDtypeStruct((m, n_gate * wq), BF16),
        ],
        scratch_shapes=[pltpu.VMEM((tm, d), BF16), pltpu.VMEM((wq // LANES, tm, LANES), F32)],
        compiler_params=_cparams(("parallel", "arbitrary")),
        name="proj",
    )(x2, g.reshape(1, d), w)


def _band_attn_kernel(*refs, nq, nkv, has_sink, want_lse):
    it = iter(refs)
    q_ref, kp_ref, kc_ref, vp_ref, vc_ref, bias_ref = (next(it) for _ in range(6))
    sink_ref = next(it) if has_sink else None
    o_ref = next(it)
    lse_ref = next(it) if want_lse else None

    nqb = q_ref.shape[0] // BLK
    first = pl.program_id(2) == 0
    q = q_ref[...] * jnp.asarray(HEAD_DIM ** -0.5, BF16)
    k = jnp.concatenate([kp_ref[...], kc_ref[...]], axis=0)
    v = jnp.concatenate([vp_ref[...], vc_ref[...]], axis=0)
    col = lax.broadcasted_iota(jnp.int32, (BLK, 2 * BLK), 1)
    edge = jnp.where(col < BLK, jnp.where(first, NEG_BIG, 0.0), 0.0)
    rep = nq // nkv
    units = [(i, h) for i in range(nqb) for h in range(nq)]
    hd = lambda t, i: t[:, i * HEAD_DIM:(i + 1) * HEAD_DIM]
    qs = lambda i, h: hd(q[i * BLK:(i + 1) * BLK], h)
    win = lambda t, i, h: hd(t[i * BLK:(i + 2) * BLK], h // rep)
    s = [_dot_nt(qs(i, h), win(k, i, h)) + bias_ref[h] + (edge if i == 0 else 0.0) for i, h in units]
    m = [jnp.max(t, axis=-1, keepdims=True) for t in s]
    if has_sink:
        m = [jnp.maximum(m[u], sink_ref[h]) for u, (i, h) in enumerate(units)]
    p = [jnp.exp(s[u] - m[u]) for u in range(len(units))]
    l = [jnp.sum(t, axis=-1, keepdims=True) for t in p]
    denom = [l[u] + jnp.exp(sink_ref[h] - m[u]) for u, (i, h) in enumerate(units)] if has_sink else l
    o = [_dot(p[u].astype(BF16), win(v, i, h)) / denom[u] for u, (i, h) in enumerate(units)]
    rows = lambda parts: jnp.concatenate(
        [jnp.concatenate(parts[i * nq:(i + 1) * nq], axis=-1) for i in range(nqb)], axis=0)
    o_ref[...] = rows(o).astype(o_ref.dtype)
    if want_lse:
        lse_ref[...] = rows([jnp.broadcast_to(m[u] + jnp.log(l[u]), (BLK, HEAD_DIM)) for u in range(len(units))])


def _band_attention(pv, bias, sink, *, dil, q_blk, k_blk, v_blk, q_w, kv_w, nq, nkv, want_lse, out_dtype, name,
                    nqb):
    b, lf, nd = pv.shape
    n = nd // dil
    nb = lf // (BLK * nqb)
    qpr, kpr = n // q_w, n // kv_w
    has_sink = sink is not None
    prev = lambda j: jnp.maximum(j * nqb - 1, 0)

    in_specs = [
        pl.BlockSpec((None, nqb * BLK, q_w), lambda bi, r, j: (bi, j, r * qpr + q_blk)),
        pl.BlockSpec((None, BLK, kv_w), lambda bi, r, j: (bi, prev(j), r * kpr + k_blk)),
        pl.BlockSpec((None, nqb * BLK, kv_w), lambda bi, r, j: (bi, j, r * kpr + k_blk)),
        pl.BlockSpec((None, BLK, kv_w), lambda bi, r, j: (bi, prev(j), r * kpr + v_blk)),
        pl.BlockSpec((None, nqb * BLK, kv_w), lambda bi, r, j: (bi, j, r * kpr + v_blk)),
        pl.BlockSpec((nq, BLK, 2 * BLK), lambda bi, r, j: (0, 0, 0)),
    ]
    args = [pv, pv, pv, pv, pv, bias]
    if has_sink:
        in_specs.append(pl.BlockSpec(memory_space=pltpu.SMEM))
        args.append(sink)
    ow = nq * HEAD_DIM
    out_spec = pl.BlockSpec((None, nqb * BLK, ow), lambda bi, r, j: (bi, j, r))
    out_shape = jax.ShapeDtypeStruct((b, lf, dil * ow), out_dtype)
    if want_lse:
        out_specs = [out_spec, out_spec]
        out_shapes = [out_shape, jax.ShapeDtypeStruct((b, lf, dil * ow), F32)]
    else:
        out_specs, out_shapes = out_spec, out_shape
    return pl.pallas_call(
        functools.partial(_band_attn_kernel, nq=nq, nkv=nkv, has_sink=has_sink, want_lse=want_lse),
        grid=(b, dil, nb),
        in_specs=in_specs,
        out_specs=out_specs,
        out_shape=out_shapes,
        compiler_params=_cparams(("parallel", "parallel", "arbitrary")),
        name=name,
    )(*args)


def _rwkv_prep_kernel(ph_ref, p_ref, mu_ref, w0_ref, wup_ref, a0_ref, aup_ref, gup_ref, kk_ref, ka_ref,
                      rk_ref, e_ref, r_o, lw_o, k_o, v_o, kk_o, b_o, g_o, bonus_o, *, tiles_per_seq):
    tm = p_ref.shape[0]
    p = p_ref[...].astype(F32)
    seq_start = pl.program_id(0) % tiles_per_seq == 0
    last = ph_ref[...].astype(F32)[HALO - 1:HALO, :]
    last = jnp.where(seq_start, 0.0, last)
    row = lax.broadcasted_iota(jnp.int32, (tm, 1), 0)
    prev = jnp.where(row == 0, last, pltpu.roll(p, 1, 0))
    pf = p + (prev - p) * mu_ref[...]

    r = pf[:, 0:B_WIDTH]
    k = pf[:, B_WIDTH:2 * B_WIDTH]
    v = pf[:, 2 * B_WIDTH:3 * B_WIDTH]
    wd = pf[:, OFF_WD:OFF_WD + LORA_PAD]
    ad = pf[:, OFF_AD:OFF_AD + LORA_PAD]
    gd = pf[:, OFF_GD:OFF_GD + LORA_GATE]

    z = w0_ref[...] + _dot3(jnp.tanh(wd), wup_ref[...])
    nz = -z
    softplus = jnp.maximum(nz, 0.0) + jnp.log(1.0 + jnp.exp(-jnp.abs(nz)))
    w = -softplus - 0.5
    lw_o[...] = -jnp.exp(w)
    a = jax.nn.sigmoid(a0_ref[...] + _dot3(ad, aup_ref[...]))
    g_o[...] = _dot3(jax.nn.sigmoid(gd), gup_ref[...])

    e = e_ref[...]
    kk = k * kk_ref[...]
    nrm = jnp.sqrt(_dot_exact_rhs(kk * kk, e))
    kk = kk / jnp.maximum(nrm, 1e-12)
    k2 = k * (1.0 + (a - 1.0) * ka_ref[...])
    r_o[...] = r
    k_o[...] = k2
    v_o[...] = v
    kk_o[...] = kk
    b_o[...] = kk * a
    bonus_o[...] = _dot_exact_rhs(r * k2 * rk_ref[...], e) * v


def _rwkv_prep(pb2, seq, mu, w0, wup, a0, aup, gup, k_k, k_a, r_k, e_bd, tm):
    m, n = pb2.shape
    tps = seq // tm
    row = lambda a: a.reshape(1, -1)
    full = lambda a: pl.BlockSpec(a.shape, lambda i: (0,) * a.ndim)
    args = [pb2, pb2, row(mu), row(w0), wup, row(a0), aup, gup, row(k_k), row(k_a), row(r_k), e_bd]
    in_specs = [
        pl.BlockSpec((HALO, n), lambda i: (jnp.maximum(i * (tm // HALO) - 1, 0), 0)),
        pl.BlockSpec((tm, n), lambda i: (i, 0)),
    ] + [full(a) for a in args[2:]]
    o_spec = pl.BlockSpec((tm, B_WIDTH), lambda i: (i, 0))
    o_shape = jax.ShapeDtypeStruct((m, B_WIDTH), F32)
    return pl.pallas_call(
        functools.partial(_rwkv_prep_kernel, tiles_per_seq=tps),
        grid=(m // tm,),
        in_specs=in_specs,
        out_specs=[o_spec] * 8,
        out_shape=[o_shape] * 8,
        compiler_params=_cparams(("parallel",)),
        name="rwkv_prep",
    )(*args)


def _wkv_kernel(r_ref, lw_ref, k_ref, v_ref, kk_ref, b_ref, y_ref, st_ref):
    c = WKV_CHUNK
    n_chunks = r_ref.shape[0] // c

    @pl.when(pl.program_id(1) == 0)
    def _():
        st_ref[...] = jnp.zeros_like(st_ref)

    rowi = lax.broadcasted_iota(jnp.int32, (c, c), 0)
    coli = lax.broadcasted_iota(jnp.int32, (c, c), 1)
    incl = rowi >= coli
    strict = rowi > coli
    row2 = lax.broadcasted_iota(jnp.int32, (2 * c, c), 0)
    col2 = lax.broadcasted_iota(jnp.int32, (2 * c, c), 1)
    tri2 = row2 - jnp.where(row2 < c, 1, c) >= col2
    tri = jnp.where(incl, 1.0, 0.0).astype(BF16)
    eye = jnp.where(rowi == coli, 1.0, 0.0)

    nu = WKV_UNROLL
    heads = range(B_HEADS)
    hsl = [slice(h * HEAD_DIM, (h + 1) * HEAD_DIM) for h in heads]
    units = [(u, h) for u in range(nu) for h in heads]

    def chunks(ci, carry):
        sls = [pl.ds(pl.multiple_of((ci * nu + u) * c, c), c) for u in range(nu)]
        lw = [lw_ref[sl, :] for sl in sls]
        cum = [_dot_exact_lhs(tri, t) for t in lw]
        tot = [t[c - 1:c, :] for t in cum]
        p_inv = [jnp.exp(-t) for t in cum]
        p_rest = [jnp.exp(tot[u] - cum[u]) for u in range(nu)]
        p_tot = [jnp.exp(t) for t in tot]
        rh_all = [(r_ref[sls[u], :] * jnp.exp(cum[u])).astype(BF16) for u in range(nu)]
        ah_all = [(-kk_ref[sls[u], :] * jnp.exp(cum[u] - lw[u])).astype(BF16) for u in range(nu)]
        b_all = [b_ref[sl, :] for sl in sls]
        k_all = [k_ref[sl, :] for sl in sls]
        bh_all = [(b_all[u] * p_inv[u]).astype(BF16) for u in range(nu)]
        kh_all = [(k_all[u] * p_inv[u]).astype(BF16) for u in range(nu)]
        bt_all = [(b_all[u] * p_rest[u]).astype(BF16) for u in range(nu)]
        kt_all = [(k_all[u] * p_rest[u]).astype(BF16) for u in range(nu)]
        v_all = [v_ref[sl, :] for sl in sls]

        ah = [ah_all[u][:, hsl[h]] for u, h in units]
        rh = [rh_all[u][:, hsl[h]] for u, h in units]
        vf = [v_all[u][:, hsl[h]] for u, h in units]
        vb = [t.astype(BF16) for t in vf]
        n = range(len(units))
        ar = [jnp.concatenate([ah[i], rh[i]], axis=0) for i in n]
        gb = [_dot_nt(ar[i], bh_all[u][:, hsl[h]]) for i, (u, h) in enumerate(units)]
        gk = [_dot_nt(ar[i], kh_all[u][:, hsl[h]]) for i, (u, h) in enumerate(units)]
        a_rb = [jnp.where(incl, t[c:], 0.0).astype(BF16) for t in gb]
        akrk = [jnp.where(tri2, t, 0.0).astype(BF16) for t in gk]
        x0 = [jnp.where(strict, t[:c], 0.0) for t in gb]
        t = [eye + xi for xi in x0]
        xb = [xi.astype(BF16) for xi in x0]
        x = [_dot(xi, xi) for xi in xb]
        for _ in range(int(math.log2(c)) - 2):
            xt = [_dot(jnp.concatenate([x[i], t[i]], axis=0).astype(BF16), x[i].astype(BF16)) for i in n]
            x = [p[:c] for p in xt]
            t = [t[i] + xt[i][c:] for i in n]
        t = [t[i] + _dot(t[i].astype(BF16), x[i].astype(BF16)) for i in n]
        tb = [ti.astype(BF16) for ti in t]
        w1y2 = [_dot(akrk[i], vb[i]) for i in n]
        w1 = [p[:c] for p in w1y2]
        y2 = [p[c:] for p in w1y2]
        n2 = [_dot(vf[i].T.astype(BF16), kt_all[u][:, hsl[h]]) for i, (u, h) in enumerate(units)]
        a2 = [_dot(tb[i], ah[i]).astype(BF16) for i in n]
        u1t = [_dot(tb[i], w1[i].astype(BF16)).T for i in n]
        st = [st_ref[h] for h in heads]
        for u in range(nu):
            o = u * B_HEADS
            s0b = [st[h].astype(BF16) for h in heads]
            utb = [(_dot_nt(s0b[h], a2[o + h]) + u1t[o + h]).astype(BF16) for h in heads]
            ys = [_dot_nt(rh[o + h], s0b[h]) + _dot_nt(a_rb[o + h], utb[h]) + y2[o + h] for h in heads]
            st = [st[h] * p_tot[u][:, hsl[h]] + _dot(utb[h], bt_all[u][:, hsl[h]]) + n2[o + h] for h in heads]
            y_ref[sls[u], :] = jnp.concatenate(ys, axis=-1)
        st_ref[...] = jnp.stack(st, axis=0)
        return carry

    lax.fori_loop(0, n_chunks // nu, chunks, 0)


def _wkv_scan(r, lw, k, v, kk, b, ts):
    bsz, s, w = r.shape
    spec = pl.BlockSpec((None, ts, w), lambda bi, j: (bi, j, 0))
    return pl.pallas_call(
        _wkv_kernel,
        grid=(bsz, s // ts),
        in_specs=[spec] * 6,
        out_specs=spec,
        out_shape=jax.ShapeDtypeStruct((bsz, s, w), F32),
        scratch_shapes=[pltpu.VMEM((B_HEADS, HEAD_DIM, HEAD_DIM), F32)],
        compiler_params=_cparams(("parallel", "arbitrary")),
        name="wkv_scan",
    )(r, lw, k, v, kk, b)


def _merge_kernel(ya_ref, o1_ref, o2_ref, o3_ref, l1_ref, l2_ref, l3_ref, yr_ref, g_ref, bonus_ref,
                  gates_ref, lng_ref, lnb_ref, e_ref, pa_ref, pb_ref, pc_ref, out_ref, unf_ref):
    tm, d = out_ref.shape

    def unfold(ref, slot, dil):
        nc = C_OUT // LANES
        for r in range(dil):
            for c in range(nc):
                unf_ref[slot * nc + c, pl.ds(r, tm // dil, stride=dil), :] = (
                    ref[:, r * C_OUT + c * LANES:r * C_OUT + (c + 1) * LANES])
        return jnp.concatenate([unf_ref[slot * nc + c] for c in range(nc)], axis=-1)

    d2, d3 = C_GROUPS[1][1], C_GROUPS[2][1]
    o1, l1 = o1_ref[...], l1_ref[...]
    o2, l2 = unfold(o2_ref, 0, d2), unfold(l2_ref, 1, d2)
    o3, l3 = unfold(o3_ref, 2, d3), unfold(l3_ref, 3, d3)
    m = jnp.maximum(jnp.maximum(l1, l2), l3)
    e1, e2, e3 = jnp.exp(l1 - m), jnp.exp(l2 - m), jnp.exp(l3 - m)
    yc = (e1 * o1 + e2 * o2 + e3 * o3) / (e1 + e2 + e3)

    e = e_ref[...]
    y = yr_ref[...]
    mean = _dot_exact_rhs(y, e) * (1.0 / HEAD_DIM)
    dv = y - mean
    var = _dot_exact_rhs(dv * dv, e) * (1.0 / HEAD_DIM)
    yb = dv * lax.rsqrt(var + B_GN_EPS) * lng_ref[...] + lnb_ref[...] + bonus_ref[...]
    yb = yb * g_ref[...]

    gates = gates_ref[...]
    merged = (gates[:, 0:d].astype(F32) * _dot(ya_ref[...], pa_ref[...])
              + gates[:, d:2 * d].astype(F32) * _dot(yb.astype(BF16), pb_ref[...])
              + gates[:, 2 * d:3 * d].astype(F32) * _dot(yc.astype(BF16), pc_ref[...]))
    out_ref[...] = merged.astype(BF16)


def _merge(ya, o1, o2, o3, l1, l2, l3, yr, g, bonus, gates, lng, lnb, e_bd, pa, pb, pc, tm):
    m, d = ya.shape[0], pa.shape[1]
    tok = lambda a: pl.BlockSpec((tm * a.shape[0] // m, a.shape[1]), lambda i: (i, 0))
    full = lambda a: pl.BlockSpec(a.shape, lambda i: (0,) * a.ndim, pipeline_mode=pl.Buffered(1))
    toks = [ya, o1, o2, o3, l1, l2, l3, yr, g, bonus, gates]
    consts = [lng.reshape(1, -1), lnb.reshape(1, -1), e_bd, pa, pb, pc]
    return pl.pallas_call(
        _merge_kernel,
        grid=(m // tm,),
        in_specs=[tok(a) for a in toks] + [full(a) for a in consts],
        out_specs=pl.BlockSpec((tm, d), lambda i: (i, 0)),
        out_shape=jax.ShapeDtypeStruct((m, d), BF16),
        scratch_shapes=[pltpu.VMEM((4 * C_OUT // LANES, tm, LANES), F32)],
        compiler_params=_cparams(("parallel",)),
        name="merge",
    )(*toks, *consts)


def _out_proj_kernel(x_ref, m_ref, w_ref, o_ref):
    o_ref[...] = x_ref[...] + _dot(m_ref[...], w_ref[...])


def _out_proj(x2, merged, wo, tm):
    m, d = x2.shape
    tok = pl.BlockSpec((tm, d), lambda i: (i, 0))
    return pl.pallas_call(
        _out_proj_kernel,
        grid=(m // tm,),
        in_specs=[tok, tok, pl.BlockSpec(wo.shape, lambda i: (0, 0), pipeline_mode=pl.Buffered(1))],
        out_specs=tok,
        out_shape=jax.ShapeDtypeStruct((m, d), F32),
        compiler_params=_cparams(("parallel",)),
        name="out_proj",
    )(x2, merged, wo)


def _ffn_kernel(xh_ref, x_ref, g_ref, wg_ref, wv_ref, cw_ref, wd_ref, fg_ref, o_ref, h_ref, *,
                tiles_per_seq, final_norm):
    j = pl.program_id(1)
    tm = x_ref.shape[0]

    @pl.when(j == 0)
    def _():
        x = x_ref[...]
        g = g_ref[...]
        h_ref[HALO:, :] = _rms(x, g).astype(BF16)
        seq_start = pl.program_id(0) % tiles_per_seq == 0
        halo = _rms(xh_ref[...], g)
        h_ref[0:HALO, :] = jnp.where(seq_start, 0.0, halo).astype(BF16)
        o_ref[...] = x

    h = h_ref[...]
    gext = _dot(h, wg_ref[...])
    val = _dot(h[HALO:], wv_ref[...])
    cw = cw_ref[...]
    gate = (cw[0:1, :] * pltpu.roll(gext, 2, 0)[HALO:]
            + cw[1:2, :] * pltpu.roll(gext, 1, 0)[HALO:]
            + cw[2:3, :] * gext[HALO:])
    hid = gate * jax.nn.sigmoid(gate) * val
    o_ref[...] += _dot(hid.astype(BF16), wd_ref[...])

    if final_norm:
        @pl.when(j == pl.num_programs(1) - 1)
        def _():
            o_ref[...] = _rms(o_ref[...], fg_ref[...])


def _ffn(x2, seq, g, wg, wv, cw, wd, fg, final_norm, tm, tf):
    m, d = x2.shape
    ffp = wg.shape[1]
    tps = seq // tm
    return pl.pallas_call(
        functools.partial(_ffn_kernel, tiles_per_seq=tps, final_norm=final_norm),
        grid=(m // tm, ffp // tf),
        in_specs=[
            pl.BlockSpec((HALO, d), lambda i, j: (jnp.maximum(i * (tm // HALO) - 1, 0), 0)),
            pl.BlockSpec((tm, d), lambda i, j: (i, 0)),
            pl.BlockSpec((1, d), lambda i, j: (0, 0)),
            pl.BlockSpec((d, tf), lambda i, j: (0, j)),
            pl.BlockSpec((d, tf), lambda i, j: (0, j)),
            pl.BlockSpec((3, tf), lambda i, j: (0, j)),
            pl.BlockSpec((tf, d), lambda i, j: (j, 0)),
            pl.BlockSpec((1, d), lambda i, j: (0, 0)),
        ],
        out_specs=pl.BlockSpec((tm, d), lambda i, j: (i, 0)),
        out_shape=jax.ShapeDtypeStruct((m, d), F32),
        scratch_shapes=[pltpu.VMEM((HALO + tm, d), BF16)],
        compiler_params=_cparams(("parallel", "arbitrary")),
        name="conv_ffn",
    )(x2, x2, g.reshape(1, d), wg, wv, cw, wd, fg.reshape(1, d))


def _t5_bucket(dist):
    small = dist < MAX_EXACT
    nf = jnp.maximum(dist, 1).astype(F32)
    large = MAX_EXACT + (jnp.log(nf / MAX_EXACT) / math.log(REL_MAX_DIST / MAX_EXACT)
                         * (N_BUCKETS - MAX_EXACT)).astype(jnp.int32)
    return jnp.where(small, dist, jnp.minimum(large, N_BUCKETS - 1))


def _band_bias(table, dilation, max_steps):
    i = jnp.arange(BLK)[:, None]
    j = jnp.arange(2 * BLK)[None, :]
    off = i + BLK - j
    onehot = jax.nn.one_hot(_t5_bucket(jnp.maximum(off, 0) * dilation), N_BUCKETS, dtype=F32)
    bias = jnp.einsum("ijb,bh->hij", onehot, table.astype(F32), precision=lax.Precision.HIGHEST)
    valid = (off >= 0) & (off <= max_steps)
    return jnp.where(valid[None], bias, NEG_BIG)


def _pad_cols(w, n):
    return jnp.pad(w, [(0, 0)] * (w.ndim - 1) + [(0, n - w.shape[-1])])


def _pad_rows(w, n):
    return jnp.pad(w, [(0, 0)] * (w.ndim - 2) + [(0, n - w.shape[-2]), (0, 0)])


def kernel(x, rel_bias, norm1_g, w_in, attn_sinks, rwkv_mu, rwkv_w0, rwkv_w_up, rwkv_a0, rwkv_a_up, rwkv_g_up,
           rwkv_k_k, rwkv_k_a, rwkv_r_k, rwkv_lnx_g, rwkv_lnx_b, proj_a, proj_b, proj_c, w_out, norm2_g,
           ffn_up, ffn_conv, ffn_down, final_g):
    bsz, seq, d = x.shape
    depth = w_in.shape[0]
    m = bsz * seq
    d_ff = ffn_conv.shape[-1]
    ffp = -(-d_ff // D_FF_PAD_TO) * D_FF_PAD_TO

    tm_proj = min(1024, seq)
    tm_tok = min(512, seq)
    tm_ffn = min(1024, seq)
    ts_wkv = min(512, seq)

    na = A_HQ
    bias_a = _band_bias(rel_bias[:, :na], 1, BLK - 1)
    bias_c = [_band_bias(rel_bias[:, na + gi * C_HG:na + (gi + 1) * C_HG], dil, win // dil)
              for gi, (win, dil) in enumerate(C_GROUPS)]

    q_end = (A_HQ + 2 * A_HKV) * HEAD_DIM
    b0 = q_end
    b_r_end = b0 + 3 * B_WIDTH
    b_wd_end = b_r_end + LORA_DECAY
    b_ad_end = b_wd_end + LORA_ICLR
    b_end = b_ad_end + LORA_GATE
    c_end = b_end + 3 * C_WIDTH

    def rwkv_cols(t):
        parts = [t[..., b0:b_r_end], _pad_cols(t[..., b_r_end:b_wd_end], LORA_PAD),
                 _pad_cols(t[..., b_wd_end:b_ad_end], LORA_PAD), t[..., b_ad_end:b_end]]
        return _pad_cols(jnp.concatenate(parts, axis=-1), N_RWKV)

    w_in_b = lax.optimization_barrier(w_in.astype(BF16))
    qkv_c = [w_in_b[..., b_end + t * C_WIDTH + gi * C_OUT:b_end + t * C_WIDTH + (gi + 1) * C_OUT]
             for gi in range(len(C_GROUPS)) for t in range(3)]
    w_all = jnp.concatenate([w_in_b[..., :q_end]] + qkv_c + [rwkv_cols(w_in_b), w_in_b[..., c_end:]], axis=-1)
    proj_blk = q_end
    n_rwkv_blk = N_RWKV // proj_blk
    n_gate_blk = (w_in.shape[-1] - c_end) // proj_blk
    mu_pad = rwkv_cols(jnp.pad(rwkv_mu, ((0, 0), (b0, 0))))
    wup_pad = _pad_rows(rwkv_w_up, LORA_PAD)
    aup_pad = _pad_rows(rwkv_a_up, LORA_PAD)
    r_k = rwkv_r_k.reshape(depth, B_WIDTH)

    head_id = jnp.arange(B_WIDTH) // HEAD_DIM
    e_bd = (head_id[:, None] == head_id[None, :]).astype(BF16)

    pa, pb, pc, wo = (t.astype(BF16) for t in (proj_a, proj_b, proj_c, w_out))
    wg = _pad_cols(ffn_up[..., :d_ff], ffp).astype(BF16)
    wv = _pad_cols(ffn_up[..., d_ff:], ffp).astype(BF16)
    cw = _pad_cols(ffn_conv, ffp)
    wd = _pad_rows(ffn_down, ffp).astype(BF16)

    x2 = x.reshape(m, d)
    for l in range(depth):
        p_swa, *p_dil, p_rwkv, gates = _proj(x2, norm1_g[l], w_all[l], n_rwkv_blk, n_gate_blk, tm_proj)

        kvw = A_HKV * HEAD_DIM
        y_a = _band_attention(p_swa.reshape(bsz, seq, -1), bias_a, attn_sinks[l], dil=1, q_blk=0,
                              k_blk=(A_HQ * HEAD_DIM) // kvw, v_blk=(A_HQ * HEAD_DIM) // kvw + 1,
                              q_w=A_HQ * HEAD_DIM, kv_w=kvw, nq=A_HQ, nkv=A_HKV, want_lse=False,
                              out_dtype=BF16, name="attn_swa", nqb=4)
        oc, lc = [], []
        for gi, (win, dil) in enumerate(C_GROUPS):
            pv = p_dil[gi].reshape(bsz, seq // dil, -1)
            o, ls = _band_attention(pv, bias_c[gi], None, dil=dil, q_blk=0, k_blk=1, v_blk=2, q_w=C_OUT,
                                    kv_w=C_OUT, nq=C_HG, nkv=C_HG, want_lse=True, out_dtype=F32,
                                    name=f"attn_dil{dil}", nqb=min(4, seq // dil // BLK))
            oc.append(o.reshape(m // dil, dil * C_OUT))
            lc.append(ls.reshape(m // dil, dil * C_OUT))

        r, lw, k2, v, kk, bb, g, bonus = _rwkv_prep(
            p_rwkv, seq, mu_pad[l], rwkv_w0[l], wup_pad[l], rwkv_a0[l], aup_pad[l], rwkv_g_up[l],
            rwkv_k_k[l], rwkv_k_a[l], r_k[l], e_bd, tm_tok)
        sh = lambda t: t.reshape(bsz, seq, B_WIDTH)
        y_raw = _wkv_scan(sh(r), sh(lw), sh(k2), sh(v), sh(kk), sh(bb), ts_wkv).reshape(m, B_WIDTH)

        merged = _merge(y_a.reshape(m, A_HQ * HEAD_DIM), oc[0], oc[1], oc[2], lc[0], lc[1], lc[2], y_raw, g,
                        bonus, gates, rwkv_lnx_g[l], rwkv_lnx_b[l], e_bd, pa[l], pb[l], pc[l], tm_tok)
        x2 = _out_proj(x2, merged, wo[l], tm_proj)
        x2 = _ffn(x2, seq, norm2_g[l], wg[l], wv[l], cw[l], wd[l], final_g, l == depth - 1, tm_ffn, 512)
    return x2.reshape(bsz, seq, d)
```

```python
import functools
import math

import jax
import jax.numpy as jnp
import numpy as np
from jax import lax
from jax.experimental import pallas as pl
from jax.experimental.pallas import tpu as pltpu

F32 = jnp.float32
BF16 = jnp.bfloat16

HEAD_DIM = 64
LANES = 128
BLK = 128
NORM_EPS = 1e-5
A_HQ, A_HKV = 8, 2
B_HEADS = 12
B_WIDTH = B_HEADS * HEAD_DIM
LORA_DECAY, LORA_ICLR, LORA_GATE = 96, 96, 256
LORA_PAD = 128
B_GN_EPS = 64e-5
C_GROUPS = ((128, 1), (512, 4), (2048, 16))
C_HG = 4
C_WIDTH = C_HG * len(C_GROUPS) * HEAD_DIM
C_OUT = C_HG * HEAD_DIM
N_BUCKETS, MAX_EXACT, REL_MAX_DIST = 32, 16, 2048
D_FF_PAD_TO = 512
NEG_BIG = -1e30

N_ATTN = A_HQ * HEAD_DIM + 2 * A_HKV * HEAD_DIM + 3 * C_WIDTH
N_RWKV = 3 * B_WIDTH + 2 * LORA_PAD + LORA_GATE + 256
OFF_WD = 3 * B_WIDTH
OFF_AD = OFF_WD + LORA_PAD
OFF_GD = OFF_AD + LORA_PAD
WKV_CHUNK = 64
WKV_UNROLL = 4
HALO = 16

VMEM_LIMIT = 56 * 1024 * 1024


def _cparams(sem):
    return pltpu.CompilerParams(dimension_semantics=sem, vmem_limit_bytes=VMEM_LIMIT)


def _dot(a, b):
    return jnp.dot(a, b, preferred_element_type=F32)


def _dot_nt(a, b):
    return lax.dot_general(a, b, (((1,), (1,)), ((), ())), preferred_element_type=F32)


def _split2(x):
    hi = x.astype(BF16)
    lo = (x - hi.astype(F32)).astype(BF16)
    return hi, lo


def _head_sums(a, e_bf16, exact=False):
    if not exact:
        return _dot(a.astype(BF16), e_bf16)
    h, l = _split2(a)
    return _dot(h, e_bf16) + _dot(l, e_bf16)


def _dot_exact_lhs(a_bf16, b):
    h, l = _split2(b)
    return _dot(a_bf16, h) + _dot(a_bf16, l)


def _dot3(a, b):
    ah, al = _split2(a)
    bh, bl = _split2(b)
    return _dot(ah, bh) + _dot(ah, bl) + _dot(al, bh)


def _rms(x, g):
    ms = jnp.mean(x * x, axis=-1, keepdims=True)
    return x * lax.rsqrt(ms + NORM_EPS) * g


def _proj_kernel(x_ref, g_ref, w_ref, pa_ref, p1_ref, p4_ref, p16_ref, pr_ref, pg_ref, h_ref, acc_ref, *,
                 n_rwkv):
    j = pl.program_id(1)
    tm = x_ref.shape[0]
    w = w_ref.shape[1]
    n_attn = 1 + len(C_GROUPS)

    @pl.when(j == 0)
    def _():
        h_ref[...] = _rms(x_ref[...], g_ref[...]).astype(BF16)

    def acc():
        return _dot(h_ref[...], w_ref[...])

    @pl.when(j == 0)
    def _():
        pa_ref[...] = acc().astype(BF16)

    @pl.when(j == 1)
    def _():
        p1_ref[...] = acc().astype(BF16)

    for jj, dil, ref in ((2, C_GROUPS[1][1], p4_ref), (3, C_GROUPS[2][1], p16_ref)):
        @pl.when(j == jj)
        def _(dil=dil, ref=ref):
            a = acc()
            for c in range(w // LANES):
                acc_ref[c] = a[:, c * LANES:(c + 1) * LANES]
            for r in range(dil):
                for c in range(w // LANES):
                    ref[:, r * w + c * LANES:r * w + (c + 1) * LANES] = (
                        acc_ref[c, pl.ds(r, tm // dil, stride=dil), :].astype(BF16))

    @pl.when(jnp.logical_and(j >= n_attn, j < n_attn + n_rwkv))
    def _():
        pr_ref[...] = acc().astype(BF16)

    @pl.when(j >= n_attn + n_rwkv)
    def _():
        pg_ref[...] = jax.nn.sigmoid(acc()).astype(BF16)


def _proj(x2, g, w, n_rwkv, n_gate, tm):
    m, d = x2.shape
    n_attn = 1 + len(C_GROUPS)
    nblk = n_attn + n_rwkv + n_gate
    wq = w.shape[1] // nblk
    d4, d16 = C_GROUPS[1][1], C_GROUPS[2][1]
    return pl.pallas_call(
        functools.partial(_proj_kernel, n_rwkv=n_rwkv),
        grid=(m // tm, nblk),
        in_specs=[
            pl.BlockSpec((tm, d), lambda i, j: (i, 0)),
            pl.BlockSpec((1, d), lambda i, j: (0, 0)),
            pl.BlockSpec((d, wq), lambda i, j: (0, j)),
        ],
        out_specs=[
            pl.BlockSpec((tm, wq), lambda i, j: (i, 0)),
            pl.BlockSpec((tm, wq), lambda i, j: (i, 0)),
            pl.BlockSpec((tm // d4, d4 * wq), lambda i, j: (i, 0)),
            pl.BlockSpec((tm // d16, d16 * wq), lambda i, j: (i, 0)),
            pl.BlockSpec((tm, wq), lambda i, j: (i, jnp.clip(j - n_attn, 0, n_rwkv - 1))),
            pl.BlockSpec((tm, wq), lambda i, j: (i, jnp.clip(j - n_attn - n_rwkv, 0, n_gate - 1))),
        ],
        out_shape=[
            jax.ShapeDtypeStruct((m, wq), BF16),
            jax.ShapeDtypeStruct((m, wq), BF16),
            jax.ShapeDtypeStruct((m // d4, d4 * wq), BF16),
            jax.ShapeDtypeStruct((m // d16, d16 * wq), BF16),
            jax.ShapeDtypeStruct((m, n_rwkv * wq), BF16),
            jax.ShapeDtypeStruct((m, n_gate * wq), BF16),
        ],
        scratch_shapes=[pltpu.VMEM((tm, d), BF16), pltpu.VMEM((wq // LANES, tm, LANES), F32)],
        compiler_params=_cparams(("parallel", "arbitrary")),
        name="proj",
    )(x2, g.reshape(1, d), w)


def _band_attn_kernel(*refs, nq, nkv, has_sink, want_lse):
    it = iter(refs)
    q_ref, kp_ref, kc_ref, vp_ref, vc_ref, bias_ref = (next(it) for _ in range(6))
    sink_ref = next(it) if has_sink else None
    o_ref = next(it)
    lse_ref = next(it) if want_lse else None

    nqb = q_ref.shape[0] // BLK
    first = pl.program_id(2) == 0
    q = q_ref[...] * jnp.asarray(HEAD_DIM ** -0.5, BF16)
    k = jnp.concatenate([kp_ref[...], kc_ref[...]], axis=0)
    v = jnp.concatenate([vp_ref[...], vc_ref[...]], axis=0)
    col = lax.broadcasted_iota(jnp.int32, (BLK, 2 * BLK), 1)
    edge = jnp.where(col < BLK, jnp.where(first, NEG_BIG, 0.0), 0.0)
    rep = nq // nkv
    units = [(i, h) for i in range(nqb) for h in range(nq)]
    hd = lambda t, i: t[:, i * HEAD_DIM:(i + 1) * HEAD_DIM]
    qs = lambda i, h: hd(q[i * BLK:(i + 1) * BLK], h)
    win = lambda t, i, h: hd(t[i * BLK:(i + 2) * BLK], h // rep)
    s = [_dot_nt(qs(i, h), win(k, i, h)) + bias_ref[h] + (edge if i == 0 else 0.0) for i, h in units]
    m = [jnp.max(t, axis=-1, keepdims=True) for t in s]
    if has_sink:
        m = [jnp.maximum(m[u], sink_ref[h]) for u, (i, h) in enumerate(units)]
    p = [jnp.exp(s[u] - m[u]).astype(BF16) for u in range(len(units))]
    ones = jnp.ones((2 * BLK, LANES), BF16)
    l = [_dot(t, ones)[:, :HEAD_DIM] for t in p]
    denom = [l[u] + jnp.exp(sink_ref[h] - m[u]) for u, (i, h) in enumerate(units)] if has_sink else l
    o = [_dot(p[u], win(v, i, h)) / denom[u] for u, (i, h) in enumerate(units)]
    rows = lambda parts: jnp.concatenate(
        [jnp.concatenate(parts[i * nq:(i + 1) * nq], axis=-1) for i in range(nqb)], axis=0)
    o_ref[...] = rows(o).astype(o_ref.dtype)
    if want_lse:
        lse_ref[...] = rows([m[u] + jnp.log(l[u]) for u in range(len(units))])


def _band_attention(pv, bias, sink, *, dil, q_blk, k_blk, v_blk, q_w, kv_w, nq, nkv, want_lse, out_dtype, name,
                    nqb):
    b, lf, nd = pv.shape
    n = nd // dil
    nb = lf // (BLK * nqb)
    qpr, kpr = n // q_w, n // kv_w
    has_sink = sink is not None
    prev = lambda j: jnp.maximum(j * nqb - 1, 0)

    in_specs = [
        pl.BlockSpec((None, nqb * BLK, q_w), lambda bi, r, j: (bi, j, r * qpr + q_blk)),
        pl.BlockSpec((None, BLK, kv_w), lambda bi, r, j: (bi, prev(j), r * kpr + k_blk)),
        pl.BlockSpec((None, nqb * BLK, kv_w), lambda bi, r, j: (bi, j, r * kpr + k_blk)),
        pl.BlockSpec((None, BLK, kv_w), lambda bi, r, j: (bi, prev(j), r * kpr + v_blk)),
        pl.BlockSpec((None, nqb * BLK, kv_w), lambda bi, r, j: (bi, j, r * kpr + v_blk)),
        pl.BlockSpec((nq, BLK, 2 * BLK), lambda bi, r, j: (0, 0, 0)),
    ]
    args = [pv, pv, pv, pv, pv, bias]
    if has_sink:
        in_specs.append(pl.BlockSpec(memory_space=pltpu.SMEM))
        args.append(sink)
    ow = nq * HEAD_DIM
    out_spec = pl.BlockSpec((None, nqb * BLK, ow), lambda bi, r, j: (bi, j, r))
    out_shape = jax.ShapeDtypeStruct((b, lf, dil * ow), out_dtype)
    if want_lse:
        out_specs = [out_spec, out_spec]
        out_shapes = [out_shape, jax.ShapeDtypeStruct((b, lf, dil * ow), F32)]
    else:
        out_specs, out_shapes = out_spec, out_shape
    return pl.pallas_call(
        functools.partial(_band_attn_kernel, nq=nq, nkv=nkv, has_sink=has_sink, want_lse=want_lse),
        grid=(b, dil, nb),
        in_specs=in_specs,
        out_specs=out_specs,
        out_shape=out_shapes,
        compiler_params=_cparams(("parallel", "parallel", "arbitrary")),
        name=name,
    )(*args)


def _rwkv_prep_kernel(ph_ref, p_ref, mu_ref, w0_ref, wup_ref, a0_ref, aup_ref, gup_ref, kk_ref, ka_ref,
                      rk_ref, e_ref, r_o, lw_o, k_o, v_o, kk_o, b_o, g_o, bonus_o, *, tiles_per_seq):
    tm = p_ref.shape[0]
    p = p_ref[...].astype(F32)
    seq_start = pl.program_id(0) % tiles_per_seq == 0
    last = ph_ref[...].astype(F32)[HALO - 1:HALO, :]
    last = jnp.where(seq_start, 0.0, last)
    row = lax.broadcasted_iota(jnp.int32, (tm, 1), 0)
    prev = jnp.where(row == 0, last, pltpu.roll(p, 1, 0))
    pf = p + (prev - p) * mu_ref[...]

    r = pf[:, 0:B_WIDTH]
    k = pf[:, B_WIDTH:2 * B_WIDTH]
    v = pf[:, 2 * B_WIDTH:3 * B_WIDTH]
    wd = pf[:, OFF_WD:OFF_WD + LORA_PAD]
    ad = pf[:, OFF_AD:OFF_AD + LORA_PAD]
    gd = pf[:, OFF_GD:OFF_GD + LORA_GATE]

    z = w0_ref[...] + _dot3(jnp.tanh(wd), wup_ref[...])
    nz = -z
    softplus = jnp.maximum(nz, 0.0) + jnp.log(1.0 + jnp.exp(-jnp.abs(nz)))
    w = -softplus - 0.5
    lw_o[...] = -jnp.exp(w)
    a = jax.nn.sigmoid(a0_ref[...] + _dot3(ad, aup_ref[...]))
    g_o[...] = _dot3(jax.nn.sigmoid(gd), gup_ref[...])

    e = e_ref[...]
    kk = k * kk_ref[...]
    nrm = jnp.sqrt(_head_sums(kk * kk, e))
    kk = kk / jnp.maximum(nrm, 1e-12)
    k2 = k * (1.0 + (a - 1.0) * ka_ref[...])
    r_o[...] = r
    k_o[...] = k2
    v_o[...] = v
    kk_o[...] = kk
    b_o[...] = kk * a
    bonus_o[...] = _head_sums(r * k2 * rk_ref[...], e) * v


def _rwkv_prep(pb2, seq, mu, w0, wup, a0, aup, gup, k_k, k_a, r_k, e_bd, tm):
    m, n = pb2.shape
    tps = seq // tm
    row = lambda a: a.reshape(1, -1)
    full = lambda a: pl.BlockSpec(a.shape, lambda i: (0,) * a.ndim)
    args = [pb2, pb2, row(mu), row(w0), wup, row(a0), aup, gup, row(k_k), row(k_a), row(r_k), e_bd]
    in_specs = [
        pl.BlockSpec((HALO, n), lambda i: (jnp.maximum(i * (tm // HALO) - 1, 0), 0)),
        pl.BlockSpec((tm, n), lambda i: (i, 0)),
    ] + [full(a) for a in args[2:]]
    o_spec = pl.BlockSpec((tm, B_WIDTH), lambda i: (i, 0))
    o_shape = jax.ShapeDtypeStruct((m, B_WIDTH), F32)
    return pl.pallas_call(
        functools.partial(_rwkv_prep_kernel, tiles_per_seq=tps),
        grid=(m // tm,),
        in_specs=in_specs,
        out_specs=[o_spec] * 8,
        out_shape=[o_shape] * 8,
        compiler_params=_cparams(("parallel",)),
        name="rwkv_prep",
    )(*args)


def _wkv_kernel(r_ref, lw_ref, k_ref, v_ref, kk_ref, b_ref, y_ref, st_ref):
    c = WKV_CHUNK
    n_chunks = r_ref.shape[0] // c

    @pl.when(pl.program_id(1) == 0)
    def _():
        st_ref[...] = jnp.zeros_like(st_ref)

    rowi = lax.broadcasted_iota(jnp.int32, (c, c), 0)
    coli = lax.broadcasted_iota(jnp.int32, (c, c), 1)
    incl = rowi >= coli
    strict = rowi > coli
    row2 = lax.broadcasted_iota(jnp.int32, (2 * c, c), 0)
    col2 = lax.broadcasted_iota(jnp.int32, (2 * c, c), 1)
    tri2 = row2 - jnp.where(row2 < c, 1, c) >= col2
    tri = jnp.where(incl, 1.0, 0.0).astype(BF16)
    eye = jnp.where(rowi == coli, 1.0, 0.0)

    nu = WKV_UNROLL
    heads = range(B_HEADS)
    hsl = [slice(h * HEAD_DIM, (h + 1) * HEAD_DIM) for h in heads]
    units = [(u, h) for u in range(nu) for h in heads]

    def chunks(ci, carry):
        sls = [pl.ds(pl.multiple_of((ci * nu + u) * c, c), c) for u in range(nu)]
        lw = [lw_ref[sl, :] for sl in sls]
        cum = [_dot_exact_lhs(tri, t) for t in lw]
        tot = [t[c - 1:c, :] for t in cum]
        p_inv = [jnp.exp(-t) for t in cum]
        p_rest = [jnp.exp(tot[u] - cum[u]) for u in range(nu)]
        p_tot = [jnp.exp(t) for t in tot]
        rh_all = [(r_ref[sls[u], :] * jnp.exp(cum[u])).astype(BF16) for u in range(nu)]
        ah_all = [(-kk_ref[sls[u], :] * jnp.exp(cum[u] - lw[u])).astype(BF16) for u in range(nu)]
        b_all = [b_ref[sl, :] for sl in sls]
        k_all = [k_ref[sl, :] for sl in sls]
        bh_all = [(b_all[u] * p_inv[u]).astype(BF16) for u in range(nu)]
        kh_all = [(k_all[u] * p_inv[u]).astype(BF16) for u in range(nu)]
        bt_all = [(b_all[u] * p_rest[u]).astype(BF16) for u in range(nu)]
        kt_all = [(k_all[u] * p_rest[u]).astype(BF16) for u in range(nu)]
        v_all = [v_ref[sl, :] for sl in sls]

        ah = [ah_all[u][:, hsl[h]] for u, h in units]
        rh = [rh_all[u][:, hsl[h]] for u, h in units]
        vf = [v_all[u][:, hsl[h]] for u, h in units]
        vb = [t.astype(BF16) for t in vf]
        n = range(len(units))
        ar = [jnp.concatenate([ah[i], rh[i]], axis=0) for i in n]
        gb = [_dot_nt(ar[i], bh_all[u][:, hsl[h]]) for i, (u, h) in enumerate(units)]
        gk = [_dot_nt(ar[i], kh_all[u][:, hsl[h]]) for i, (u, h) in enumerate(units)]
        a_rb = [jnp.where(incl, t[c:], 0.0).astype(BF16) for t in gb]
        akrk = [jnp.where(tri2, t, 0.0).astype(BF16) for t in gk]
        x0 = [jnp.where(strict, t[:c], 0.0) for t in gb]
        t = [eye + xi for xi in x0]
        xb = [xi.astype(BF16) for xi in x0]
        x = [_dot(xi, xi) for xi in xb]
        for _ in range(int(math.log2(c)) - 2):
            xt = [_dot(jnp.concatenate([x[i], t[i]], axis=0).astype(BF16), x[i].astype(BF16)) for i in n]
            x = [p[:c] for p in xt]
            t = [t[i] + xt[i][c:] for i in n]
        t = [t[i] + _dot(t[i].astype(BF16), x[i].astype(BF16)) for i in n]
        tb = [ti.astype(BF16) for ti in t]
        w1y2 = [_dot(akrk[i], vb[i]) for i in n]
        w1 = [p[:c] for p in w1y2]
        y2 = [p[c:] for p in w1y2]
        n2 = [_dot(vf[i].T.astype(BF16), kt_all[u][:, hsl[h]]) for i, (u, h) in enumerate(units)]
        a2 = [_dot(tb[i], ah[i]).astype(BF16) for i in n]
        u1t = [_dot(tb[i], w1[i].astype(BF16)).T for i in n]
        st = [st_ref[h] for h in heads]
        for u in range(nu):
            o = u * B_HEADS
            s0b = [st[h].astype(BF16) for h in heads]
            utb = [(_dot_nt(s0b[h], a2[o + h]) + u1t[o + h]).astype(BF16) for h in heads]
            ys = [_dot_nt(rh[o + h], s0b[h]) + _dot_nt(a_rb[o + h], utb[h]) + y2[o + h] for h in heads]
            st = [st[h] * p_tot[u][:, hsl[h]] + _dot(utb[h], bt_all[u][:, hsl[h]]) + n2[o + h] for h in heads]
            y_ref[sls[u], :] = jnp.concatenate(ys, axis=-1)
        st_ref[...] = jnp.stack(st, axis=0)
        return carry

    lax.fori_loop(0, n_chunks // nu, chunks, 0)


def _wkv_scan(r, lw, k, v, kk, b, ts):
    bsz, s, w = r.shape
    spec = pl.BlockSpec((None, ts, w), lambda bi, j: (bi, j, 0))
    return pl.pallas_call(
        _wkv_kernel,
        grid=(bsz, s // ts),
        in_specs=[spec] * 6,
        out_specs=spec,
        out_shape=jax.ShapeDtypeStruct((bsz, s, w), F32),
        scratch_shapes=[pltpu.VMEM((B_HEADS, HEAD_DIM, HEAD_DIM), F32)],
        compiler_params=_cparams(("parallel", "arbitrary")),
        name="wkv_scan",
    )(r, lw, k, v, kk, b)


def _merge_kernel(ya_ref, o1_ref, o2_ref, o3_ref, l1_ref, l2_ref, l3_ref, yr_ref, g_ref, bonus_ref,
                  gates_ref, lng_ref, lnb_ref, e_ref, pa_ref, pb_ref, pc_ref, out_ref, unf_ref):
    tm, d = out_ref.shape

    def unfold(ref, slot, dil):
        nc = C_OUT // LANES
        for r in range(dil):
            for c in range(nc):
                unf_ref[slot * nc + c, pl.ds(r, tm // dil, stride=dil), :] = (
                    ref[:, r * C_OUT + c * LANES:r * C_OUT + (c + 1) * LANES])
        return jnp.concatenate([unf_ref[slot * nc + c] for c in range(nc)], axis=-1)

    d2, d3 = C_GROUPS[1][1], C_GROUPS[2][1]
    o1, l1 = o1_ref[...], l1_ref[...]
    o2, l2 = unfold(o2_ref, 0, d2), unfold(l2_ref, 1, d2)
    o3, l3 = unfold(o3_ref, 2, d3), unfold(l3_ref, 3, d3)
    m = jnp.maximum(jnp.maximum(l1, l2), l3)
    e1, e2, e3 = jnp.exp(l1 - m), jnp.exp(l2 - m), jnp.exp(l3 - m)
    yc = (e1 * o1 + e2 * o2 + e3 * o3) / (e1 + e2 + e3)

    e = e_ref[...]
    y = yr_ref[...]
    mean = _head_sums(y, e, exact=True) * (1.0 / HEAD_DIM)
    dv = y - mean
    var = _head_sums(dv * dv, e) * (1.0 / HEAD_DIM)
    yb = dv * lax.rsqrt(var + B_GN_EPS) * lng_ref[...] + lnb_ref[...] + bonus_ref[...]
    yb = yb * g_ref[...]

    gates = gates_ref[...]
    merged = (gates[:, 0:d].astype(F32) * _dot(ya_ref[...], pa_ref[...])
              + gates[:, d:2 * d].astype(F32) * _dot(yb.astype(BF16), pb_ref[...])
              + gates[:, 2 * d:3 * d].astype(F32) * _dot(yc.astype(BF16), pc_ref[...]))
    out_ref[...] = merged.astype(BF16)


def _merge(ya, o1, o2, o3, l1, l2, l3, yr, g, bonus, gates, lng, lnb, e_bd, pa, pb, pc, tm):
    m, d = ya.shape[0], pa.shape[1]
    tok = lambda a: pl.BlockSpec((tm * a.shape[0] // m, a.shape[1]), lambda i: (i, 0))
    full = lambda a: pl.BlockSpec(a.shape, lambda i: (0,) * a.ndim, pipeline_mode=pl.Buffered(1))
    toks = [ya, o1, o2, o3, l1, l2, l3, yr, g, bonus, gates]
    consts = [lng.reshape(1, -1), lnb.reshape(1, -1), e_bd, pa, pb, pc]
    return pl.pallas_call(
        _merge_kernel,
        grid=(m // tm,),
        in_specs=[tok(a) for a in toks] + [full(a) for a in consts],
        out_specs=pl.BlockSpec((tm, d), lambda i: (i, 0)),
        out_shape=jax.ShapeDtypeStruct((m, d), BF16),
        scratch_shapes=[pltpu.VMEM((4 * C_OUT // LANES, tm, LANES), F32)],
        compiler_params=_cparams(("parallel",)),
        name="merge",
    )(*toks, *consts)


def _out_proj_kernel(x_ref, m_ref, w_ref, o_ref):
    o_ref[...] = x_ref[...] + _dot(m_ref[...], w_ref[...])


def _out_proj(x2, merged, wo, tm):
    m, d = x2.shape
    tok = pl.BlockSpec((tm, d), lambda i: (i, 0))
    return pl.pallas_call(
        _out_proj_kernel,
        grid=(m // tm,),
        in_specs=[tok, tok, pl.BlockSpec(wo.shape, lambda i: (0, 0), pipeline_mode=pl.Buffered(1))],
        out_specs=tok,
        out_shape=jax.ShapeDtypeStruct((m, d), F32),
        compiler_params=_cparams(("parallel",)),
        name="out_proj",
    )(x2, merged, wo)


def _ffn_kernel(xh_ref, x_ref, g_ref, wg_ref, wv_ref, cw_ref, wd_ref, fg_ref, o_ref, h_ref, *,
                tiles_per_seq, final_norm):
    j = pl.program_id(1)
    tm = x_ref.shape[0]

    @pl.when(j == 0)
    def _():
        x = x_ref[...]
        g = g_ref[...]
        h_ref[HALO:, :] = _rms(x, g).astype(BF16)
        seq_start = pl.program_id(0) % tiles_per_seq == 0
        halo = _rms(xh_ref[...], g)
        h_ref[0:HALO, :] = jnp.where(seq_start, 0.0, halo).astype(BF16)
        o_ref[...] = x

    h = h_ref[...]
    gext = _dot(h, wg_ref[...])
    val = _dot(h[HALO:], wv_ref[...])
    cw = cw_ref[...]
    gate = (cw[0:1, :] * pltpu.roll(gext, 2, 0)[HALO:]
            + cw[1:2, :] * pltpu.roll(gext, 1, 0)[HALO:]
            + cw[2:3, :] * gext[HALO:])
    hid = gate * jax.nn.sigmoid(gate) * val
    o_ref[...] += _dot(hid.astype(BF16), wd_ref[...])

    if final_norm:
        @pl.when(j == pl.num_programs(1) - 1)
        def _():
            o_ref[...] = _rms(o_ref[...], fg_ref[...])


def _ffn(x2, seq, g, wg, wv, cw, wd, fg, final_norm, tm, tf):
    m, d = x2.shape
    ffp = wg.shape[1]
    tps = seq // tm
    return pl.pallas_call(
        functools.partial(_ffn_kernel, tiles_per_seq=tps, final_norm=final_norm),
        grid=(m // tm, ffp // tf),
        in_specs=[
            pl.BlockSpec((HALO, d), lambda i, j: (jnp.maximum(i * (tm // HALO) - 1, 0), 0)),
            pl.BlockSpec((tm, d), lambda i, j: (i, 0)),
            pl.BlockSpec((1, d), lambda i, j: (0, 0)),
            pl.BlockSpec((d, tf), lambda i, j: (0, j)),
            pl.BlockSpec((d, tf), lambda i, j: (0, j)),
            pl.BlockSpec((3, tf), lambda i, j: (0, j)),
            pl.BlockSpec((tf, d), lambda i, j: (j, 0)),
            pl.BlockSpec((1, d), lambda i, j: (0, 0)),
        ],
        out_specs=pl.BlockSpec((tm, d), lambda i, j: (i, 0)),
        out_shape=jax.ShapeDtypeStruct((m, d), F32),
        scratch_shapes=[pltpu.VMEM((HALO + tm, d), BF16)],
        compiler_params=_cparams(("parallel", "arbitrary")),
        name="conv_ffn",
    )(x2, x2, g.reshape(1, d), wg, wv, cw, wd, fg.reshape(1, d))


def _t5_bucket(dist):
    small = dist < MAX_EXACT
    nf = jnp.maximum(dist, 1).astype(F32)
    large = MAX_EXACT + (jnp.log(nf / MAX_EXACT) / math.log(REL_MAX_DIST / MAX_EXACT)
                         * (N_BUCKETS - MAX_EXACT)).astype(jnp.int32)
    return jnp.where(small, dist, jnp.minimum(large, N_BUCKETS - 1))


def _band_bias(table, dilation, max_steps):
    i = jnp.arange(BLK)[:, None]
    j = jnp.arange(2 * BLK)[None, :]
    off = i + BLK - j
    onehot = jax.nn.one_hot(_t5_bucket(jnp.maximum(off, 0) * dilation), N_BUCKETS, dtype=F32)
    bias = jnp.einsum("ijb,bh->hij", onehot, table.astype(F32), precision=lax.Precision.HIGHEST)
    valid = (off >= 0) & (off <= max_steps)
    return jnp.where(valid[None], bias, NEG_BIG)


def _pad_cols(w, n):
    return jnp.pad(w, [(0, 0)] * (w.ndim - 1) + [(0, n - w.shape[-1])])


def _pad_rows(w, n):
    return jnp.pad(w, [(0, 0)] * (w.ndim - 2) + [(0, n - w.shape[-2]), (0, 0)])


def kernel(x, rel_bias, norm1_g, w_in, attn_sinks, rwkv_mu, rwkv_w0, rwkv_w_up, rwkv_a0, rwkv_a_up, rwkv_g_up,
           rwkv_k_k, rwkv_k_a, rwkv_r_k, rwkv_lnx_g, rwkv_lnx_b, proj_a, proj_b, proj_c, w_out, norm2_g,
           ffn_up, ffn_conv, ffn_down, final_g):
    bsz, seq, d = x.shape
    depth = w_in.shape[0]
    m = bsz * seq
    d_ff = ffn_conv.shape[-1]
    ffp = -(-d_ff // D_FF_PAD_TO) * D_FF_PAD_TO

    tm_proj = min(1024, seq)
    tm_tok = min(512, seq)
    tm_ffn = min(1024, seq)
    ts_wkv = min(512, seq)

    na = A_HQ
    bias_a = _band_bias(rel_bias[:, :na], 1, BLK - 1)
    bias_c = [_band_bias(rel_bias[:, na + gi * C_HG:na + (gi + 1) * C_HG], dil, win // dil)
              for gi, (win, dil) in enumerate(C_GROUPS)]

    q_end = (A_HQ + 2 * A_HKV) * HEAD_DIM
    b0 = q_end
    b_r_end = b0 + 3 * B_WIDTH
    b_wd_end = b_r_end + LORA_DECAY
    b_ad_end = b_wd_end + LORA_ICLR
    b_end = b_ad_end + LORA_GATE
    c_end = b_end + 3 * C_WIDTH

    def rwkv_cols(t):
        parts = [t[..., b0:b_r_end], _pad_cols(t[..., b_r_end:b_wd_end], LORA_PAD),
                 _pad_cols(t[..., b_wd_end:b_ad_end], LORA_PAD), t[..., b_ad_end:b_end]]
        return _pad_cols(jnp.concatenate(parts, axis=-1), N_RWKV)

    w_in_b = lax.optimization_barrier(w_in.astype(BF16))
    qkv_c = [w_in_b[..., b_end + t * C_WIDTH + gi * C_OUT:b_end + t * C_WIDTH + (gi + 1) * C_OUT]
             for gi in range(len(C_GROUPS)) for t in range(3)]
    w_all = jnp.concatenate([w_in_b[..., :q_end]] + qkv_c + [rwkv_cols(w_in_b), w_in_b[..., c_end:]], axis=-1)
    proj_blk = q_end
    n_rwkv_blk = N_RWKV // proj_blk
    n_gate_blk = (w_in.shape[-1] - c_end) // proj_blk
    mu_pad = rwkv_cols(jnp.pad(rwkv_mu, ((0, 0), (b0, 0))))
    wup_pad = _pad_rows(rwkv_w_up, LORA_PAD)
    aup_pad = _pad_rows(rwkv_a_up, LORA_PAD)
    r_k = rwkv_r_k.reshape(depth, B_WIDTH)

    head_id = jnp.arange(B_WIDTH) // HEAD_DIM
    e_bd = (head_id[:, None] == head_id[None, :]).astype(BF16)

    pa, pb, pc, wo = (t.astype(BF16) for t in (proj_a, proj_b, proj_c, w_out))
    wg = _pad_cols(ffn_up[..., :d_ff], ffp).astype(BF16)
    wv = _pad_cols(ffn_up[..., d_ff:], ffp).astype(BF16)
    cw = _pad_cols(ffn_conv, ffp)
    wd = _pad_rows(ffn_down, ffp).astype(BF16)

    x2 = x.reshape(m, d)
    for l in range(depth):
        p_swa, *p_dil, p_rwkv, gates = _proj(x2, norm1_g[l], w_all[l], n_rwkv_blk, n_gate_blk, tm_proj)

        kvw = A_HKV * HEAD_DIM
        y_a = _band_attention(p_swa.reshape(bsz, seq, -1), bias_a, attn_sinks[l], dil=1, q_blk=0,
                              k_blk=(A_HQ * HEAD_DIM) // kvw, v_blk=(A_HQ * HEAD_DIM) // kvw + 1,
                              q_w=A_HQ * HEAD_DIM, kv_w=kvw, nq=A_HQ, nkv=A_HKV, want_lse=False,
                              out_dtype=BF16, name="attn_swa", nqb=4)
        oc, lc = [], []
        for gi, (win, dil) in enumerate(C_GROUPS):
            pv = p_dil[gi].reshape(bsz, seq // dil, -1)
            o, ls = _band_attention(pv, bias_c[gi], None, dil=dil, q_blk=0, k_blk=1, v_blk=2, q_w=C_OUT,
                                    kv_w=C_OUT, nq=C_HG, nkv=C_HG, want_lse=True, out_dtype=F32,
                                    name=f"attn_dil{dil}", nqb=min(4, seq // dil // BLK))
            oc.append(o.reshape(m // dil, dil * C_OUT))
            lc.append(ls.reshape(m // dil, dil * C_OUT))

        r, lw, k2, v, kk, bb, g, bonus = _rwkv_prep(
            p_rwkv, seq, mu_pad[l], rwkv_w0[l], wup_pad[l], rwkv_a0[l], aup_pad[l], rwkv_g_up[l],
            rwkv_k_k[l], rwkv_k_a[l], r_k[l], e_bd, tm_tok)
        sh = lambda t: t.reshape(bsz, seq, B_WIDTH)
        y_raw = _wkv_scan(sh(r), sh(lw), sh(k2), sh(v), sh(kk), sh(bb), ts_wkv).reshape(m, B_WIDTH)

        merged = _merge(y_a.reshape(m, A_HQ * HEAD_DIM), oc[0], oc[1], oc[2], lc[0], lc[1], lc[2], y_raw, g,
                        bonus, gates, rwkv_lnx_g[l], rwkv_lnx_b[l], e_bd, pa[l], pb[l], pc[l], tm_tok)
        x2 = _out_proj(x2, merged, wo[l], tm_proj)
        x2 = _ffn(x2, seq, norm2_g[l], wg[l], wv[l], cw[l], wd[l], final_g, l == depth - 1, tm_ffn, 512)
    return x2.reshape(bsz, seq, d)
```

```python
import functools
import math

import jax
import jax.numpy as jnp
import numpy as np
from jax import lax
from jax.experimental import pallas as pl
from jax.experimental.pallas import tpu as pltpu

F32 = jnp.float32
BF16 = jnp.bfloat16

HEAD_DIM = 64
LANES = 128
BLK = 128
NORM_EPS = 1e-5
A_HQ, A_HKV = 8, 2
B_HEADS = 12
B_WIDTH = B_HEADS * HEAD_DIM
LORA_DECAY, LORA_ICLR, LORA_GATE = 96, 96, 256
LORA_PAD = 128
B_GN_EPS = 64e-5
C_GROUPS = ((128, 1), (512, 4), (2048, 16))
C_HG = 4
C_WIDTH = C_HG * len(C_GROUPS) * HEAD_DIM
C_OUT = C_HG * HEAD_DIM
N_BUCKETS, MAX_EXACT, REL_MAX_DIST = 32, 16, 2048
D_FF_PAD_TO = 512
NEG_BIG = -1e30

N_ATTN = A_HQ * HEAD_DIM + 2 * A_HKV * HEAD_DIM + 3 * C_WIDTH
N_RWKV = 3 * B_WIDTH + 2 * LORA_PAD + LORA_GATE + 256
OFF_WD = 3 * B_WIDTH
OFF_AD = OFF_WD + LORA_PAD
OFF_GD = OFF_AD + LORA_PAD
WKV_CHUNK = 64
WKV_UNROLL = 4
HALO = 16

VMEM_LIMIT = 56 * 1024 * 1024


def _cparams(sem):
    return pltpu.CompilerParams(dimension_semantics=sem, vmem_limit_bytes=VMEM_LIMIT)


def _dot(a, b):
    return jnp.dot(a, b, preferred_element_type=F32)


def _dot_nt(a, b):
    return lax.dot_general(a, b, (((1,), (1,)), ((), ())), preferred_element_type=F32)


def _split2(x):
    hi = x.astype(BF16)
    lo = (x - hi.astype(F32)).astype(BF16)
    return hi, lo


def _head_sums(a, e_bf16, exact=False):
    if not exact:
        return _dot(a.astype(BF16), e_bf16)
    h, l = _split2(a)
    return _dot(h, e_bf16) + _dot(l, e_bf16)


def _dot_exact_lhs(a_bf16, b):
    h, l = _split2(b)
    return _dot(a_bf16, h) + _dot(a_bf16, l)


def _dot3(a, b):
    ah, al = _split2(a)
    bh, bl = _split2(b)
    return _dot(ah, bh) + _dot(ah, bl) + _dot(al, bh)


def _rms(x, g):
    ms = jnp.mean(x * x, axis=-1, keepdims=True)
    return x * lax.rsqrt(ms + NORM_EPS) * g


def _proj_kernel(x_ref, g_ref, w_ref, pa_ref, p1_ref, p4_ref, p16_ref, pr_ref, pg_ref, h_ref, acc_ref, *,
                 n_rwkv):
    j = pl.program_id(1)
    tm = x_ref.shape[0]
    w = w_ref.shape[1]
    n_attn = 1 + len(C_GROUPS)

    @pl.when(j == 0)
    def _():
        h_ref[...] = _rms(x_ref[...], g_ref[...]).astype(BF16)

    def acc():
        return _dot(h_ref[...], w_ref[...])

    @pl.when(j == 0)
    def _():
        pa_ref[...] = acc().astype(BF16)

    @pl.when(j == 1)
    def _():
        p1_ref[...] = acc().astype(BF16)

    for jj, dil, ref in ((2, C_GROUPS[1][1], p4_ref), (3, C_GROUPS[2][1], p16_ref)):
        @pl.when(j == jj)
        def _(dil=dil, ref=ref):
            a = acc()
            for c in range(w // LANES):
                acc_ref[c] = a[:, c * LANES:(c + 1) * LANES]
            for r in range(dil):
                for c in range(w // LANES):
                    ref[:, r * w + c * LANES:r * w + (c + 1) * LANES] = (
                        acc_ref[c, pl.ds(r, tm // dil, stride=dil), :].astype(BF16))

    @pl.when(jnp.logical_and(j >= n_attn, j < n_attn + n_rwkv))
    def _():
        pr_ref[...] = acc().astype(BF16)

    @pl.when(j >= n_attn + n_rwkv)
    def _():
        pg_ref[...] = jax.nn.sigmoid(acc()).astype(BF16)


def _proj(x2, g, w, n_rwkv, n_gate, tm):
    m, d = x2.shape
    n_attn = 1 + len(C_GROUPS)
    nblk = n_attn + n_rwkv + n_gate
    wq = w.shape[1] // nblk
    d4, d16 = C_GROUPS[1][1], C_GROUPS[2][1]
    return pl.pallas_call(
        functools.partial(_proj_kernel, n_rwkv=n_rwkv),
        grid=(m // tm, nblk),
        in_specs=[
            pl.BlockSpec((tm, d), lambda i, j: (i, 0)),
            pl.BlockSpec((1, d), lambda i, j: (0, 0)),
            pl.BlockSpec((d, wq), lambda i, j: (0, j)),
        ],
        out_specs=[
            pl.BlockSpec((tm, wq), lambda i, j: (i, 0)),
            pl.BlockSpec((tm, wq), lambda i, j: (i, 0)),
            pl.BlockSpec((tm // d4, d4 * wq), lambda i, j: (i, 0)),
            pl.BlockSpec((tm // d16, d16 * wq), lambda i, j: (i, 0)),
            pl.BlockSpec((tm, wq), lambda i, j: (i, jnp.clip(j - n_attn, 0, n_rwkv - 1))),
            pl.BlockSpec((tm, wq), lambda i, j: (i, jnp.clip(j - n_attn - n_rwkv, 0, n_gate - 1))),
        ],
        out_shape=[
            jax.ShapeDtypeStruct((m, wq), BF16),
            jax.ShapeDtypeStruct((m, wq), BF16),
            jax.ShapeDtypeStruct((m // d4, d4 * wq), BF16),
            jax.ShapeDtypeStruct((m // d16, d16 * wq), BF16),
            jax.ShapeDtypeStruct((m, n_rwkv * wq), BF16),
            jax.ShapeDtypeStruct((m, n_gate * wq), BF16),
        ],
        scratch_shapes=[pltpu.VMEM((tm, d), BF16), pltpu.VMEM((wq // LANES, tm, LANES), F32)],
        compiler_params=_cparams(("parallel", "arbitrary")),
        name="proj",
    )(x2, g.reshape(1, d), w)


def _band_attn_kernel(*refs, nq, nkv, has_sink, want_lse):
    it = iter(refs)
    q_ref, kp_ref, kc_ref, vp_ref, vc_ref, bias_ref = (next(it) for _ in range(6))
    sink_ref = next(it) if has_sink else None
    o_ref = next(it)
    lse_ref = next(it) if want_lse else None

    nqb = q_ref.shape[0] // BLK
    first = pl.program_id(2) == 0
    q = q_ref[...] * jnp.asarray(HEAD_DIM ** -0.5, BF16)
    k = jnp.concatenate([kp_ref[...], kc_ref[...]], axis=0)
    v = jnp.concatenate([vp_ref[...], vc_ref[...]], axis=0)
    col = lax.broadcasted_iota(jnp.int32, (BLK, 2 * BLK), 1)
    edge = jnp.where(col < BLK, jnp.where(first, NEG_BIG, 0.0), 0.0)
    rep = nq // nkv
    units = [(i, h) for i in range(nqb) for h in range(nq)]
    hd = lambda t, i: t[:, i * HEAD_DIM:(i + 1) * HEAD_DIM]
    qs = lambda i, h: hd(q[i * BLK:(i + 1) * BLK], h)
    win = lambda t, i, h: hd(t[i * BLK:(i + 2) * BLK], h // rep)
    s = [_dot_nt(qs(i, h), win(k, i, h)) + bias_ref[h] for i, h in units]
    s = [s[u] + edge if i == 0 else s[u] for u, (i, h) in enumerate(units)]
    m = [jnp.max(t, axis=-1, keepdims=True) for t in s]
    if has_sink:
        m = [jnp.maximum(m[u], sink_ref[h]) for u, (i, h) in enumerate(units)]
    p = [jnp.exp(s[u] - m[u]).astype(BF16) for u in range(len(units))]
    ones = jnp.ones((2 * BLK, LANES), BF16)
    l = [_dot(t, ones)[:, :HEAD_DIM] for t in p]
    denom = [l[u] + jnp.exp(sink_ref[h] - m[u]) for u, (i, h) in enumerate(units)] if has_sink else l
    o = [_dot(p[u], win(v, i, h)) / denom[u] for u, (i, h) in enumerate(units)]
    rows = lambda parts: jnp.concatenate(
        [jnp.concatenate(parts[i * nq:(i + 1) * nq], axis=-1) for i in range(nqb)], axis=0)
    o_ref[...] = rows(o).astype(o_ref.dtype)
    if want_lse:
        lse_ref[...] = rows([m[u] + jnp.log(l[u]) for u in range(len(units))])


def _band_attention(pv, bias, sink, *, dil, q_blk, k_blk, v_blk, q_w, kv_w, nq, nkv, want_lse, out_dtype, name,
                    nqb):
    b, lf, nd = pv.shape
    n = nd // dil
    nb = lf // (BLK * nqb)
    qpr, kpr = n // q_w, n // kv_w
    has_sink = sink is not None
    prev = lambda j: jnp.maximum(j * nqb - 1, 0)

    in_specs = [
        pl.BlockSpec((None, nqb * BLK, q_w), lambda bi, r, j: (bi, j, r * qpr + q_blk)),
        pl.BlockSpec((None, BLK, kv_w), lambda bi, r, j: (bi, prev(j), r * kpr + k_blk)),
        pl.BlockSpec((None, nqb * BLK, kv_w), lambda bi, r, j: (bi, j, r * kpr + k_blk)),
        pl.BlockSpec((None, BLK, kv_w), lambda bi, r, j: (bi, prev(j), r * kpr + v_blk)),
        pl.BlockSpec((None, nqb * BLK, kv_w), lambda bi, r, j: (bi, j, r * kpr + v_blk)),
        pl.BlockSpec((nq, BLK, 2 * BLK), lambda bi, r, j: (0, 0, 0)),
    ]
    args = [pv, pv, pv, pv, pv, bias]
    if has_sink:
        in_specs.append(pl.BlockSpec(memory_space=pltpu.SMEM))
        args.append(sink)
    ow = nq * HEAD_DIM
    out_spec = pl.BlockSpec((None, nqb * BLK, ow), lambda bi, r, j: (bi, j, r))
    out_shape = jax.ShapeDtypeStruct((b, lf, dil * ow), out_dtype)
    if want_lse:
        out_specs = [out_spec, out_spec]
        out_shapes = [out_shape, jax.ShapeDtypeStruct((b, lf, dil * ow), F32)]
    else:
        out_specs, out_shapes = out_spec, out_shape
    return pl.pallas_call(
        functools.partial(_band_attn_kernel, nq=nq, nkv=nkv, has_sink=has_sink, want_lse=want_lse),
        grid=(b, dil, nb),
        in_specs=in_specs,
        out_specs=out_specs,
        out_shape=out_shapes,
        compiler_params=_cparams(("parallel", "parallel", "arbitrary")),
        name=name,
    )(*args)


def _rwkv_prep_kernel(ph_ref, p_ref, mu_ref, w0_ref, wup_ref, a0_ref, aup_ref, gup_ref, kk_ref, ka_ref,
                      rk_ref, e_ref, r_o, lw_o, k_o, v_o, kk_o, b_o, g_o, bonus_o, *, tiles_per_seq):
    tm = p_ref.shape[0]
    p = p_ref[...].astype(F32)
    seq_start = pl.program_id(0) % tiles_per_seq == 0
    last = ph_ref[...].astype(F32)[HALO - 1:HALO, :]
    last = jnp.where(seq_start, 0.0, last)
    row = lax.broadcasted_iota(jnp.int32, (tm, 1), 0)
    prev = jnp.where(row == 0, last, pltpu.roll(p, 1, 0))
    pf = p + (prev - p) * mu_ref[...]

    r = pf[:, 0:B_WIDTH]
    k = pf[:, B_WIDTH:2 * B_WIDTH]
    v = pf[:, 2 * B_WIDTH:3 * B_WIDTH]
    wd = pf[:, OFF_WD:OFF_WD + LORA_PAD]
    ad = pf[:, OFF_AD:OFF_AD + LORA_PAD]
    gd = pf[:, OFF_GD:OFF_GD + LORA_GATE]

    z = w0_ref[...] + _dot3(jnp.tanh(wd), wup_ref[...])
    nz = -z
    softplus = jnp.maximum(nz, 0.0) + jnp.log(1.0 + jnp.exp(-jnp.abs(nz)))
    w = -softplus - 0.5
    lw_o[...] = -jnp.exp(w)
    a = jax.nn.sigmoid(a0_ref[...] + _dot3(ad, aup_ref[...]))
    g_o[...] = _dot3(jax.nn.sigmoid(gd), gup_ref[...])

    e = e_ref[...]
    kk = k * kk_ref[...]
    nrm = jnp.sqrt(_head_sums(kk * kk, e))
    kk = kk / jnp.maximum(nrm, 1e-12)
    k2 = k * (1.0 + (a - 1.0) * ka_ref[...])
    r_o[...] = r
    k_o[...] = k2
    v_o[...] = v
    kk_o[...] = kk
    b_o[...] = kk * a
    bonus_o[...] = _head_sums(r * k2 * rk_ref[...], e) * v


def _rwkv_prep(pb2, seq, mu, w0, wup, a0, aup, gup, k_k, k_a, r_k, e_bd, tm):
    m, n = pb2.shape
    tps = seq // tm
    row = lambda a: a.reshape(1, -1)
    full = lambda a: pl.BlockSpec(a.shape, lambda i: (0,) * a.ndim)
    args = [pb2, pb2, row(mu), row(w0), wup, row(a0), aup, gup, row(k_k), row(k_a), row(r_k), e_bd]
    in_specs = [
        pl.BlockSpec((HALO, n), lambda i: (jnp.maximum(i * (tm // HALO) - 1, 0), 0)),
        pl.BlockSpec((tm, n), lambda i: (i, 0)),
    ] + [full(a) for a in args[2:]]
    o_spec = pl.BlockSpec((tm, B_WIDTH), lambda i: (i, 0))
    o_shape = jax.ShapeDtypeStruct((m, B_WIDTH), F32)
    return pl.pallas_call(
        functools.partial(_rwkv_prep_kernel, tiles_per_seq=tps),
        grid=(m // tm,),
        in_specs=in_specs,
        out_specs=[o_spec] * 8,
        out_shape=[o_shape] * 8,
        compiler_params=_cparams(("parallel",)),
        name="rwkv_prep",
    )(*args)


def _wkv_kernel(r_ref, lw_ref, k_ref, v_ref, kk_ref, b_ref, y_ref, st_ref):
    c = WKV_CHUNK
    n_chunks = r_ref.shape[0] // c

    @pl.when(pl.program_id(1) == 0)
    def _():
        st_ref[...] = jnp.zeros_like(st_ref)

    rowi = lax.broadcasted_iota(jnp.int32, (c, c), 0)
    coli = lax.broadcasted_iota(jnp.int32, (c, c), 1)
    incl = rowi >= coli
    strict = rowi > coli
    row2 = lax.broadcasted_iota(jnp.int32, (2 * c, c), 0)
    col2 = lax.broadcasted_iota(jnp.int32, (2 * c, c), 1)
    tri2 = row2 - jnp.where(row2 < c, 1, c) >= col2
    tri = jnp.where(incl, 1.0, 0.0).astype(BF16)
    eye = jnp.where(rowi == coli, 1.0, 0.0)

    nu = WKV_UNROLL
    heads = range(B_HEADS)
    hsl = [slice(h * HEAD_DIM, (h + 1) * HEAD_DIM) for h in heads]
    units = [(u, h) for u in range(nu) for h in heads]

    def chunks(ci, carry):
        sls = [pl.ds(pl.multiple_of((ci * nu + u) * c, c), c) for u in range(nu)]
        lw = [lw_ref[sl, :] for sl in sls]
        cum = [_dot_exact_lhs(tri, t) for t in lw]
        tot = [t[c - 1:c, :] for t in cum]
        p_inv = [jnp.exp(-t) for t in cum]
        p_rest = [jnp.exp(tot[u] - cum[u]) for u in range(nu)]
        p_tot = [jnp.exp(t) for t in tot]
        rh_all = [(r_ref[sls[u], :] * jnp.exp(cum[u])).astype(BF16) for u in range(nu)]
        ah_all = [(-kk_ref[sls[u], :] * jnp.exp(cum[u] - lw[u])).astype(BF16) for u in range(nu)]
        b_all = [b_ref[sl, :] for sl in sls]
        k_all = [k_ref[sl, :] for sl in sls]
        bh_all = [(b_all[u] * p_inv[u]).astype(BF16) for u in range(nu)]
        kh_all = [(k_all[u] * p_inv[u]).astype(BF16) for u in range(nu)]
        bt_all = [(b_all[u] * p_rest[u]).astype(BF16) for u in range(nu)]
        kt_all = [(k_all[u] * p_rest[u]).astype(BF16) for u in range(nu)]
        v_all = [v_ref[sl, :] for sl in sls]

        ah = [ah_all[u][:, hsl[h]] for u, h in units]
        rh = [rh_all[u][:, hsl[h]] for u, h in units]
        vf = [v_all[u][:, hsl[h]] for u, h in units]
        vb = [t.astype(BF16) for t in vf]
        n = range(len(units))
        ar = [jnp.concatenate([ah[i], rh[i]], axis=0) for i in n]
        gb = [_dot_nt(ar[i], bh_all[u][:, hsl[h]]) for i, (u, h) in enumerate(units)]
        gk = [_dot_nt(ar[i], kh_all[u][:, hsl[h]]) for i, (u, h) in enumerate(units)]
        a_rb = [jnp.where(incl, t[c:], 0.0).astype(BF16) for t in gb]
        akrk = [jnp.where(tri2, t, 0.0).astype(BF16) for t in gk]
        x0 = [jnp.where(strict, t[:c], 0.0) for t in gb]
        t = [eye + xi for xi in x0]
        xb = [xi.astype(BF16) for xi in x0]
        x = [_dot(xi, xi) for xi in xb]
        for _ in range(int(math.log2(c)) - 2):
            xt = [_dot(jnp.concatenate([x[i], t[i]], axis=0).astype(BF16), x[i].astype(BF16)) for i in n]
            x = [p[:c] for p in xt]
            t = [t[i] + xt[i][c:] for i in n]
        t = [t[i] + _dot(t[i].astype(BF16), x[i].astype(BF16)) for i in n]
        tb = [ti.astype(BF16) for ti in t]
        w1y2 = [_dot(akrk[i], vb[i]) for i in n]
        w1 = [p[:c] for p in w1y2]
        y2 = [p[c:] for p in w1y2]
        n2 = [_dot(vf[i].T.astype(BF16), kt_all[u][:, hsl[h]]) for i, (u, h) in enumerate(units)]
        a2 = [_dot(tb[i], ah[i]).astype(BF16) for i in n]
        u1t = [_dot(tb[i], w1[i].astype(BF16)).T for i in n]
        st = [st_ref[h] for h in heads]
        for u in range(nu):
            o = u * B_HEADS
            s0b = [st[h].astype(BF16) for h in heads]
            utb = [(_dot_nt(s0b[h], a2[o + h]) + u1t[o + h]).astype(BF16) for h in heads]
            ys = [_dot_nt(rh[o + h], s0b[h]) + _dot_nt(a_rb[o + h], utb[h]) + y2[o + h] for h in heads]
            st = [st[h] * p_tot[u][:, hsl[h]] + _dot(utb[h], bt_all[u][:, hsl[h]]) + n2[o + h] for h in heads]
            y_ref[sls[u], :] = jnp.concatenate(ys, axis=-1)
        st_ref[...] = jnp.stack(st, axis=0)
        return carry

    lax.fori_loop(0, n_chunks // nu, chunks, 0)


def _wkv_scan(r, lw, k, v, kk, b, ts):
    bsz, s, w = r.shape
    spec = pl.BlockSpec((None, ts, w), lambda bi, j: (bi, j, 0))
    return pl.pallas_call(
        _wkv_kernel,
        grid=(bsz, s // ts),
        in_specs=[spec] * 6,
        out_specs=spec,
        out_shape=jax.ShapeDtypeStruct((bsz, s, w), F32),
        scratch_shapes=[pltpu.VMEM((B_HEADS, HEAD_DIM, HEAD_DIM), F32)],
        compiler_params=_cparams(("parallel", "arbitrary")),
        name="wkv_scan",
    )(r, lw, k, v, kk, b)


def _merge_kernel(ya_ref, o1_ref, o2_ref, o3_ref, l1_ref, l2_ref, l3_ref, yr_ref, g_ref, bonus_ref,
                  gates_ref, lng_ref, lnb_ref, e_ref, pa_ref, pb_ref, pc_ref, out_ref, unf_ref):
    tm, d = out_ref.shape

    def unfold(ref, slot, dil):
        nc = C_OUT // LANES
        for r in range(dil):
            for c in range(nc):
                unf_ref[slot * nc + c, pl.ds(r, tm // dil, stride=dil), :] = (
                    ref[:, r * C_OUT + c * LANES:r * C_OUT + (c + 1) * LANES])
        return jnp.concatenate([unf_ref[slot * nc + c] for c in range(nc)], axis=-1)

    d2, d3 = C_GROUPS[1][1], C_GROUPS[2][1]
    o1, l1 = o1_ref[...], l1_ref[...]
    o2, l2 = unfold(o2_ref, 0, d2), unfold(l2_ref, 1, d2)
    o3, l3 = unfold(o3_ref, 2, d3), unfold(l3_ref, 3, d3)
    m = jnp.maximum(jnp.maximum(l1, l2), l3)
    e1, e2, e3 = jnp.exp(l1 - m), jnp.exp(l2 - m), jnp.exp(l3 - m)
    yc = (e1 * o1 + e2 * o2 + e3 * o3) / (e1 + e2 + e3)

    e = e_ref[...]
    y = yr_ref[...]
    mean = _head_sums(y, e, exact=True) * (1.0 / HEAD_DIM)
    dv = y - mean
    var = _head_sums(dv * dv, e) * (1.0 / HEAD_DIM)
    yb = dv * lax.rsqrt(var + B_GN_EPS) * lng_ref[...] + lnb_ref[...] + bonus_ref[...]
    yb = yb * g_ref[...]

    gates = gates_ref[...]
    merged = (gates[:, 0:d].astype(F32) * _dot(ya_ref[...], pa_ref[...])
              + gates[:, d:2 * d].astype(F32) * _dot(yb.astype(BF16), pb_ref[...])
              + gates[:, 2 * d:3 * d].astype(F32) * _dot(yc.astype(BF16), pc_ref[...]))
    out_ref[...] = merged.astype(BF16)


def _merge(ya, o1, o2, o3, l1, l2, l3, yr, g, bonus, gates, lng, lnb, e_bd, pa, pb, pc, tm):
    m, d = ya.shape[0], pa.shape[1]
    tok = lambda a: pl.BlockSpec((tm * a.shape[0] // m, a.shape[1]), lambda i: (i, 0))
    full = lambda a: pl.BlockSpec(a.shape, lambda i: (0,) * a.ndim, pipeline_mode=pl.Buffered(1))
    toks = [ya, o1, o2, o3, l1, l2, l3, yr, g, bonus, gates]
    consts = [lng.reshape(1, -1), lnb.reshape(1, -1), e_bd, pa, pb, pc]
    return pl.pallas_call(
        _merge_kernel,
        grid=(m // tm,),
        in_specs=[tok(a) for a in toks] + [full(a) for a in consts],
        out_specs=pl.BlockSpec((tm, d), lambda i: (i, 0)),
        out_shape=jax.ShapeDtypeStruct((m, d), BF16),
        scratch_shapes=[pltpu.VMEM((4 * C_OUT // LANES, tm, LANES), F32)],
        compiler_params=_cparams(("parallel",)),
        name="merge",
    )(*toks, *consts)


def _out_proj_kernel(x_ref, m_ref, w_ref, o_ref):
    o_ref[...] = x_ref[...] + _dot(m_ref[...], w_ref[...])


def _out_proj(x2, merged, wo, tm):
    m, d = x2.shape
    tok = pl.BlockSpec((tm, d), lambda i: (i, 0))
    return pl.pallas_call(
        _out_proj_kernel,
        grid=(m // tm,),
        in_specs=[tok, tok, pl.BlockSpec(wo.shape, lambda i: (0, 0), pipeline_mode=pl.Buffered(1))],
        out_specs=tok,
        out_shape=jax.ShapeDtypeStruct((m, d), F32),
        compiler_params=_cparams(("parallel",)),
        name="out_proj",
    )(x2, merged, wo)


def _ffn_kernel(xh_ref, x_ref, g_ref, wg_ref, wv_ref, cw_ref, wd_ref, fg_ref, o_ref, h_ref, *,
                tiles_per_seq, final_norm):
    j = pl.program_id(1)
    tm = x_ref.shape[0]

    @pl.when(j == 0)
    def _():
        x = x_ref[...]
        g = g_ref[...]
        h_ref[HALO:, :] = _rms(x, g).astype(BF16)
        seq_start = pl.program_id(0) % tiles_per_seq == 0
        halo = _rms(xh_ref[...], g)
        h_ref[0:HALO, :] = jnp.where(seq_start, 0.0, halo).astype(BF16)
        o_ref[...] = x

    h = h_ref[...]
    gext = _dot(h, wg_ref[...])
    val = _dot(h[HALO:], wv_ref[...])
    cw = cw_ref[...]
    gate = (cw[0:1, :] * pltpu.roll(gext, 2, 0)[HALO:]
            + cw[1:2, :] * pltpu.roll(gext, 1, 0)[HALO:]
            + cw[2:3, :] * gext[HALO:])
    hid = gate * jax.nn.sigmoid(gate) * val
    o_ref[...] += _dot(hid.astype(BF16), wd_ref[...])

    if final_norm:
        @pl.when(j == pl.num_programs(1) - 1)
        def _():
            o_ref[...] = _rms(o_ref[...], fg_ref[...])


def _ffn(x2, seq, g, wg, wv, cw, wd, fg, final_norm, tm, tf):
    m, d = x2.shape
    ffp = wg.shape[1]
    tps = seq // tm
    return pl.pallas_call(
        functools.partial(_ffn_kernel, tiles_per_seq=tps, final_norm=final_norm),
        grid=(m // tm, ffp // tf),
        in_specs=[
            pl.BlockSpec((HALO, d), lambda i, j: (jnp.maximum(i * (tm // HALO) - 1, 0), 0)),
            pl.BlockSpec((tm, d), lambda i, j: (i, 0)),
            pl.BlockSpec((1, d), lambda i, j: (0, 0)),
            pl.BlockSpec((d, tf), lambda i, j: (0, j)),
            pl.BlockSpec((d, tf), lambda i, j: (0, j)),
            pl.BlockSpec((3, tf), lambda i, j: (0, j)),
            pl.BlockSpec((tf, d), lambda i, j: (j, 0)),
            pl.BlockSpec((1, d), lambda i, j: (0, 0)),
        ],
        out_specs=pl.BlockSpec((tm, d), lambda i, j: (i, 0)),
        out_shape=jax.ShapeDtypeStruct((m, d), F32),
        scratch_shapes=[pltpu.VMEM((HALO + tm, d), BF16)],
        compiler_params=_cparams(("parallel", "arbitrary")),
        name="conv_ffn",
    )(x2, x2, g.reshape(1, d), wg, wv, cw, wd, fg.reshape(1, d))


def _t5_bucket(dist):
    small = dist < MAX_EXACT
    nf = jnp.maximum(dist, 1).astype(F32)
    large = MAX_EXACT + (jnp.log(nf / MAX_EXACT) / math.log(REL_MAX_DIST / MAX_EXACT)
                         * (N_BUCKETS - MAX_EXACT)).astype(jnp.int32)
    return jnp.where(small, dist, jnp.minimum(large, N_BUCKETS - 1))


def _band_bias(table, dilation, max_steps):
    i = jnp.arange(BLK)[:, None]
    j = jnp.arange(2 * BLK)[None, :]
    off = i + BLK - j
    onehot = jax.nn.one_hot(_t5_bucket(jnp.maximum(off, 0) * dilation), N_BUCKETS, dtype=F32)
    bias = jnp.einsum("ijb,bh->hij", onehot, table.astype(F32), precision=lax.Precision.HIGHEST)
    valid = (off >= 0) & (off <= max_steps)
    return jnp.where(valid[None], bias, NEG_BIG)


def _relayout_kernel(tab_ref, a_ref, b_ref, o_ref, *, phases):
    j = pl.program_id(1)
    ph = tab_ref[1, j]
    valid = tab_ref[2, j]
    a = a_ref[...]
    b = b_ref[...]
    lane = lax.broadcasted_iota(jnp.int32, a.shape, 1)
    out = a
    for p in phases:
        if p:
            shifted = jnp.where(lane < LANES - p, pltpu.roll(a, LANES - p, 1), pltpu.roll(b, LANES - p, 1))
            out = jnp.where(ph == p, shifted, out)
    o_ref[...] = jnp.where(lane < valid, out, 0.0).astype(o_ref.dtype)


def _relayout_cols(w, segs):
    depth, d, n = w.shape
    rows = []
    for dst, src, valid, total in sorted(segs):
        assert dst == len(rows) * LANES and total % LANES == 0
        for off in range(0, total, LANES):
            s = src + off if off < valid else 0
            rows.append((s // LANES, s % LANES, min(max(valid - off, 0), LANES)))
    tab = jnp.asarray(np.array(rows, np.int32).T)
    phases = tuple(sorted({r[1] for r in rows}))
    blk = lambda f: pl.BlockSpec((None, d, LANES), f)
    return pl.pallas_call(
        functools.partial(_relayout_kernel, phases=phases),
        grid_spec=pltpu.PrefetchScalarGridSpec(
            num_scalar_prefetch=1,
            grid=(depth, len(rows)),
            in_specs=[blk(lambda l, j, t: (l, 0, t[0, j])), blk(lambda l, j, t: (l, 0, t[0, j] + 1))],
            out_specs=blk(lambda l, j, t: (l, 0, j)),
        ),
        out_shape=jax.ShapeDtypeStruct((depth, d, len(rows) * LANES), BF16),
        compiler_params=_cparams(("parallel", "parallel")),
        name="relayout_w_in",
    )(tab, w, w)


def _pad_cols(w, n):
    return jnp.pad(w, [(0, 0)] * (w.ndim - 1) + [(0, n - w.shape[-1])])


def _pad_rows(w, n):
    return jnp.pad(w, [(0, 0)] * (w.ndim - 2) + [(0, n - w.shape[-2]), (0, 0)])


def kernel(x, rel_bias, norm1_g, w_in, attn_sinks, rwkv_mu, rwkv_w0, rwkv_w_up, rwkv_a0, rwkv_a_up, rwkv_g_up,
           rwkv_k_k, rwkv_k_a, rwkv_r_k, rwkv_lnx_g, rwkv_lnx_b, proj_a, proj_b, proj_c, w_out, norm2_g,
           ffn_up, ffn_conv, ffn_down, final_g):
    bsz, seq, d = x.shape
    depth = w_in.shape[0]
    m = bsz * seq
    d_ff = ffn_conv.shape[-1]
    ffp = -(-d_ff // D_FF_PAD_TO) * D_FF_PAD_TO

    tm_proj = min(1024, seq)
    tm_tok = min(512, seq)
    tm_ffn = min(1024, seq)
    ts_wkv = min(512, seq)

    na = A_HQ
    bias_a = _band_bias(rel_bias[:, :na], 1, BLK - 1)
    bias_c = [_band_bias(rel_bias[:, na + gi * C_HG:na + (gi + 1) * C_HG], dil, win // dil)
              for gi, (win, dil) in enumerate(C_GROUPS)]

    q_end = (A_HQ + 2 * A_HKV) * HEAD_DIM
    b0 = q_end
    b_r_end = b0 + 3 * B_WIDTH
    b_wd_end = b_r_end + LORA_DECAY
    b_ad_end = b_wd_end + LORA_ICLR
    b_end = b_ad_end + LORA_GATE
    c_end = b_end + 3 * C_WIDTH

    def rwkv_cols(t):
        parts = [t[..., b0:b_r_end], _pad_cols(t[..., b_r_end:b_wd_end], LORA_PAD),
                 _pad_cols(t[..., b_wd_end:b_ad_end], LORA_PAD), t[..., b_ad_end:b_end]]
        return _pad_cols(jnp.concatenate(parts, axis=-1), N_RWKV)

    segs = [(0, 0, q_end, q_end)]
    segs += [(q_end + (3 * gi + t) * C_OUT, b_end + t * C_WIDTH + gi * C_OUT, C_OUT, C_OUT)
             for gi in range(len(C_GROUPS)) for t in range(3)]
    o = N_ATTN
    for src, valid, total in ((b0, 3 * B_WIDTH, 3 * B_WIDTH), (b_r_end, LORA_DECAY, LORA_PAD),
                              (b_wd_end, LORA_ICLR, LORA_PAD), (b_ad_end, LORA_GATE, LORA_GATE)):
        segs.append((o, src, valid, total))
        o += total
    segs.append((o, 0, 0, N_ATTN + N_RWKV - o))
    segs.append((N_ATTN + N_RWKV, c_end, w_in.shape[-1] - c_end, w_in.shape[-1] - c_end))
    w_all = _relayout_cols(w_in, segs)
    proj_blk = q_end
    n_rwkv_blk = N_RWKV // proj_blk
    n_gate_blk = (w_in.shape[-1] - c_end) // proj_blk
    mu_pad = rwkv_cols(jnp.pad(rwkv_mu, ((0, 0), (b0, 0))))
    wup_pad = _pad_rows(rwkv_w_up, LORA_PAD)
    aup_pad = _pad_rows(rwkv_a_up, LORA_PAD)
    r_k = rwkv_r_k.reshape(depth, B_WIDTH)

    head_id = jnp.arange(B_WIDTH) // HEAD_DIM
    e_bd = (head_id[:, None] == head_id[None, :]).astype(BF16)

    pa, pb, pc, wo = (t.astype(BF16) for t in (proj_a, proj_b, proj_c, w_out))
    wg = _pad_cols(ffn_up[..., :d_ff], ffp).astype(BF16)
    wv = _pad_cols(ffn_up[..., d_ff:], ffp).astype(BF16)
    cw = _pad_cols(ffn_conv, ffp)
    wd = _pad_rows(ffn_down, ffp).astype(BF16)

    x2 = x.reshape(m, d)
    for l in range(depth):
        p_swa, *p_dil, p_rwkv, gates = _proj(x2, norm1_g[l], w_all[l], n_rwkv_blk, n_gate_blk, tm_proj)

        kvw = A_HKV * HEAD_DIM
        y_a = _band_attention(p_swa.reshape(bsz, seq, -1), bias_a, attn_sinks[l], dil=1, q_blk=0,
                              k_blk=(A_HQ * HEAD_DIM) // kvw, v_blk=(A_HQ * HEAD_DIM) // kvw + 1,
                              q_w=A_HQ * HEAD_DIM, kv_w=kvw, nq=A_HQ, nkv=A_HKV, want_lse=False,
                              out_dtype=BF16, name="attn_swa", nqb=4)
        oc, lc = [], []
        for gi, (win, dil) in enumerate(C_GROUPS):
            pv = p_dil[gi].reshape(bsz, seq // dil, -1)
            o, ls = _band_attention(pv, bias_c[gi], None, dil=dil, q_blk=0, k_blk=1, v_blk=2, q_w=C_OUT,
                                    kv_w=C_OUT, nq=C_HG, nkv=C_HG, want_lse=True, out_dtype=F32,
                                    name=f"attn_dil{dil}", nqb=min(4, seq // dil // BLK))
            oc.append(o.reshape(m // dil, dil * C_OUT))
            lc.append(ls.reshape(m // dil, dil * C_OUT))

        r, lw, k2, v, kk, bb, g, bonus = _rwkv_prep(
            p_rwkv, seq, mu_pad[l], rwkv_w0[l], wup_pad[l], rwkv_a0[l], aup_pad[l], rwkv_g_up[l],
            rwkv_k_k[l], rwkv_k_a[l], r_k[l], e_bd, tm_tok)
        sh = lambda t: t.reshape(bsz, seq, B_WIDTH)
        y_raw = _wkv_scan(sh(r), sh(lw), sh(k2), sh(v), sh(kk), sh(bb), ts_wkv).reshape(m, B_WIDTH)

        merged = _merge(y_a.reshape(m, A_HQ * HEAD_DIM), oc[0], oc[1], oc[2], lc[0], lc[1], lc[2], y_raw, g,
                        bonus, gates, rwkv_lnx_g[l], rwkv_lnx_b[l], e_bd, pa[l], pb[l], pc[l], tm_tok)
        x2 = _out_proj(x2, merged, wo[l], tm_proj)
        x2 = _ffn(x2, seq, norm2_g[l], wg[l], wv[l], cw[l], wd[l], final_g, l == depth - 1, tm_ffn, 512)
    return x2.reshape(bsz, seq, d)
```

```python
import functools
import math

import jax
import jax.numpy as jnp
import numpy as np
from jax import lax
from jax.experimental import pallas as pl
from jax.experimental.pallas import tpu as pltpu

F32 = jnp.float32
BF16 = jnp.bfloat16

HEAD_DIM = 64
LANES = 128
BLK = 128
NORM_EPS = 1e-5
A_HQ, A_HKV = 8, 2
B_HEADS = 12
B_WIDTH = B_HEADS * HEAD_DIM
LORA_DECAY, LORA_ICLR, LORA_GATE = 96, 96, 256
LORA_PAD = 128
B_GN_EPS = 64e-5
C_GROUPS = ((128, 1), (512, 4), (2048, 16))
C_HG = 4
C_WIDTH = C_HG * len(C_GROUPS) * HEAD_DIM
C_OUT = C_HG * HEAD_DIM
N_BUCKETS, MAX_EXACT, REL_MAX_DIST = 32, 16, 2048
D_FF_PAD_TO = 512
NEG_BIG = -1e30

N_ATTN = A_HQ * HEAD_DIM + 2 * A_HKV * HEAD_DIM + 3 * C_WIDTH
N_RWKV = 3 * B_WIDTH + 2 * LORA_PAD + LORA_GATE + 256
OFF_WD = 3 * B_WIDTH
OFF_AD = OFF_WD + LORA_PAD
OFF_GD = OFF_AD + LORA_PAD
WKV_CHUNK = 64
WKV_UNROLL = 2
HALO = 16

VMEM_LIMIT = 56 * 1024 * 1024


def _cparams(sem):
    return pltpu.CompilerParams(dimension_semantics=sem, vmem_limit_bytes=VMEM_LIMIT)


def _dot(a, b):
    return jnp.dot(a, b, preferred_element_type=F32)


def _dot_nt(a, b):
    return lax.dot_general(a, b, (((1,), (1,)), ((), ())), preferred_element_type=F32)


def _split2(x):
    hi = x.astype(BF16)
    lo = (x - hi.astype(F32)).astype(BF16)
    return hi, lo


def _head_sums(a, e_bf16, exact=False):
    if not exact:
        return _dot(a.astype(BF16), e_bf16)
    h, l = _split2(a)
    return _dot(h, e_bf16) + _dot(l, e_bf16)


def _dot_exact_lhs(a_bf16, b):
    h, l = _split2(b)
    return _dot(a_bf16, h) + _dot(a_bf16, l)


def _dot3(a, b):
    ah, al = _split2(a)
    bh, bl = _split2(b)
    return _dot(ah, bh) + _dot(ah, bl) + _dot(al, bh)


def _rms(x, g):
    ms = jnp.mean(x * x, axis=-1, keepdims=True)
    return x * lax.rsqrt(ms + NORM_EPS) * g


def _proj_kernel(x_ref, g_ref, w_ref, pa_ref, p1_ref, p4_ref, p16_ref, pr_ref, pg_ref, h_ref, acc_ref, *,
                 n_rwkv):
    j = pl.program_id(1)
    tm = x_ref.shape[0]
    w = w_ref.shape[1]
    n_attn = 1 + len(C_GROUPS)

    @pl.when(j == 0)
    def _():
        h_ref[...] = _rms(x_ref[...], g_ref[...]).astype(BF16)

    def acc():
        return _dot(h_ref[...], w_ref[...])

    @pl.when(j == 0)
    def _():
        pa_ref[...] = acc().astype(BF16)

    @pl.when(j == 1)
    def _():
        p1_ref[...] = acc().astype(BF16)

    for jj, dil, ref in ((2, C_GROUPS[1][1], p4_ref), (3, C_GROUPS[2][1], p16_ref)):
        @pl.when(j == jj)
        def _(dil=dil, ref=ref):
            a = acc()
            for c in range(w // LANES):
                acc_ref[c] = a[:, c * LANES:(c + 1) * LANES]
            for r in range(dil):
                for c in range(w // LANES):
                    ref[:, r * w + c * LANES:r * w + (c + 1) * LANES] = (
                        acc_ref[c, pl.ds(r, tm // dil, stride=dil), :].astype(BF16))

    @pl.when(jnp.logical_and(j >= n_attn, j < n_attn + n_rwkv))
    def _():
        pr_ref[...] = acc().astype(BF16)

    @pl.when(j >= n_attn + n_rwkv)
    def _():
        pg_ref[...] = jax.nn.sigmoid(acc()).astype(BF16)


def _proj(x2, g, w, n_rwkv, n_gate, tm):
    m, d = x2.shape
    n_attn = 1 + len(C_GROUPS)
    nblk = n_attn + n_rwkv + n_gate
    wq = w.shape[1] // nblk
    d4, d16 = C_GROUPS[1][1], C_GROUPS[2][1]
    return pl.pallas_call(
        functools.partial(_proj_kernel, n_rwkv=n_rwkv),
        grid=(m // tm, nblk),
        in_specs=[
            pl.BlockSpec((tm, d), lambda i, j: (i, 0)),
            pl.BlockSpec((1, d), lambda i, j: (0, 0)),
            pl.BlockSpec((d, wq), lambda i, j: (0, j)),
        ],
        out_specs=[
            pl.BlockSpec((tm, wq), lambda i, j: (i, 0)),
            pl.BlockSpec((tm, wq), lambda i, j: (i, 0)),
            pl.BlockSpec((tm // d4, d4 * wq), lambda i, j: (i, 0)),
            pl.BlockSpec((tm // d16, d16 * wq), lambda i, j: (i, 0)),
            pl.BlockSpec((tm, wq), lambda i, j: (i, jnp.clip(j - n_attn, 0, n_rwkv - 1))),
            pl.BlockSpec((tm, wq), lambda i, j: (i, jnp.clip(j - n_attn - n_rwkv, 0, n_gate - 1))),
        ],
        out_shape=[
            jax.ShapeDtypeStruct((m, wq), BF16),
            jax.ShapeDtypeStruct((m, wq), BF16),
            jax.ShapeDtypeStruct((m // d4, d4 * wq), BF16),
            jax.ShapeDtypeStruct((m // d16, d16 * wq), BF16),
            jax.ShapeDtypeStruct((m, n_rwkv * wq), BF16),
            jax.ShapeDtypeStruct((m, n_gate * wq), BF16),
        ],
        scratch_shapes=[pltpu.VMEM((tm, d), BF16), pltpu.VMEM((wq // LANES, tm, LANES), F32)],
        compiler_params=_cparams(("parallel", "arbitrary")),
        name="proj",
    )(x2, g.reshape(1, d), w)


def _band_attn_kernel(*refs, nq, nkv, has_sink, want_lse):
    it = iter(refs)
    q_ref, kp_ref, kc_ref, vp_ref, vc_ref, bias_ref = (next(it) for _ in range(6))
    sink_ref = next(it) if has_sink else None
    o_ref = next(it)
    lse_ref = next(it) if want_lse else None

    nqb = q_ref.shape[0] // BLK
    first = pl.program_id(2) == 0
    q = q_ref[...] * jnp.asarray(HEAD_DIM ** -0.5, BF16)
    k = jnp.concatenate([kp_ref[...], kc_ref[...]], axis=0)
    v = jnp.concatenate([vp_ref[...], vc_ref[...]], axis=0)
    col = lax.broadcasted_iota(jnp.int32, (BLK, 2 * BLK), 1)
    edge = jnp.where(col < BLK, jnp.where(first, NEG_BIG, 0.0), 0.0)
    rep = nq // nkv
    units = [(i, h) for i in range(nqb) for h in range(nq)]
    hd = lambda t, i: t[:, i * HEAD_DIM:(i + 1) * HEAD_DIM]
    qs = lambda i, h: hd(q[i * BLK:(i + 1) * BLK], h)
    win = lambda t, i, h: hd(t[i * BLK:(i + 2) * BLK], h // rep)
    s = [_dot_nt(qs(i, h), win(k, i, h)) + bias_ref[h] for i, h in units]
    s = [s[u] + edge if i == 0 else s[u] for u, (i, h) in enumerate(units)]
    m = [jnp.max(t, axis=-1, keepdims=True) for t in s]
    if has_sink:
        m = [jnp.maximum(m[u], sink_ref[h]) for u, (i, h) in enumerate(units)]
    p = [jnp.exp(s[u] - m[u]).astype(BF16) for u in range(len(units))]
    ones = jnp.ones((2 * BLK, LANES), BF16)
    l = [_dot(t, ones)[:, :HEAD_DIM] for t in p]
    denom = [l[u] + jnp.exp(sink_ref[h] - m[u]) for u, (i, h) in enumerate(units)] if has_sink else l
    o = [_dot(p[u], win(v, i, h)) / denom[u] for u, (i, h) in enumerate(units)]
    rows = lambda parts: jnp.concatenate(
        [jnp.concatenate(parts[i * nq:(i + 1) * nq], axis=-1) for i in range(nqb)], axis=0)
    o_ref[...] = rows(o).astype(o_ref.dtype)
    if want_lse:
        lse_ref[...] = rows([m[u] + jnp.log(l[u]) for u in range(len(units))])


def _band_attention(pv, bias, sink, *, dil, q_blk, k_blk, v_blk, q_w, kv_w, nq, nkv, want_lse, out_dtype, name,
                    nqb):
    b, lf, nd = pv.shape
    n = nd // dil
    nb = lf // (BLK * nqb)
    qpr, kpr = n // q_w, n // kv_w
    has_sink = sink is not None
    prev = lambda j: jnp.maximum(j * nqb - 1, 0)

    in_specs = [
        pl.BlockSpec((None, nqb * BLK, q_w), lambda bi, r, j: (bi, j, r * qpr + q_blk)),
        pl.BlockSpec((None, BLK, kv_w), lambda bi, r, j: (bi, prev(j), r * kpr + k_blk)),
        pl.BlockSpec((None, nqb * BLK, kv_w), lambda bi, r, j: (bi, j, r * kpr + k_blk)),
        pl.BlockSpec((None, BLK, kv_w), lambda bi, r, j: (bi, prev(j), r * kpr + v_blk)),
        pl.BlockSpec((None, nqb * BLK, kv_w), lambda bi, r, j: (bi, j, r * kpr + v_blk)),
        pl.BlockSpec((nq, BLK, 2 * BLK), lambda bi, r, j: (0, 0, 0)),
    ]
    args = [pv, pv, pv, pv, pv, bias]
    if has_sink:
        in_specs.append(pl.BlockSpec(memory_space=pltpu.SMEM))
        args.append(sink)
    ow = nq * HEAD_DIM
    out_spec = pl.BlockSpec((None, nqb * BLK, ow), lambda bi, r, j: (bi, j, r))
    out_shape = jax.ShapeDtypeStruct((b, lf, dil * ow), out_dtype)
    if want_lse:
        out_specs = [out_spec, out_spec]
        out_shapes = [out_shape, jax.ShapeDtypeStruct((b, lf, dil * ow), F32)]
    else:
        out_specs, out_shapes = out_spec, out_shape
    return pl.pallas_call(
        functools.partial(_band_attn_kernel, nq=nq, nkv=nkv, has_sink=has_sink, want_lse=want_lse),
        grid=(b, dil, nb),
        in_specs=in_specs,
        out_specs=out_specs,
        out_shape=out_shapes,
        compiler_params=_cparams(("parallel", "parallel", "arbitrary")),
        name=name,
    )(*args)


def _rwkv_prep_kernel(ph_ref, p_ref, mu_ref, w0_ref, wup_ref, a0_ref, aup_ref, gup_ref, kk_ref, ka_ref,
                      rk_ref, e_ref, r_o, lw_o, k_o, v_o, kk_o, b_o, g_o, bonus_o, *, tiles_per_seq):
    tm = p_ref.shape[0]
    p = p_ref[...].astype(F32)
    seq_start = pl.program_id(0) % tiles_per_seq == 0
    last = ph_ref[...].astype(F32)[HALO - 1:HALO, :]
    last = jnp.where(seq_start, 0.0, last)
    row = lax.broadcasted_iota(jnp.int32, (tm, 1), 0)
    prev = jnp.where(row == 0, last, pltpu.roll(p, 1, 0))
    pf = p + (prev - p) * mu_ref[...]

    r = pf[:, 0:B_WIDTH]
    k = pf[:, B_WIDTH:2 * B_WIDTH]
    v = pf[:, 2 * B_WIDTH:3 * B_WIDTH]
    wd = pf[:, OFF_WD:OFF_WD + LORA_PAD]
    ad = pf[:, OFF_AD:OFF_AD + LORA_PAD]
    gd = pf[:, OFF_GD:OFF_GD + LORA_GATE]

    z = w0_ref[...] + _dot3(jnp.tanh(wd), wup_ref[...])
    nz = -z
    softplus = jnp.maximum(nz, 0.0) + jnp.log(1.0 + jnp.exp(-jnp.abs(nz)))
    w = -softplus - 0.5
    lw_o[...] = -jnp.exp(w)
    a = jax.nn.sigmoid(a0_ref[...] + _dot3(ad, aup_ref[...]))
    g_o[...] = _dot3(jax.nn.sigmoid(gd), gup_ref[...])

    e = e_ref[...]
    kk = k * kk_ref[...]
    nrm = jnp.sqrt(_head_sums(kk * kk, e))
    kk = kk / jnp.maximum(nrm, 1e-12)
    k2 = k * (1.0 + (a - 1.0) * ka_ref[...])
    r_o[...] = r
    k_o[...] = k2
    v_o[...] = v
    kk_o[...] = kk
    b_o[...] = kk * a
    bonus_o[...] = _head_sums(r * k2 * rk_ref[...], e) * v


def _rwkv_prep(pb2, seq, mu, w0, wup, a0, aup, gup, k_k, k_a, r_k, e_bd, tm):
    m, n = pb2.shape
    tps = seq // tm
    row = lambda a: a.reshape(1, -1)
    full = lambda a: pl.BlockSpec(a.shape, lambda i: (0,) * a.ndim)
    args = [pb2, pb2, row(mu), row(w0), wup, row(a0), aup, gup, row(k_k), row(k_a), row(r_k), e_bd]
    in_specs = [
        pl.BlockSpec((HALO, n), lambda i: (jnp.maximum(i * (tm // HALO) - 1, 0), 0)),
        pl.BlockSpec((tm, n), lambda i: (i, 0)),
    ] + [full(a) for a in args[2:]]
    o_spec = pl.BlockSpec((tm, B_WIDTH), lambda i: (i, 0))
    o_shape = jax.ShapeDtypeStruct((m, B_WIDTH), F32)
    return pl.pallas_call(
        functools.partial(_rwkv_prep_kernel, tiles_per_seq=tps),
        grid=(m // tm,),
        in_specs=in_specs,
        out_specs=[o_spec] * 8,
        out_shape=[o_shape] * 8,
        compiler_params=_cparams(("parallel",)),
        name="rwkv_prep",
    )(*args)


def _wkv_kernel(r_ref, lw_ref, k_ref, v_ref, kk_ref, b_ref, y_ref, st_ref):
    c = WKV_CHUNK
    nb = r_ref.shape[0]
    n_chunks = r_ref.shape[1] // c

    @pl.when(pl.program_id(0) == 0)
    def _():
        st_ref[...] = jnp.zeros_like(st_ref)

    rowi = lax.broadcasted_iota(jnp.int32, (c, c), 0)
    coli = lax.broadcasted_iota(jnp.int32, (c, c), 1)
    incl = rowi >= coli
    strict = rowi > coli
    row2 = lax.broadcasted_iota(jnp.int32, (2 * c, 2 * c), 0)
    col2 = lax.broadcasted_iota(jnp.int32, (2 * c, 2 * c), 1)
    tri2 = row2 - jnp.where(row2 < c, 1, c) >= jnp.where(col2 >= c, col2 - c, 2 * c)
    tri = jnp.where(incl, 1.0, 0.0).astype(BF16)
    eye = jnp.where(rowi == coli, 1.0, 0.0)

    nu = WKV_UNROLL
    heads = range(B_HEADS)
    hsl = [slice(h * HEAD_DIM, (h + 1) * HEAD_DIM) for h in heads]
    segs = [(b, u) for b in range(nb) for u in range(nu)]
    ns = range(len(segs))
    units = [(s, h) for s in ns for h in heads]

    def chunks(ci, carry):
        sls = [pl.ds(pl.multiple_of((ci * nu + u) * c, c), c) for u in range(nu)]
        ld = lambda ref, s: ref[segs[s][0], sls[segs[s][1]], :]
        lw = [ld(lw_ref, s) for s in ns]
        cum = [_dot_exact_lhs(tri, t) for t in lw]
        tot = [t[c - 1:c, :] for t in cum]
        p_inv = [jnp.exp(-t) for t in cum]
        p_rest = [jnp.exp(tot[s] - cum[s]) for s in ns]
        p_tot = [jnp.exp(t) for t in tot]
        rh_all = [(ld(r_ref, s) * jnp.exp(cum[s])).astype(BF16) for s in ns]
        ah_all = [(-ld(kk_ref, s) * jnp.exp(cum[s] - lw[s])).astype(BF16) for s in ns]
        b_all = [ld(b_ref, s) for s in ns]
        k_all = [ld(k_ref, s) for s in ns]
        bh_all = [(b_all[s] * p_inv[s]).astype(BF16) for s in ns]
        kh_all = [(k_all[s] * p_inv[s]).astype(BF16) for s in ns]
        bt_all = [(b_all[s] * p_rest[s]).astype(BF16) for s in ns]
        kt_all = [(k_all[s] * p_rest[s]).astype(BF16) for s in ns]
        v_all = [ld(v_ref, s) for s in ns]

        ah = [ah_all[u][:, hsl[h]] for u, h in units]
        rh = [rh_all[u][:, hsl[h]] for u, h in units]
        vf = [v_all[u][:, hsl[h]] for u, h in units]
        vb = [t.astype(BF16) for t in vf]
        n = range(len(units))
        ar = [jnp.concatenate([ah[i], rh[i]], axis=0) for i in n]
        g = [_dot_nt(ar[i], jnp.concatenate([bh_all[u][:, hsl[h]], kh_all[u][:, hsl[h]]], axis=0))
             for i, (u, h) in enumerate(units)]
        gb = [t[:, :c] for t in g]
        a_rb = [jnp.where(incl, t[c:], 0.0).astype(BF16) for t in gb]
        akrk = [jnp.where(tri2, t, 0.0).astype(BF16) for t in g]
        x0 = [jnp.where(strict, t[:c], 0.0) for t in gb]
        t = [eye + xi for xi in x0]
        xb = [xi.astype(BF16) for xi in x0]
        x = [_dot(xi, xi) for xi in xb]
        for _ in range(int(math.log2(c)) - 2):
            xt = [_dot(jnp.concatenate([x[i], t[i]], axis=0).astype(BF16), x[i].astype(BF16)) for i in n]
            x = [p[:c] for p in xt]
            t = [t[i] + xt[i][c:] for i in n]
        t = [t[i] + _dot(t[i].astype(BF16), x[i].astype(BF16)) for i in n]
        tb = [ti.astype(BF16) for ti in t]
        zv = jnp.zeros((c, HEAD_DIM), BF16)
        w1y2 = [_dot(akrk[i], jnp.concatenate([zv, vb[i]], axis=0)) for i in n]
        w1 = [p[:c] for p in w1y2]
        y2 = [p[c:] for p in w1y2]
        n2 = [_dot(vf[i].T.astype(BF16), kt_all[u][:, hsl[h]]) for i, (u, h) in enumerate(units)]
        a2 = [_dot(tb[i], ah[i]).astype(BF16) for i in n]
        u1t = [_dot(tb[i], w1[i].astype(BF16)).T for i in n]
        chains = [(b, h) for b in range(nb) for h in heads]
        nc = range(len(chains))
        st = [st_ref[i] for i in nc]
        for u in range(nu):
            seg = [b * nu + u for b, h in chains]
            un = [seg[i] * B_HEADS + chains[i][1] for i in nc]
            hs = [hsl[h] for b, h in chains]
            s0b = [t.astype(BF16) for t in st]
            utb = [(_dot_nt(s0b[i], a2[un[i]]) + u1t[un[i]]).astype(BF16) for i in nc]
            ys = [_dot_nt(rh[un[i]], s0b[i]) + _dot_nt(a_rb[un[i]], utb[i]) + y2[un[i]] for i in nc]
            st = [st[i] * p_tot[seg[i]][:, hs[i]] + _dot(utb[i], bt_all[seg[i]][:, hs[i]]) + n2[un[i]] for i in nc]
            for b in range(nb):
                y_ref[b, sls[u], :] = jnp.concatenate(ys[b * B_HEADS:(b + 1) * B_HEADS], axis=-1)
        st_ref[...] = jnp.stack(st, axis=0)
        return carry

    lax.fori_loop(0, n_chunks // nu, chunks, 0)


def _wkv_scan(r, lw, k, v, kk, b, ts):
    bsz, s, w = r.shape
    spec = pl.BlockSpec((bsz, ts, w), lambda j: (0, j, 0))
    return pl.pallas_call(
        _wkv_kernel,
        grid=(s // ts,),
        in_specs=[spec] * 6,
        out_specs=spec,
        out_shape=jax.ShapeDtypeStruct((bsz, s, w), F32),
        scratch_shapes=[pltpu.VMEM((bsz * B_HEADS, HEAD_DIM, HEAD_DIM), F32)],
        compiler_params=_cparams(("arbitrary",)),
        name="wkv_scan",
    )(r, lw, k, v, kk, b)


def _merge_kernel(ya_ref, o1_ref, o2_ref, o3_ref, l1_ref, l2_ref, l3_ref, yr_ref, g_ref, bonus_ref,
                  gates_ref, lng_ref, lnb_ref, e_ref, pa_ref, pb_ref, pc_ref, out_ref, unf_ref):
    tm, d = out_ref.shape

    def unfold(ref, slot, dil):
        nc = C_OUT // LANES
        for r in range(dil):
            for c in range(nc):
                unf_ref[slot * nc + c, pl.ds(r, tm // dil, stride=dil), :] = (
                    ref[:, r * C_OUT + c * LANES:r * C_OUT + (c + 1) * LANES])
        return jnp.concatenate([unf_ref[slot * nc + c] for c in range(nc)], axis=-1)

    d2, d3 = C_GROUPS[1][1], C_GROUPS[2][1]
    o1, l1 = o1_ref[...], l1_ref[...]
    o2, l2 = unfold(o2_ref, 0, d2), unfold(l2_ref, 1, d2)
    o3, l3 = unfold(o3_ref, 2, d3), unfold(l3_ref, 3, d3)
    m = jnp.maximum(jnp.maximum(l1, l2), l3)
    e1, e2, e3 = jnp.exp(l1 - m), jnp.exp(l2 - m), jnp.exp(l3 - m)
    yc = (e1 * o1 + e2 * o2 + e3 * o3) / (e1 + e2 + e3)

    e = e_ref[...]
    y = yr_ref[...]
    mean = _head_sums(y, e, exact=True) * (1.0 / HEAD_DIM)
    dv = y - mean
    var = _head_sums(dv * dv, e) * (1.0 / HEAD_DIM)
    yb = dv * lax.rsqrt(var + B_GN_EPS) * lng_ref[...] + lnb_ref[...] + bonus_ref[...]
    yb = yb * g_ref[...]

    gates = gates_ref[...]
    merged = (gates[:, 0:d].astype(F32) * _dot(ya_ref[...], pa_ref[...])
              + gates[:, d:2 * d].astype(F32) * _dot(yb.astype(BF16), pb_ref[...])
              + gates[:, 2 * d:3 * d].astype(F32) * _dot(yc.astype(BF16), pc_ref[...]))
    out_ref[...] = merged.astype(BF16)


def _merge(ya, o1, o2, o3, l1, l2, l3, yr, g, bonus, gates, lng, lnb, e_bd, pa, pb, pc, tm):
    m, d = ya.shape[0], pa.shape[1]
    tok = lambda a: pl.BlockSpec((tm * a.shape[0] // m, a.shape[1]), lambda i: (i, 0))
    full = lambda a: pl.BlockSpec(a.shape, lambda i: (0,) * a.ndim, pipeline_mode=pl.Buffered(1))
    toks = [ya, o1, o2, o3, l1, l2, l3, yr, g, bonus, gates]
    consts = [lng.reshape(1, -1), lnb.reshape(1, -1), e_bd, pa, pb, pc]
    return pl.pallas_call(
        _merge_kernel,
        grid=(m // tm,),
        in_specs=[tok(a) for a in toks] + [full(a) for a in consts],
        out_specs=pl.BlockSpec((tm, d), lambda i: (i, 0)),
        out_shape=jax.ShapeDtypeStruct((m, d), BF16),
        scratch_shapes=[pltpu.VMEM((4 * C_OUT // LANES, tm, LANES), F32)],
        compiler_params=_cparams(("parallel",)),
        name="merge",
    )(*toks, *consts)


def _out_proj_kernel(x_ref, m_ref, w_ref, o_ref):
    o_ref[...] = x_ref[...] + _dot(m_ref[...], w_ref[...])


def _out_proj(x2, merged, wo, tm):
    m, d = x2.shape
    tok = pl.BlockSpec((tm, d), lambda i: (i, 0))
    return pl.pallas_call(
        _out_proj_kernel,
        grid=(m // tm,),
        in_specs=[tok, tok, pl.BlockSpec(wo.shape, lambda i: (0, 0), pipeline_mode=pl.Buffered(1))],
        out_specs=tok,
        out_shape=jax.ShapeDtypeStruct((m, d), F32),
        compiler_params=_cparams(("parallel",)),
        name="out_proj",
    )(x2, merged, wo)


def _ffn_kernel(xh_ref, x_ref, g_ref, wg_ref, wv_ref, cw_ref, wd_ref, fg_ref, o_ref, h_ref, *,
                tiles_per_seq, final_norm):
    j = pl.program_id(1)
    tm = x_ref.shape[0]

    @pl.when(j == 0)
    def _():
        x = x_ref[...]
        g = g_ref[...]
        h_ref[HALO:, :] = _rms(x, g).astype(BF16)
        seq_start = pl.program_id(0) % tiles_per_seq == 0
        halo = _rms(xh_ref[...], g)
        h_ref[0:HALO, :] = jnp.where(seq_start, 0.0, halo).astype(BF16)
        o_ref[...] = x

    h = h_ref[...]
    gext = _dot(h, wg_ref[...])
    val = _dot(h[HALO:], wv_ref[...])
    cw = cw_ref[...]
    gate = (cw[0:1, :] * pltpu.roll(gext, 2, 0)[HALO:]
            + cw[1:2, :] * pltpu.roll(gext, 1, 0)[HALO:]
            + cw[2:3, :] * gext[HALO:])
    hid = gate * jax.nn.sigmoid(gate) * val
    o_ref[...] += _dot(hid.astype(BF16), wd_ref[...])

    if final_norm:
        @pl.when(j == pl.num_programs(1) - 1)
        def _():
            o_ref[...] = _rms(o_ref[...], fg_ref[...])


def _ffn(x2, seq, g, wg, wv, cw, wd, fg, final_norm, tm, tf):
    m, d = x2.shape
    ffp = wg.shape[1]
    tps = seq // tm
    return pl.pallas_call(
        functools.partial(_ffn_kernel, tiles_per_seq=tps, final_norm=final_norm),
        grid=(m // tm, ffp // tf),
        in_specs=[
            pl.BlockSpec((HALO, d), lambda i, j: (jnp.maximum(i * (tm // HALO) - 1, 0), 0)),
            pl.BlockSpec((tm, d), lambda i, j: (i, 0)),
            pl.BlockSpec((1, d), lambda i, j: (0, 0)),
            pl.BlockSpec((d, tf), lambda i, j: (0, j)),
            pl.BlockSpec((d, tf), lambda i, j: (0, j)),
            pl.BlockSpec((3, tf), lambda i, j: (0, j)),
            pl.BlockSpec((tf, d), lambda i, j: (j, 0)),
            pl.BlockSpec((1, d), lambda i, j: (0, 0)),
        ],
        out_specs=pl.BlockSpec((tm, d), lambda i, j: (i, 0)),
        out_shape=jax.ShapeDtypeStruct((m, d), F32),
        scratch_shapes=[pltpu.VMEM((HALO + tm, d), BF16)],
        compiler_params=_cparams(("parallel", "arbitrary")),
        name="conv_ffn",
    )(x2, x2, g.reshape(1, d), wg, wv, cw, wd, fg.reshape(1, d))


def _t5_bucket(dist):
    small = dist < MAX_EXACT
    nf = jnp.maximum(dist, 1).astype(F32)
    large = MAX_EXACT + (jnp.log(nf / MAX_EXACT) / math.log(REL_MAX_DIST / MAX_EXACT)
                         * (N_BUCKETS - MAX_EXACT)).astype(jnp.int32)
    return jnp.where(small, dist, jnp.minimum(large, N_BUCKETS - 1))


def _band_bias(table, dilation, max_steps):
    i = jnp.arange(BLK)[:, None]
    j = jnp.arange(2 * BLK)[None, :]
    off = i + BLK - j
    onehot = jax.nn.one_hot(_t5_bucket(jnp.maximum(off, 0) * dilation), N_BUCKETS, dtype=F32)
    bias = jnp.einsum("ijb,bh->hij", onehot, table.astype(F32), precision=lax.Precision.HIGHEST)
    valid = (off >= 0) & (off <= max_steps)
    return jnp.where(valid[None], bias, NEG_BIG)


def _pad_cols(w, n):
    return jnp.pad(w, [(0, 0)] * (w.ndim - 1) + [(0, n - w.shape[-1])])


def _pad_rows(w, n):
    return jnp.pad(w, [(0, 0)] * (w.ndim - 2) + [(0, n - w.shape[-2]), (0, 0)])


def kernel(x, rel_bias, norm1_g, w_in, attn_sinks, rwkv_mu, rwkv_w0, rwkv_w_up, rwkv_a0, rwkv_a_up, rwkv_g_up,
           rwkv_k_k, rwkv_k_a, rwkv_r_k, rwkv_lnx_g, rwkv_lnx_b, proj_a, proj_b, proj_c, w_out, norm2_g,
           ffn_up, ffn_conv, ffn_down, final_g):
    bsz, seq, d = x.shape
    depth = w_in.shape[0]
    m = bsz * seq
    d_ff = ffn_conv.shape[-1]
    ffp = -(-d_ff // D_FF_PAD_TO) * D_FF_PAD_TO

    tm_proj = min(1024, seq)
    tm_tok = min(512, seq)
    tm_ffn = min(1024, seq)
    ts_wkv = min(512, seq)

    na = A_HQ
    bias_a = _band_bias(rel_bias[:, :na], 1, BLK - 1)
    bias_c = [_band_bias(rel_bias[:, na + gi * C_HG:na + (gi + 1) * C_HG], dil, win // dil)
              for gi, (win, dil) in enumerate(C_GROUPS)]

    q_end = (A_HQ + 2 * A_HKV) * HEAD_DIM
    b0 = q_end
    b_r_end = b0 + 3 * B_WIDTH
    b_wd_end = b_r_end + LORA_DECAY
    b_ad_end = b_wd_end + LORA_ICLR
    b_end = b_ad_end + LORA_GATE
    c_end = b_end + 3 * C_WIDTH

    def rwkv_cols(t):
        parts = [t[..., b0:b_r_end], _pad_cols(t[..., b_r_end:b_wd_end], LORA_PAD),
                 _pad_cols(t[..., b_wd_end:b_ad_end], LORA_PAD), t[..., b_ad_end:b_end]]
        return _pad_cols(jnp.concatenate(parts, axis=-1), N_RWKV)

    w_in_b = lax.optimization_barrier(w_in.astype(BF16))
    qkv_c = [w_in_b[..., b_end + t * C_WIDTH + gi * C_OUT:b_end + t * C_WIDTH + (gi + 1) * C_OUT]
             for gi in range(len(C_GROUPS)) for t in range(3)]
    w_all = jnp.concatenate([w_in_b[..., :q_end]] + qkv_c + [rwkv_cols(w_in_b), w_in_b[..., c_end:]], axis=-1)
    proj_blk = q_end
    n_rwkv_blk = N_RWKV // proj_blk
    n_gate_blk = (w_in.shape[-1] - c_end) // proj_blk
    mu_pad = rwkv_cols(jnp.pad(rwkv_mu, ((0, 0), (b0, 0))))
    wup_pad = _pad_rows(rwkv_w_up, LORA_PAD)
    aup_pad = _pad_rows(rwkv_a_up, LORA_PAD)
    r_k = rwkv_r_k.reshape(depth, B_WIDTH)

    head_id = jnp.arange(B_WIDTH) // HEAD_DIM
    e_bd = (head_id[:, None] == head_id[None, :]).astype(BF16)

    pa, pb, pc, wo = (t.astype(BF16) for t in (proj_a, proj_b, proj_c, w_out))
    ffn_up_b, ffn_down_b = lax.optimization_barrier((ffn_up.astype(BF16), ffn_down.astype(BF16)))
    wg = _pad_cols(ffn_up_b[..., :d_ff], ffp)
    wv = _pad_cols(ffn_up_b[..., d_ff:], ffp)
    cw = _pad_cols(ffn_conv, ffp)
    wd = _pad_rows(ffn_down_b, ffp)

    x2 = x.reshape(m, d)
    for l in range(depth):
        p_swa, *p_dil, p_rwkv, gates = _proj(x2, norm1_g[l], w_all[l], n_rwkv_blk, n_gate_blk, tm_proj)

        kvw = A_HKV * HEAD_DIM
        y_a = _band_attention(p_swa.reshape(bsz, seq, -1), bias_a, attn_sinks[l], dil=1, q_blk=0,
                              k_blk=(A_HQ * HEAD_DIM) // kvw, v_blk=(A_HQ * HEAD_DIM) // kvw + 1,
                              q_w=A_HQ * HEAD_DIM, kv_w=kvw, nq=A_HQ, nkv=A_HKV, want_lse=False,
                              out_dtype=BF16, name="attn_swa", nqb=4)
        oc, lc = [], []
        for gi, (win, dil) in enumerate(C_GROUPS):
            pv = p_dil[gi].reshape(bsz, seq // dil, -1)
            o, ls = _band_attention(pv, bias_c[gi], None, dil=dil, q_blk=0, k_blk=1, v_blk=2, q_w=C_OUT,
                                    kv_w=C_OUT, nq=C_HG, nkv=C_HG, want_lse=True, out_dtype=F32,
                                    name=f"attn_dil{dil}", nqb=min(4, seq // dil // BLK))
            oc.append(o.reshape(m // dil, dil * C_OUT))
            lc.append(ls.reshape(m // dil, dil * C_OUT))

        r, lw, k2, v, kk, bb, g, bonus = _rwkv_prep(
            p_rwkv, seq, mu_pad[l], rwkv_w0[l], wup_pad[l], rwkv_a0[l], aup_pad[l], rwkv_g_up[l],
            rwkv_k_k[l], rwkv_k_a[l], r_k[l], e_bd, tm_tok)
        sh = lambda t: t.reshape(bsz, seq, B_WIDTH)
        y_raw = _wkv_scan(sh(r), sh(lw), sh(k2), sh(v), sh(kk), sh(bb), ts_wkv).reshape(m, B_WIDTH)

        merged = _merge(y_a.reshape(m, A_HQ * HEAD_DIM), oc[0], oc[1], oc[2], lc[0], lc[1], lc[2], y_raw, g,
                        bonus, gates, rwkv_lnx_g[l], rwkv_lnx_b[l], e_bd, pa[l], pb[l], pc[l], tm_tok)
        x2 = _out_proj(x2, merged, wo[l], tm_proj)
        x2 = _ffn(x2, seq, norm2_g[l], wg[l], wv[l], cw[l], wd[l], final_g, l == depth - 1, tm_ffn, 512)
    return x2.reshape(bsz, seq, d)
```

```python
import functools
import math

import jax
import jax.numpy as jnp
import numpy as np
from jax import lax
from jax.experimental import pallas as pl
from jax.experimental.pallas import tpu as pltpu

F32 = jnp.float32
BF16 = jnp.bfloat16

HEAD_DIM = 64
LANES = 128
BLK = 128
NORM_EPS = 1e-5
A_HQ, A_HKV = 8, 2
B_HEADS = 12
B_WIDTH = B_HEADS * HEAD_DIM
LORA_DECAY, LORA_ICLR, LORA_GATE = 96, 96, 256
LORA_PAD = 128
B_GN_EPS = 64e-5
C_GROUPS = ((128, 1), (512, 4), (2048, 16))
C_HG = 4
C_WIDTH = C_HG * len(C_GROUPS) * HEAD_DIM
C_OUT = C_HG * HEAD_DIM
N_BUCKETS, MAX_EXACT, REL_MAX_DIST = 32, 16, 2048
D_FF_PAD_TO = 512
NEG_BIG = -1e30

N_ATTN = A_HQ * HEAD_DIM + 2 * A_HKV * HEAD_DIM + 3 * C_WIDTH
N_RWKV = 3 * B_WIDTH + 2 * LORA_PAD + LORA_GATE + 256
OFF_WD = 3 * B_WIDTH
OFF_AD = OFF_WD + LORA_PAD
OFF_GD = OFF_AD + LORA_PAD
WKV_CHUNK = 64
WKV_UNROLL = 2
HALO = 16

VMEM_LIMIT = 56 * 1024 * 1024


def _cparams(sem):
    return pltpu.CompilerParams(dimension_semantics=sem, vmem_limit_bytes=VMEM_LIMIT)


def _dot(a, b):
    return jnp.dot(a, b, preferred_element_type=F32)


def _dot_nt(a, b):
    return lax.dot_general(a, b, (((1,), (1,)), ((), ())), preferred_element_type=F32)


def _split2(x):
    hi = x.astype(BF16)
    lo = (x - hi.astype(F32)).astype(BF16)
    return hi, lo


def _head_sums(a, e_bf16, exact=False):
    if not exact:
        return _dot(a.astype(BF16), e_bf16)
    h, l = _split2(a)
    return _dot(h, e_bf16) + _dot(l, e_bf16)


def _dot_exact_lhs(a_bf16, b):
    h, l = _split2(b)
    return _dot(a_bf16, h) + _dot(a_bf16, l)


def _dot3(a, b):
    ah, al = _split2(a)
    bh, bl = _split2(b)
    return _dot(ah, bh) + _dot(ah, bl) + _dot(al, bh)


def _rms(x, g):
    ms = jnp.mean(x * x, axis=-1, keepdims=True)
    return x * lax.rsqrt(ms + NORM_EPS) * g


def _proj_kernel(x_ref, g_ref, w_ref, pa_ref, p1_ref, p4_ref, p16_ref, pr_ref, pg_ref, h_ref, acc_ref, *,
                 n_rwkv):
    j = pl.program_id(1)
    tm = x_ref.shape[0]
    w = w_ref.shape[1]
    n_attn = 1 + len(C_GROUPS)

    @pl.when(j == 0)
    def _():
        h_ref[...] = _rms(x_ref[...], g_ref[...]).astype(BF16)

    def acc():
        return _dot(h_ref[...], w_ref[...])

    @pl.when(j == 0)
    def _():
        pa_ref[...] = acc().astype(BF16)

    @pl.when(j == 1)
    def _():
        p1_ref[...] = acc().astype(BF16)

    for jj, dil, ref in ((2, C_GROUPS[1][1], p4_ref), (3, C_GROUPS[2][1], p16_ref)):
        @pl.when(j == jj)
        def _(dil=dil, ref=ref):
            a = acc()
            for c in range(w // LANES):
                acc_ref[c] = a[:, c * LANES:(c + 1) * LANES]
            for r in range(dil):
                for c in range(w // LANES):
                    ref[:, r * w + c * LANES:r * w + (c + 1) * LANES] = (
                        acc_ref[c, pl.ds(r, tm // dil, stride=dil), :].astype(BF16))

    @pl.when(jnp.logical_and(j >= n_attn, j < n_attn + n_rwkv))
    def _():
        pr_ref[...] = acc().astype(BF16)

    @pl.when(j >= n_attn + n_rwkv)
    def _():
        pg_ref[...] = jax.nn.sigmoid(acc()).astype(BF16)


def _proj(x2, g, w, n_rwkv, n_gate, tm):
    m, d = x2.shape
    n_attn = 1 + len(C_GROUPS)
    nblk = n_attn + n_rwkv + n_gate
    wq = w.shape[1] // nblk
    d4, d16 = C_GROUPS[1][1], C_GROUPS[2][1]
    return pl.pallas_call(
        functools.partial(_proj_kernel, n_rwkv=n_rwkv),
        grid=(m // tm, nblk),
        in_specs=[
            pl.BlockSpec((tm, d), lambda i, j: (i, 0)),
            pl.BlockSpec((1, d), lambda i, j: (0, 0)),
            pl.BlockSpec((d, wq), lambda i, j: (0, j)),
        ],
        out_specs=[
            pl.BlockSpec((tm, wq), lambda i, j: (i, 0)),
            pl.BlockSpec((tm, wq), lambda i, j: (i, 0)),
            pl.BlockSpec((tm // d4, d4 * wq), lambda i, j: (i, 0)),
            pl.BlockSpec((tm // d16, d16 * wq), lambda i, j: (i, 0)),
            pl.BlockSpec((tm, wq), lambda i, j: (i, jnp.clip(j - n_attn, 0, n_rwkv - 1))),
            pl.BlockSpec((tm, wq), lambda i, j: (i, jnp.clip(j - n_attn - n_rwkv, 0, n_gate - 1))),
        ],
        out_shape=[
            jax.ShapeDtypeStruct((m, wq), BF16),
            jax.ShapeDtypeStruct((m, wq), BF16),
            jax.ShapeDtypeStruct((m // d4, d4 * wq), BF16),
            jax.ShapeDtypeStruct((m // d16, d16 * wq), BF16),
            jax.ShapeDtypeStruct((m, n_rwkv * wq), BF16),
            jax.ShapeDtypeStruct((m, n_gate * wq), BF16),
        ],
        scratch_shapes=[pltpu.VMEM((tm, d), BF16), pltpu.VMEM((wq // LANES, tm, LANES), F32)],
        compiler_params=_cparams(("parallel", "arbitrary")),
        name="proj",
    )(x2, g.reshape(1, d), w)


def _band_attn_kernel(*refs, nq, nkv, has_sink, want_lse):
    it = iter(refs)
    q_ref, kp_ref, kc_ref, vp_ref, vc_ref, bias_ref = (next(it) for _ in range(6))
    sink_ref = next(it) if has_sink else None
    o_ref = next(it)
    lse_ref = next(it) if want_lse else None

    nqb = q_ref.shape[0] // BLK
    first = pl.program_id(2) == 0
    q = q_ref[...] * jnp.asarray(HEAD_DIM ** -0.5, BF16)
    k = jnp.concatenate([kp_ref[...], kc_ref[...]], axis=0)
    v = jnp.concatenate([vp_ref[...], vc_ref[...]], axis=0)
    col = lax.broadcasted_iota(jnp.int32, (BLK, 2 * BLK), 1)
    edge = jnp.where(col < BLK, jnp.where(first, NEG_BIG, 0.0), 0.0)
    rep = nq // nkv
    units = [(i, h) for i in range(nqb) for h in range(nq)]
    hd = lambda t, i: t[:, i * HEAD_DIM:(i + 1) * HEAD_DIM]
    qs = lambda i, h: hd(q[i * BLK:(i + 1) * BLK], h)
    win = lambda t, i, h: hd(t[i * BLK:(i + 2) * BLK], h // rep)
    s = [_dot_nt(qs(i, h), win(k, i, h)) + bias_ref[h] for i, h in units]
    s = [s[u] + edge if i == 0 else s[u] for u, (i, h) in enumerate(units)]
    m = [jnp.max(t, axis=-1, keepdims=True) for t in s]
    if has_sink:
        m = [jnp.maximum(m[u], sink_ref[h]) for u, (i, h) in enumerate(units)]
    p = [jnp.exp(s[u] - m[u]).astype(BF16) for u in range(len(units))]
    ones = jnp.ones((2 * BLK, LANES), BF16)
    l = [_dot(t, ones)[:, :HEAD_DIM] for t in p]
    denom = [l[u] + jnp.exp(sink_ref[h] - m[u]) for u, (i, h) in enumerate(units)] if has_sink else l
    o = [_dot(p[u], win(v, i, h)) / denom[u] for u, (i, h) in enumerate(units)]
    rows = lambda parts: jnp.concatenate(
        [jnp.concatenate(parts[i * nq:(i + 1) * nq], axis=-1) for i in range(nqb)], axis=0)
    o_ref[...] = rows(o).astype(o_ref.dtype)
    if want_lse:
        lse_ref[...] = rows([m[u] + jnp.log(l[u]) for u in range(len(units))])


def _band_attention(pv, bias, sink, *, dil, q_blk, k_blk, v_blk, q_w, kv_w, nq, nkv, want_lse, out_dtype, name,
                    nqb):
    b, lf, nd = pv.shape
    n = nd // dil
    nb = lf // (BLK * nqb)
    qpr, kpr = n // q_w, n // kv_w
    has_sink = sink is not None
    prev = lambda j: jnp.maximum(j * nqb - 1, 0)

    in_specs = [
        pl.BlockSpec((None, nqb * BLK, q_w), lambda bi, r, j: (bi, j, r * qpr + q_blk)),
        pl.BlockSpec((None, BLK, kv_w), lambda bi, r, j: (bi, prev(j), r * kpr + k_blk)),
        pl.BlockSpec((None, nqb * BLK, kv_w), lambda bi, r, j: (bi, j, r * kpr + k_blk)),
        pl.BlockSpec((None, BLK, kv_w), lambda bi, r, j: (bi, prev(j), r * kpr + v_blk)),
        pl.BlockSpec((None, nqb * BLK, kv_w), lambda bi, r, j: (bi, j, r * kpr + v_blk)),
        pl.BlockSpec((nq, BLK, 2 * BLK), lambda bi, r, j: (0, 0, 0)),
    ]
    args = [pv, pv, pv, pv, pv, bias]
    if has_sink:
        in_specs.append(pl.BlockSpec(memory_space=pltpu.SMEM))
        args.append(sink)
    ow = nq * HEAD_DIM
    out_spec = pl.BlockSpec((None, nqb * BLK, ow), lambda bi, r, j: (bi, j, r))
    out_shape = jax.ShapeDtypeStruct((b, lf, dil * ow), out_dtype)
    if want_lse:
        out_specs = [out_spec, out_spec]
        out_shapes = [out_shape, jax.ShapeDtypeStruct((b, lf, dil * ow), F32)]
    else:
        out_specs, out_shapes = out_spec, out_shape
    return pl.pallas_call(
        functools.partial(_band_attn_kernel, nq=nq, nkv=nkv, has_sink=has_sink, want_lse=want_lse),
        grid=(b, dil, nb),
        in_specs=in_specs,
        out_specs=out_specs,
        out_shape=out_shapes,
        compiler_params=_cparams(("parallel", "parallel", "arbitrary")),
        name=name,
    )(*args)


def _rwkv_prep_kernel(ph_ref, p_ref, mu_ref, w0_ref, wup_ref, a0_ref, aup_ref, gup_ref, kk_ref, ka_ref,
                      rk_ref, e_ref, r_o, lw_o, k_o, v_o, kk_o, b_o, g_o, bonus_o, *, tiles_per_seq):
    tm = p_ref.shape[0]
    p = p_ref[...].astype(F32)
    seq_start = pl.program_id(0) % tiles_per_seq == 0
    last = ph_ref[...].astype(F32)[HALO - 1:HALO, :]
    last = jnp.where(seq_start, 0.0, last)
    row = lax.broadcasted_iota(jnp.int32, (tm, 1), 0)
    prev = jnp.where(row == 0, last, pltpu.roll(p, 1, 0))
    pf = p + (prev - p) * mu_ref[...]

    r = pf[:, 0:B_WIDTH]
    k = pf[:, B_WIDTH:2 * B_WIDTH]
    v = pf[:, 2 * B_WIDTH:3 * B_WIDTH]
    wd = pf[:, OFF_WD:OFF_WD + LORA_PAD]
    ad = pf[:, OFF_AD:OFF_AD + LORA_PAD]
    gd = pf[:, OFF_GD:OFF_GD + LORA_GATE]

    z = w0_ref[...] + _dot3(jnp.tanh(wd), wup_ref[...])
    nz = -z
    softplus = jnp.maximum(nz, 0.0) + jnp.log(1.0 + jnp.exp(-jnp.abs(nz)))
    w = -softplus - 0.5
    lw_o[...] = -jnp.exp(w)
    a = jax.nn.sigmoid(a0_ref[...] + _dot3(ad, aup_ref[...]))
    g_o[...] = _dot3(jax.nn.sigmoid(gd), gup_ref[...])

    e = e_ref[...]
    kk = k * kk_ref[...]
    nrm = jnp.sqrt(_head_sums(kk * kk, e))
    kk = kk / jnp.maximum(nrm, 1e-12)
    k2 = k * (1.0 + (a - 1.0) * ka_ref[...])
    r_o[...] = r
    k_o[...] = k2
    v_o[...] = v
    kk_o[...] = kk
    b_o[...] = kk * a
    bonus_o[...] = _head_sums(r * k2 * rk_ref[...], e) * v


def _rwkv_prep(pb2, seq, mu, w0, wup, a0, aup, gup, k_k, k_a, r_k, e_bd, tm):
    m, n = pb2.shape
    tps = seq // tm
    row = lambda a: a.reshape(1, -1)
    full = lambda a: pl.BlockSpec(a.shape, lambda i: (0,) * a.ndim)
    args = [pb2, pb2, row(mu), row(w0), wup, row(a0), aup, gup, row(k_k), row(k_a), row(r_k), e_bd]
    in_specs = [
        pl.BlockSpec((HALO, n), lambda i: (jnp.maximum(i * (tm // HALO) - 1, 0), 0)),
        pl.BlockSpec((tm, n), lambda i: (i, 0)),
    ] + [full(a) for a in args[2:]]
    o_spec = pl.BlockSpec((tm, B_WIDTH), lambda i: (i, 0))
    o_shape = jax.ShapeDtypeStruct((m, B_WIDTH), F32)
    return pl.pallas_call(
        functools.partial(_rwkv_prep_kernel, tiles_per_seq=tps),
        grid=(m // tm,),
        in_specs=in_specs,
        out_specs=[o_spec] * 8,
        out_shape=[o_shape] * 8,
        compiler_params=_cparams(("parallel",)),
        name="rwkv_prep",
    )(*args)


def _wkv_kernel(r_ref, lw_ref, k_ref, v_ref, kk_ref, b_ref, y_ref, st_ref):
    c = WKV_CHUNK
    nb = r_ref.shape[0]
    n_chunks = r_ref.shape[1] // c

    @pl.when(pl.program_id(0) == 0)
    def _():
        st_ref[...] = jnp.zeros_like(st_ref)

    rowi = lax.broadcasted_iota(jnp.int32, (c, c), 0)
    coli = lax.broadcasted_iota(jnp.int32, (c, c), 1)
    incl = rowi >= coli
    strict = rowi > coli
    row2 = lax.broadcasted_iota(jnp.int32, (2 * c, 2 * c), 0)
    col2 = lax.broadcasted_iota(jnp.int32, (2 * c, 2 * c), 1)
    tri2 = row2 - jnp.where(row2 < c, 1, c) >= jnp.where(col2 >= c, col2 - c, 2 * c)
    tri = jnp.where(incl, 1.0, 0.0).astype(BF16)
    eye = jnp.where(rowi == coli, 1.0, 0.0)

    nu = WKV_UNROLL
    heads = range(B_HEADS)
    hsl = [slice(h * HEAD_DIM, (h + 1) * HEAD_DIM) for h in heads]
    segs = [(b, u) for b in range(nb) for u in range(nu)]
    ns = range(len(segs))
    units = [(s, h) for s in ns for h in heads]

    def chunks(ci, carry):
        sls = [pl.ds(pl.multiple_of((ci * nu + u) * c, c), c) for u in range(nu)]
        ld = lambda ref, s: ref[segs[s][0], sls[segs[s][1]], :]
        lw = [ld(lw_ref, s) for s in ns]
        cum = [_dot_exact_lhs(tri, t) for t in lw]
        tot = [t[c - 1:c, :] for t in cum]
        p_inv = [jnp.exp(-t) for t in cum]
        p_rest = [jnp.exp(tot[s] - cum[s]) for s in ns]
        p_tot = [jnp.exp(t) for t in tot]
        rh_all = [(ld(r_ref, s) * jnp.exp(cum[s])).astype(BF16) for s in ns]
        ah_all = [(-ld(kk_ref, s) * jnp.exp(cum[s] - lw[s])).astype(BF16) for s in ns]
        b_all = [ld(b_ref, s) for s in ns]
        k_all = [ld(k_ref, s) for s in ns]
        bh_all = [(b_all[s] * p_inv[s]).astype(BF16) for s in ns]
        kh_all = [(k_all[s] * p_inv[s]).astype(BF16) for s in ns]
        bt_all = [(b_all[s] * p_rest[s]).astype(BF16) for s in ns]
        kt_all = [(k_all[s] * p_rest[s]).astype(BF16) for s in ns]
        v_all = [ld(v_ref, s) for s in ns]

        ah = [ah_all[u][:, hsl[h]] for u, h in units]
        rh = [rh_all[u][:, hsl[h]] for u, h in units]
        vf = [v_all[u][:, hsl[h]] for u, h in units]
        vb = [t.astype(BF16) for t in vf]
        n = range(len(units))
        ar = [jnp.concatenate([ah[i], rh[i]], axis=0) for i in n]
        g = [_dot_nt(ar[i], jnp.concatenate([bh_all[u][:, hsl[h]], kh_all[u][:, hsl[h]]], axis=0))
             for i, (u, h) in enumerate(units)]
        gb = [t[:, :c] for t in g]
        a_rb = [jnp.where(incl, t[c:], 0.0).astype(BF16) for t in gb]
        akrk = [jnp.where(tri2, t, 0.0).astype(BF16) for t in g]
        x0 = [jnp.where(strict, t[:c], 0.0) for t in gb]
        t = [eye + xi for xi in x0]
        xb = [xi.astype(BF16) for xi in x0]
        x = [_dot(xi, xi) for xi in xb]
        for _ in range(int(math.log2(c)) - 2):
            xt = [_dot(jnp.concatenate([x[i], t[i]], axis=0).astype(BF16), x[i].astype(BF16)) for i in n]
            x = [p[:c] for p in xt]
            t = [t[i] + xt[i][c:] for i in n]
        t = [t[i] + _dot(t[i].astype(BF16), x[i].astype(BF16)) for i in n]
        tb = [ti.astype(BF16) for ti in t]
        zv = jnp.zeros((c, HEAD_DIM), BF16)
        w1y2 = [_dot(akrk[i], jnp.concatenate([zv, vb[i]], axis=0)) for i in n]
        w1 = [p[:c] for p in w1y2]
        y2 = [p[c:] for p in w1y2]
        n2 = [_dot(vf[i].T.astype(BF16), kt_all[u][:, hsl[h]]) for i, (u, h) in enumerate(units)]
        a2 = [_dot(tb[i], ah[i]).astype(BF16) for i in n]
        u1t = [_dot(tb[i], w1[i].astype(BF16)).T for i in n]
        chains = [(b, h) for b in range(nb) for h in heads]
        nc = range(len(chains))
        st = [st_ref[i] for i in nc]
        for u in range(nu):
            seg = [b * nu + u for b, h in chains]
            un = [seg[i] * B_HEADS + chains[i][1] for i in nc]
            hs = [hsl[h] for b, h in chains]
            s0b = [t.astype(BF16) for t in st]
            utb = [(_dot_nt(s0b[i], a2[un[i]]) + u1t[un[i]]).astype(BF16) for i in nc]
            ys = [_dot_nt(rh[un[i]], s0b[i]) + _dot_nt(a_rb[un[i]], utb[i]) + y2[un[i]] for i in nc]
            st = [st[i] * p_tot[seg[i]][:, hs[i]] + _dot(utb[i], bt_all[seg[i]][:, hs[i]]) + n2[un[i]] for i in nc]
            for b in range(nb):
                y_ref[b, sls[u], :] = jnp.concatenate(ys[b * B_HEADS:(b + 1) * B_HEADS], axis=-1)
        st_ref[...] = jnp.stack(st, axis=0)
        return carry

    lax.fori_loop(0, n_chunks // nu, chunks, 0)


def _wkv_scan(r, lw, k, v, kk, b, ts):
    bsz, s, w = r.shape
    spec = pl.BlockSpec((bsz, ts, w), lambda j: (0, j, 0))
    return pl.pallas_call(
        _wkv_kernel,
        grid=(s // ts,),
        in_specs=[spec] * 6,
        out_specs=spec,
        out_shape=jax.ShapeDtypeStruct((bsz, s, w), F32),
        scratch_shapes=[pltpu.VMEM((bsz * B_HEADS, HEAD_DIM, HEAD_DIM), F32)],
        compiler_params=_cparams(("arbitrary",)),
        name="wkv_scan",
    )(r, lw, k, v, kk, b)


def _merge_kernel(ya_ref, o1_ref, o2_ref, o3_ref, l1_ref, l2_ref, l3_ref, yr_ref, g_ref, bonus_ref,
                  gates_ref, lng_ref, lnb_ref, e_ref, pa_ref, pb_ref, pc_ref, out_ref, unf_ref):
    tm, d = out_ref.shape

    def unfold(ref, slot, dil):
        nc = C_OUT // LANES
        for r in range(dil):
            for c in range(nc):
                unf_ref[slot * nc + c, pl.ds(r, tm // dil, stride=dil), :] = (
                    ref[:, r * C_OUT + c * LANES:r * C_OUT + (c + 1) * LANES])
        return jnp.concatenate([unf_ref[slot * nc + c] for c in range(nc)], axis=-1)

    d2, d3 = C_GROUPS[1][1], C_GROUPS[2][1]
    o1, l1 = o1_ref[...], l1_ref[...]
    o2, l2 = unfold(o2_ref, 0, d2), unfold(l2_ref, 1, d2)
    o3, l3 = unfold(o3_ref, 2, d3), unfold(l3_ref, 3, d3)
    m = jnp.maximum(jnp.maximum(l1, l2), l3)
    e1, e2, e3 = jnp.exp(l1 - m), jnp.exp(l2 - m), jnp.exp(l3 - m)
    yc = (e1 * o1 + e2 * o2 + e3 * o3) / (e1 + e2 + e3)

    e = e_ref[...]
    y = yr_ref[...]
    mean = _head_sums(y, e, exact=True) * (1.0 / HEAD_DIM)
    dv = y - mean
    var = _head_sums(dv * dv, e) * (1.0 / HEAD_DIM)
    yb = dv * lax.rsqrt(var + B_GN_EPS) * lng_ref[...] + lnb_ref[...] + bonus_ref[...]
    yb = yb * g_ref[...]

    gates = gates_ref[...]
    merged = (gates[:, 0:d].astype(F32) * _dot(ya_ref[...], pa_ref[...])
              + gates[:, d:2 * d].astype(F32) * _dot(yb.astype(BF16), pb_ref[...])
              + gates[:, 2 * d:3 * d].astype(F32) * _dot(yc.astype(BF16), pc_ref[...]))
    out_ref[...] = merged.astype(BF16)


def _merge(ya, o1, o2, o3, l1, l2, l3, yr, g, bonus, gates, lng, lnb, e_bd, pa, pb, pc, tm):
    m, d = ya.shape[0], pa.shape[1]
    tok = lambda a: pl.BlockSpec((tm * a.shape[0] // m, a.shape[1]), lambda i: (i, 0))
    full = lambda a: pl.BlockSpec(a.shape, lambda i: (0,) * a.ndim, pipeline_mode=pl.Buffered(1))
    toks = [ya, o1, o2, o3, l1, l2, l3, yr, g, bonus, gates]
    consts = [lng.reshape(1, -1), lnb.reshape(1, -1), e_bd, pa, pb, pc]
    return pl.pallas_call(
        _merge_kernel,
        grid=(m // tm,),
        in_specs=[tok(a) for a in toks] + [full(a) for a in consts],
        out_specs=pl.BlockSpec((tm, d), lambda i: (i, 0)),
        out_shape=jax.ShapeDtypeStruct((m, d), BF16),
        scratch_shapes=[pltpu.VMEM((4 * C_OUT // LANES, tm, LANES), F32)],
        compiler_params=_cparams(("parallel",)),
        name="merge",
    )(*toks, *consts)


def _out_proj_kernel(x_ref, m_ref, w_ref, o_ref):
    o_ref[...] = x_ref[...] + _dot(m_ref[...], w_ref[...])


def _out_proj(x2, merged, wo, tm):
    m, d = x2.shape
    tok = pl.BlockSpec((tm, d), lambda i: (i, 0))
    return pl.pallas_call(
        _out_proj_kernel,
        grid=(m // tm,),
        in_specs=[tok, tok, pl.BlockSpec(wo.shape, lambda i: (0, 0), pipeline_mode=pl.Buffered(1))],
        out_specs=tok,
        out_shape=jax.ShapeDtypeStruct((m, d), F32),
        compiler_params=_cparams(("parallel",)),
        name="out_proj",
    )(x2, merged, wo)


def _ffn_kernel(xh_ref, x_ref, g_ref, wg_ref, wv_ref, cw_ref, wd_ref, fg_ref, o_ref, h_ref, *,
                tiles_per_seq, final_norm):
    j = pl.program_id(1)
    tm = x_ref.shape[0]

    @pl.when(j == 0)
    def _():
        x = x_ref[...]
        g = g_ref[...]
        h_ref[HALO:, :] = _rms(x, g).astype(BF16)
        seq_start = pl.program_id(0) % tiles_per_seq == 0
        halo = _rms(xh_ref[...], g)
        h_ref[0:HALO, :] = jnp.where(seq_start, 0.0, halo).astype(BF16)
        o_ref[...] = x

    h = h_ref[...]
    gext = _dot(h, wg_ref[...])
    val = _dot(h[HALO:], wv_ref[...])
    cw = cw_ref[...]
    gate = (cw[0:1, :] * pltpu.roll(gext, 2, 0)[HALO:]
            + cw[1:2, :] * pltpu.roll(gext, 1, 0)[HALO:]
            + cw[2:3, :] * gext[HALO:])
    hid = gate * jax.nn.sigmoid(gate) * val
    o_ref[...] += _dot(hid.astype(BF16), wd_ref[...])

    if final_norm:
        @pl.when(j == pl.num_programs(1) - 1)
        def _():
            o_ref[...] = _rms(o_ref[...], fg_ref[...])


def _ffn(x2, seq, g, wg, wv, cw, wd, fg, final_norm, tm, tf):
    m, d = x2.shape
    ffp = wg.shape[1]
    tps = seq // tm
    return pl.pallas_call(
        functools.partial(_ffn_kernel, tiles_per_seq=tps, final_norm=final_norm),
        grid=(m // tm, ffp // tf),
        in_specs=[
            pl.BlockSpec((HALO, d), lambda i, j: (jnp.maximum(i * (tm // HALO) - 1, 0), 0)),
            pl.BlockSpec((tm, d), lambda i, j: (i, 0)),
            pl.BlockSpec((1, d), lambda i, j: (0, 0)),
            pl.BlockSpec((d, tf), lambda i, j: (0, j)),
            pl.BlockSpec((d, tf), lambda i, j: (0, j)),
            pl.BlockSpec((3, tf), lambda i, j: (0, j)),
            pl.BlockSpec((tf, d), lambda i, j: (j, 0)),
            pl.BlockSpec((1, d), lambda i, j: (0, 0)),
        ],
        out_specs=pl.BlockSpec((tm, d), lambda i, j: (i, 0)),
        out_shape=jax.ShapeDtypeStruct((m, d), F32),
        scratch_shapes=[pltpu.VMEM((HALO + tm, d), BF16)],
        compiler_params=_cparams(("parallel", "arbitrary")),
        name="conv_ffn",
    )(x2, x2, g.reshape(1, d), wg, wv, cw, wd, fg.reshape(1, d))


def _t5_bucket(dist):
    small = dist < MAX_EXACT
    nf = jnp.maximum(dist, 1).astype(F32)
    large = MAX_EXACT + (jnp.log(nf / MAX_EXACT) / math.log(REL_MAX_DIST / MAX_EXACT)
                         * (N_BUCKETS - MAX_EXACT)).astype(jnp.int32)
    return jnp.where(small, dist, jnp.minimum(large, N_BUCKETS - 1))


def _band_bias(table, dilation, max_steps):
    i = jnp.arange(BLK)[:, None]
    j = jnp.arange(2 * BLK)[None, :]
    off = i + BLK - j
    onehot = jax.nn.one_hot(_t5_bucket(jnp.maximum(off, 0) * dilation), N_BUCKETS, dtype=F32)
    bias = jnp.einsum("ijb,bh->hij", onehot, table.astype(F32), precision=lax.Precision.HIGHEST)
    valid = (off >= 0) & (off <= max_steps)
    return jnp.where(valid[None], bias, NEG_BIG)


def _pad_cols(w, n):
    return jnp.pad(w, [(0, 0)] * (w.ndim - 1) + [(0, n - w.shape[-1])])


def _pad_rows(w, n):
    return jnp.pad(w, [(0, 0)] * (w.ndim - 2) + [(0, n - w.shape[-2]), (0, 0)])


def kernel(x, rel_bias, norm1_g, w_in, attn_sinks, rwkv_mu, rwkv_w0, rwkv_w_up, rwkv_a0, rwkv_a_up, rwkv_g_up,
           rwkv_k_k, rwkv_k_a, rwkv_r_k, rwkv_lnx_g, rwkv_lnx_b, proj_a, proj_b, proj_c, w_out, norm2_g,
           ffn_up, ffn_conv, ffn_down, final_g):
    bsz, seq, d = x.shape
    depth = w_in.shape[0]
    m = bsz * seq
    d_ff = ffn_conv.shape[-1]
    ffp = -(-d_ff // D_FF_PAD_TO) * D_FF_PAD_TO

    tm_proj = min(1024, seq)
    tm_tok = min(512, seq)
    tm_ffn = min(1024, seq)
    ts_wkv = min(512, seq)

    na = A_HQ
    bias_a = _band_bias(rel_bias[:, :na], 1, BLK - 1)
    bias_c = [_band_bias(rel_bias[:, na + gi * C_HG:na + (gi + 1) * C_HG], dil, win // dil)
              for gi, (win, dil) in enumerate(C_GROUPS)]

    q_end = (A_HQ + 2 * A_HKV) * HEAD_DIM
    b0 = q_end
    b_r_end = b0 + 3 * B_WIDTH
    b_wd_end = b_r_end + LORA_DECAY
    b_ad_end = b_wd_end + LORA_ICLR
    b_end = b_ad_end + LORA_GATE
    c_end = b_end + 3 * C_WIDTH

    def rwkv_cols(t):
        parts = [t[..., b0:b_r_end], _pad_cols(t[..., b_r_end:b_wd_end], LORA_PAD),
                 _pad_cols(t[..., b_wd_end:b_ad_end], LORA_PAD), t[..., b_ad_end:b_end]]
        return _pad_cols(jnp.concatenate(parts, axis=-1), N_RWKV)

    w_in_b = lax.optimization_barrier(w_in.astype(BF16))
    qkv_c = [w_in_b[..., b_end + t * C_WIDTH + gi * C_OUT:b_end + t * C_WIDTH + (gi + 1) * C_OUT]
             for gi in range(len(C_GROUPS)) for t in range(3)]
    w_all = jnp.concatenate([w_in_b[..., :q_end]] + qkv_c + [rwkv_cols(w_in_b), w_in_b[..., c_end:]], axis=-1)
    proj_blk = q_end
    n_rwkv_blk = N_RWKV // proj_blk
    n_gate_blk = (w_in.shape[-1] - c_end) // proj_blk
    mu_pad = rwkv_cols(jnp.pad(rwkv_mu, ((0, 0), (b0, 0))))
    wup_pad = _pad_rows(rwkv_w_up, LORA_PAD)
    aup_pad = _pad_rows(rwkv_a_up, LORA_PAD)
    r_k = rwkv_r_k.reshape(depth, B_WIDTH)

    head_id = jnp.arange(B_WIDTH) // HEAD_DIM
    e_bd = (head_id[:, None] == head_id[None, :]).astype(BF16)

    pa, pb, pc, wo = (t.astype(BF16) for t in (proj_a, proj_b, proj_c, w_out))
    wg = _pad_cols(ffn_up[..., :d_ff], ffp).astype(BF16)
    wv = _pad_cols(ffn_up[..., d_ff:], ffp).astype(BF16)
    cw = _pad_cols(ffn_conv, ffp)
    wd = _pad_rows(ffn_down, ffp).astype(BF16)

    x2 = x.reshape(m, d)
    for l in range(depth):
        p_swa, *p_dil, p_rwkv, gates = _proj(x2, norm1_g[l], w_all[l], n_rwkv_blk, n_gate_blk, tm_proj)

        kvw = A_HKV * HEAD_DIM
        y_a = _band_attention(p_swa.reshape(bsz, seq, -1), bias_a, attn_sinks[l], dil=1, q_blk=0,
                              k_blk=(A_HQ * HEAD_DIM) // kvw, v_blk=(A_HQ * HEAD_DIM) // kvw + 1,
                              q_w=A_HQ * HEAD_DIM, kv_w=kvw, nq=A_HQ, nkv=A_HKV, want_lse=False,
                              out_dtype=BF16, name="attn_swa", nqb=4)
        oc, lc = [], []
        for gi, (win, dil) in enumerate(C_GROUPS):
            pv = p_dil[gi].reshape(bsz, seq // dil, -1)
            o, ls = _band_attention(pv, bias_c[gi], None, dil=dil, q_blk=0, k_blk=1, v_blk=2, q_w=C_OUT,
                                    kv_w=C_OUT, nq=C_HG, nkv=C_HG, want_lse=True, out_dtype=F32,
                                    name=f"attn_dil{dil}", nqb=min(4, seq // dil // BLK))
            oc.append(o.reshape(m // dil, dil * C_OUT))
            lc.append(ls.reshape(m // dil, dil * C_OUT))

        r, lw, k2, v, kk, bb, g, bonus = _rwkv_prep(
            p_rwkv, seq, mu_pad[l], rwkv_w0[l], wup_pad[l], rwkv_a0[l], aup_pad[l], rwkv_g_up[l],
            rwkv_k_k[l], rwkv_k_a[l], r_k[l], e_bd, tm_tok)
        sh = lambda t: t.reshape(bsz, seq, B_WIDTH)
        y_raw = _wkv_scan(sh(r), sh(lw), sh(k2), sh(v), sh(kk), sh(bb), ts_wkv).reshape(m, B_WIDTH)

        merged = _merge(y_a.reshape(m, A_HQ * HEAD_DIM), oc[0], oc[1], oc[2], lc[0], lc[1], lc[2], y_raw, g,
                        bonus, gates, rwkv_lnx_g[l], rwkv_lnx_b[l], e_bd, pa[l], pb[l], pc[l], tm_tok)
        x2 = _out_proj(x2, merged, wo[l], tm_proj)
        x2 = _ffn(x2, seq, norm2_g[l], wg[l], wv[l], cw[l], wd[l], final_g, l == depth - 1, tm_ffn, 512)
    return x2.reshape(bsz, seq, d)
```

```python
import functools
import math

import jax
import jax.numpy as jnp
import numpy as np
from jax import lax
from jax.experimental import pallas as pl
from jax.experimental.pallas import tpu as pltpu

F32 = jnp.float32
BF16 = jnp.bfloat16

HEAD_DIM = 64
LANES = 128
BLK = 128
NORM_EPS = 1e-5
A_HQ, A_HKV = 8, 2
B_HEADS = 12
B_WIDTH = B_HEADS * HEAD_DIM
LORA_DECAY, LORA_ICLR, LORA_GATE = 96, 96, 256
LORA_PAD = 128
B_GN_EPS = 64e-5
C_GROUPS = ((128, 1), (512, 4), (2048, 16))
C_HG = 4
C_WIDTH = C_HG * len(C_GROUPS) * HEAD_DIM
C_OUT = C_HG * HEAD_DIM
N_BUCKETS, MAX_EXACT, REL_MAX_DIST = 32, 16, 2048
D_FF_PAD_TO = 512
NEG_BIG = -1e30

N_ATTN = A_HQ * HEAD_DIM + 2 * A_HKV * HEAD_DIM + 3 * C_WIDTH
N_RWKV = 3 * B_WIDTH + 2 * LORA_PAD + LORA_GATE + 256
OFF_WD = 3 * B_WIDTH
OFF_AD = OFF_WD + LORA_PAD
OFF_GD = OFF_AD + LORA_PAD
WKV_CHUNK = 64
WKV_UNROLL = 2
HALO = 16

VMEM_LIMIT = 56 * 1024 * 1024


def _cparams(sem):
    return pltpu.CompilerParams(dimension_semantics=sem, vmem_limit_bytes=VMEM_LIMIT)


def _dot(a, b):
    return jnp.dot(a, b, preferred_element_type=F32)


def _dot_nt(a, b):
    return lax.dot_general(a, b, (((1,), (1,)), ((), ())), preferred_element_type=F32)


def _split2(x):
    hi = x.astype(BF16)
    lo = (x - hi.astype(F32)).astype(BF16)
    return hi, lo


def _head_sums(a, e_bf16, exact=False):
    if not exact:
        return _dot(a.astype(BF16), e_bf16)
    h, l = _split2(a)
    return _dot(h, e_bf16) + _dot(l, e_bf16)


def _dot_exact_lhs(a_bf16, b):
    h, l = _split2(b)
    return _dot(a_bf16, h) + _dot(a_bf16, l)


def _dot3(a, b):
    ah, al = _split2(a)
    bh, bl = _split2(b)
    return _dot(ah, bh) + _dot(ah, bl) + _dot(al, bh)


def _rms(x, g):
    ms = jnp.mean(x * x, axis=-1, keepdims=True)
    return x * lax.rsqrt(ms + NORM_EPS) * g


def _proj_kernel(x_ref, g_ref, w_ref, pa_ref, p1_ref, p4_ref, p16_ref, pr_ref, pg_ref, h_ref, acc_ref, *,
                 n_rwkv):
    j = pl.program_id(1)
    tm = x_ref.shape[0]
    w = w_ref.shape[1]
    n_attn = 1 + len(C_GROUPS)

    @pl.when(j == 0)
    def _():
        h_ref[...] = _rms(x_ref[...], g_ref[...]).astype(BF16)

    def acc():
        return _dot(h_ref[...], w_ref[...])

    @pl.when(j == 0)
    def _():
        pa_ref[...] = acc().astype(BF16)

    @pl.when(j == 1)
    def _():
        p1_ref[...] = acc().astype(BF16)

    for jj, dil, ref in ((2, C_GROUPS[1][1], p4_ref), (3, C_GROUPS[2][1], p16_ref)):
        @pl.when(j == jj)
        def _(dil=dil, ref=ref):
            a = acc()
            for c in range(w // LANES):
                acc_ref[c] = a[:, c * LANES:(c + 1) * LANES]
            for r in range(dil):
                for c in range(w // LANES):
                    ref[:, r * w + c * LANES:r * w + (c + 1) * LANES] = (
                        acc_ref[c, pl.ds(r, tm // dil, stride=dil), :].astype(BF16))

    @pl.when(jnp.logical_and(j >= n_attn, j < n_attn + n_rwkv))
    def _():
        pr_ref[...] = acc().astype(BF16)

    @pl.when(j >= n_attn + n_rwkv)
    def _():
        pg_ref[...] = jax.nn.sigmoid(acc()).astype(BF16)


def _proj(x2, g, w, n_rwkv, n_gate, tm):
    m, d = x2.shape
    n_attn = 1 + len(C_GROUPS)
    nblk = n_attn + n_rwkv + n_gate
    wq = w.shape[1] // nblk
    d4, d16 = C_GROUPS[1][1], C_GROUPS[2][1]
    return pl.pallas_call(
        functools.partial(_proj_kernel, n_rwkv=n_rwkv),
        grid=(m // tm, nblk),
        in_specs=[
            pl.BlockSpec((tm, d), lambda i, j: (i, 0)),
            pl.BlockSpec((1, d), lambda i, j: (0, 0)),
            pl.BlockSpec((d, wq), lambda i, j: (0, j)),
        ],
        out_specs=[
            pl.BlockSpec((tm, wq), lambda i, j: (i, 0)),
            pl.BlockSpec((tm, wq), lambda i, j: (i, 0)),
            pl.BlockSpec((tm // d4, d4 * wq), lambda i, j: (i, 0)),
            pl.BlockSpec((tm // d16, d16 * wq), lambda i, j: (i, 0)),
            pl.BlockSpec((tm, wq), lambda i, j: (i, jnp.clip(j - n_attn, 0, n_rwkv - 1))),
            pl.BlockSpec((tm, wq), lambda i, j: (i, jnp.clip(j - n_attn - n_rwkv, 0, n_gate - 1))),
        ],
        out_shape=[
            jax.ShapeDtypeStruct((m, wq), BF16),
            jax.ShapeDtypeStruct((m, wq), BF16),
            jax.ShapeDtypeStruct((m // d4, d4 * wq), BF16),
            jax.ShapeDtypeStruct((m // d16, d16 * wq), BF16),
            jax.ShapeDtypeStruct((m, n_rwkv * wq), BF16),
            jax.ShapeDtypeStruct((m, n_gate * wq), BF16),
        ],
        scratch_shapes=[pltpu.VMEM((tm, d), BF16), pltpu.VMEM((wq // LANES, tm, LANES), F32)],
        compiler_params=_cparams(("parallel", "arbitrary")),
        name="proj",
    )(x2, g.reshape(1, d), w)


def _band_attn_kernel(*refs, nq, nkv, has_sink, want_lse):
    it = iter(refs)
    q_ref, kp_ref, kc_ref, vp_ref, vc_ref, bias_ref = (next(it) for _ in range(6))
    sink_ref = next(it) if has_sink else None
    o_ref = next(it)
    lse_ref = next(it) if want_lse else None

    nqb = q_ref.shape[0] // BLK
    first = pl.program_id(2) == 0
    q = q_ref[...] * jnp.asarray(HEAD_DIM ** -0.5, BF16)
    k = jnp.concatenate([kp_ref[...], kc_ref[...]], axis=0)
    v = jnp.concatenate([vp_ref[...], vc_ref[...]], axis=0)
    col = lax.broadcasted_iota(jnp.int32, (BLK, 2 * BLK), 1)
    edge = jnp.where(col < BLK, jnp.where(first, NEG_BIG, 0.0), 0.0)
    rep = nq // nkv
    units = [(i, h) for i in range(nqb) for h in range(nq)]
    hd = lambda t, i: t[:, i * HEAD_DIM:(i + 1) * HEAD_DIM]
    qs = lambda i, h: hd(q[i * BLK:(i + 1) * BLK], h)
    win = lambda t, i, h: hd(t[i * BLK:(i + 2) * BLK], h // rep)
    s = [_dot_nt(qs(i, h), win(k, i, h)) + bias_ref[h] for i, h in units]
    s = [s[u] + edge if i == 0 else s[u] for u, (i, h) in enumerate(units)]
    m = [jnp.max(t, axis=-1, keepdims=True) for t in s]
    if has_sink:
        m = [jnp.maximum(m[u], sink_ref[h]) for u, (i, h) in enumerate(units)]
    p = [jnp.exp(s[u] - m[u]).astype(BF16) for u in range(len(units))]
    ones = jnp.ones((2 * BLK, LANES), BF16)
    l = [_dot(t, ones)[:, :HEAD_DIM] for t in p]
    denom = [l[u] + jnp.exp(sink_ref[h] - m[u]) for u, (i, h) in enumerate(units)] if has_sink else l
    o = [_dot(p[u], win(v, i, h)) / denom[u] for u, (i, h) in enumerate(units)]
    rows = lambda parts: jnp.concatenate(
        [jnp.concatenate(parts[i * nq:(i + 1) * nq], axis=-1) for i in range(nqb)], axis=0)
    o_ref[...] = rows(o).astype(o_ref.dtype)
    if want_lse:
        lse_ref[...] = rows([m[u] + jnp.log(l[u]) for u in range(len(units))])


def _band_attention(pv, bias, sink, *, dil, q_blk, k_blk, v_blk, q_w, kv_w, nq, nkv, want_lse, out_dtype, name,
                    nqb):
    b, lf, nd = pv.shape
    n = nd // dil
    nb = lf // (BLK * nqb)
    qpr, kpr = n // q_w, n // kv_w
    has_sink = sink is not None
    prev = lambda j: jnp.maximum(j * nqb - 1, 0)

    in_specs = [
        pl.BlockSpec((None, nqb * BLK, q_w), lambda bi, r, j: (bi, j, r * qpr + q_blk)),
        pl.BlockSpec((None, BLK, kv_w), lambda bi, r, j: (bi, prev(j), r * kpr + k_blk)),
        pl.BlockSpec((None, nqb * BLK, kv_w), lambda bi, r, j: (bi, j, r * kpr + k_blk)),
        pl.BlockSpec((None, BLK, kv_w), lambda bi, r, j: (bi, prev(j), r * kpr + v_blk)),
        pl.BlockSpec((None, nqb * BLK, kv_w), lambda bi, r, j: (bi, j, r * kpr + v_blk)),
        pl.BlockSpec((nq, BLK, 2 * BLK), lambda bi, r, j: (0, 0, 0)),
    ]
    args = [pv, pv, pv, pv, pv, bias]
    if has_sink:
        in_specs.append(pl.BlockSpec(memory_space=pltpu.SMEM))
        args.append(sink)
    ow = nq * HEAD_DIM
    out_spec = pl.BlockSpec((None, nqb * BLK, ow), lambda bi, r, j: (bi, j, r))
    out_shape = jax.ShapeDtypeStruct((b, lf, dil * ow), out_dtype)
    if want_lse:
        out_specs = [out_spec, out_spec]
        out_shapes = [out_shape, jax.ShapeDtypeStruct((b, lf, dil * ow), F32)]
    else:
        out_specs, out_shapes = out_spec, out_shape
    return pl.pallas_call(
        functools.partial(_band_attn_kernel, nq=nq, nkv=nkv, has_sink=has_sink, want_lse=want_lse),
        grid=(b, dil, nb),
        in_specs=in_specs,
        out_specs=out_specs,
        out_shape=out_shapes,
        compiler_params=_cparams(("parallel", "parallel", "arbitrary")),
        name=name,
    )(*args)


def _rwkv_prep_kernel(ph_ref, p_ref, mu_ref, w0_ref, wup_ref, a0_ref, aup_ref, gup_ref, kk_ref, ka_ref,
                      rk_ref, e_ref, r_o, lw_o, k_o, v_o, kk_o, b_o, g_o, bonus_o, *, tiles_per_seq):
    tm = p_ref.shape[0]
    p = p_ref[...].astype(F32)
    seq_start = pl.program_id(0) % tiles_per_seq == 0
    last = ph_ref[...].astype(F32)[HALO - 1:HALO, :]
    last = jnp.where(seq_start, 0.0, last)
    row = lax.broadcasted_iota(jnp.int32, (tm, 1), 0)
    prev = jnp.where(row == 0, last, pltpu.roll(p, 1, 0))
    pf = p + (prev - p) * mu_ref[...]

    r = pf[:, 0:B_WIDTH]
    k = pf[:, B_WIDTH:2 * B_WIDTH]
    v = pf[:, 2 * B_WIDTH:3 * B_WIDTH]
    wd = pf[:, OFF_WD:OFF_WD + LORA_PAD]
    ad = pf[:, OFF_AD:OFF_AD + LORA_PAD]
    gd = pf[:, OFF_GD:OFF_GD + LORA_GATE]

    z = w0_ref[...] + _dot3(jnp.tanh(wd), wup_ref[...])
    nz = -z
    softplus = jnp.maximum(nz, 0.0) + jnp.log(1.0 + jnp.exp(-jnp.abs(nz)))
    w = -softplus - 0.5
    lw_o[...] = -jnp.exp(w)
    a = jax.nn.sigmoid(a0_ref[...] + _dot3(ad, aup_ref[...]))
    g_o[...] = _dot3(jax.nn.sigmoid(gd), gup_ref[...])

    e = e_ref[...]
    kk = k * kk_ref[...]
    nrm = jnp.sqrt(_head_sums(kk * kk, e))
    kk = kk / jnp.maximum(nrm, 1e-12)
    k2 = k * (1.0 + (a - 1.0) * ka_ref[...])
    r_o[...] = r
    k_o[...] = k2
    v_o[...] = v
    kk_o[...] = kk
    b_o[...] = kk * a
    bonus_o[...] = _head_sums(r * k2 * rk_ref[...], e) * v


def _rwkv_prep(pb2, seq, mu, w0, wup, a0, aup, gup, k_k, k_a, r_k, e_bd, tm):
    m, n = pb2.shape
    tps = seq // tm
    row = lambda a: a.reshape(1, -1)
    full = lambda a: pl.BlockSpec(a.shape, lambda i: (0,) * a.ndim)
    args = [pb2, pb2, row(mu), row(w0), wup, row(a0), aup, gup, row(k_k), row(k_a), row(r_k), e_bd]
    in_specs = [
        pl.BlockSpec((HALO, n), lambda i: (jnp.maximum(i * (tm // HALO) - 1, 0), 0)),
        pl.BlockSpec((tm, n), lambda i: (i, 0)),
    ] + [full(a) for a in args[2:]]
    o_spec = pl.BlockSpec((tm, B_WIDTH), lambda i: (i, 0))
    o_shape = jax.ShapeDtypeStruct((m, B_WIDTH), F32)
    return pl.pallas_call(
        functools.partial(_rwkv_prep_kernel, tiles_per_seq=tps),
        grid=(m // tm,),
        in_specs=in_specs,
        out_specs=[o_spec] * 8,
        out_shape=[o_shape] * 8,
        compiler_params=_cparams(("parallel",)),
        name="rwkv_prep",
    )(*args)


def _wkv_kernel(r_ref, lw_ref, k_ref, v_ref, kk_ref, b_ref, y_ref, st_ref):
    c = WKV_CHUNK
    nb = r_ref.shape[0]
    n_chunks = r_ref.shape[1] // c

    @pl.when(pl.program_id(0) == 0)
    def _():
        st_ref[...] = jnp.zeros_like(st_ref)

    rowi = lax.broadcasted_iota(jnp.int32, (c, c), 0)
    coli = lax.broadcasted_iota(jnp.int32, (c, c), 1)
    incl = rowi >= coli
    strict = rowi > coli
    row2 = lax.broadcasted_iota(jnp.int32, (2 * c, 2 * c), 0)
    col2 = lax.broadcasted_iota(jnp.int32, (2 * c, 2 * c), 1)
    tri2 = row2 - jnp.where(row2 < c, 1, c) >= jnp.where(col2 >= c, col2 - c, 2 * c)
    tri = jnp.where(incl, 1.0, 0.0).astype(BF16)
    eye = jnp.where(rowi == coli, 1.0, 0.0)

    nu = WKV_UNROLL
    heads = range(B_HEADS)
    hsl = [slice(h * HEAD_DIM, (h + 1) * HEAD_DIM) for h in heads]
    segs = [(b, u) for b in range(nb) for u in range(nu)]
    ns = range(len(segs))
    units = [(s, h) for s in ns for h in heads]

    def chunks(ci, carry):
        sls = [pl.ds(pl.multiple_of((ci * nu + u) * c, c), c) for u in range(nu)]
        ld = lambda ref, s: ref[segs[s][0], sls[segs[s][1]], :]
        lw = [ld(lw_ref, s) for s in ns]
        cum = [_dot_exact_lhs(tri, t) for t in lw]
        tot = [t[c - 1:c, :] for t in cum]
        p_inv = [jnp.exp(-t) for t in cum]
        p_rest = [jnp.exp(tot[s] - cum[s]) for s in ns]
        p_tot = [jnp.exp(t) for t in tot]
        rh_all = [(ld(r_ref, s) * jnp.exp(cum[s])).astype(BF16) for s in ns]
        ah_all = [(-ld(kk_ref, s) * jnp.exp(cum[s] - lw[s])).astype(BF16) for s in ns]
        b_all = [ld(b_ref, s) for s in ns]
        k_all = [ld(k_ref, s) for s in ns]
        bh_all = [(b_all[s] * p_inv[s]).astype(BF16) for s in ns]
        kh_all = [(k_all[s] * p_inv[s]).astype(BF16) for s in ns]
        bt_all = [(b_all[s] * p_rest[s]).astype(BF16) for s in ns]
        kt_all = [(k_all[s] * p_rest[s]).astype(BF16) for s in ns]
        v_all = [ld(v_ref, s) for s in ns]

        ah = [ah_all[u][:, hsl[h]] for u, h in units]
        rh = [rh_all[u][:, hsl[h]] for u, h in units]
        vf = [v_all[u][:, hsl[h]] for u, h in units]
        vb = [t.astype(BF16) for t in vf]
        n = range(len(units))
        ar = [jnp.concatenate([ah[i], rh[i]], axis=0) for i in n]
        g = [_dot_nt(ar[i], jnp.concatenate([bh_all[u][:, hsl[h]], kh_all[u][:, hsl[h]]], axis=0))
             for i, (u, h) in enumerate(units)]
        gb = [t[:, :c] for t in g]
        a_rb = [jnp.where(incl, t[c:], 0.0).astype(BF16) for t in gb]
        akrk = [jnp.where(tri2, t, 0.0).astype(BF16) for t in g]
        x0 = [jnp.where(strict, t[:c], 0.0) for t in gb]
        t = [eye + xi for xi in x0]
        xb = [xi.astype(BF16) for xi in x0]
        x = [_dot(xi, xi) for xi in xb]
        for _ in range(int(math.log2(c)) - 2):
            xt = [_dot(jnp.concatenate([x[i], t[i]], axis=0).astype(BF16), x[i].astype(BF16)) for i in n]
            x = [p[:c] for p in xt]
            t = [t[i] + xt[i][c:] for i in n]
        t = [t[i] + _dot(t[i].astype(BF16), x[i].astype(BF16)) for i in n]
        tb = [ti.astype(BF16) for ti in t]
        zv = jnp.zeros((c, HEAD_DIM), BF16)
        w1y2 = [_dot(akrk[i], jnp.concatenate([zv, vb[i]], axis=0)) for i in n]
        w1 = [p[:c] for p in w1y2]
        y2 = [p[c:] for p in w1y2]
        n2 = [_dot(vf[i].T.astype(BF16), kt_all[u][:, hsl[h]]) for i, (u, h) in enumerate(units)]
        a2 = [_dot(tb[i], ah[i]).astype(BF16) for i in n]
        u1t = [_dot(tb[i], w1[i].astype(BF16)).T for i in n]
        chains = [(b, h) for b in range(nb) for h in heads]
        nc = range(len(chains))
        st = [st_ref[i] for i in nc]
        for u in range(nu):
            seg = [b * nu + u for b, h in chains]
            un = [seg[i] * B_HEADS + chains[i][1] for i in nc]
            hs = [hsl[h] for b, h in chains]
            s0b = [t.astype(BF16) for t in st]
            utb = [(_dot_nt(s0b[i], a2[un[i]]) + u1t[un[i]]).astype(BF16) for i in nc]
            ys = [_dot_nt(rh[un[i]], s0b[i]) + _dot_nt(a_rb[un[i]], utb[i]) + y2[un[i]] for i in nc]
            st = [st[i] * p_tot[seg[i]][:, hs[i]] + _dot(utb[i], bt_all[seg[i]][:, hs[i]]) + n2[un[i]] for i in nc]
            for b in range(nb):
                y_ref[b, sls[u], :] = jnp.concatenate(ys[b * B_HEADS:(b + 1) * B_HEADS], axis=-1)
        st_ref[...] = jnp.stack(st, axis=0)
        return carry

    lax.fori_loop(0, n_chunks // nu, chunks, 0)


def _wkv_scan(r, lw, k, v, kk, b, ts):
    bsz, s, w = r.shape
    spec = pl.BlockSpec((bsz, ts, w), lambda j: (0, j, 0))
    return pl.pallas_call(
        _wkv_kernel,
        grid=(s // ts,),
        in_specs=[spec] * 6,
        out_specs=spec,
        out_shape=jax.ShapeDtypeStruct((bsz, s, w), F32),
        scratch_shapes=[pltpu.VMEM((bsz * B_HEADS, HEAD_DIM, HEAD_DIM), F32)],
        compiler_params=_cparams(("arbitrary",)),
        name="wkv_scan",
    )(r, lw, k, v, kk, b)


def _merge_kernel(ya_ref, o1_ref, o2_ref, o3_ref, l1_ref, l2_ref, l3_ref, yr_ref, g_ref, bonus_ref,
                  gates_ref, lng_ref, lnb_ref, e_ref, pa_ref, pb_ref, pc_ref, out_ref, unf_ref):
    tm, d = out_ref.shape

    def unfold(ref, slot, dil):
        nc = C_OUT // LANES
        for r in range(dil):
            for c in range(nc):
                unf_ref[slot * nc + c, pl.ds(r, tm // dil, stride=dil), :] = (
                    ref[:, r * C_OUT + c * LANES:r * C_OUT + (c + 1) * LANES].astype(F32))
        return jnp.concatenate([unf_ref[slot * nc + c] for c in range(nc)], axis=-1)

    d2, d3 = C_GROUPS[1][1], C_GROUPS[2][1]
    o1, l1 = o1_ref[...].astype(F32), l1_ref[...]
    o2, l2 = unfold(o2_ref, 0, d2), unfold(l2_ref, 1, d2)
    o3, l3 = unfold(o3_ref, 2, d3), unfold(l3_ref, 3, d3)
    m = jnp.maximum(jnp.maximum(l1, l2), l3)
    e1, e2, e3 = jnp.exp(l1 - m), jnp.exp(l2 - m), jnp.exp(l3 - m)
    yc = (e1 * o1 + e2 * o2 + e3 * o3) / (e1 + e2 + e3)

    e = e_ref[...]
    y = yr_ref[...]
    mean = _head_sums(y, e, exact=True) * (1.0 / HEAD_DIM)
    dv = y - mean
    var = _head_sums(dv * dv, e) * (1.0 / HEAD_DIM)
    yb = dv * lax.rsqrt(var + B_GN_EPS) * lng_ref[...] + lnb_ref[...] + bonus_ref[...]
    yb = yb * g_ref[...]

    gates = gates_ref[...]
    merged = (gates[:, 0:d].astype(F32) * _dot(ya_ref[...], pa_ref[...])
              + gates[:, d:2 * d].astype(F32) * _dot(yb.astype(BF16), pb_ref[...])
              + gates[:, 2 * d:3 * d].astype(F32) * _dot(yc.astype(BF16), pc_ref[...]))
    out_ref[...] = merged.astype(BF16)


def _merge(ya, o1, o2, o3, l1, l2, l3, yr, g, bonus, gates, lng, lnb, e_bd, pa, pb, pc, tm):
    m, d = ya.shape[0], pa.shape[1]
    tok = lambda a: pl.BlockSpec((tm * a.shape[0] // m, a.shape[1]), lambda i: (i, 0))
    full = lambda a: pl.BlockSpec(a.shape, lambda i: (0,) * a.ndim, pipeline_mode=pl.Buffered(1))
    toks = [ya, o1, o2, o3, l1, l2, l3, yr, g, bonus, gates]
    consts = [lng.reshape(1, -1), lnb.reshape(1, -1), e_bd, pa, pb, pc]
    return pl.pallas_call(
        _merge_kernel,
        grid=(m // tm,),
        in_specs=[tok(a) for a in toks] + [full(a) for a in consts],
        out_specs=pl.BlockSpec((tm, d), lambda i: (i, 0)),
        out_shape=jax.ShapeDtypeStruct((m, d), BF16),
        scratch_shapes=[pltpu.VMEM((4 * C_OUT // LANES, tm, LANES), F32)],
        compiler_params=_cparams(("parallel",)),
        name="merge",
    )(*toks, *consts)


def _out_proj_kernel(x_ref, m_ref, w_ref, o_ref):
    o_ref[...] = x_ref[...] + _dot(m_ref[...], w_ref[...])


def _out_proj(x2, merged, wo, tm):
    m, d = x2.shape
    tok = pl.BlockSpec((tm, d), lambda i: (i, 0))
    return pl.pallas_call(
        _out_proj_kernel,
        grid=(m // tm,),
        in_specs=[tok, tok, pl.BlockSpec(wo.shape, lambda i: (0, 0), pipeline_mode=pl.Buffered(1))],
        out_specs=tok,
        out_shape=jax.ShapeDtypeStruct((m, d), F32),
        compiler_params=_cparams(("parallel",)),
        name="out_proj",
    )(x2, merged, wo)


def _ffn_kernel(xh_ref, x_ref, g_ref, wg_ref, wv_ref, cw_ref, wd_ref, fg_ref, o_ref, h_ref, *,
                tiles_per_seq, final_norm):
    j = pl.program_id(1)
    tm = x_ref.shape[0]

    @pl.when(j == 0)
    def _():
        x = x_ref[...]
        g = g_ref[...]
        h_ref[HALO:, :] = _rms(x, g).astype(BF16)
        seq_start = pl.program_id(0) % tiles_per_seq == 0
        halo = _rms(xh_ref[...], g)
        h_ref[0:HALO, :] = jnp.where(seq_start, 0.0, halo).astype(BF16)
        o_ref[...] = x

    h = h_ref[...]
    gext = _dot(h, wg_ref[...])
    val = _dot(h[HALO:], wv_ref[...])
    cw = cw_ref[...]
    gate = (cw[0:1, :] * pltpu.roll(gext, 2, 0)[HALO:]
            + cw[1:2, :] * pltpu.roll(gext, 1, 0)[HALO:]
            + cw[2:3, :] * gext[HALO:])
    hid = gate * jax.nn.sigmoid(gate) * val
    o_ref[...] += _dot(hid.astype(BF16), wd_ref[...])

    if final_norm:
        @pl.when(j == pl.num_programs(1) - 1)
        def _():
            o_ref[...] = _rms(o_ref[...], fg_ref[...])


def _ffn(x2, seq, g, wg, wv, cw, wd, fg, final_norm, tm, tf):
    m, d = x2.shape
    ffp = wg.shape[1]
    tps = seq // tm
    return pl.pallas_call(
        functools.partial(_ffn_kernel, tiles_per_seq=tps, final_norm=final_norm),
        grid=(m // tm, ffp // tf),
        in_specs=[
            pl.BlockSpec((HALO, d), lambda i, j: (jnp.maximum(i * (tm // HALO) - 1, 0), 0)),
            pl.BlockSpec((tm, d), lambda i, j: (i, 0)),
            pl.BlockSpec((1, d), lambda i, j: (0, 0)),
            pl.BlockSpec((d, tf), lambda i, j: (0, j)),
            pl.BlockSpec((d, tf), lambda i, j: (0, j)),
            pl.BlockSpec((3, tf), lambda i, j: (0, j)),
            pl.BlockSpec((tf, d), lambda i, j: (j, 0)),
            pl.BlockSpec((1, d), lambda i, j: (0, 0)),
        ],
        out_specs=pl.BlockSpec((tm, d), lambda i, j: (i, 0)),
        out_shape=jax.ShapeDtypeStruct((m, d), F32),
        scratch_shapes=[pltpu.VMEM((HALO + tm, d), BF16)],
        compiler_params=_cparams(("parallel", "arbitrary")),
        name="conv_ffn",
    )(x2, x2, g.reshape(1, d), wg, wv, cw, wd, fg.reshape(1, d))


def _t5_bucket(dist):
    small = dist < MAX_EXACT
    nf = jnp.maximum(dist, 1).astype(F32)
    large = MAX_EXACT + (jnp.log(nf / MAX_EXACT) / math.log(REL_MAX_DIST / MAX_EXACT)
                         * (N_BUCKETS - MAX_EXACT)).astype(jnp.int32)
    return jnp.where(small, dist, jnp.minimum(large, N_BUCKETS - 1))


def _band_bias(table, dilation, max_steps):
    i = jnp.arange(BLK)[:, None]
    j = jnp.arange(2 * BLK)[None, :]
    off = i + BLK - j
    onehot = jax.nn.one_hot(_t5_bucket(jnp.maximum(off, 0) * dilation), N_BUCKETS, dtype=F32)
    bias = jnp.einsum("ijb,bh->hij", onehot, table.astype(F32), precision=lax.Precision.HIGHEST)
    valid = (off >= 0) & (off <= max_steps)
    return jnp.where(valid[None], bias, NEG_BIG)


def _pad_cols(w, n):
    return jnp.pad(w, [(0, 0)] * (w.ndim - 1) + [(0, n - w.shape[-1])])


def _pad_rows(w, n):
    return jnp.pad(w, [(0, 0)] * (w.ndim - 2) + [(0, n - w.shape[-2]), (0, 0)])


def kernel(x, rel_bias, norm1_g, w_in, attn_sinks, rwkv_mu, rwkv_w0, rwkv_w_up, rwkv_a0, rwkv_a_up, rwkv_g_up,
           rwkv_k_k, rwkv_k_a, rwkv_r_k, rwkv_lnx_g, rwkv_lnx_b, proj_a, proj_b, proj_c, w_out, norm2_g,
           ffn_up, ffn_conv, ffn_down, final_g):
    bsz, seq, d = x.shape
    depth = w_in.shape[0]
    m = bsz * seq
    d_ff = ffn_conv.shape[-1]
    ffp = -(-d_ff // D_FF_PAD_TO) * D_FF_PAD_TO

    tm_proj = min(1024, seq)
    tm_tok = min(512, seq)
    tm_ffn = min(1024, seq)
    ts_wkv = min(512, seq)

    na = A_HQ
    bias_a = _band_bias(rel_bias[:, :na], 1, BLK - 1)
    bias_c = [_band_bias(rel_bias[:, na + gi * C_HG:na + (gi + 1) * C_HG], dil, win // dil)
              for gi, (win, dil) in enumerate(C_GROUPS)]

    q_end = (A_HQ + 2 * A_HKV) * HEAD_DIM
    b0 = q_end
    b_r_end = b0 + 3 * B_WIDTH
    b_wd_end = b_r_end + LORA_DECAY
    b_ad_end = b_wd_end + LORA_ICLR
    b_end = b_ad_end + LORA_GATE
    c_end = b_end + 3 * C_WIDTH

    def rwkv_cols(t):
        parts = [t[..., b0:b_r_end], _pad_cols(t[..., b_r_end:b_wd_end], LORA_PAD),
                 _pad_cols(t[..., b_wd_end:b_ad_end], LORA_PAD), t[..., b_ad_end:b_end]]
        return _pad_cols(jnp.concatenate(parts, axis=-1), N_RWKV)

    w_in_b = lax.optimization_barrier(w_in.astype(BF16))
    tail = lax.optimization_barrier(w_in_b[..., b_ad_end:])
    t_c, t_g = b_end - b_ad_end, c_end - b_ad_end
    qkv_c = [tail[..., t_c + t * C_WIDTH + gi * C_OUT:t_c + t * C_WIDTH + (gi + 1) * C_OUT]
             for gi in range(len(C_GROUPS)) for t in range(3)]
    rwkv_w = _pad_cols(jnp.concatenate(
        [w_in_b[..., b0:b_r_end], _pad_cols(w_in_b[..., b_r_end:b_wd_end], LORA_PAD),
         _pad_cols(w_in_b[..., b_wd_end:b_ad_end], LORA_PAD), tail[..., :t_c]], axis=-1), N_RWKV)
    w_all = jnp.concatenate([w_in_b[..., :q_end]] + qkv_c + [rwkv_w, tail[..., t_g:]], axis=-1)
    proj_blk = q_end
    n_rwkv_blk = N_RWKV // proj_blk
    n_gate_blk = (w_in.shape[-1] - c_end) // proj_blk
    mu_pad = rwkv_cols(jnp.pad(rwkv_mu, ((0, 0), (b0, 0))))
    wup_pad = _pad_rows(rwkv_w_up, LORA_PAD)
    aup_pad = _pad_rows(rwkv_a_up, LORA_PAD)
    r_k = rwkv_r_k.reshape(depth, B_WIDTH)

    head_id = jnp.arange(B_WIDTH) // HEAD_DIM
    e_bd = (head_id[:, None] == head_id[None, :]).astype(BF16)

    pa, pb, pc, wo = (t.astype(BF16) for t in (proj_a, proj_b, proj_c, w_out))
    wg = _pad_cols(ffn_up[..., :d_ff], ffp).astype(BF16)
    wv = _pad_cols(ffn_up[..., d_ff:], ffp).astype(BF16)
    cw = _pad_cols(ffn_conv, ffp)
    wd = _pad_rows(ffn_down, ffp).astype(BF16)

    x2 = x.reshape(m, d)
    for l in range(depth):
        p_swa, *p_dil, p_rwkv, gates = _proj(x2, norm1_g[l], w_all[l], n_rwkv_blk, n_gate_blk, tm_proj)

        kvw = A_HKV * HEAD_DIM
        y_a = _band_attention(p_swa.reshape(bsz, seq, -1), bias_a, attn_sinks[l], dil=1, q_blk=0,
                              k_blk=(A_HQ * HEAD_DIM) // kvw, v_blk=(A_HQ * HEAD_DIM) // kvw + 1,
                              q_w=A_HQ * HEAD_DIM, kv_w=kvw, nq=A_HQ, nkv=A_HKV, want_lse=False,
                              out_dtype=BF16, name="attn_swa", nqb=4)
        oc, lc = [], []
        for gi, (win, dil) in enumerate(C_GROUPS):
            pv = p_dil[gi].reshape(bsz, seq // dil, -1)
            o, ls = _band_attention(pv, bias_c[gi], None, dil=dil, q_blk=0, k_blk=1, v_blk=2, q_w=C_OUT,
                                    kv_w=C_OUT, nq=C_HG, nkv=C_HG, want_lse=True, out_dtype=BF16,
                                    name=f"attn_dil{dil}", nqb=min(4, seq // dil // BLK))
            oc.append(o.reshape(m // dil, dil * C_OUT))
            lc.append(ls.reshape(m // dil, dil * C_OUT))

        r, lw, k2, v, kk, bb, g, bonus = _rwkv_prep(
            p_rwkv, seq, mu_pad[l], rwkv_w0[l], wup_pad[l], rwkv_a0[l], aup_pad[l], rwkv_g_up[l],
            rwkv_k_k[l], rwkv_k_a[l], r_k[l], e_bd, tm_tok)
        sh = lambda t: t.reshape(bsz, seq, B_WIDTH)
        y_raw = _wkv_scan(sh(r), sh(lw), sh(k2), sh(v), sh(kk), sh(bb), ts_wkv).reshape(m, B_WIDTH)

        merged = _merge(y_a.reshape(m, A_HQ * HEAD_DIM), oc[0], oc[1], oc[2], lc[0], lc[1], lc[2], y_raw, g,
                        bonus, gates, rwkv_lnx_g[l], rwkv_lnx_b[l], e_bd, pa[l], pb[l], pc[l], tm_tok)
        x2 = _out_proj(x2, merged, wo[l], tm_proj)
        x2 = _ffn(x2, seq, norm2_g[l], wg[l], wv[l], cw[l], wd[l], final_g, l == depth - 1, tm_ffn, 512)
    return x2.reshape(bsz, seq, d)
```

```python
import functools
import math

import jax
import jax.numpy as jnp
from jax import lax
from jax.experimental import pallas as pl
from jax.experimental.pallas import tpu as pltpu

F32 = jnp.float32
BF16 = jnp.bfloat16

HEAD_DIM = 64
LANES = 128
BLK = 128
NORM_EPS = 1e-5
A_HQ, A_HKV = 8, 2
B_HEADS = 12
B_WIDTH = B_HEADS * HEAD_DIM
LORA_DECAY, LORA_ICLR, LORA_GATE = 96, 96, 256
LORA_PAD = 128
B_GN_EPS = 64e-5
C_GROUPS = ((128, 1), (512, 4), (2048, 16))
C_HG = 4
C_WIDTH = C_HG * len(C_GROUPS) * HEAD_DIM
C_OUT = C_HG * HEAD_DIM
N_BUCKETS, MAX_EXACT, REL_MAX_DIST = 32, 16, 2048
D_FF_PAD_TO = 512
NEG_BIG = -1e30

N_RWKV = 3 * B_WIDTH + 2 * LORA_PAD + LORA_GATE + 256
OFF_WD = 3 * B_WIDTH
OFF_AD = OFF_WD + LORA_PAD
OFF_GD = OFF_AD + LORA_PAD
WKV_CHUNK = 64
WKV_UNROLL = 2
HALO = 16

VMEM_LIMIT = 56 * 1024 * 1024


def _cparams(sem):
    return pltpu.CompilerParams(dimension_semantics=sem, vmem_limit_bytes=VMEM_LIMIT)


def _dot(a, b):
    return jnp.dot(a, b, preferred_element_type=F32)


def _dot_nt(a, b):
    return lax.dot_general(a, b, (((1,), (1,)), ((), ())), preferred_element_type=F32)


def _split2(x):
    hi = x.astype(BF16)
    lo = (x - hi.astype(F32)).astype(BF16)
    return hi, lo


def _head_sums(a, e_bf16, exact=False):
    if not exact:
        return _dot(a.astype(BF16), e_bf16)
    h, l = _split2(a)
    return _dot(h, e_bf16) + _dot(l, e_bf16)


def _dot_exact_lhs(a_bf16, b):
    h, l = _split2(b)
    return _dot(a_bf16, h) + _dot(a_bf16, l)


def _dot3(a, b):
    ah, al = _split2(a)
    bh, bl = _split2(b)
    return _dot(ah, bh) + _dot(ah, bl) + _dot(al, bh)


def _rms(x, g):
    ms = jnp.mean(x * x, axis=-1, keepdims=True)
    return x * lax.rsqrt(ms + NORM_EPS) * g


def _proj_kernel(x_ref, g_ref, w_ref, pa_ref, p1_ref, p4_ref, p16_ref, pr_ref, pg_ref, h_ref, acc_ref, *,
                 n_rwkv):
    j = pl.program_id(1)
    tm = x_ref.shape[0]
    w = w_ref.shape[1]
    n_attn = 1 + len(C_GROUPS)

    @pl.when(j == 0)
    def _():
        h_ref[...] = _rms(x_ref[...], g_ref[...]).astype(BF16)

    def acc():
        return _dot(h_ref[...], w_ref[...])

    @pl.when(j == 0)
    def _():
        pa_ref[...] = acc().astype(BF16)

    @pl.when(j == 1)
    def _():
        p1_ref[...] = acc().astype(BF16)

    for jj, dil, ref in ((2, C_GROUPS[1][1], p4_ref), (3, C_GROUPS[2][1], p16_ref)):
        @pl.when(j == jj)
        def _(dil=dil, ref=ref):
            a = acc()
            for c in range(w // LANES):
                acc_ref[c] = a[:, c * LANES:(c + 1) * LANES]
            for r in range(dil):
                for c in range(w // LANES):
                    ref[:, r * w + c * LANES:r * w + (c + 1) * LANES] = (
                        acc_ref[c, pl.ds(r, tm // dil, stride=dil), :].astype(BF16))

    @pl.when(jnp.logical_and(j >= n_attn, j < n_attn + n_rwkv))
    def _():
        pr_ref[...] = acc().astype(BF16)

    @pl.when(j >= n_attn + n_rwkv)
    def _():
        pg_ref[...] = jax.nn.sigmoid(acc()).astype(BF16)


def _proj(x2, g, w, layer, n_rwkv, n_gate, tm):
    m, d = x2.shape
    n_attn = 1 + len(C_GROUPS)
    nblk = n_attn + n_rwkv + n_gate
    wq = w.shape[2] // nblk
    d4, d16 = C_GROUPS[1][1], C_GROUPS[2][1]
    return pl.pallas_call(
        functools.partial(_proj_kernel, n_rwkv=n_rwkv),
        grid=(m // tm, nblk),
        in_specs=[
            pl.BlockSpec((tm, d), lambda i, j: (i, 0)),
            pl.BlockSpec((1, d), lambda i, j: (0, 0)),
            pl.BlockSpec((None, d, wq), lambda i, j: (layer, 0, j)),
        ],
        out_specs=[
            pl.BlockSpec((tm, wq), lambda i, j: (i, 0)),
            pl.BlockSpec((tm, wq), lambda i, j: (i, 0)),
            pl.BlockSpec((tm // d4, d4 * wq), lambda i, j: (i, 0)),
            pl.BlockSpec((tm // d16, d16 * wq), lambda i, j: (i, 0)),
            pl.BlockSpec((tm, wq), lambda i, j: (i, jnp.clip(j - n_attn, 0, n_rwkv - 1))),
            pl.BlockSpec((tm, wq), lambda i, j: (i, jnp.clip(j - n_attn - n_rwkv, 0, n_gate - 1))),
        ],
        out_shape=[
            jax.ShapeDtypeStruct((m, wq), BF16),
            jax.ShapeDtypeStruct((m, wq), BF16),
            jax.ShapeDtypeStruct((m // d4, d4 * wq), BF16),
            jax.ShapeDtypeStruct((m // d16, d16 * wq), BF16),
            jax.ShapeDtypeStruct((m, n_rwkv * wq), BF16),
            jax.ShapeDtypeStruct((m, n_gate * wq), BF16),
        ],
        scratch_shapes=[pltpu.VMEM((tm, d), BF16), pltpu.VMEM((wq // LANES, tm, LANES), F32)],
        compiler_params=_cparams(("parallel", "arbitrary")),
        name="proj",
    )(x2, g.reshape(1, d), w)


def _band_attn_kernel(*refs, nq, nkv, has_sink, want_lse):
    it = iter(refs)
    q_ref, kp_ref, kc_ref, vp_ref, vc_ref, bias_ref = (next(it) for _ in range(6))
    sink_ref = next(it) if has_sink else None
    o_ref = next(it)
    lse_ref = next(it) if want_lse else None

    nqb = q_ref.shape[0] // BLK
    first = pl.program_id(2) == 0
    q = q_ref[...] * jnp.asarray(HEAD_DIM ** -0.5, BF16)
    k = jnp.concatenate([kp_ref[...], kc_ref[...]], axis=0)
    v = jnp.concatenate([vp_ref[...], vc_ref[...]], axis=0)
    col = lax.broadcasted_iota(jnp.int32, (BLK, 2 * BLK), 1)
    edge = jnp.where(col < BLK, jnp.where(first, NEG_BIG, 0.0), 0.0)
    rep = nq // nkv
    units = [(i, h) for i in range(nqb) for h in range(nq)]
    hd = lambda t, i: t[:, i * HEAD_DIM:(i + 1) * HEAD_DIM]
    qs = lambda i, h: hd(q[i * BLK:(i + 1) * BLK], h)
    win = lambda t, i, h: hd(t[i * BLK:(i + 2) * BLK], h // rep)
    s = [_dot_nt(qs(i, h), win(k, i, h)) + bias_ref[h] for i, h in units]
    s = [s[u] + edge if i == 0 else s[u] for u, (i, h) in enumerate(units)]
    m = [jnp.max(t, axis=-1, keepdims=True) for t in s]
    if has_sink:
        m = [jnp.maximum(m[u], sink_ref[h]) for u, (i, h) in enumerate(units)]
    p = [jnp.exp(s[u] - m[u]).astype(BF16) for u in range(len(units))]
    ones = jnp.ones((2 * BLK, LANES), BF16)
    l = [_dot(t, ones)[:, :HEAD_DIM] for t in p]
    denom = [l[u] + jnp.exp(sink_ref[h] - m[u]) for u, (i, h) in enumerate(units)] if has_sink else l
    o = [_dot(p[u], win(v, i, h)) / denom[u] for u, (i, h) in enumerate(units)]
    rows = lambda parts: jnp.concatenate(
        [jnp.concatenate(parts[i * nq:(i + 1) * nq], axis=-1) for i in range(nqb)], axis=0)
    o_ref[...] = rows(o).astype(o_ref.dtype)
    if want_lse:
        lse_ref[...] = rows([m[u] + jnp.log(l[u]) for u in range(len(units))])


def _band_attention(pv, bias, sink, *, dil, q_blk, k_blk, v_blk, q_w, kv_w, nq, nkv, want_lse, out_dtype, name,
                    nqb):
    b, lf, nd = pv.shape
    n = nd // dil
    nb = lf // (BLK * nqb)
    qpr, kpr = n // q_w, n // kv_w
    has_sink = sink is not None
    prev = lambda j: jnp.maximum(j * nqb - 1, 0)

    in_specs = [
        pl.BlockSpec((None, nqb * BLK, q_w), lambda bi, r, j: (bi, j, r * qpr + q_blk)),
        pl.BlockSpec((None, BLK, kv_w), lambda bi, r, j: (bi, prev(j), r * kpr + k_blk)),
        pl.BlockSpec((None, nqb * BLK, kv_w), lambda bi, r, j: (bi, j, r * kpr + k_blk)),
        pl.BlockSpec((None, BLK, kv_w), lambda bi, r, j: (bi, prev(j), r * kpr + v_blk)),
        pl.BlockSpec((None, nqb * BLK, kv_w), lambda bi, r, j: (bi, j, r * kpr + v_blk)),
        pl.BlockSpec((nq, BLK, 2 * BLK), lambda bi, r, j: (0, 0, 0)),
    ]
    args = [pv, pv, pv, pv, pv, bias]
    if has_sink:
        in_specs.append(pl.BlockSpec(memory_space=pltpu.SMEM))
        args.append(sink)
    ow = nq * HEAD_DIM
    out_spec = pl.BlockSpec((None, nqb * BLK, ow), lambda bi, r, j: (bi, j, r))
    out_shape = jax.ShapeDtypeStruct((b, lf, dil * ow), out_dtype)
    if want_lse:
        out_specs = [out_spec, out_spec]
        out_shapes = [out_shape, jax.ShapeDtypeStruct((b, lf, dil * ow), F32)]
    else:
        out_specs, out_shapes = out_spec, out_shape
    return pl.pallas_call(
        functools.partial(_band_attn_kernel, nq=nq, nkv=nkv, has_sink=has_sink, want_lse=want_lse),
        grid=(b, dil, nb),
        in_specs=in_specs,
        out_specs=out_specs,
        out_shape=out_shapes,
        compiler_params=_cparams(("parallel", "parallel", "arbitrary")),
        name=name,
    )(*args)


def _rwkv_prep_kernel(ph_ref, p_ref, mu_ref, w0_ref, wup_ref, a0_ref, aup_ref, gup_ref, kk_ref, ka_ref,
                      rk_ref, e_ref, r_o, lw_o, k_o, v_o, kk_o, b_o, g_o, bonus_o, *, tiles_per_seq):
    tm = p_ref.shape[0]
    p = p_ref[...].astype(F32)
    seq_start = pl.program_id(0) % tiles_per_seq == 0
    last = ph_ref[...].astype(F32)[HALO - 1:HALO, :]
    last = jnp.where(seq_start, 0.0, last)
    row = lax.broadcasted_iota(jnp.int32, (tm, 1), 0)
    prev = jnp.where(row == 0, last, pltpu.roll(p, 1, 0))
    pf = p + (prev - p) * mu_ref[...]

    r = pf[:, 0:B_WIDTH]
    k = pf[:, B_WIDTH:2 * B_WIDTH]
    v = pf[:, 2 * B_WIDTH:3 * B_WIDTH]
    wd = pf[:, OFF_WD:OFF_WD + LORA_PAD]
    ad = pf[:, OFF_AD:OFF_AD + LORA_PAD]
    gd = pf[:, OFF_GD:OFF_GD + LORA_GATE]

    z = w0_ref[...] + _dot3(jnp.tanh(wd), wup_ref[...])
    nz = -z
    softplus = jnp.maximum(nz, 0.0) + jnp.log(1.0 + jnp.exp(-jnp.abs(nz)))
    w = -softplus - 0.5
    lw_o[...] = -jnp.exp(w)
    a = jax.nn.sigmoid(a0_ref[...] + _dot3(ad, aup_ref[...]))
    g_o[...] = _dot3(jax.nn.sigmoid(gd), gup_ref[...])

    e = e_ref[...]
    kk = k * kk_ref[...]
    nrm = jnp.sqrt(_head_sums(kk * kk, e))
    kk = kk / jnp.maximum(nrm, 1e-12)
    k2 = k * (1.0 + (a - 1.0) * ka_ref[...])
    r_o[...] = r
    k_o[...] = k2
    v_o[...] = v
    kk_o[...] = kk
    b_o[...] = kk * a
    bonus_o[...] = _head_sums(r * k2 * rk_ref[...], e) * v


def _rwkv_prep(pb2, seq, mu, w0, wup, a0, aup, gup, k_k, k_a, r_k, e_bd, tm):
    m, n = pb2.shape
    tps = seq // tm
    row = lambda a: a.reshape(1, -1)
    full = lambda a: pl.BlockSpec(a.shape, lambda i: (0,) * a.ndim)
    args = [pb2, pb2, row(mu), row(w0), wup, row(a0), aup, gup, row(k_k), row(k_a), row(r_k), e_bd]
    in_specs = [
        pl.BlockSpec((HALO, n), lambda i: (jnp.maximum(i * (tm // HALO) - 1, 0), 0)),
        pl.BlockSpec((tm, n), lambda i: (i, 0)),
    ] + [full(a) for a in args[2:]]
    o_spec = pl.BlockSpec((tm, B_WIDTH), lambda i: (i, 0))
    o_shape = jax.ShapeDtypeStruct((m, B_WIDTH), F32)
    return pl.pallas_call(
        functools.partial(_rwkv_prep_kernel, tiles_per_seq=tps),
        grid=(m // tm,),
        in_specs=in_specs,
        out_specs=[o_spec] * 8,
        out_shape=[o_shape] * 8,
        compiler_params=_cparams(("parallel",)),
        name="rwkv_prep",
    )(*args)


def _wkv_kernel(r_ref, lw_ref, k_ref, v_ref, kk_ref, b_ref, y_ref, st_ref):
    c = WKV_CHUNK
    nb = r_ref.shape[0]
    n_chunks = r_ref.shape[1] // c

    @pl.when(pl.program_id(0) == 0)
    def _():
        st_ref[...] = jnp.zeros_like(st_ref)

    rowi = lax.broadcasted_iota(jnp.int32, (c, c), 0)
    coli = lax.broadcasted_iota(jnp.int32, (c, c), 1)
    incl = rowi >= coli
    strict = rowi > coli
    row2 = lax.broadcasted_iota(jnp.int32, (2 * c, 2 * c), 0)
    col2 = lax.broadcasted_iota(jnp.int32, (2 * c, 2 * c), 1)
    tri2 = row2 - jnp.where(row2 < c, 1, c) >= jnp.where(col2 >= c, col2 - c, 2 * c)
    tri = jnp.where(incl, 1.0, 0.0).astype(BF16)
    eye = jnp.where(rowi == coli, 1.0, 0.0)

    nu = WKV_UNROLL
    heads = range(B_HEADS)
    hsl = [slice(h * HEAD_DIM, (h + 1) * HEAD_DIM) for h in heads]
    segs = [(b, u) for b in range(nb) for u in range(nu)]
    ns = range(len(segs))
    units = [(s, h) for s in ns for h in heads]

    def chunks(ci, carry):
        sls = [pl.ds(pl.multiple_of((ci * nu + u) * c, c), c) for u in range(nu)]
        ld = lambda ref, s: ref[segs[s][0], sls[segs[s][1]], :]
        lw = [ld(lw_ref, s) for s in ns]
        cum = [_dot_exact_lhs(tri, t) for t in lw]
        tot = [t[c - 1:c, :] for t in cum]
        p_inv = [jnp.exp(-t) for t in cum]
        p_rest = [jnp.exp(tot[s] - cum[s]) for s in ns]
        p_tot = [jnp.exp(t) for t in tot]
        rh_all = [(ld(r_ref, s) * jnp.exp(cum[s])).astype(BF16) for s in ns]
        ah_all = [(-ld(kk_ref, s) * jnp.exp(cum[s] - lw[s])).astype(BF16) for s in ns]
        b_all = [ld(b_ref, s) for s in ns]
        k_all = [ld(k_ref, s) for s in ns]
        bh_all = [(b_all[s] * p_inv[s]).astype(BF16) for s in ns]
        kh_all = [(k_all[s] * p_inv[s]).astype(BF16) for s in ns]
        bt_all = [(b_all[s] * p_rest[s]).astype(BF16) for s in ns]
        kt_all = [(k_all[s] * p_rest[s]).astype(BF16) for s in ns]
        v_all = [ld(v_ref, s) for s in ns]

        ah = [ah_all[u][:, hsl[h]] for u, h in units]
        rh = [rh_all[u][:, hsl[h]] for u, h in units]
        vf = [v_all[u][:, hsl[h]] for u, h in units]
        vb = [t.astype(BF16) for t in vf]
        n = range(len(units))
        ar = [jnp.concatenate([ah[i], rh[i]], axis=0) for i in n]
        g = [_dot_nt(ar[i], jnp.concatenate([bh_all[u][:, hsl[h]], kh_all[u][:, hsl[h]]], axis=0))
             for i, (u, h) in enumerate(units)]
        gb = [t[:, :c] for t in g]
        a_rb = [jnp.where(incl, t[c:], 0.0).astype(BF16) for t in gb]
        akrk = [jnp.where(tri2, t, 0.0).astype(BF16) for t in g]
        x0 = [jnp.where(strict, t[:c], 0.0) for t in gb]
        t = [eye + xi for xi in x0]
        xb = [xi.astype(BF16) for xi in x0]
        x = [_dot(xi, xi) for xi in xb]
        for _ in range(int(math.log2(c)) - 2):
            xt = [_dot(jnp.concatenate([x[i], t[i]], axis=0).astype(BF16), x[i].astype(BF16)) for i in n]
            x = [p[:c] for p in xt]
            t = [t[i] + xt[i][c:] for i in n]
        t = [t[i] + _dot(t[i].astype(BF16), x[i].astype(BF16)) for i in n]
        tb = [ti.astype(BF16) for ti in t]
        zv = jnp.zeros((c, HEAD_DIM), BF16)
        w1y2 = [_dot(akrk[i], jnp.concatenate([zv, vb[i]], axis=0)) for i in n]
        w1 = [p[:c] for p in w1y2]
        y2 = [p[c:] for p in w1y2]
        n2 = [_dot(vf[i].T.astype(BF16), kt_all[u][:, hsl[h]]) for i, (u, h) in enumerate(units)]
        a2 = [_dot(tb[i], ah[i]).astype(BF16) for i in n]
        u1t = [_dot(tb[i], w1[i].astype(BF16)).T for i in n]
        chains = [(b, h) for b in range(nb) for h in heads]
        nc = range(len(chains))
        st = [st_ref[i] for i in nc]
        for u in range(nu):
            seg = [b * nu + u for b, h in chains]
            un = [seg[i] * B_HEADS + chains[i][1] for i in nc]
            hs = [hsl[h] for b, h in chains]
            s0b = [t.astype(BF16) for t in st]
            utb = [(_dot_nt(s0b[i], a2[un[i]]) + u1t[un[i]]).astype(BF16) for i in nc]
            ys = [_dot_nt(rh[un[i]], s0b[i]) + _dot_nt(a_rb[un[i]], utb[i]) + y2[un[i]] for i in nc]
            st = [st[i] * p_tot[seg[i]][:, hs[i]] + _dot(utb[i], bt_all[seg[i]][:, hs[i]]) + n2[un[i]] for i in nc]
            for b in range(nb):
                y_ref[b, sls[u], :] = jnp.concatenate(ys[b * B_HEADS:(b + 1) * B_HEADS], axis=-1)
        st_ref[...] = jnp.stack(st, axis=0)
        return carry

    lax.fori_loop(0, n_chunks // nu, chunks, 0)


def _wkv_scan(r, lw, k, v, kk, b, ts):
    bsz, s, w = r.shape
    spec = pl.BlockSpec((bsz, ts, w), lambda j: (0, j, 0))
    return pl.pallas_call(
        _wkv_kernel,
        grid=(s // ts,),
        in_specs=[spec] * 6,
        out_specs=spec,
        out_shape=jax.ShapeDtypeStruct((bsz, s, w), F32),
        scratch_shapes=[pltpu.VMEM((bsz * B_HEADS, HEAD_DIM, HEAD_DIM), F32)],
        compiler_params=_cparams(("arbitrary",)),
        name="wkv_scan",
    )(r, lw, k, v, kk, b)


def _merge_kernel(ya_ref, o1_ref, o2_ref, o3_ref, l1_ref, l2_ref, l3_ref, yr_ref, g_ref, bonus_ref,
                  gates_ref, lng_ref, lnb_ref, e_ref, pa_ref, pb_ref, pc_ref, out_ref, unf_ref):
    tm, d = out_ref.shape

    def unfold(ref, slot, dil):
        nc = C_OUT // LANES
        for r in range(dil):
            for c in range(nc):
                unf_ref[slot * nc + c, pl.ds(r, tm // dil, stride=dil), :] = (
                    ref[:, r * C_OUT + c * LANES:r * C_OUT + (c + 1) * LANES].astype(F32))
        return jnp.concatenate([unf_ref[slot * nc + c] for c in range(nc)], axis=-1)

    d2, d3 = C_GROUPS[1][1], C_GROUPS[2][1]
    o1, l1 = o1_ref[...].astype(F32), l1_ref[...]
    o2, l2 = unfold(o2_ref, 0, d2), unfold(l2_ref, 1, d2)
    o3, l3 = unfold(o3_ref, 2, d3), unfold(l3_ref, 3, d3)
    m = jnp.maximum(jnp.maximum(l1, l2), l3)
    e1, e2, e3 = jnp.exp(l1 - m), jnp.exp(l2 - m), jnp.exp(l3 - m)
    yc = (e1 * o1 + e2 * o2 + e3 * o3) / (e1 + e2 + e3)

    e = e_ref[...]
    y = yr_ref[...]
    mean = _head_sums(y, e, exact=True) * (1.0 / HEAD_DIM)
    dv = y - mean
    var = _head_sums(dv * dv, e) * (1.0 / HEAD_DIM)
    yb = dv * lax.rsqrt(var + B_GN_EPS) * lng_ref[...] + lnb_ref[...] + bonus_ref[...]
    yb = yb * g_ref[...]

    gates = gates_ref[...]
    merged = (gates[:, 0:d].astype(F32) * _dot(ya_ref[...], pa_ref[...])
              + gates[:, d:2 * d].astype(F32) * _dot(yb.astype(BF16), pb_ref[...])
              + gates[:, 2 * d:3 * d].astype(F32) * _dot(yc.astype(BF16), pc_ref[...]))
    out_ref[...] = merged.astype(BF16)


def _merge(ya, o1, o2, o3, l1, l2, l3, yr, g, bonus, gates, lng, lnb, e_bd, pa, pb, pc, layer, tm):
    m, d = ya.shape[0], pa.shape[2]
    tok = lambda a: pl.BlockSpec((tm * a.shape[0] // m, a.shape[1]), lambda i: (i, 0))
    full = lambda a: pl.BlockSpec(a.shape, lambda i: (0,) * a.ndim, pipeline_mode=pl.Buffered(1))
    at_layer = lambda a: pl.BlockSpec((None,) + a.shape[1:], lambda i: (layer, 0, 0), pipeline_mode=pl.Buffered(1))
    toks = [ya, o1, o2, o3, l1, l2, l3, yr, g, bonus, gates]
    consts = [lng.reshape(1, -1), lnb.reshape(1, -1), e_bd]
    stacked = [pa, pb, pc]
    return pl.pallas_call(
        _merge_kernel,
        grid=(m // tm,),
        in_specs=[tok(a) for a in toks] + [full(a) for a in consts] + [at_layer(a) for a in stacked],
        out_specs=pl.BlockSpec((tm, d), lambda i: (i, 0)),
        out_shape=jax.ShapeDtypeStruct((m, d), BF16),
        scratch_shapes=[pltpu.VMEM((4 * C_OUT // LANES, tm, LANES), F32)],
        compiler_params=_cparams(("parallel",)),
        name="merge",
    )(*toks, *consts, *stacked)


def _out_proj_kernel(x_ref, m_ref, w_ref, o_ref):
    o_ref[...] = x_ref[...] + _dot(m_ref[...], w_ref[...])


def _out_proj(x2, merged, wo, layer, tm):
    m, d = x2.shape
    tok = pl.BlockSpec((tm, d), lambda i: (i, 0))
    return pl.pallas_call(
        _out_proj_kernel,
        grid=(m // tm,),
        in_specs=[tok, tok,
                  pl.BlockSpec((None,) + wo.shape[1:], lambda i: (layer, 0, 0), pipeline_mode=pl.Buffered(1))],
        out_specs=tok,
        out_shape=jax.ShapeDtypeStruct((m, d), F32),
        compiler_params=_cparams(("parallel",)),
        name="out_proj",
    )(x2, merged, wo)


def _ffn_kernel(xh_ref, x_ref, g_ref, wg_ref, wv_ref, cw_ref, wd_ref, fg_ref, o_ref, h_ref, *,
                tiles_per_seq, final_norm):
    j = pl.program_id(1)
    tm = x_ref.shape[0]

    @pl.when(j == 0)
    def _():
        x = x_ref[...]
        g = g_ref[...]
        h_ref[HALO:, :] = _rms(x, g).astype(BF16)
        seq_start = pl.program_id(0) % tiles_per_seq == 0
        halo = _rms(xh_ref[...], g)
        h_ref[0:HALO, :] = jnp.where(seq_start, 0.0, halo).astype(BF16)
        o_ref[...] = x

    h = h_ref[...]
    gext = _dot(h, wg_ref[...])
    val = _dot(h[HALO:], wv_ref[...])
    cw = cw_ref[...]
    gate = (cw[0:1, :] * pltpu.roll(gext, 2, 0)[HALO:]
            + cw[1:2, :] * pltpu.roll(gext, 1, 0)[HALO:]
            + cw[2:3, :] * gext[HALO:])
    hid = gate * jax.nn.sigmoid(gate) * val
    o_ref[...] += _dot(hid.astype(BF16), wd_ref[...])

    if final_norm:
        @pl.when(j == pl.num_programs(1) - 1)
        def _():
            o_ref[...] = _rms(o_ref[...], fg_ref[...])


def _ffn(x2, seq, g, wgv, cw, wd, layer, fg, final_norm, tm, tf):
    m, d = x2.shape
    ffp = wgv.shape[2] // 2
    tps = seq // tm
    nf = ffp // tf
    return pl.pallas_call(
        functools.partial(_ffn_kernel, tiles_per_seq=tps, final_norm=final_norm),
        grid=(m // tm, nf),
        in_specs=[
            pl.BlockSpec((HALO, d), lambda i, j: (jnp.maximum(i * (tm // HALO) - 1, 0), 0)),
            pl.BlockSpec((tm, d), lambda i, j: (i, 0)),
            pl.BlockSpec((1, d), lambda i, j: (0, 0)),
            pl.BlockSpec((None, d, tf), lambda i, j: (layer, 0, j)),
            pl.BlockSpec((None, d, tf), lambda i, j: (layer, 0, nf + j)),
            pl.BlockSpec((None, 3, tf), lambda i, j: (layer, 0, j)),
            pl.BlockSpec((None, tf, d), lambda i, j: (layer, j, 0)),
            pl.BlockSpec((1, d), lambda i, j: (0, 0)),
        ],
        out_specs=pl.BlockSpec((tm, d), lambda i, j: (i, 0)),
        out_shape=jax.ShapeDtypeStruct((m, d), F32),
        scratch_shapes=[pltpu.VMEM((HALO + tm, d), BF16)],
        compiler_params=_cparams(("parallel", "arbitrary")),
        name="conv_ffn",
    )(x2, x2, g.reshape(1, d), wgv, wgv, cw, wd, fg.reshape(1, d))


def _t5_bucket(dist):
    small = dist < MAX_EXACT
    nf = jnp.maximum(dist, 1).astype(F32)
    large = MAX_EXACT + (jnp.log(nf / MAX_EXACT) / math.log(REL_MAX_DIST / MAX_EXACT)
                         * (N_BUCKETS - MAX_EXACT)).astype(jnp.int32)
    return jnp.where(small, dist, jnp.minimum(large, N_BUCKETS - 1))


def _band_bias(table, dilation, max_steps):
    i = jnp.arange(BLK)[:, None]
    j = jnp.arange(2 * BLK)[None, :]
    off = i + BLK - j
    onehot = jax.nn.one_hot(_t5_bucket(jnp.maximum(off, 0) * dilation), N_BUCKETS, dtype=F32)
    bias = jnp.einsum("ijb,bh->hij", onehot, table.astype(F32), precision=lax.Precision.HIGHEST)
    valid = (off >= 0) & (off <= max_steps)
    return jnp.where(valid[None], bias, NEG_BIG)


def _pad_cols(w, n):
    return jnp.pad(w, [(0, 0)] * (w.ndim - 1) + [(0, n - w.shape[-1])])


def _pad_rows(w, n):
    return jnp.pad(w, [(0, 0)] * (w.ndim - 2) + [(0, n - w.shape[-2]), (0, 0)])


def kernel(x, rel_bias, norm1_g, w_in, attn_sinks, rwkv_mu, rwkv_w0, rwkv_w_up, rwkv_a0, rwkv_a_up, rwkv_g_up,
           rwkv_k_k, rwkv_k_a, rwkv_r_k, rwkv_lnx_g, rwkv_lnx_b, proj_a, proj_b, proj_c, w_out, norm2_g,
           ffn_up, ffn_conv, ffn_down, final_g):
    bsz, seq, d = x.shape
    depth = w_in.shape[0]
    m = bsz * seq
    d_ff = ffn_conv.shape[-1]
    ffp = -(-d_ff // D_FF_PAD_TO) * D_FF_PAD_TO

    tm_proj = min(1024, seq)
    tm_tok = min(512, seq)
    tm_ffn = min(1024, seq)
    ts_wkv = min(512, seq)

    na = A_HQ
    bias_a = _band_bias(rel_bias[:, :na], 1, BLK - 1)
    bias_c = [_band_bias(rel_bias[:, na + gi * C_HG:na + (gi + 1) * C_HG], dil, win // dil)
              for gi, (win, dil) in enumerate(C_GROUPS)]

    q_end = (A_HQ + 2 * A_HKV) * HEAD_DIM
    b0 = q_end
    b_r_end = b0 + 3 * B_WIDTH
    b_wd_end = b_r_end + LORA_DECAY
    b_ad_end = b_wd_end + LORA_ICLR
    b_end = b_ad_end + LORA_GATE
    c_end = b_end + 3 * C_WIDTH

    def rwkv_cols(t):
        parts = [t[..., b0:b_r_end], _pad_cols(t[..., b_r_end:b_wd_end], LORA_PAD),
                 _pad_cols(t[..., b_wd_end:b_ad_end], LORA_PAD), t[..., b_ad_end:b_end]]
        return _pad_cols(jnp.concatenate(parts, axis=-1), N_RWKV)

    w_in_b = lax.optimization_barrier(w_in.astype(BF16))
    tail = lax.optimization_barrier(w_in_b[..., b_ad_end:])
    t_c, t_g = b_end - b_ad_end, c_end - b_ad_end
    qkv_c = [tail[..., t_c + t * C_WIDTH + gi * C_OUT:t_c + t * C_WIDTH + (gi + 1) * C_OUT]
             for gi in range(len(C_GROUPS)) for t in range(3)]
    rwkv_w = _pad_cols(jnp.concatenate(
        [w_in_b[..., b0:b_r_end], _pad_cols(w_in_b[..., b_r_end:b_wd_end], LORA_PAD),
         _pad_cols(w_in_b[..., b_wd_end:b_ad_end], LORA_PAD), tail[..., :t_c]], axis=-1), N_RWKV)
    w_all = jnp.concatenate([w_in_b[..., :q_end]] + qkv_c + [rwkv_w, tail[..., t_g:]], axis=-1)
    proj_blk = q_end
    n_rwkv_blk = N_RWKV // proj_blk
    n_gate_blk = (w_in.shape[-1] - c_end) // proj_blk
    mu_pad = rwkv_cols(jnp.pad(rwkv_mu, ((0, 0), (b0, 0))))
    wup_pad = _pad_rows(rwkv_w_up, LORA_PAD)
    aup_pad = _pad_rows(rwkv_a_up, LORA_PAD)
    r_k = rwkv_r_k.reshape(depth, B_WIDTH)

    head_id = jnp.arange(B_WIDTH) // HEAD_DIM
    e_bd = (head_id[:, None] == head_id[None, :]).astype(BF16)

    pa, pb, pc, wo = (t.astype(BF16) for t in (proj_a, proj_b, proj_c, w_out))
    zc = jnp.zeros(ffn_up.shape[:2] + (ffp - d_ff,), ffn_up.dtype)
    wgv = jnp.concatenate([ffn_up[..., :d_ff], zc, ffn_up[..., d_ff:], zc], axis=-1).astype(BF16)
    cw = _pad_cols(ffn_conv, ffp)
    wd = jnp.concatenate([ffn_down, jnp.zeros((depth, ffp - d_ff, d), ffn_down.dtype)], axis=1).astype(BF16)

    x2 = x.reshape(m, d)
    for l in range(depth):
        p_swa, *p_dil, p_rwkv, gates = _proj(x2, norm1_g[l], w_all, l, n_rwkv_blk, n_gate_blk, tm_proj)

        kvw = A_HKV * HEAD_DIM
        y_a = _band_attention(p_swa.reshape(bsz, seq, -1), bias_a, attn_sinks[l], dil=1, q_blk=0,
                              k_blk=(A_HQ * HEAD_DIM) // kvw, v_blk=(A_HQ * HEAD_DIM) // kvw + 1,
                              q_w=A_HQ * HEAD_DIM, kv_w=kvw, nq=A_HQ, nkv=A_HKV, want_lse=False,
                              out_dtype=BF16, name="attn_swa", nqb=4)
        oc, lc = [], []
        for gi, (win, dil) in enumerate(C_GROUPS):
            pv = p_dil[gi].reshape(bsz, seq // dil, -1)
            o, ls = _band_attention(pv, bias_c[gi], None, dil=dil, q_blk=0, k_blk=1, v_blk=2, q_w=C_OUT,
                                    kv_w=C_OUT, nq=C_HG, nkv=C_HG, want_lse=True, out_dtype=BF16,
                                    name=f"attn_dil{dil}", nqb=min(4, seq // dil // BLK))
            oc.append(o.reshape(m // dil, dil * C_OUT))
            lc.append(ls.reshape(m // dil, dil * C_OUT))

        r, lw, k2, v, kk, bb, g, bonus = _rwkv_prep(
            p_rwkv, seq, mu_pad[l], rwkv_w0[l], wup_pad[l], rwkv_a0[l], aup_pad[l], rwkv_g_up[l],
            rwkv_k_k[l], rwkv_k_a[l], r_k[l], e_bd, tm_tok)
        sh = lambda t: t.reshape(bsz, seq, B_WIDTH)
        y_raw = _wkv_scan(sh(r), sh(lw), sh(k2), sh(v), sh(kk), sh(bb), ts_wkv).reshape(m, B_WIDTH)

        merged = _merge(y_a.reshape(m, A_HQ * HEAD_DIM), oc[0], oc[1], oc[2], lc[0], lc[1], lc[2], y_raw, g,
                        bonus, gates, rwkv_lnx_g[l], rwkv_lnx_b[l], e_bd, pa, pb, pc, l, tm_tok)
        x2 = _out_proj(x2, merged, wo, l, tm_proj)
        x2 = _ffn(x2, seq, norm2_g[l], wgv, cw, wd, l, final_g, l == depth - 1, tm_ffn, 512)
    return x2.reshape(bsz, seq, d)
```

```python
import functools
import math

import jax
import jax.numpy as jnp
from jax import lax
from jax.experimental import pallas as pl
from jax.experimental.pallas import tpu as pltpu

F32 = jnp.float32
BF16 = jnp.bfloat16

HEAD_DIM = 64
LANES = 128
BLK = 128
NORM_EPS = 1e-5
A_HQ, A_HKV = 8, 2
B_HEADS = 12
B_WIDTH = B_HEADS * HEAD_DIM
LORA_DECAY, LORA_ICLR, LORA_GATE = 96, 96, 256
LORA_PAD = 128
B_GN_EPS = 64e-5
C_GROUPS = ((128, 1), (512, 4), (2048, 16))
C_HG = 4
C_WIDTH = C_HG * len(C_GROUPS) * HEAD_DIM
C_OUT = C_HG * HEAD_DIM
N_BUCKETS, MAX_EXACT, REL_MAX_DIST = 32, 16, 2048
D_FF_PAD_TO = 512
NEG_BIG = -1e30

N_RWKV = 3 * B_WIDTH + 2 * LORA_PAD + LORA_GATE + 256
OFF_WD = 3 * B_WIDTH
OFF_AD = OFF_WD + LORA_PAD
OFF_GD = OFF_AD + LORA_PAD
WKV_CHUNK = 64
WKV_UNROLL = 2
HALO = 16

VMEM_LIMIT = 56 * 1024 * 1024


def _cparams(sem):
    return pltpu.CompilerParams(dimension_semantics=sem, vmem_limit_bytes=VMEM_LIMIT)


def _dot(a, b):
    return jnp.dot(a, b, preferred_element_type=F32)


def _dot_nt(a, b):
    return lax.dot_general(a, b, (((1,), (1,)), ((), ())), preferred_element_type=F32)


def _split2(x):
    hi = x.astype(BF16)
    lo = (x - hi.astype(F32)).astype(BF16)
    return hi, lo


def _head_sums(a, e_bf16, exact=False):
    if not exact:
        return _dot(a.astype(BF16), e_bf16)
    h, l = _split2(a)
    return _dot(h, e_bf16) + _dot(l, e_bf16)


def _dot_exact_lhs(a_bf16, b):
    h, l = _split2(b)
    return _dot(a_bf16, h) + _dot(a_bf16, l)


def _dot3(a, b):
    ah, al = _split2(a)
    bh, bl = _split2(b)
    return _dot(ah, bh) + _dot(ah, bl) + _dot(al, bh)


def _rms(x, g):
    ms = jnp.mean(x * x, axis=-1, keepdims=True)
    return x * lax.rsqrt(ms + NORM_EPS) * g


def _proj_kernel(x_ref, g_ref, w_ref, pa_ref, p1_ref, p4_ref, p16_ref, pr_ref, pg_ref, h_ref, acc_ref, *,
                 n_rwkv):
    j = pl.program_id(1)
    tm = x_ref.shape[0]
    w = w_ref.shape[1]
    n_attn = 1 + len(C_GROUPS)

    @pl.when(j == 0)
    def _():
        h_ref[...] = _rms(x_ref[...], g_ref[...]).astype(BF16)

    def acc():
        return _dot(h_ref[...], w_ref[...])

    @pl.when(j == 0)
    def _():
        pa_ref[...] = acc().astype(BF16)

    @pl.when(j == 1)
    def _():
        p1_ref[...] = acc().astype(BF16)

    for jj, dil, ref in ((2, C_GROUPS[1][1], p4_ref), (3, C_GROUPS[2][1], p16_ref)):
        @pl.when(j == jj)
        def _(dil=dil, ref=ref):
            a = acc()
            for c in range(w // LANES):
                acc_ref[c] = a[:, c * LANES:(c + 1) * LANES]
            for r in range(dil):
                for c in range(w // LANES):
                    ref[:, r * w + c * LANES:r * w + (c + 1) * LANES] = (
                        acc_ref[c, pl.ds(r, tm // dil, stride=dil), :].astype(BF16))

    @pl.when(jnp.logical_and(j >= n_attn, j < n_attn + n_rwkv))
    def _():
        pr_ref[...] = acc().astype(BF16)

    @pl.when(j >= n_attn + n_rwkv)
    def _():
        pg_ref[...] = jax.nn.sigmoid(acc()).astype(BF16)


def _proj(x2, g, w, layer, n_rwkv, n_gate, tm):
    m, d = x2.shape
    n_attn = 1 + len(C_GROUPS)
    nblk = n_attn + n_rwkv + n_gate
    wq = w.shape[2] // nblk
    d4, d16 = C_GROUPS[1][1], C_GROUPS[2][1]
    return pl.pallas_call(
        functools.partial(_proj_kernel, n_rwkv=n_rwkv),
        grid=(m // tm, nblk),
        in_specs=[
            pl.BlockSpec((tm, d), lambda i, j: (i, 0)),
            pl.BlockSpec((1, d), lambda i, j: (0, 0)),
            pl.BlockSpec((None, d, wq), lambda i, j: (layer, 0, j)),
        ],
        out_specs=[
            pl.BlockSpec((tm, wq), lambda i, j: (i, 0)),
            pl.BlockSpec((tm, wq), lambda i, j: (i, 0)),
            pl.BlockSpec((tm // d4, d4 * wq), lambda i, j: (i, 0)),
            pl.BlockSpec((tm // d16, d16 * wq), lambda i, j: (i, 0)),
            pl.BlockSpec((tm, wq), lambda i, j: (i, jnp.clip(j - n_attn, 0, n_rwkv - 1))),
            pl.BlockSpec((tm, wq), lambda i, j: (i, jnp.clip(j - n_attn - n_rwkv, 0, n_gate - 1))),
        ],
        out_shape=[
            jax.ShapeDtypeStruct((m, wq), BF16),
            jax.ShapeDtypeStruct((m, wq), BF16),
            jax.ShapeDtypeStruct((m // d4, d4 * wq), BF16),
            jax.ShapeDtypeStruct((m // d16, d16 * wq), BF16),
            jax.ShapeDtypeStruct((m, n_rwkv * wq), BF16),
            jax.ShapeDtypeStruct((m, n_gate * wq), BF16),
        ],
        scratch_shapes=[pltpu.VMEM((tm, d), BF16), pltpu.VMEM((wq // LANES, tm, LANES), F32)],
        compiler_params=_cparams(("parallel", "arbitrary")),
        name="proj",
    )(x2, g.reshape(1, d), w)


def _band_attn_kernel(*refs, nq, nkv, has_sink, want_lse):
    it = iter(refs)
    q_ref, kp_ref, kc_ref, vp_ref, vc_ref, bias_ref = (next(it) for _ in range(6))
    sink_ref = next(it) if has_sink else None
    o_ref = next(it)
    lse_ref = next(it) if want_lse else None

    nqb = q_ref.shape[0] // BLK
    first = pl.program_id(2) == 0
    q = q_ref[...] * jnp.asarray(HEAD_DIM ** -0.5, BF16)
    k = jnp.concatenate([kp_ref[...], kc_ref[...]], axis=0)
    v = jnp.concatenate([vp_ref[...], vc_ref[...]], axis=0)
    col = lax.broadcasted_iota(jnp.int32, (BLK, 2 * BLK), 1)
    edge = jnp.where(col < BLK, jnp.where(first, NEG_BIG, 0.0), 0.0)
    rep = nq // nkv
    units = [(i, h) for i in range(nqb) for h in range(nq)]
    hd = lambda t, i: t[:, i * HEAD_DIM:(i + 1) * HEAD_DIM]
    qs = lambda i, h: hd(q[i * BLK:(i + 1) * BLK], h)
    win = lambda t, i, h: hd(t[i * BLK:(i + 2) * BLK], h // rep)
    s = [_dot_nt(qs(i, h), win(k, i, h)) + bias_ref[h] for i, h in units]
    s = [s[u] + edge if i == 0 else s[u] for u, (i, h) in enumerate(units)]
    m = [jnp.max(t, axis=-1, keepdims=True) for t in s]
    if has_sink:
        m = [jnp.maximum(m[u], sink_ref[h]) for u, (i, h) in enumerate(units)]
    p = [jnp.exp(s[u] - m[u]).astype(BF16) for u in range(len(units))]
    ones = jnp.ones((2 * BLK, LANES), BF16)
    l = [_dot(t, ones)[:, :HEAD_DIM] for t in p]
    denom = [l[u] + jnp.exp(sink_ref[h] - m[u]) for u, (i, h) in enumerate(units)] if has_sink else l
    o = [_dot(p[u], win(v, i, h)) / denom[u] for u, (i, h) in enumerate(units)]
    rows = lambda parts: jnp.concatenate(
        [jnp.concatenate(parts[i * nq:(i + 1) * nq], axis=-1) for i in range(nqb)], axis=0)
    o_ref[...] = rows(o).astype(o_ref.dtype)
    if want_lse:
        lse_ref[...] = rows([m[u] + jnp.log(l[u]) for u in range(len(units))])


def _band_attention(pv, bias, sink, *, dil, q_blk, k_blk, v_blk, q_w, kv_w, nq, nkv, want_lse, out_dtype, name,
                    nqb):
    b, lf, nd = pv.shape
    n = nd // dil
    nb = lf // (BLK * nqb)
    qpr, kpr = n // q_w, n // kv_w
    has_sink = sink is not None
    prev = lambda j: jnp.maximum(j * nqb - 1, 0)

    in_specs = [
        pl.BlockSpec((None, nqb * BLK, q_w), lambda bi, r, j: (bi, j, r * qpr + q_blk)),
        pl.BlockSpec((None, BLK, kv_w), lambda bi, r, j: (bi, prev(j), r * kpr + k_blk)),
        pl.BlockSpec((None, nqb * BLK, kv_w), lambda bi, r, j: (bi, j, r * kpr + k_blk)),
        pl.BlockSpec((None, BLK, kv_w), lambda bi, r, j: (bi, prev(j), r * kpr + v_blk)),
        pl.BlockSpec((None, nqb * BLK, kv_w), lambda bi, r, j: (bi, j, r * kpr + v_blk)),
        pl.BlockSpec((nq, BLK, 2 * BLK), lambda bi, r, j: (0, 0, 0)),
    ]
    args = [pv, pv, pv, pv, pv, bias]
    if has_sink:
        in_specs.append(pl.BlockSpec(memory_space=pltpu.SMEM))
        args.append(sink)
    ow = nq * HEAD_DIM
    out_spec = pl.BlockSpec((None, nqb * BLK, ow), lambda bi, r, j: (bi, j, r))
    out_shape = jax.ShapeDtypeStruct((b, lf, dil * ow), out_dtype)
    if want_lse:
        out_specs = [out_spec, out_spec]
        out_shapes = [out_shape, jax.ShapeDtypeStruct((b, lf, dil * ow), F32)]
    else:
        out_specs, out_shapes = out_spec, out_shape
    return pl.pallas_call(
        functools.partial(_band_attn_kernel, nq=nq, nkv=nkv, has_sink=has_sink, want_lse=want_lse),
        grid=(b, dil, nb),
        in_specs=in_specs,
        out_specs=out_specs,
        out_shape=out_shapes,
        compiler_params=_cparams(("parallel", "parallel", "arbitrary")),
        name=name,
    )(*args)


def _rwkv_prep_kernel(ph_ref, p_ref, mu_ref, w0_ref, wup_ref, a0_ref, aup_ref, gup_ref, kk_ref, ka_ref,
                      rk_ref, e_ref, r_o, lw_o, k_o, v_o, kk_o, b_o, g_o, bonus_o, *, tiles_per_seq):
    tm = p_ref.shape[0]
    p = p_ref[...].astype(F32)
    seq_start = pl.program_id(0) % tiles_per_seq == 0
    last = ph_ref[...].astype(F32)[HALO - 1:HALO, :]
    last = jnp.where(seq_start, 0.0, last)
    row = lax.broadcasted_iota(jnp.int32, (tm, 1), 0)
    prev = jnp.where(row == 0, last, pltpu.roll(p, 1, 0))
    pf = p + (prev - p) * mu_ref[...]

    r = pf[:, 0:B_WIDTH]
    k = pf[:, B_WIDTH:2 * B_WIDTH]
    v = pf[:, 2 * B_WIDTH:3 * B_WIDTH]
    wd = pf[:, OFF_WD:OFF_WD + LORA_PAD]
    ad = pf[:, OFF_AD:OFF_AD + LORA_PAD]
    gd = pf[:, OFF_GD:OFF_GD + LORA_GATE]

    z = w0_ref[...] + _dot3(jnp.tanh(wd), wup_ref[...])
    nz = -z
    softplus = jnp.maximum(nz, 0.0) + jnp.log(1.0 + jnp.exp(-jnp.abs(nz)))
    w = -softplus - 0.5
    lw_o[...] = -jnp.exp(w)
    a = jax.nn.sigmoid(a0_ref[...] + _dot3(ad, aup_ref[...]))
    g_o[...] = _dot3(jax.nn.sigmoid(gd), gup_ref[...])

    e = e_ref[...]
    kk = k * kk_ref[...]
    nrm = jnp.sqrt(_head_sums(kk * kk, e))
    kk = kk / jnp.maximum(nrm, 1e-12)
    k2 = k * (1.0 + (a - 1.0) * ka_ref[...])
    r_o[...] = r
    k_o[...] = k2
    v_o[...] = v
    kk_o[...] = kk
    b_o[...] = kk * a
    bonus_o[...] = _head_sums(r * k2 * rk_ref[...], e) * v


def _rwkv_prep(pb2, seq, mu, w0, wup, a0, aup, gup, k_k, k_a, r_k, e_bd, tm):
    m, n = pb2.shape
    tps = seq // tm
    row = lambda a: a.reshape(1, -1)
    full = lambda a: pl.BlockSpec(a.shape, lambda i: (0,) * a.ndim)
    args = [pb2, pb2, row(mu), row(w0), wup, row(a0), aup, gup, row(k_k), row(k_a), row(r_k), e_bd]
    in_specs = [
        pl.BlockSpec((HALO, n), lambda i: (jnp.maximum(i * (tm // HALO) - 1, 0), 0)),
        pl.BlockSpec((tm, n), lambda i: (i, 0)),
    ] + [full(a) for a in args[2:]]
    o_spec = pl.BlockSpec((tm, B_WIDTH), lambda i: (i, 0))
    o_shape = jax.ShapeDtypeStruct((m, B_WIDTH), F32)
    return pl.pallas_call(
        functools.partial(_rwkv_prep_kernel, tiles_per_seq=tps),
        grid=(m // tm,),
        in_specs=in_specs,
        out_specs=[o_spec] * 8,
        out_shape=[o_shape] * 8,
        compiler_params=_cparams(("parallel",)),
        name="rwkv_prep",
    )(*args)


def _wkv_kernel(r_ref, lw_ref, k_ref, v_ref, kk_ref, b_ref, y_ref, st_ref):
    c = WKV_CHUNK
    nb = r_ref.shape[0]
    n_chunks = r_ref.shape[1] // c

    @pl.when(pl.program_id(0) == 0)
    def _():
        st_ref[...] = jnp.zeros_like(st_ref)

    rowi = lax.broadcasted_iota(jnp.int32, (c, c), 0)
    coli = lax.broadcasted_iota(jnp.int32, (c, c), 1)
    incl = rowi >= coli
    strict = rowi > coli
    row2 = lax.broadcasted_iota(jnp.int32, (2 * c, 2 * c), 0)
    col2 = lax.broadcasted_iota(jnp.int32, (2 * c, 2 * c), 1)
    tri2 = row2 - jnp.where(row2 < c, 1, c) >= jnp.where(col2 >= c, col2 - c, 2 * c)
    tri = jnp.where(incl, 1.0, 0.0).astype(BF16)
    eye = jnp.where(rowi == coli, 1.0, 0.0)

    nu = WKV_UNROLL
    heads = range(B_HEADS)
    hsl = [slice(h * HEAD_DIM, (h + 1) * HEAD_DIM) for h in heads]
    segs = [(b, u) for b in range(nb) for u in range(nu)]
    ns = range(len(segs))
    units = [(s, h) for s in ns for h in heads]

    def chunks(ci, carry):
        sls = [pl.ds(pl.multiple_of((ci * nu + u) * c, c), c) for u in range(nu)]
        ld = lambda ref, s: ref[segs[s][0], sls[segs[s][1]], :]
        lw = [ld(lw_ref, s) for s in ns]
        cum = [_dot_exact_lhs(tri, t) for t in lw]
        tot = [t[c - 1:c, :] for t in cum]
        p_inv = [jnp.exp(-t) for t in cum]
        p_rest = [jnp.exp(tot[s] - cum[s]) for s in ns]
        p_tot = [jnp.exp(t) for t in tot]
        rh_all = [(ld(r_ref, s) * jnp.exp(cum[s])).astype(BF16) for s in ns]
        ah_all = [(-ld(kk_ref, s) * jnp.exp(cum[s] - lw[s])).astype(BF16) for s in ns]
        b_all = [ld(b_ref, s) for s in ns]
        k_all = [ld(k_ref, s) for s in ns]
        bh_all = [(b_all[s] * p_inv[s]).astype(BF16) for s in ns]
        kh_all = [(k_all[s] * p_inv[s]).astype(BF16) for s in ns]
        bt_all = [(b_all[s] * p_rest[s]).astype(BF16) for s in ns]
        kt_all = [(k_all[s] * p_rest[s]).astype(BF16) for s in ns]
        v_all = [ld(v_ref, s) for s in ns]

        ah = [ah_all[u][:, hsl[h]] for u, h in units]
        rh = [rh_all[u][:, hsl[h]] for u, h in units]
        vf = [v_all[u][:, hsl[h]] for u, h in units]
        vb = [t.astype(BF16) for t in vf]
        n = range(len(units))
        ar = [jnp.concatenate([ah[i], rh[i]], axis=0) for i in n]
        g = [_dot_nt(ar[i], jnp.concatenate([bh_all[u][:, hsl[h]], kh_all[u][:, hsl[h]]], axis=0))
             for i, (u, h) in enumerate(units)]
        gb = [t[:, :c] for t in g]
        a_rb = [jnp.where(incl, t[c:], 0.0).astype(BF16) for t in gb]
        akrk = [jnp.where(tri2, t, 0.0).astype(BF16) for t in g]
        x0 = [jnp.where(strict, t[:c], 0.0) for t in gb]
        t = [eye + xi for xi in x0]
        xb = [xi.astype(BF16) for xi in x0]
        x = [_dot(xi, xi) for xi in xb]
        for _ in range(int(math.log2(c)) - 2):
            xt = [_dot(jnp.concatenate([x[i], t[i]], axis=0).astype(BF16), x[i].astype(BF16)) for i in n]
            x = [p[:c] for p in xt]
            t = [t[i] + xt[i][c:] for i in n]
        t = [t[i] + _dot(t[i].astype(BF16), x[i].astype(BF16)) for i in n]
        tb = [ti.astype(BF16) for ti in t]
        zv = jnp.zeros((c, HEAD_DIM), BF16)
        w1y2 = [_dot(akrk[i], jnp.concatenate([zv, vb[i]], axis=0)) for i in n]
        w1 = [p[:c] for p in w1y2]
        y2 = [p[c:] for p in w1y2]
        n2 = [_dot(vf[i].T.astype(BF16), kt_all[u][:, hsl[h]]) for i, (u, h) in enumerate(units)]
        a2 = [_dot(tb[i], ah[i]).astype(BF16) for i in n]
        u1t = [_dot(tb[i], w1[i].astype(BF16)).T for i in n]
        chains = [(b, h) for b in range(nb) for h in heads]
        nc = range(len(chains))
        st = [st_ref[i] for i in nc]
        for u in range(nu):
            seg = [b * nu + u for b, h in chains]
            un = [seg[i] * B_HEADS + chains[i][1] for i in nc]
            hs = [hsl[h] for b, h in chains]
            s0b = [t.astype(BF16) for t in st]
            utb = [(_dot_nt(s0b[i], a2[un[i]]) + u1t[un[i]]).astype(BF16) for i in nc]
            ys = [_dot_nt(rh[un[i]], s0b[i]) + _dot_nt(a_rb[un[i]], utb[i]) + y2[un[i]] for i in nc]
            st = [st[i] * p_tot[seg[i]][:, hs[i]] + _dot(utb[i], bt_all[seg[i]][:, hs[i]]) + n2[un[i]] for i in nc]
            for b in range(nb):
                y_ref[b, sls[u], :] = jnp.concatenate(ys[b * B_HEADS:(b + 1) * B_HEADS], axis=-1)
        st_ref[...] = jnp.stack(st, axis=0)
        return carry

    lax.fori_loop(0, n_chunks // nu, chunks, 0)


def _wkv_scan(r, lw, k, v, kk, b, ts):
    bsz, s, w = r.shape
    spec = pl.BlockSpec((bsz, ts, w), lambda j: (0, j, 0))
    return pl.pallas_call(
        _wkv_kernel,
        grid=(s // ts,),
        in_specs=[spec] * 6,
        out_specs=spec,
        out_shape=jax.ShapeDtypeStruct((bsz, s, w), F32),
        scratch_shapes=[pltpu.VMEM((bsz * B_HEADS, HEAD_DIM, HEAD_DIM), F32)],
        compiler_params=_cparams(("arbitrary",)),
        name="wkv_scan",
    )(r, lw, k, v, kk, b)


def _merge_kernel(ya_ref, o1_ref, o2_ref, o3_ref, l1_ref, l2_ref, l3_ref, yr_ref, g_ref, bonus_ref,
                  gates_ref, lng_ref, lnb_ref, e_ref, pa_ref, pb_ref, pc_ref, out_ref, unf_ref):
    tm, d = out_ref.shape

    def unfold(ref, slot, dil):
        nc = C_OUT // LANES
        for r in range(dil):
            for c in range(nc):
                unf_ref[slot * nc + c, pl.ds(r, tm // dil, stride=dil), :] = (
                    ref[:, r * C_OUT + c * LANES:r * C_OUT + (c + 1) * LANES].astype(F32))
        return jnp.concatenate([unf_ref[slot * nc + c] for c in range(nc)], axis=-1)

    d2, d3 = C_GROUPS[1][1], C_GROUPS[2][1]
    o1, l1 = o1_ref[...].astype(F32), l1_ref[...]
    o2, l2 = unfold(o2_ref, 0, d2), unfold(l2_ref, 1, d2)
    o3, l3 = unfold(o3_ref, 2, d3), unfold(l3_ref, 3, d3)
    m = jnp.maximum(jnp.maximum(l1, l2), l3)
    e1, e2, e3 = jnp.exp(l1 - m), jnp.exp(l2 - m), jnp.exp(l3 - m)
    yc = (e1 * o1 + e2 * o2 + e3 * o3) / (e1 + e2 + e3)

    e = e_ref[...]
    y = yr_ref[...]
    mean = _head_sums(y, e, exact=True) * (1.0 / HEAD_DIM)
    dv = y - mean
    var = _head_sums(dv * dv, e) * (1.0 / HEAD_DIM)
    yb = dv * lax.rsqrt(var + B_GN_EPS) * lng_ref[...] + lnb_ref[...] + bonus_ref[...]
    yb = yb * g_ref[...]

    gates = gates_ref[...]
    merged = (gates[:, 0:d].astype(F32) * _dot(ya_ref[...], pa_ref[...])
              + gates[:, d:2 * d].astype(F32) * _dot(yb.astype(BF16), pb_ref[...])
              + gates[:, 2 * d:3 * d].astype(F32) * _dot(yc.astype(BF16), pc_ref[...]))
    out_ref[...] = merged.astype(BF16)


def _merge(ya, o1, o2, o3, l1, l2, l3, yr, g, bonus, gates, lng, lnb, e_bd, pa, pb, pc, layer, tm):
    m, d = ya.shape[0], pa.shape[2]
    tok = lambda a: pl.BlockSpec((tm * a.shape[0] // m, a.shape[1]), lambda i: (i, 0))
    full = lambda a: pl.BlockSpec(a.shape, lambda i: (0,) * a.ndim, pipeline_mode=pl.Buffered(1))
    at_layer = lambda a: pl.BlockSpec((None,) + a.shape[1:], lambda i: (layer, 0, 0), pipeline_mode=pl.Buffered(1))
    toks = [ya, o1, o2, o3, l1, l2, l3, yr, g, bonus, gates]
    consts = [lng.reshape(1, -1), lnb.reshape(1, -1), e_bd]
    stacked = [pa, pb, pc]
    return pl.pallas_call(
        _merge_kernel,
        grid=(m // tm,),
        in_specs=[tok(a) for a in toks] + [full(a) for a in consts] + [at_layer(a) for a in stacked],
        out_specs=pl.BlockSpec((tm, d), lambda i: (i, 0)),
        out_shape=jax.ShapeDtypeStruct((m, d), BF16),
        scratch_shapes=[pltpu.VMEM((4 * C_OUT // LANES, tm, LANES), F32)],
        compiler_params=_cparams(("parallel",)),
        name="merge",
    )(*toks, *consts, *stacked)


def _out_proj_kernel(x_ref, m_ref, w_ref, o_ref):
    o_ref[...] = x_ref[...] + _dot(m_ref[...], w_ref[...])


def _out_proj(x2, merged, wo, layer, tm):
    m, d = x2.shape
    tok = pl.BlockSpec((tm, d), lambda i: (i, 0))
    return pl.pallas_call(
        _out_proj_kernel,
        grid=(m // tm,),
        in_specs=[tok, tok,
                  pl.BlockSpec((None,) + wo.shape[1:], lambda i: (layer, 0, 0), pipeline_mode=pl.Buffered(1))],
        out_specs=tok,
        out_shape=jax.ShapeDtypeStruct((m, d), F32),
        compiler_params=_cparams(("parallel",)),
        name="out_proj",
    )(x2, merged, wo)


def _ffn_kernel(xh_ref, x_ref, g_ref, wg_ref, wv_ref, cw_ref, wd_ref, fg_ref, o_ref, h_ref, *,
                tiles_per_seq, final_norm, d_ff):
    j = pl.program_id(1)
    tm = x_ref.shape[0]
    tf = wg_ref.shape[1]

    @pl.when(j == 0)
    def _():
        x = x_ref[...]
        g = g_ref[...]
        h_ref[HALO:, :] = _rms(x, g).astype(BF16)
        seq_start = pl.program_id(0) % tiles_per_seq == 0
        halo = _rms(xh_ref[...], g)
        h_ref[0:HALO, :] = jnp.where(seq_start, 0.0, halo).astype(BF16)
        o_ref[...] = x

    h = h_ref[...]
    gext = _dot(h, wg_ref[...])
    val = _dot(h[HALO:], wv_ref[...])
    cw = cw_ref[...]
    gate = (cw[0:1, :] * pltpu.roll(gext, 2, 0)[HALO:]
            + cw[1:2, :] * pltpu.roll(gext, 1, 0)[HALO:]
            + cw[2:3, :] * gext[HALO:])
    col = j * tf + lax.broadcasted_iota(jnp.int32, (1, tf), 1)
    row = j * tf + lax.broadcasted_iota(jnp.int32, (tf, 1), 0)
    hid = jnp.where(col < d_ff, gate * jax.nn.sigmoid(gate) * val, 0.0)
    wd = jnp.where(row < d_ff, wd_ref[...], jnp.zeros((), BF16))
    o_ref[...] += _dot(hid.astype(BF16), wd)

    if final_norm:
        @pl.when(j == pl.num_programs(1) - 1)
        def _():
            o_ref[...] = _rms(o_ref[...], fg_ref[...])


def _ffn(x2, seq, g, wg, wv, cw, wd, layer, fg, final_norm, tm, tf):
    m, d = x2.shape
    ffp = wg.shape[2]
    d_ff = wv.shape[2]
    tps = seq // tm
    return pl.pallas_call(
        functools.partial(_ffn_kernel, tiles_per_seq=tps, final_norm=final_norm, d_ff=d_ff),
        grid=(m // tm, ffp // tf),
        in_specs=[
            pl.BlockSpec((HALO, d), lambda i, j: (jnp.maximum(i * (tm // HALO) - 1, 0), 0)),
            pl.BlockSpec((tm, d), lambda i, j: (i, 0)),
            pl.BlockSpec((1, d), lambda i, j: (0, 0)),
            pl.BlockSpec((None, d, tf), lambda i, j: (layer, 0, j)),
            pl.BlockSpec((None, d, tf), lambda i, j: (layer, 0, j)),
            pl.BlockSpec((None, 3, tf), lambda i, j: (layer, 0, j)),
            pl.BlockSpec((None, tf, d), lambda i, j: (layer, j, 0)),
            pl.BlockSpec((1, d), lambda i, j: (0, 0)),
        ],
        out_specs=pl.BlockSpec((tm, d), lambda i, j: (i, 0)),
        out_shape=jax.ShapeDtypeStruct((m, d), F32),
        scratch_shapes=[pltpu.VMEM((HALO + tm, d), BF16)],
        compiler_params=_cparams(("parallel", "arbitrary")),
        name="conv_ffn",
    )(x2, x2, g.reshape(1, d), wg, wv, cw, wd, fg.reshape(1, d))


def _t5_bucket(dist):
    small = dist < MAX_EXACT
    nf = jnp.maximum(dist, 1).astype(F32)
    large = MAX_EXACT + (jnp.log(nf / MAX_EXACT) / math.log(REL_MAX_DIST / MAX_EXACT)
                         * (N_BUCKETS - MAX_EXACT)).astype(jnp.int32)
    return jnp.where(small, dist, jnp.minimum(large, N_BUCKETS - 1))


def _band_bias(table, dilation, max_steps):
    i = jnp.arange(BLK)[:, None]
    j = jnp.arange(2 * BLK)[None, :]
    off = i + BLK - j
    onehot = jax.nn.one_hot(_t5_bucket(jnp.maximum(off, 0) * dilation), N_BUCKETS, dtype=F32)
    bias = jnp.einsum("ijb,bh->hij", onehot, table.astype(F32), precision=lax.Precision.HIGHEST)
    valid = (off >= 0) & (off <= max_steps)
    return jnp.where(valid[None], bias, NEG_BIG)


def _pad_cols(w, n):
    return jnp.pad(w, [(0, 0)] * (w.ndim - 1) + [(0, n - w.shape[-1])])


def _pad_rows(w, n):
    return jnp.pad(w, [(0, 0)] * (w.ndim - 2) + [(0, n - w.shape[-2]), (0, 0)])


def kernel(x, rel_bias, norm1_g, w_in, attn_sinks, rwkv_mu, rwkv_w0, rwkv_w_up, rwkv_a0, rwkv_a_up, rwkv_g_up,
           rwkv_k_k, rwkv_k_a, rwkv_r_k, rwkv_lnx_g, rwkv_lnx_b, proj_a, proj_b, proj_c, w_out, norm2_g,
           ffn_up, ffn_conv, ffn_down, final_g):
    bsz, seq, d = x.shape
    depth = w_in.shape[0]
    m = bsz * seq
    d_ff = ffn_conv.shape[-1]
    ffp = -(-d_ff // D_FF_PAD_TO) * D_FF_PAD_TO

    tm_proj = min(1024, seq)
    tm_tok = min(512, seq)
    tm_ffn = min(1024, seq)
    ts_wkv = min(512, seq)

    na = A_HQ
    bias_a = _band_bias(rel_bias[:, :na], 1, BLK - 1)
    bias_c = [_band_bias(rel_bias[:, na + gi * C_HG:na + (gi + 1) * C_HG], dil, win // dil)
              for gi, (win, dil) in enumerate(C_GROUPS)]

    q_end = (A_HQ + 2 * A_HKV) * HEAD_DIM
    b0 = q_end
    b_r_end = b0 + 3 * B_WIDTH
    b_wd_end = b_r_end + LORA_DECAY
    b_ad_end = b_wd_end + LORA_ICLR
    b_end = b_ad_end + LORA_GATE
    c_end = b_end + 3 * C_WIDTH

    def rwkv_cols(t):
        parts = [t[..., b0:b_r_end], _pad_cols(t[..., b_r_end:b_wd_end], LORA_PAD),
                 _pad_cols(t[..., b_wd_end:b_ad_end], LORA_PAD), t[..., b_ad_end:b_end]]
        return _pad_cols(jnp.concatenate(parts, axis=-1), N_RWKV)

    w_in_b = lax.optimization_barrier(w_in.astype(BF16))
    tail = lax.optimization_barrier(w_in_b[..., b_ad_end:])
    t_c, t_g = b_end - b_ad_end, c_end - b_ad_end
    qkv_c = [tail[..., t_c + t * C_WIDTH + gi * C_OUT:t_c + t * C_WIDTH + (gi + 1) * C_OUT]
             for gi in range(len(C_GROUPS)) for t in range(3)]
    rwkv_w = _pad_cols(jnp.concatenate(
        [w_in_b[..., b0:b_r_end], _pad_cols(w_in_b[..., b_r_end:b_wd_end], LORA_PAD),
         _pad_cols(w_in_b[..., b_wd_end:b_ad_end], LORA_PAD), tail[..., :t_c]], axis=-1), N_RWKV)
    w_all = jnp.concatenate([w_in_b[..., :q_end]] + qkv_c + [rwkv_w, tail[..., t_g:]], axis=-1)
    proj_blk = q_end
    n_rwkv_blk = N_RWKV // proj_blk
    n_gate_blk = (w_in.shape[-1] - c_end) // proj_blk
    mu_pad = rwkv_cols(jnp.pad(rwkv_mu, ((0, 0), (b0, 0))))
    wup_pad = _pad_rows(rwkv_w_up, LORA_PAD)
    aup_pad = _pad_rows(rwkv_a_up, LORA_PAD)
    r_k = rwkv_r_k.reshape(depth, B_WIDTH)

    head_id = jnp.arange(B_WIDTH) // HEAD_DIM
    e_bd = (head_id[:, None] == head_id[None, :]).astype(BF16)

    pa, pb, pc, wo = (t.astype(BF16) for t in (proj_a, proj_b, proj_c, w_out))
    wg = _pad_cols(ffn_up[..., :d_ff], ffp).astype(BF16)
    wv = ffn_up[..., d_ff:].astype(BF16)
    cw = _pad_cols(ffn_conv, ffp)
    wd = ffn_down.astype(BF16)

    x2 = x.reshape(m, d)
    for l in range(depth):
        p_swa, *p_dil, p_rwkv, gates = _proj(x2, norm1_g[l], w_all, l, n_rwkv_blk, n_gate_blk, tm_proj)

        kvw = A_HKV * HEAD_DIM
        y_a = _band_attention(p_swa.reshape(bsz, seq, -1), bias_a, attn_sinks[l], dil=1, q_blk=0,
                              k_blk=(A_HQ * HEAD_DIM) // kvw, v_blk=(A_HQ * HEAD_DIM) // kvw + 1,
                              q_w=A_HQ * HEAD_DIM, kv_w=kvw, nq=A_HQ, nkv=A_HKV, want_lse=False,
                              out_dtype=BF16, name="attn_swa", nqb=4)
        oc, lc = [], []
        for gi, (win, dil) in enumerate(C_GROUPS):
            pv = p_dil[gi].reshape(bsz, seq // dil, -1)
            o, ls = _band_attention(pv, bias_c[gi], None, dil=dil, q_blk=0, k_blk=1, v_blk=2, q_w=C_OUT,
                                    kv_w=C_OUT, nq=C_HG, nkv=C_HG, want_lse=True, out_dtype=BF16,
                                    name=f"attn_dil{dil}", nqb=min(4, seq // dil // BLK))
            oc.append(o.reshape(m // dil, dil * C_OUT))
            lc.append(ls.reshape(m // dil, dil * C_OUT))

        r, lw, k2, v, kk, bb, g, bonus = _rwkv_prep(
            p_rwkv, seq, mu_pad[l], rwkv_w0[l], wup_pad[l], rwkv_a0[l], aup_pad[l], rwkv_g_up[l],
            rwkv_k_k[l], rwkv_k_a[l], r_k[l], e_bd, tm_tok)
        sh = lambda t: t.reshape(bsz, seq, B_WIDTH)
        y_raw = _wkv_scan(sh(r), sh(lw), sh(k2), sh(v), sh(kk), sh(bb), ts_wkv).reshape(m, B_WIDTH)

        merged = _merge(y_a.reshape(m, A_HQ * HEAD_DIM), oc[0], oc[1], oc[2], lc[0], lc[1], lc[2], y_raw, g,
                        bonus, gates, rwkv_lnx_g[l], rwkv_lnx_b[l], e_bd, pa, pb, pc, l, tm_tok)
        x2 = _out_proj(x2, merged, wo, l, tm_proj)
        x2 = _ffn(x2, seq, norm2_g[l], wg, wv, cw, wd, l, final_g, l == depth - 1, tm_ffn, 512)
    return x2.reshape(bsz, seq, d)
```

```python
import functools
import math

import jax
import jax.numpy as jnp
from jax import lax
from jax.experimental import pallas as pl
from jax.experimental.pallas import tpu as pltpu

F32 = jnp.float32
BF16 = jnp.bfloat16

HEAD_DIM = 64
LANES = 128
BLK = 128
NORM_EPS = 1e-5
A_HQ, A_HKV = 8, 2
B_HEADS = 12
B_WIDTH = B_HEADS * HEAD_DIM
LORA_DECAY, LORA_ICLR, LORA_GATE = 96, 96, 256
LORA_PAD = 128
B_GN_EPS = 64e-5
C_GROUPS = ((128, 1), (512, 4), (2048, 16))
C_HG = 4
C_WIDTH = C_HG * len(C_GROUPS) * HEAD_DIM
C_OUT = C_HG * HEAD_DIM
N_BUCKETS, MAX_EXACT, REL_MAX_DIST = 32, 16, 2048
D_FF_PAD_TO = 512
NEG_BIG = -1e30

N_RWKV = 3 * B_WIDTH + 2 * LORA_PAD + LORA_GATE + 256
OFF_WD = 3 * B_WIDTH
OFF_AD = OFF_WD + LORA_PAD
OFF_GD = OFF_AD + LORA_PAD
WKV_CHUNK = 64
WKV_UNROLL = 2
HALO = 16

VMEM_LIMIT = 56 * 1024 * 1024
VMEM_LIMIT_PROJ = 62 * 1024 * 1024


def _cparams(sem, vmem_limit=VMEM_LIMIT):
    return pltpu.CompilerParams(dimension_semantics=sem, vmem_limit_bytes=vmem_limit)


def _dot(a, b):
    return jnp.dot(a, b, preferred_element_type=F32)


def _dot_nt(a, b):
    return lax.dot_general(a, b, (((1,), (1,)), ((), ())), preferred_element_type=F32)


def _split2(x):
    hi = x.astype(BF16)
    lo = (x - hi.astype(F32)).astype(BF16)
    return hi, lo


def _head_sums(a, e_bf16, exact=False):
    if not exact:
        return _dot(a.astype(BF16), e_bf16)
    h, l = _split2(a)
    return _dot(h, e_bf16) + _dot(l, e_bf16)


def _dot_exact_lhs(a_bf16, b):
    h, l = _split2(b)
    return _dot(a_bf16, h) + _dot(a_bf16, l)


def _dot3(a, b):
    ah, al = _split2(a)
    bh, bl = _split2(b)
    return _dot(ah, bh) + _dot(ah, bl) + _dot(al, bh)


def _rms(x, g):
    ms = jnp.mean(x * x, axis=-1, keepdims=True)
    return x * lax.rsqrt(ms + NORM_EPS) * g


def _proj_kernel(x_ref, g_ref, wa_ref, t0_ref, t1_ref, t2_ref, wl_ref, pa_ref, p1_ref, p4_ref, p16_ref, pr_ref,
                 pg_ref, h_ref, acc_ref, *, n_rwkv):
    j = pl.program_id(1)
    tm = x_ref.shape[0]
    w = wa_ref.shape[1]
    n_attn = 1 + len(C_GROUPS)

    @pl.when(j == 0)
    def _():
        h_ref[...] = _rms(x_ref[...], g_ref[...]).astype(BF16)

    def from_block(ref):
        return _dot(h_ref[...], ref[...])

    def from_thirds():
        h = h_ref[...]
        return jnp.concatenate([_dot(h, t0_ref[...]), _dot(h, t1_ref[...]), _dot(h, t2_ref[...])], axis=-1)

    @pl.when(j == 0)
    def _():
        pa_ref[...] = from_block(wa_ref).astype(BF16)

    @pl.when(j == 1)
    def _():
        p1_ref[...] = from_thirds().astype(BF16)

    for jj, dil, ref in ((2, C_GROUPS[1][1], p4_ref), (3, C_GROUPS[2][1], p16_ref)):
        @pl.when(j == jj)
        def _(dil=dil, ref=ref):
            a = from_thirds()
            for c in range(w // LANES):
                acc_ref[c] = a[:, c * LANES:(c + 1) * LANES]
            for r in range(dil):
                for c in range(w // LANES):
                    ref[:, r * w + c * LANES:r * w + (c + 1) * LANES] = (
                        acc_ref[c, pl.ds(r, tm // dil, stride=dil), :].astype(BF16))

    @pl.when(jnp.logical_and(j >= n_attn, j < n_attn + n_rwkv - 1))
    def _():
        pr_ref[...] = from_block(wa_ref).astype(BF16)

    @pl.when(j == n_attn + n_rwkv - 1)
    def _():
        pr_ref[...] = from_block(wl_ref).astype(BF16)

    @pl.when(j >= n_attn + n_rwkv)
    def _():
        pg_ref[...] = jax.nn.sigmoid(from_thirds()).astype(BF16)


def _proj(x2, g, w_b, tail, w_lora, layer, wq, n_rwkv, n_gate, tail_c0, tail_g0, tm):
    m, d = x2.shape
    ng = len(C_GROUPS)
    n_attn = 1 + ng
    nblk = n_attn + n_rwkv + n_gate
    third = wq // 3
    d4, d16 = C_GROUPS[1][1], C_GROUPS[2][1]
    g0 = n_attn + n_rwkv

    def wa_idx(i, j):
        return (layer, 0, jnp.where(j == 0, 0, jnp.clip(j - n_attn + 1, 1, n_rwkv - 1)))

    def third_idx(p):
        def idx(i, j):
            grp = jnp.clip(j, 1, ng) - 1
            gate = jnp.maximum(j, g0) - g0
            return (layer, 0, jnp.where(j <= ng, tail_c0 + grp + ng * p, tail_g0 + 3 * gate + p))
        return idx

    return pl.pallas_call(
        functools.partial(_proj_kernel, n_rwkv=n_rwkv),
        grid=(m // tm, nblk),
        in_specs=[
            pl.BlockSpec((tm, d), lambda i, j: (i, 0)),
            pl.BlockSpec((1, d), lambda i, j: (0, 0)),
            pl.BlockSpec((None, d, wq), wa_idx),
            pl.BlockSpec((None, d, third), third_idx(0)),
            pl.BlockSpec((None, d, third), third_idx(1)),
            pl.BlockSpec((None, d, third), third_idx(2)),
            pl.BlockSpec((None, d, wq), lambda i, j: (layer, 0, 0), pipeline_mode=pl.Buffered(1)),
        ],
        out_specs=[
            pl.BlockSpec((tm, wq), lambda i, j: (i, 0)),
            pl.BlockSpec((tm, wq), lambda i, j: (i, 0)),
            pl.BlockSpec((tm // d4, d4 * wq), lambda i, j: (i, 0)),
            pl.BlockSpec((tm // d16, d16 * wq), lambda i, j: (i, 0)),
            pl.BlockSpec((tm, wq), lambda i, j: (i, jnp.clip(j - n_attn, 0, n_rwkv - 1))),
            pl.BlockSpec((tm, wq), lambda i, j: (i, jnp.clip(j - g0, 0, n_gate - 1))),
        ],
        out_shape=[
            jax.ShapeDtypeStruct((m, wq), BF16),
            jax.ShapeDtypeStruct((m, wq), BF16),
            jax.ShapeDtypeStruct((m // d4, d4 * wq), BF16),
            jax.ShapeDtypeStruct((m // d16, d16 * wq), BF16),
            jax.ShapeDtypeStruct((m, n_rwkv * wq), BF16),
            jax.ShapeDtypeStruct((m, n_gate * wq), BF16),
        ],
        scratch_shapes=[pltpu.VMEM((tm, d), BF16), pltpu.VMEM((wq // LANES, tm, LANES), F32)],
        compiler_params=_cparams(("parallel", "arbitrary"), VMEM_LIMIT_PROJ),
        name="proj",
    )(x2, g.reshape(1, d), w_b, tail, tail, tail, w_lora)


def _band_attn_kernel(*refs, nq, nkv, has_sink, want_lse):
    it = iter(refs)
    q_ref, kp_ref, kc_ref, vp_ref, vc_ref, bias_ref = (next(it) for _ in range(6))
    sink_ref = next(it) if has_sink else None
    o_ref = next(it)
    lse_ref = next(it) if want_lse else None

    nqb = q_ref.shape[0] // BLK
    first = pl.program_id(2) == 0
    q = q_ref[...] * jnp.asarray(HEAD_DIM ** -0.5, BF16)
    k = jnp.concatenate([kp_ref[...], kc_ref[...]], axis=0)
    v = jnp.concatenate([vp_ref[...], vc_ref[...]], axis=0)
    col = lax.broadcasted_iota(jnp.int32, (BLK, 2 * BLK), 1)
    edge = jnp.where(col < BLK, jnp.where(first, NEG_BIG, 0.0), 0.0)
    rep = nq // nkv
    units = [(i, h) for i in range(nqb) for h in range(nq)]
    hd = lambda t, i: t[:, i * HEAD_DIM:(i + 1) * HEAD_DIM]
    qs = lambda i, h: hd(q[i * BLK:(i + 1) * BLK], h)
    win = lambda t, i, h: hd(t[i * BLK:(i + 2) * BLK], h // rep)
    s = [_dot_nt(qs(i, h), win(k, i, h)) + bias_ref[h] for i, h in units]
    s = [s[u] + edge if i == 0 else s[u] for u, (i, h) in enumerate(units)]
    m = [jnp.max(t, axis=-1, keepdims=True) for t in s]
    if has_sink:
        m = [jnp.maximum(m[u], sink_ref[h]) for u, (i, h) in enumerate(units)]
    p = [jnp.exp(s[u] - m[u]).astype(BF16) for u in range(len(units))]
    ones = jnp.ones((2 * BLK, LANES), BF16)
    l = [_dot(t, ones)[:, :HEAD_DIM] for t in p]
    denom = [l[u] + jnp.exp(sink_ref[h] - m[u]) for u, (i, h) in enumerate(units)] if has_sink else l
    o = [_dot(p[u], win(v, i, h)) / denom[u] for u, (i, h) in enumerate(units)]
    rows = lambda parts: jnp.concatenate(
        [jnp.concatenate(parts[i * nq:(i + 1) * nq], axis=-1) for i in range(nqb)], axis=0)
    o_ref[...] = rows(o).astype(o_ref.dtype)
    if want_lse:
        lse_ref[...] = rows([m[u] + jnp.log(l[u]) for u in range(len(units))])


def _band_attention(pv, bias, sink, *, dil, q_blk, k_blk, v_blk, q_w, kv_w, nq, nkv, want_lse, out_dtype, name,
                    nqb):
    b, lf, nd = pv.shape
    n = nd // dil
    nb = lf // (BLK * nqb)
    qpr, kpr = n // q_w, n // kv_w
    has_sink = sink is not None
    prev = lambda j: jnp.maximum(j * nqb - 1, 0)

    in_specs = [
        pl.BlockSpec((None, nqb * BLK, q_w), lambda bi, r, j: (bi, j, r * qpr + q_blk)),
        pl.BlockSpec((None, BLK, kv_w), lambda bi, r, j: (bi, prev(j), r * kpr + k_blk)),
        pl.BlockSpec((None, nqb * BLK, kv_w), lambda bi, r, j: (bi, j, r * kpr + k_blk)),
        pl.BlockSpec((None, BLK, kv_w), lambda bi, r, j: (bi, prev(j), r * kpr + v_blk)),
        pl.BlockSpec((None, nqb * BLK, kv_w), lambda bi, r, j: (bi, j, r * kpr + v_blk)),
        pl.BlockSpec((nq, BLK, 2 * BLK), lambda bi, r, j: (0, 0, 0)),
    ]
    args = [pv, pv, pv, pv, pv, bias]
    if has_sink:
        in_specs.append(pl.BlockSpec(memory_space=pltpu.SMEM))
        args.append(sink)
    ow = nq * HEAD_DIM
    out_spec = pl.BlockSpec((None, nqb * BLK, ow), lambda bi, r, j: (bi, j, r))
    out_shape = jax.ShapeDtypeStruct((b, lf, dil * ow), out_dtype)
    if want_lse:
        out_specs = [out_spec, out_spec]
        out_shapes = [out_shape, jax.ShapeDtypeStruct((b, lf, dil * ow), F32)]
    else:
        out_specs, out_shapes = out_spec, out_shape
    return pl.pallas_call(
        functools.partial(_band_attn_kernel, nq=nq, nkv=nkv, has_sink=has_sink, want_lse=want_lse),
        grid=(b, dil, nb),
        in_specs=in_specs,
        out_specs=out_specs,
        out_shape=out_shapes,
        compiler_params=_cparams(("parallel", "parallel", "arbitrary")),
        name=name,
    )(*args)


def _rwkv_prep_kernel(ph_ref, p_ref, mu_ref, w0_ref, wup_ref, a0_ref, aup_ref, gup_ref, kk_ref, ka_ref,
                      rk_ref, e_ref, r_o, lw_o, k_o, v_o, kk_o, b_o, g_o, bonus_o, *, tiles_per_seq):
    tm = p_ref.shape[0]
    p = p_ref[...].astype(F32)
    seq_start = pl.program_id(0) % tiles_per_seq == 0
    last = ph_ref[...].astype(F32)[HALO - 1:HALO, :]
    last = jnp.where(seq_start, 0.0, last)
    row = lax.broadcasted_iota(jnp.int32, (tm, 1), 0)
    prev = jnp.where(row == 0, last, pltpu.roll(p, 1, 0))
    pf = p + (prev - p) * mu_ref[...]

    r = pf[:, 0:B_WIDTH]
    k = pf[:, B_WIDTH:2 * B_WIDTH]
    v = pf[:, 2 * B_WIDTH:3 * B_WIDTH]
    wd = pf[:, OFF_WD:OFF_WD + LORA_PAD]
    ad = pf[:, OFF_AD:OFF_AD + LORA_PAD]
    gd = pf[:, OFF_GD:OFF_GD + LORA_GATE]

    z = w0_ref[...] + _dot3(jnp.tanh(wd), wup_ref[...])
    nz = -z
    softplus = jnp.maximum(nz, 0.0) + jnp.log(1.0 + jnp.exp(-jnp.abs(nz)))
    w = -softplus - 0.5
    lw_o[...] = -jnp.exp(w)
    a = jax.nn.sigmoid(a0_ref[...] + _dot3(ad, aup_ref[...]))
    g_o[...] = _dot3(jax.nn.sigmoid(gd), gup_ref[...])

    e = e_ref[...]
    kk = k * kk_ref[...]
    nrm = jnp.sqrt(_head_sums(kk * kk, e))
    kk = kk / jnp.maximum(nrm, 1e-12)
    k2 = k * (1.0 + (a - 1.0) * ka_ref[...])
    r_o[...] = r
    k_o[...] = k2
    v_o[...] = v
    kk_o[...] = kk
    b_o[...] = kk * a
    bonus_o[...] = _head_sums(r * k2 * rk_ref[...], e) * v


def _rwkv_prep(pb2, seq, mu, w0, wup, a0, aup, gup, k_k, k_a, r_k, e_bd, tm):
    m, n = pb2.shape
    tps = seq // tm
    row = lambda a: a.reshape(1, -1)
    full = lambda a: pl.BlockSpec(a.shape, lambda i: (0,) * a.ndim)
    args = [pb2, pb2, row(mu), row(w0), wup, row(a0), aup, gup, row(k_k), row(k_a), row(r_k), e_bd]
    in_specs = [
        pl.BlockSpec((HALO, n), lambda i: (jnp.maximum(i * (tm // HALO) - 1, 0), 0)),
        pl.BlockSpec((tm, n), lambda i: (i, 0)),
    ] + [full(a) for a in args[2:]]
    o_spec = pl.BlockSpec((tm, B_WIDTH), lambda i: (i, 0))
    o_shape = jax.ShapeDtypeStruct((m, B_WIDTH), F32)
    return pl.pallas_call(
        functools.partial(_rwkv_prep_kernel, tiles_per_seq=tps),
        grid=(m // tm,),
        in_specs=in_specs,
        out_specs=[o_spec] * 8,
        out_shape=[o_shape] * 8,
        compiler_params=_cparams(("parallel",)),
        name="rwkv_prep",
    )(*args)


def _wkv_kernel(r_ref, lw_ref, k_ref, v_ref, kk_ref, b_ref, y_ref, st_ref):
    c = WKV_CHUNK
    nb = r_ref.shape[0]
    n_chunks = r_ref.shape[1] // c

    @pl.when(pl.program_id(0) == 0)
    def _():
        st_ref[...] = jnp.zeros_like(st_ref)

    rowi = lax.broadcasted_iota(jnp.int32, (c, c), 0)
    coli = lax.broadcasted_iota(jnp.int32, (c, c), 1)
    incl = rowi >= coli
    strict = rowi > coli
    row2 = lax.broadcasted_iota(jnp.int32, (2 * c, 2 * c), 0)
    col2 = lax.broadcasted_iota(jnp.int32, (2 * c, 2 * c), 1)
    tri2 = row2 - jnp.where(row2 < c, 1, c) >= jnp.where(col2 >= c, col2 - c, 2 * c)
    tri = jnp.where(incl, 1.0, 0.0).astype(BF16)
    eye = jnp.where(rowi == coli, 1.0, 0.0)

    nu = WKV_UNROLL
    heads = range(B_HEADS)
    hsl = [slice(h * HEAD_DIM, (h + 1) * HEAD_DIM) for h in heads]
    segs = [(b, u) for b in range(nb) for u in range(nu)]
    ns = range(len(segs))
    units = [(s, h) for s in ns for h in heads]

    def chunks(ci, carry):
        sls = [pl.ds(pl.multiple_of((ci * nu + u) * c, c), c) for u in range(nu)]
        ld = lambda ref, s: ref[segs[s][0], sls[segs[s][1]], :]
        lw = [ld(lw_ref, s) for s in ns]
        cum = [_dot_exact_lhs(tri, t) for t in lw]
        tot = [t[c - 1:c, :] for t in cum]
        p_inv = [jnp.exp(-t) for t in cum]
        p_rest = [jnp.exp(tot[s] - cum[s]) for s in ns]
        p_tot = [jnp.exp(t) for t in tot]
        rh_all = [(ld(r_ref, s) * jnp.exp(cum[s])).astype(BF16) for s in ns]
        ah_all = [(-ld(kk_ref, s) * jnp.exp(cum[s] - lw[s])).astype(BF16) for s in ns]
        b_all = [ld(b_ref, s) for s in ns]
        k_all = [ld(k_ref, s) for s in ns]
        bh_all = [(b_all[s] * p_inv[s]).astype(BF16) for s in ns]
        kh_all = [(k_all[s] * p_inv[s]).astype(BF16) for s in ns]
        bt_all = [(b_all[s] * p_rest[s]).astype(BF16) for s in ns]
        kt_all = [(k_all[s] * p_rest[s]).astype(BF16) for s in ns]
        v_all = [ld(v_ref, s) for s in ns]

        ah = [ah_all[u][:, hsl[h]] for u, h in units]
        rh = [rh_all[u][:, hsl[h]] for u, h in units]
        vf = [v_all[u][:, hsl[h]] for u, h in units]
        vb = [t.astype(BF16) for t in vf]
        n = range(len(units))
        ar = [jnp.concatenate([ah[i], rh[i]], axis=0) for i in n]
        g = [_dot_nt(ar[i], jnp.concatenate([bh_all[u][:, hsl[h]], kh_all[u][:, hsl[h]]], axis=0))
             for i, (u, h) in enumerate(units)]
        gb = [t[:, :c] for t in g]
        a_rb = [jnp.where(incl, t[c:], 0.0).astype(BF16) for t in gb]
        akrk = [jnp.where(tri2, t, 0.0).astype(BF16) for t in g]
        x0 = [jnp.where(strict, t[:c], 0.0) for t in gb]
        t = [eye + xi for xi in x0]
        xb = [xi.astype(BF16) for xi in x0]
        x = [_dot(xi, xi) for xi in xb]
        for _ in range(int(math.log2(c)) - 2):
            xt = [_dot(jnp.concatenate([x[i], t[i]], axis=0).astype(BF16), x[i].astype(BF16)) for i in n]
            x = [p[:c] for p in xt]
            t = [t[i] + xt[i][c:] for i in n]
        t = [t[i] + _dot(t[i].astype(BF16), x[i].astype(BF16)) for i in n]
        tb = [ti.astype(BF16) for ti in t]
        zv = jnp.zeros((c, HEAD_DIM), BF16)
        w1y2 = [_dot(akrk[i], jnp.concatenate([zv, vb[i]], axis=0)) for i in n]
        w1 = [p[:c] for p in w1y2]
        y2 = [p[c:] for p in w1y2]
        n2 = [_dot(vf[i].T.astype(BF16), kt_all[u][:, hsl[h]]) for i, (u, h) in enumerate(units)]
        a2 = [_dot(tb[i], ah[i]).astype(BF16) for i in n]
        u1t = [_dot(tb[i], w1[i].astype(BF16)).T for i in n]
        chains = [(b, h) for b in range(nb) for h in heads]
        nc = range(len(chains))
        st = [st_ref[i] for i in nc]
        for u in range(nu):
            seg = [b * nu + u for b, h in chains]
            un = [seg[i] * B_HEADS + chains[i][1] for i in nc]
            hs = [hsl[h] for b, h in chains]
            s0b = [t.astype(BF16) for t in st]
            utb = [(_dot_nt(s0b[i], a2[un[i]]) + u1t[un[i]]).astype(BF16) for i in nc]
            ys = [_dot_nt(rh[un[i]], s0b[i]) + _dot_nt(a_rb[un[i]], utb[i]) + y2[un[i]] for i in nc]
            st = [st[i] * p_tot[seg[i]][:, hs[i]] + _dot(utb[i], bt_all[seg[i]][:, hs[i]]) + n2[un[i]] for i in nc]
            for b in range(nb):
                y_ref[b, sls[u], :] = jnp.concatenate(ys[b * B_HEADS:(b + 1) * B_HEADS], axis=-1)
        st_ref[...] = jnp.stack(st, axis=0)
        return carry

    lax.fori_loop(0, n_chunks // nu, chunks, 0)


def _wkv_scan(r, lw, k, v, kk, b, ts):
    bsz, s, w = r.shape
    spec = pl.BlockSpec((bsz, ts, w), lambda j: (0, j, 0))
    return pl.pallas_call(
        _wkv_kernel,
        grid=(s // ts,),
        in_specs=[spec] * 6,
        out_specs=spec,
        out_shape=jax.ShapeDtypeStruct((bsz, s, w), F32),
        scratch_shapes=[pltpu.VMEM((bsz * B_HEADS, HEAD_DIM, HEAD_DIM), F32)],
        compiler_params=_cparams(("arbitrary",)),
        name="wkv_scan",
    )(r, lw, k, v, kk, b)


def _merge_kernel(ya_ref, o1_ref, o2_ref, o3_ref, l1_ref, l2_ref, l3_ref, yr_ref, g_ref, bonus_ref,
                  gates_ref, lng_ref, lnb_ref, e_ref, pa_ref, pb_ref, pc_ref, out_ref, unf_ref):
    tm, d = out_ref.shape

    def unfold(ref, slot, dil):
        nc = C_OUT // LANES
        for r in range(dil):
            for c in range(nc):
                unf_ref[slot * nc + c, pl.ds(r, tm // dil, stride=dil), :] = (
                    ref[:, r * C_OUT + c * LANES:r * C_OUT + (c + 1) * LANES].astype(F32))
        return jnp.concatenate([unf_ref[slot * nc + c] for c in range(nc)], axis=-1)

    d2, d3 = C_GROUPS[1][1], C_GROUPS[2][1]
    o1, l1 = o1_ref[...].astype(F32), l1_ref[...]
    o2, l2 = unfold(o2_ref, 0, d2), unfold(l2_ref, 1, d2)
    o3, l3 = unfold(o3_ref, 2, d3), unfold(l3_ref, 3, d3)
    m = jnp.maximum(jnp.maximum(l1, l2), l3)
    e1, e2, e3 = jnp.exp(l1 - m), jnp.exp(l2 - m), jnp.exp(l3 - m)
    yc = (e1 * o1 + e2 * o2 + e3 * o3) / (e1 + e2 + e3)

    e = e_ref[...]
    y = yr_ref[...]
    mean = _head_sums(y, e, exact=True) * (1.0 / HEAD_DIM)
    dv = y - mean
    var = _head_sums(dv * dv, e) * (1.0 / HEAD_DIM)
    yb = dv * lax.rsqrt(var + B_GN_EPS) * lng_ref[...] + lnb_ref[...] + bonus_ref[...]
    yb = yb * g_ref[...]

    gates = gates_ref[...]
    merged = (gates[:, 0:d].astype(F32) * _dot(ya_ref[...], pa_ref[...])
              + gates[:, d:2 * d].astype(F32) * _dot(yb.astype(BF16), pb_ref[...])
              + gates[:, 2 * d:3 * d].astype(F32) * _dot(yc.astype(BF16), pc_ref[...]))
    out_ref[...] = merged.astype(BF16)


def _merge(ya, o1, o2, o3, l1, l2, l3, yr, g, bonus, gates, lng, lnb, e_bd, pa, pb, pc, layer, tm):
    m, d = ya.shape[0], pa.shape[2]
    tok = lambda a: pl.BlockSpec((tm * a.shape[0] // m, a.shape[1]), lambda i: (i, 0))
    full = lambda a: pl.BlockSpec(a.shape, lambda i: (0,) * a.ndim, pipeline_mode=pl.Buffered(1))
    at_layer = lambda a: pl.BlockSpec((None,) + a.shape[1:], lambda i: (layer, 0, 0), pipeline_mode=pl.Buffered(1))
    toks = [ya, o1, o2, o3, l1, l2, l3, yr, g, bonus, gates]
    consts = [lng.reshape(1, -1), lnb.reshape(1, -1), e_bd]
    stacked = [pa, pb, pc]
    return pl.pallas_call(
        _merge_kernel,
        grid=(m // tm,),
        in_specs=[tok(a) for a in toks] + [full(a) for a in consts] + [at_layer(a) for a in stacked],
        out_specs=pl.BlockSpec((tm, d), lambda i: (i, 0)),
        out_shape=jax.ShapeDtypeStruct((m, d), BF16),
        scratch_shapes=[pltpu.VMEM((4 * C_OUT // LANES, tm, LANES), F32)],
        compiler_params=_cparams(("parallel",)),
        name="merge",
    )(*toks, *consts, *stacked)


def _out_proj_kernel(x_ref, m_ref, w_ref, o_ref):
    o_ref[...] = x_ref[...] + _dot(m_ref[...], w_ref[...])


def _out_proj(x2, merged, wo, layer, tm):
    m, d = x2.shape
    tok = pl.BlockSpec((tm, d), lambda i: (i, 0))
    return pl.pallas_call(
        _out_proj_kernel,
        grid=(m // tm,),
        in_specs=[tok, tok,
                  pl.BlockSpec((None,) + wo.shape[1:], lambda i: (layer, 0, 0), pipeline_mode=pl.Buffered(1))],
        out_specs=tok,
        out_shape=jax.ShapeDtypeStruct((m, d), F32),
        compiler_params=_cparams(("parallel",)),
        name="out_proj",
    )(x2, merged, wo)


def _ffn_kernel(xh_ref, x_ref, g_ref, wg_ref, wv_ref, cw_ref, wd_ref, fg_ref, o_ref, h_ref, *,
                tiles_per_seq, final_norm, d_ff):
    j = pl.program_id(1)
    tm = x_ref.shape[0]
    tf = wg_ref.shape[1]

    @pl.when(j == 0)
    def _():
        x = x_ref[...]
        g = g_ref[...]
        h_ref[HALO:, :] = _rms(x, g).astype(BF16)
        seq_start = pl.program_id(0) % tiles_per_seq == 0
        halo = _rms(xh_ref[...], g)
        h_ref[0:HALO, :] = jnp.where(seq_start, 0.0, halo).astype(BF16)
        o_ref[...] = x

    h = h_ref[...]
    gext = _dot(h, wg_ref[...])
    val = _dot(h[HALO:], wv_ref[...])
    cw = cw_ref[...]
    gate = (cw[0:1, :] * pltpu.roll(gext, 2, 0)[HALO:]
            + cw[1:2, :] * pltpu.roll(gext, 1, 0)[HALO:]
            + cw[2:3, :] * gext[HALO:])
    col = j * tf + lax.broadcasted_iota(jnp.int32, (1, tf), 1)
    row = j * tf + lax.broadcasted_iota(jnp.int32, (tf, 1), 0)
    hid = jnp.where(col < d_ff, gate * jax.nn.sigmoid(gate) * val, 0.0)
    wd = jnp.where(row < d_ff, wd_ref[...], jnp.zeros((), BF16))
    o_ref[...] += _dot(hid.astype(BF16), wd)

    if final_norm:
        @pl.when(j == pl.num_programs(1) - 1)
        def _():
            o_ref[...] = _rms(o_ref[...], fg_ref[...])


def _ffn(x2, seq, g, wg, wv, cw, wd, layer, fg, final_norm, tm, tf):
    m, d = x2.shape
    ffp = wg.shape[2]
    d_ff = wv.shape[2]
    tps = seq // tm
    return pl.pallas_call(
        functools.partial(_ffn_kernel, tiles_per_seq=tps, final_norm=final_norm, d_ff=d_ff),
        grid=(m // tm, ffp // tf),
        in_specs=[
            pl.BlockSpec((HALO, d), lambda i, j: (jnp.maximum(i * (tm // HALO) - 1, 0), 0)),
            pl.BlockSpec((tm, d), lambda i, j: (i, 0)),
            pl.BlockSpec((1, d), lambda i, j: (0, 0)),
            pl.BlockSpec((None, d, tf), lambda i, j: (layer, 0, j)),
            pl.BlockSpec((None, d, tf), lambda i, j: (layer, 0, j)),
            pl.BlockSpec((None, 3, tf), lambda i, j: (layer, 0, j)),
            pl.BlockSpec((None, tf, d), lambda i, j: (layer, j, 0)),
            pl.BlockSpec((1, d), lambda i, j: (0, 0)),
        ],
        out_specs=pl.BlockSpec((tm, d), lambda i, j: (i, 0)),
        out_shape=jax.ShapeDtypeStruct((m, d), F32),
        scratch_shapes=[pltpu.VMEM((HALO + tm, d), BF16)],
        compiler_params=_cparams(("parallel", "arbitrary")),
        name="conv_ffn",
    )(x2, x2, g.reshape(1, d), wg, wv, cw, wd, fg.reshape(1, d))


def _t5_bucket(dist):
    small = dist < MAX_EXACT
    nf = jnp.maximum(dist, 1).astype(F32)
    large = MAX_EXACT + (jnp.log(nf / MAX_EXACT) / math.log(REL_MAX_DIST / MAX_EXACT)
                         * (N_BUCKETS - MAX_EXACT)).astype(jnp.int32)
    return jnp.where(small, dist, jnp.minimum(large, N_BUCKETS - 1))


def _band_bias(table, dilation, max_steps):
    i = jnp.arange(BLK)[:, None]
    j = jnp.arange(2 * BLK)[None, :]
    off = i + BLK - j
    onehot = jax.nn.one_hot(_t5_bucket(jnp.maximum(off, 0) * dilation), N_BUCKETS, dtype=F32)
    bias = jnp.einsum("ijb,bh->hij", onehot, table.astype(F32), precision=lax.Precision.HIGHEST)
    valid = (off >= 0) & (off <= max_steps)
    return jnp.where(valid[None], bias, NEG_BIG)


def _pad_cols(w, n):
    return jnp.pad(w, [(0, 0)] * (w.ndim - 1) + [(0, n - w.shape[-1])])


def _pad_rows(w, n):
    return jnp.pad(w, [(0, 0)] * (w.ndim - 2) + [(0, n - w.shape[-2]), (0, 0)])


def kernel(x, rel_bias, norm1_g, w_in, attn_sinks, rwkv_mu, rwkv_w0, rwkv_w_up, rwkv_a0, rwkv_a_up, rwkv_g_up,
           rwkv_k_k, rwkv_k_a, rwkv_r_k, rwkv_lnx_g, rwkv_lnx_b, proj_a, proj_b, proj_c, w_out, norm2_g,
           ffn_up, ffn_conv, ffn_down, final_g):
    bsz, seq, d = x.shape
    depth = w_in.shape[0]
    m = bsz * seq
    d_ff = ffn_conv.shape[-1]
    ffp = -(-d_ff // D_FF_PAD_TO) * D_FF_PAD_TO

    tm_proj = min(1024, seq)
    tm_tok = min(512, seq)
    tm_ffn = min(1024, seq)
    ts_wkv = min(512, seq)

    na = A_HQ
    bias_a = _band_bias(rel_bias[:, :na], 1, BLK - 1)
    bias_c = [_band_bias(rel_bias[:, na + gi * C_HG:na + (gi + 1) * C_HG], dil, win // dil)
              for gi, (win, dil) in enumerate(C_GROUPS)]

    q_end = (A_HQ + 2 * A_HKV) * HEAD_DIM
    b0 = q_end
    b_r_end = b0 + 3 * B_WIDTH
    b_wd_end = b_r_end + LORA_DECAY
    b_ad_end = b_wd_end + LORA_ICLR
    b_end = b_ad_end + LORA_GATE
    c_end = b_end + 3 * C_WIDTH

    def rwkv_cols(t):
        parts = [t[..., b0:b_r_end], _pad_cols(t[..., b_r_end:b_wd_end], LORA_PAD),
                 _pad_cols(t[..., b_wd_end:b_ad_end], LORA_PAD), t[..., b_ad_end:b_end]]
        return _pad_cols(jnp.concatenate(parts, axis=-1), N_RWKV)

    proj_blk = q_end
    w_in_b = lax.optimization_barrier(w_in.astype(BF16))
    tail = lax.optimization_barrier(w_in_b[..., b_ad_end:])
    t_c, t_g = b_end - b_ad_end, c_end - b_ad_end
    w_lora = _pad_cols(jnp.concatenate(
        [_pad_cols(w_in_b[..., b_r_end:b_wd_end], LORA_PAD), _pad_cols(w_in_b[..., b_wd_end:b_ad_end], LORA_PAD),
         tail[..., :t_c]], axis=-1), proj_blk)
    third = proj_blk // 3
    n_rwkv_blk = N_RWKV // proj_blk
    n_gate_blk = (w_in.shape[-1] - c_end) // proj_blk
    mu_pad = rwkv_cols(jnp.pad(rwkv_mu, ((0, 0), (b0, 0))))
    wup_pad = _pad_rows(rwkv_w_up, LORA_PAD)
    aup_pad = _pad_rows(rwkv_a_up, LORA_PAD)
    r_k = rwkv_r_k.reshape(depth, B_WIDTH)

    head_id = jnp.arange(B_WIDTH) // HEAD_DIM
    e_bd = (head_id[:, None] == head_id[None, :]).astype(BF16)

    pa, pb, pc, wo = (t.astype(BF16) for t in (proj_a, proj_b, proj_c, w_out))
    wg = _pad_cols(ffn_up[..., :d_ff], ffp).astype(BF16)
    wv = ffn_up[..., d_ff:].astype(BF16)
    cw = _pad_cols(ffn_conv, ffp)
    wd = ffn_down.astype(BF16)

    x2 = x.reshape(m, d)
    for l in range(depth):
        p_swa, *p_dil, p_rwkv, gates = _proj(x2, norm1_g[l], w_in_b, tail, w_lora, l, proj_blk, n_rwkv_blk,
                                             n_gate_blk, t_c // third, t_g // third, tm_proj)

        kvw = A_HKV * HEAD_DIM
        y_a = _band_attention(p_swa.reshape(bsz, seq, -1), bias_a, attn_sinks[l], dil=1, q_blk=0,
                              k_blk=(A_HQ * HEAD_DIM) // kvw, v_blk=(A_HQ * HEAD_DIM) // kvw + 1,
                              q_w=A_HQ * HEAD_DIM, kv_w=kvw, nq=A_HQ, nkv=A_HKV, want_lse=False,
                              out_dtype=BF16, name="attn_swa", nqb=4)
        oc, lc = [], []
        for gi, (win, dil) in enumerate(C_GROUPS):
            pv = p_dil[gi].reshape(bsz, seq // dil, -1)
            o, ls = _band_attention(pv, bias_c[gi], None, dil=dil, q_blk=0, k_blk=1, v_blk=2, q_w=C_OUT,
                                    kv_w=C_OUT, nq=C_HG, nkv=C_HG, want_lse=True, out_dtype=BF16,
                                    name=f"attn_dil{dil}", nqb=min(4, seq // dil // BLK))
            oc.append(o.reshape(m // dil, dil * C_OUT))
            lc.append(ls.reshape(m // dil, dil * C_OUT))

        r, lw, k2, v, kk, bb, g, bonus = _rwkv_prep(
            p_rwkv, seq, mu_pad[l], rwkv_w0[l], wup_pad[l], rwkv_a0[l], aup_pad[l], rwkv_g_up[l],
            rwkv_k_k[l], rwkv_k_a[l], r_k[l], e_bd, tm_tok)
        sh = lambda t: t.reshape(bsz, seq, B_WIDTH)
        y_raw = _wkv_scan(sh(r), sh(lw), sh(k2), sh(v), sh(kk), sh(bb), ts_wkv).reshape(m, B_WIDTH)

        merged = _merge(y_a.reshape(m, A_HQ * HEAD_DIM), oc[0], oc[1], oc[2], lc[0], lc[1], lc[2], y_raw, g,
                        bonus, gates, rwkv_lnx_g[l], rwkv_lnx_b[l], e_bd, pa, pb, pc, l, tm_tok)
        x2 = _out_proj(x2, merged, wo, l, tm_proj)
        x2 = _ffn(x2, seq, norm2_g[l], wg, wv, cw, wd, l, final_g, l == depth - 1, tm_ffn, 512)
    return x2.reshape(bsz, seq, d)
```

```python
import functools
import math

import jax
import jax.numpy as jnp
from jax import lax
from jax.experimental import pallas as pl
from jax.experimental.pallas import tpu as pltpu

F32 = jnp.float32
BF16 = jnp.bfloat16

HEAD_DIM = 64
LANES = 128
BLK = 128
NORM_EPS = 1e-5
A_HQ, A_HKV = 8, 2
B_HEADS = 12
B_WIDTH = B_HEADS * HEAD_DIM
LORA_DECAY, LORA_ICLR, LORA_GATE = 96, 96, 256
LORA_PAD = 128
B_GN_EPS = 64e-5
C_GROUPS = ((128, 1), (512, 4), (2048, 16))
C_HG = 4
C_WIDTH = C_HG * len(C_GROUPS) * HEAD_DIM
C_OUT = C_HG * HEAD_DIM
N_BUCKETS, MAX_EXACT, REL_MAX_DIST = 32, 16, 2048
D_FF_PAD_TO = 512
NEG_BIG = -1e30

N_RWKV = 3 * B_WIDTH + 2 * LORA_PAD + LORA_GATE + 256
OFF_WD = 3 * B_WIDTH
OFF_AD = OFF_WD + LORA_PAD
OFF_GD = OFF_AD + LORA_PAD
WKV_CHUNK = 64
WKV_UNROLL = 2
HALO = 16

VMEM_LIMIT = 56 * 1024 * 1024


def _cparams(sem):
    return pltpu.CompilerParams(dimension_semantics=sem, vmem_limit_bytes=VMEM_LIMIT)


def _dot(a, b):
    return jnp.dot(a, b, preferred_element_type=F32)


def _dot_nt(a, b):
    return lax.dot_general(a, b, (((1,), (1,)), ((), ())), preferred_element_type=F32)


def _split2(x):
    hi = x.astype(BF16)
    lo = (x - hi.astype(F32)).astype(BF16)
    return hi, lo


def _head_sums(a, e_bf16, exact=False):
    if not exact:
        return _dot(a.astype(BF16), e_bf16)
    h, l = _split2(a)
    return _dot(h, e_bf16) + _dot(l, e_bf16)


def _dot_exact_lhs(a_bf16, b):
    h, l = _split2(b)
    return _dot(a_bf16, h) + _dot(a_bf16, l)


def _dot3(a, b):
    ah, al = _split2(a)
    bh, bl = _split2(b)
    return _dot(ah, bh) + _dot(ah, bl) + _dot(al, bh)


def _rms(x, g):
    ms = jnp.mean(x * x, axis=-1, keepdims=True)
    return x * lax.rsqrt(ms + NORM_EPS) * g


def _proj_kernel(x_ref, g_ref, w_ref, pa_ref, p1_ref, p4_ref, p16_ref, pr_ref, pg_ref, h_ref, acc_ref, *,
                 n_rwkv):
    j = pl.program_id(1)
    tm = x_ref.shape[0]
    w = w_ref.shape[1]
    n_attn = 1 + len(C_GROUPS)

    @pl.when(j == 0)
    def _():
        h_ref[...] = _rms(x_ref[...], g_ref[...]).astype(BF16)

    def acc():
        return _dot(h_ref[...], w_ref[...])

    @pl.when(j == 0)
    def _():
        pa_ref[...] = acc().astype(BF16)

    @pl.when(j == 1)
    def _():
        p1_ref[...] = acc().astype(BF16)

    for jj, dil, ref in ((2, C_GROUPS[1][1], p4_ref), (3, C_GROUPS[2][1], p16_ref)):
        @pl.when(j == jj)
        def _(dil=dil, ref=ref):
            a = acc()
            for c in range(w // LANES):
                acc_ref[c] = a[:, c * LANES:(c + 1) * LANES]
            for r in range(dil):
                for c in range(w // LANES):
                    ref[:, r * w + c * LANES:r * w + (c + 1) * LANES] = (
                        acc_ref[c, pl.ds(r, tm // dil, stride=dil), :].astype(BF16))

    @pl.when(jnp.logical_and(j >= n_attn, j < n_attn + n_rwkv))
    def _():
        pr_ref[...] = acc().astype(BF16)

    @pl.when(j >= n_attn + n_rwkv)
    def _():
        pg_ref[...] = jax.nn.sigmoid(acc()).astype(BF16)


def _proj(x2, g, w, layer, n_rwkv, n_gate, tm):
    m, d = x2.shape
    n_attn = 1 + len(C_GROUPS)
    nblk = n_attn + n_rwkv + n_gate
    wq = w.shape[2] // nblk
    d4, d16 = C_GROUPS[1][1], C_GROUPS[2][1]
    last = m // tm - 1
    after = lambda k: (lambda i, j: (jnp.minimum(i + jnp.where(j > k, 1, 0), last), 0))
    return pl.pallas_call(
        functools.partial(_proj_kernel, n_rwkv=n_rwkv),
        grid=(m // tm, nblk),
        in_specs=[
            pl.BlockSpec((tm, d), lambda i, j: (i, 0)),
            pl.BlockSpec((1, d), lambda i, j: (0, 0)),
            pl.BlockSpec((None, d, wq), lambda i, j: (layer, 0, j)),
        ],
        out_specs=[
            pl.BlockSpec((tm, wq), after(0)),
            pl.BlockSpec((tm, wq), after(1)),
            pl.BlockSpec((tm // d4, d4 * wq), after(2)),
            pl.BlockSpec((tm // d16, d16 * wq), after(3)),
            pl.BlockSpec((tm, wq), lambda i, j: (i, jnp.clip(j - n_attn, 0, n_rwkv - 1))),
            pl.BlockSpec((tm, wq), lambda i, j: (i, jnp.clip(j - n_attn - n_rwkv, 0, n_gate - 1))),
        ],
        out_shape=[
            jax.ShapeDtypeStruct((m, wq), BF16),
            jax.ShapeDtypeStruct((m, wq), BF16),
            jax.ShapeDtypeStruct((m // d4, d4 * wq), BF16),
            jax.ShapeDtypeStruct((m // d16, d16 * wq), BF16),
            jax.ShapeDtypeStruct((m, n_rwkv * wq), BF16),
            jax.ShapeDtypeStruct((m, n_gate * wq), BF16),
        ],
        scratch_shapes=[pltpu.VMEM((tm, d), BF16), pltpu.VMEM((wq // LANES, tm, LANES), F32)],
        compiler_params=_cparams(("arbitrary", "arbitrary")),
        name="proj",
    )(x2, g.reshape(1, d), w)


def _band_attn_kernel(*refs, nq, nkv, has_sink, want_lse):
    it = iter(refs)
    q_ref, kp_ref, kc_ref, vp_ref, vc_ref, bias_ref = (next(it) for _ in range(6))
    sink_ref = next(it) if has_sink else None
    o_ref = next(it)
    lse_ref = next(it) if want_lse else None

    nqb = q_ref.shape[0] // BLK
    first = pl.program_id(2) == 0
    q = q_ref[...] * jnp.asarray(HEAD_DIM ** -0.5, BF16)
    k = jnp.concatenate([kp_ref[...], kc_ref[...]], axis=0)
    v = jnp.concatenate([vp_ref[...], vc_ref[...]], axis=0)
    col = lax.broadcasted_iota(jnp.int32, (BLK, 2 * BLK), 1)
    edge = jnp.where(col < BLK, jnp.where(first, NEG_BIG, 0.0), 0.0)
    rep = nq // nkv
    units = [(i, h) for i in range(nqb) for h in range(nq)]
    hd = lambda t, i: t[:, i * HEAD_DIM:(i + 1) * HEAD_DIM]
    qs = lambda i, h: hd(q[i * BLK:(i + 1) * BLK], h)
    win = lambda t, i, h: hd(t[i * BLK:(i + 2) * BLK], h // rep)
    s = [_dot_nt(qs(i, h), win(k, i, h)) + bias_ref[h] for i, h in units]
    s = [s[u] + edge if i == 0 else s[u] for u, (i, h) in enumerate(units)]
    m = [jnp.max(t, axis=-1, keepdims=True) for t in s]
    if has_sink:
        m = [jnp.maximum(m[u], sink_ref[h]) for u, (i, h) in enumerate(units)]
    p = [jnp.exp(s[u] - m[u]).astype(BF16) for u in range(len(units))]
    ones = jnp.ones((2 * BLK, LANES), BF16)
    l = [_dot(t, ones)[:, :HEAD_DIM] for t in p]
    denom = [l[u] + jnp.exp(sink_ref[h] - m[u]) for u, (i, h) in enumerate(units)] if has_sink else l
    o = [_dot(p[u], win(v, i, h)) / denom[u] for u, (i, h) in enumerate(units)]
    rows = lambda parts: jnp.concatenate(
        [jnp.concatenate(parts[i * nq:(i + 1) * nq], axis=-1) for i in range(nqb)], axis=0)
    o_ref[...] = rows(o).astype(o_ref.dtype)
    if want_lse:
        lse_ref[...] = rows([m[u] + jnp.log(l[u]) for u in range(len(units))])


def _band_attention(pv, bias, sink, *, dil, q_blk, k_blk, v_blk, q_w, kv_w, nq, nkv, want_lse, out_dtype, name,
                    nqb):
    b, lf, nd = pv.shape
    n = nd // dil
    nb = lf // (BLK * nqb)
    qpr, kpr = n // q_w, n // kv_w
    has_sink = sink is not None
    prev = lambda j: jnp.maximum(j * nqb - 1, 0)

    in_specs = [
        pl.BlockSpec((None, nqb * BLK, q_w), lambda bi, r, j: (bi, j, r * qpr + q_blk)),
        pl.BlockSpec((None, BLK, kv_w), lambda bi, r, j: (bi, prev(j), r * kpr + k_blk)),
        pl.BlockSpec((None, nqb * BLK, kv_w), lambda bi, r, j: (bi, j, r * kpr + k_blk)),
        pl.BlockSpec((None, BLK, kv_w), lambda bi, r, j: (bi, prev(j), r * kpr + v_blk)),
        pl.BlockSpec((None, nqb * BLK, kv_w), lambda bi, r, j: (bi, j, r * kpr + v_blk)),
        pl.BlockSpec((nq, BLK, 2 * BLK), lambda bi, r, j: (0, 0, 0)),
    ]
    args = [pv, pv, pv, pv, pv, bias]
    if has_sink:
        in_specs.append(pl.BlockSpec(memory_space=pltpu.SMEM))
        args.append(sink)
    ow = nq * HEAD_DIM
    out_spec = pl.BlockSpec((None, nqb * BLK, ow), lambda bi, r, j: (bi, j, r))
    out_shape = jax.ShapeDtypeStruct((b, lf, dil * ow), out_dtype)
    if want_lse:
        out_specs = [out_spec, out_spec]
        out_shapes = [out_shape, jax.ShapeDtypeStruct((b, lf, dil * ow), F32)]
    else:
        out_specs, out_shapes = out_spec, out_shape
    return pl.pallas_call(
        functools.partial(_band_attn_kernel, nq=nq, nkv=nkv, has_sink=has_sink, want_lse=want_lse),
        grid=(b, dil, nb),
        in_specs=in_specs,
        out_specs=out_specs,
        out_shape=out_shapes,
        compiler_params=_cparams(("parallel", "parallel", "arbitrary")),
        name=name,
    )(*args)


def _rwkv_prep_kernel(ph_ref, p_ref, mu_ref, w0_ref, wup_ref, a0_ref, aup_ref, gup_ref, kk_ref, ka_ref,
                      rk_ref, e_ref, r_o, lw_o, k_o, v_o, kk_o, b_o, g_o, bonus_o, *, tiles_per_seq):
    tm = p_ref.shape[0]
    p = p_ref[...].astype(F32)
    seq_start = pl.program_id(0) % tiles_per_seq == 0
    last = ph_ref[...].astype(F32)[HALO - 1:HALO, :]
    last = jnp.where(seq_start, 0.0, last)
    row = lax.broadcasted_iota(jnp.int32, (tm, 1), 0)
    prev = jnp.where(row == 0, last, pltpu.roll(p, 1, 0))
    pf = p + (prev - p) * mu_ref[...]

    r = pf[:, 0:B_WIDTH]
    k = pf[:, B_WIDTH:2 * B_WIDTH]
    v = pf[:, 2 * B_WIDTH:3 * B_WIDTH]
    wd = pf[:, OFF_WD:OFF_WD + LORA_PAD]
    ad = pf[:, OFF_AD:OFF_AD + LORA_PAD]
    gd = pf[:, OFF_GD:OFF_GD + LORA_GATE]

    z = w0_ref[...] + _dot3(jnp.tanh(wd), wup_ref[...])
    nz = -z
    softplus = jnp.maximum(nz, 0.0) + jnp.log(1.0 + jnp.exp(-jnp.abs(nz)))
    w = -softplus - 0.5
    lw_o[...] = -jnp.exp(w)
    a = jax.nn.sigmoid(a0_ref[...] + _dot3(ad, aup_ref[...]))
    g_o[...] = _dot3(jax.nn.sigmoid(gd), gup_ref[...])

    e = e_ref[...]
    kk = k * kk_ref[...]
    nrm = jnp.sqrt(_head_sums(kk * kk, e))
    kk = kk / jnp.maximum(nrm, 1e-12)
    k2 = k * (1.0 + (a - 1.0) * ka_ref[...])
    r_o[...] = r
    k_o[...] = k2
    v_o[...] = v
    kk_o[...] = kk
    b_o[...] = kk * a
    bonus_o[...] = _head_sums(r * k2 * rk_ref[...], e) * v


def _rwkv_prep(pb2, seq, mu, w0, wup, a0, aup, gup, k_k, k_a, r_k, e_bd, tm):
    m, n = pb2.shape
    tps = seq // tm
    row = lambda a: a.reshape(1, -1)
    full = lambda a: pl.BlockSpec(a.shape, lambda i: (0,) * a.ndim)
    args = [pb2, pb2, row(mu), row(w0), wup, row(a0), aup, gup, row(k_k), row(k_a), row(r_k), e_bd]
    in_specs = [
        pl.BlockSpec((HALO, n), lambda i: (jnp.maximum(i * (tm // HALO) - 1, 0), 0)),
        pl.BlockSpec((tm, n), lambda i: (i, 0)),
    ] + [full(a) for a in args[2:]]
    o_spec = pl.BlockSpec((tm, B_WIDTH), lambda i: (i, 0))
    o_shape = jax.ShapeDtypeStruct((m, B_WIDTH), F32)
    return pl.pallas_call(
        functools.partial(_rwkv_prep_kernel, tiles_per_seq=tps),
        grid=(m // tm,),
        in_specs=in_specs,
        out_specs=[o_spec] * 8,
        out_shape=[o_shape] * 8,
        compiler_params=_cparams(("parallel",)),
        name="rwkv_prep",
    )(*args)


def _wkv_kernel(r_ref, lw_ref, k_ref, v_ref, kk_ref, b_ref, y_ref, st_ref):
    c = WKV_CHUNK
    nb = r_ref.shape[0]
    n_chunks = r_ref.shape[1] // c

    @pl.when(pl.program_id(0) == 0)
    def _():
        st_ref[...] = jnp.zeros_like(st_ref)

    rowi = lax.broadcasted_iota(jnp.int32, (c, c), 0)
    coli = lax.broadcasted_iota(jnp.int32, (c, c), 1)
    incl = rowi >= coli
    strict = rowi > coli
    row2 = lax.broadcasted_iota(jnp.int32, (2 * c, 2 * c), 0)
    col2 = lax.broadcasted_iota(jnp.int32, (2 * c, 2 * c), 1)
    tri2 = row2 - jnp.where(row2 < c, 1, c) >= jnp.where(col2 >= c, col2 - c, 2 * c)
    tri = jnp.where(incl, 1.0, 0.0).astype(BF16)
    eye = jnp.where(rowi == coli, 1.0, 0.0)

    nu = WKV_UNROLL
    heads = range(B_HEADS)
    hsl = [slice(h * HEAD_DIM, (h + 1) * HEAD_DIM) for h in heads]
    segs = [(b, u) for b in range(nb) for u in range(nu)]
    ns = range(len(segs))
    units = [(s, h) for s in ns for h in heads]

    def chunks(ci, carry):
        sls = [pl.ds(pl.multiple_of((ci * nu + u) * c, c), c) for u in range(nu)]
        ld = lambda ref, s: ref[segs[s][0], sls[segs[s][1]], :]
        lw = [ld(lw_ref, s) for s in ns]
        cum = [_dot_exact_lhs(tri, t) for t in lw]
        tot = [t[c - 1:c, :] for t in cum]
        p_inv = [jnp.exp(-t) for t in cum]
        p_rest = [jnp.exp(tot[s] - cum[s]) for s in ns]
        p_tot = [jnp.exp(t) for t in tot]
        rh_all = [(ld(r_ref, s) * jnp.exp(cum[s])).astype(BF16) for s in ns]
        ah_all = [(-ld(kk_ref, s) * jnp.exp(cum[s] - lw[s])).astype(BF16) for s in ns]
        b_all = [ld(b_ref, s) for s in ns]
        k_all = [ld(k_ref, s) for s in ns]
        bh_all = [(b_all[s] * p_inv[s]).astype(BF16) for s in ns]
        kh_all = [(k_all[s] * p_inv[s]).astype(BF16) for s in ns]
        bt_all = [(b_all[s] * p_rest[s]).astype(BF16) for s in ns]
        kt_all = [(k_all[s] * p_rest[s]).astype(BF16) for s in ns]
        v_all = [ld(v_ref, s) for s in ns]

        ah = [ah_all[u][:, hsl[h]] for u, h in units]
        rh = [rh_all[u][:, hsl[h]] for u, h in units]
        vf = [v_all[u][:, hsl[h]] for u, h in units]
        vb = [t.astype(BF16) for t in vf]
        n = range(len(units))
        ar = [jnp.concatenate([ah[i], rh[i]], axis=0) for i in n]
        g = [_dot_nt(ar[i], jnp.concatenate([bh_all[u][:, hsl[h]], kh_all[u][:, hsl[h]]], axis=0))
             for i, (u, h) in enumerate(units)]
        gb = [t[:, :c] for t in g]
        a_rb = [jnp.where(incl, t[c:], 0.0).astype(BF16) for t in gb]
        akrk = [jnp.where(tri2, t, 0.0).astype(BF16) for t in g]
        x0 = [jnp.where(strict, t[:c], 0.0) for t in gb]
        t = [eye + xi for xi in x0]
        xb = [xi.astype(BF16) for xi in x0]
        x = [_dot(xi, xi) for xi in xb]
        for _ in range(int(math.log2(c)) - 2):
            xt = [_dot(jnp.concatenate([x[i], t[i]], axis=0).astype(BF16), x[i].astype(BF16)) for i in n]
            x = [p[:c] for p in xt]
            t = [t[i] + xt[i][c:] for i in n]
        t = [t[i] + _dot(t[i].astype(BF16), x[i].astype(BF16)) for i in n]
        tb = [ti.astype(BF16) for ti in t]
        zv = jnp.zeros((c, HEAD_DIM), BF16)
        w1y2 = [_dot(akrk[i], jnp.concatenate([zv, vb[i]], axis=0)) for i in n]
        w1 = [p[:c] for p in w1y2]
        y2 = [p[c:] for p in w1y2]
        n2 = [_dot(vf[i].T.astype(BF16), kt_all[u][:, hsl[h]]) for i, (u, h) in enumerate(units)]
        a2 = [_dot(tb[i], ah[i]).astype(BF16) for i in n]
        u1t = [_dot(tb[i], w1[i].astype(BF16)).T for i in n]
        chains = [(b, h) for b in range(nb) for h in heads]
        nc = range(len(chains))
        st = [st_ref[i] for i in nc]
        for u in range(nu):
            seg = [b * nu + u for b, h in chains]
            un = [seg[i] * B_HEADS + chains[i][1] for i in nc]
            hs = [hsl[h] for b, h in chains]
            s0b = [t.astype(BF16) for t in st]
            utb = [(_dot_nt(s0b[i], a2[un[i]]) + u1t[un[i]]).astype(BF16) for i in nc]
            ys = [_dot_nt(rh[un[i]], s0b[i]) + _dot_nt(a_rb[un[i]], utb[i]) + y2[un[i]] for i in nc]
            st = [st[i] * p_tot[seg[i]][:, hs[i]] + _dot(utb[i], bt_all[seg[i]][:, hs[i]]) + n2[un[i]] for i in nc]
            for b in range(nb):
                y_ref[b, sls[u], :] = jnp.concatenate(ys[b * B_HEADS:(b + 1) * B_HEADS], axis=-1)
        st_ref[...] = jnp.stack(st, axis=0)
        return carry

    lax.fori_loop(0, n_chunks // nu, chunks, 0)


def _wkv_scan(r, lw, k, v, kk, b, ts):
    bsz, s, w = r.shape
    spec = pl.BlockSpec((bsz, ts, w), lambda j: (0, j, 0))
    return pl.pallas_call(
        _wkv_kernel,
        grid=(s // ts,),
        in_specs=[spec] * 6,
        out_specs=spec,
        out_shape=jax.ShapeDtypeStruct((bsz, s, w), F32),
        scratch_shapes=[pltpu.VMEM((bsz * B_HEADS, HEAD_DIM, HEAD_DIM), F32)],
        compiler_params=_cparams(("arbitrary",)),
        name="wkv_scan",
    )(r, lw, k, v, kk, b)


def _merge_kernel(ya_ref, o1_ref, o2_ref, o3_ref, l1_ref, l2_ref, l3_ref, yr_ref, g_ref, bonus_ref,
                  gates_ref, lng_ref, lnb_ref, e_ref, pa_ref, pb_ref, pc_ref, out_ref, unf_ref):
    tm, d = out_ref.shape

    def unfold(ref, slot, dil):
        nc = C_OUT // LANES
        for r in range(dil):
            for c in range(nc):
                unf_ref[slot * nc + c, pl.ds(r, tm // dil, stride=dil), :] = (
                    ref[:, r * C_OUT + c * LANES:r * C_OUT + (c + 1) * LANES].astype(F32))
        return jnp.concatenate([unf_ref[slot * nc + c] for c in range(nc)], axis=-1)

    d2, d3 = C_GROUPS[1][1], C_GROUPS[2][1]
    o1, l1 = o1_ref[...].astype(F32), l1_ref[...]
    o2, l2 = unfold(o2_ref, 0, d2), unfold(l2_ref, 1, d2)
    o3, l3 = unfold(o3_ref, 2, d3), unfold(l3_ref, 3, d3)
    m = jnp.maximum(jnp.maximum(l1, l2), l3)
    e1, e2, e3 = jnp.exp(l1 - m), jnp.exp(l2 - m), jnp.exp(l3 - m)
    yc = (e1 * o1 + e2 * o2 + e3 * o3) / (e1 + e2 + e3)

    e = e_ref[...]
    y = yr_ref[...]
    mean = _head_sums(y, e, exact=True) * (1.0 / HEAD_DIM)
    dv = y - mean
    var = _head_sums(dv * dv, e) * (1.0 / HEAD_DIM)
    yb = dv * lax.rsqrt(var + B_GN_EPS) * lng_ref[...] + lnb_ref[...] + bonus_ref[...]
    yb = yb * g_ref[...]

    gates = gates_ref[...]
    merged = (gates[:, 0:d].astype(F32) * _dot(ya_ref[...], pa_ref[...])
              + gates[:, d:2 * d].astype(F32) * _dot(yb.astype(BF16), pb_ref[...])
              + gates[:, 2 * d:3 * d].astype(F32) * _dot(yc.astype(BF16), pc_ref[...]))
    out_ref[...] = merged.astype(BF16)


def _merge(ya, o1, o2, o3, l1, l2, l3, yr, g, bonus, gates, lng, lnb, e_bd, pa, pb, pc, layer, tm):
    m, d = ya.shape[0], pa.shape[2]
    tok = lambda a: pl.BlockSpec((tm * a.shape[0] // m, a.shape[1]), lambda i: (i, 0))
    full = lambda a: pl.BlockSpec(a.shape, lambda i: (0,) * a.ndim, pipeline_mode=pl.Buffered(1))
    at_layer = lambda a: pl.BlockSpec((None,) + a.shape[1:], lambda i: (layer, 0, 0), pipeline_mode=pl.Buffered(1))
    toks = [ya, o1, o2, o3, l1, l2, l3, yr, g, bonus, gates]
    consts = [lng.reshape(1, -1), lnb.reshape(1, -1), e_bd]
    stacked = [pa, pb, pc]
    return pl.pallas_call(
        _merge_kernel,
        grid=(m // tm,),
        in_specs=[tok(a) for a in toks] + [full(a) for a in consts] + [at_layer(a) for a in stacked],
        out_specs=pl.BlockSpec((tm, d), lambda i: (i, 0)),
        out_shape=jax.ShapeDtypeStruct((m, d), BF16),
        scratch_shapes=[pltpu.VMEM((4 * C_OUT // LANES, tm, LANES), F32)],
        compiler_params=_cparams(("parallel",)),
        name="merge",
    )(*toks, *consts, *stacked)


def _out_proj_kernel(x_ref, m_ref, w_ref, o_ref):
    o_ref[...] = x_ref[...] + _dot(m_ref[...], w_ref[...])


def _out_proj(x2, merged, wo, layer, tm):
    m, d = x2.shape
    tok = pl.BlockSpec((tm, d), lambda i: (i, 0))
    return pl.pallas_call(
        _out_proj_kernel,
        grid=(m // tm,),
        in_specs=[tok, tok,
                  pl.BlockSpec((None,) + wo.shape[1:], lambda i: (layer, 0, 0), pipeline_mode=pl.Buffered(1))],
        out_specs=tok,
        out_shape=jax.ShapeDtypeStruct((m, d), F32),
        compiler_params=_cparams(("parallel",)),
        name="out_proj",
    )(x2, merged, wo)


def _ffn_kernel(xh_ref, x_ref, g_ref, wg_ref, wv_ref, cw_ref, wd_ref, fg_ref, o_ref, h_ref, *,
                tiles_per_seq, final_norm, d_ff):
    j = pl.program_id(1)
    tm = x_ref.shape[0]
    tf = wg_ref.shape[1]

    @pl.when(j == 0)
    def _():
        x = x_ref[...]
        g = g_ref[...]
        h_ref[HALO:, :] = _rms(x, g).astype(BF16)
        seq_start = pl.program_id(0) % tiles_per_seq == 0
        halo = _rms(xh_ref[...], g)
        h_ref[0:HALO, :] = jnp.where(seq_start, 0.0, halo).astype(BF16)
        o_ref[...] = x

    h = h_ref[...]
    gext = _dot(h, wg_ref[...])
    val = _dot(h[HALO:], wv_ref[...])
    cw = cw_ref[...]
    gate = (cw[0:1, :] * pltpu.roll(gext, 2, 0)[HALO:]
            + cw[1:2, :] * pltpu.roll(gext, 1, 0)[HALO:]
            + cw[2:3, :] * gext[HALO:])
    col = j * tf + lax.broadcasted_iota(jnp.int32, (1, tf), 1)
    row = j * tf + lax.broadcasted_iota(jnp.int32, (tf, 1), 0)
    hid = jnp.where(col < d_ff, gate * jax.nn.sigmoid(gate) * val, 0.0)
    wd = jnp.where(row < d_ff, wd_ref[...], jnp.zeros((), BF16))
    o_ref[...] += _dot(hid.astype(BF16), wd)

    if final_norm:
        @pl.when(j == pl.num_programs(1) - 1)
        def _():
            o_ref[...] = _rms(o_ref[...], fg_ref[...])


def _ffn(x2, seq, g, wg, wv, cw, wd, layer, fg, final_norm, tm, tf):
    m, d = x2.shape
    ffp = wg.shape[2]
    d_ff = wv.shape[2]
    tps = seq // tm
    return pl.pallas_call(
        functools.partial(_ffn_kernel, tiles_per_seq=tps, final_norm=final_norm, d_ff=d_ff),
        grid=(m // tm, ffp // tf),
        in_specs=[
            pl.BlockSpec((HALO, d), lambda i, j: (jnp.maximum(i * (tm // HALO) - 1, 0), 0)),
            pl.BlockSpec((tm, d), lambda i, j: (i, 0)),
            pl.BlockSpec((1, d), lambda i, j: (0, 0)),
            pl.BlockSpec((None, d, tf), lambda i, j: (layer, 0, j)),
            pl.BlockSpec((None, d, tf), lambda i, j: (layer, 0, j)),
            pl.BlockSpec((None, 3, tf), lambda i, j: (layer, 0, j)),
            pl.BlockSpec((None, tf, d), lambda i, j: (layer, j, 0)),
            pl.BlockSpec((1, d), lambda i, j: (0, 0)),
        ],
        out_specs=pl.BlockSpec((tm, d), lambda i, j: (i, 0)),
        out_shape=jax.ShapeDtypeStruct((m, d), F32),
        scratch_shapes=[pltpu.VMEM((HALO + tm, d), BF16)],
        compiler_params=_cparams(("parallel", "arbitrary")),
        name="conv_ffn",
    )(x2, x2, g.reshape(1, d), wg, wv, cw, wd, fg.reshape(1, d))


def _t5_bucket(dist):
    small = dist < MAX_EXACT
    nf = jnp.maximum(dist, 1).astype(F32)
    large = MAX_EXACT + (jnp.log(nf / MAX_EXACT) / math.log(REL_MAX_DIST / MAX_EXACT)
                         * (N_BUCKETS - MAX_EXACT)).astype(jnp.int32)
    return jnp.where(small, dist, jnp.minimum(large, N_BUCKETS - 1))


def _band_bias(table, dilation, max_steps):
    i = jnp.arange(BLK)[:, None]
    j = jnp.arange(2 * BLK)[None, :]
    off = i + BLK - j
    onehot = jax.nn.one_hot(_t5_bucket(jnp.maximum(off, 0) * dilation), N_BUCKETS, dtype=F32)
    bias = jnp.einsum("ijb,bh->hij", onehot, table.astype(F32), precision=lax.Precision.HIGHEST)
    valid = (off >= 0) & (off <= max_steps)
    return jnp.where(valid[None], bias, NEG_BIG)


def _pad_cols(w, n):
    return jnp.pad(w, [(0, 0)] * (w.ndim - 1) + [(0, n - w.shape[-1])])


def _pad_rows(w, n):
    return jnp.pad(w, [(0, 0)] * (w.ndim - 2) + [(0, n - w.shape[-2]), (0, 0)])


def kernel(x, rel_bias, norm1_g, w_in, attn_sinks, rwkv_mu, rwkv_w0, rwkv_w_up, rwkv_a0, rwkv_a_up, rwkv_g_up,
           rwkv_k_k, rwkv_k_a, rwkv_r_k, rwkv_lnx_g, rwkv_lnx_b, proj_a, proj_b, proj_c, w_out, norm2_g,
           ffn_up, ffn_conv, ffn_down, final_g):
    bsz, seq, d = x.shape
    depth = w_in.shape[0]
    m = bsz * seq
    d_ff = ffn_conv.shape[-1]
    ffp = -(-d_ff // D_FF_PAD_TO) * D_FF_PAD_TO

    tm_proj = min(1024, seq)
    tm_tok = min(512, seq)
    tm_ffn = min(1024, seq)
    ts_wkv = min(512, seq)

    na = A_HQ
    bias_a = _band_bias(rel_bias[:, :na], 1, BLK - 1)
    bias_c = [_band_bias(rel_bias[:, na + gi * C_HG:na + (gi + 1) * C_HG], dil, win // dil)
              for gi, (win, dil) in enumerate(C_GROUPS)]

    q_end = (A_HQ + 2 * A_HKV) * HEAD_DIM
    b0 = q_end
    b_r_end = b0 + 3 * B_WIDTH
    b_wd_end = b_r_end + LORA_DECAY
    b_ad_end = b_wd_end + LORA_ICLR
    b_end = b_ad_end + LORA_GATE
    c_end = b_end + 3 * C_WIDTH

    def rwkv_cols(t):
        parts = [t[..., b0:b_r_end], _pad_cols(t[..., b_r_end:b_wd_end], LORA_PAD),
                 _pad_cols(t[..., b_wd_end:b_ad_end], LORA_PAD), t[..., b_ad_end:b_end]]
        return _pad_cols(jnp.concatenate(parts, axis=-1), N_RWKV)

    w_in_b = lax.optimization_barrier(w_in.astype(BF16))
    tail = lax.optimization_barrier(w_in_b[..., b_ad_end:])
    t_c, t_g = b_end - b_ad_end, c_end - b_ad_end
    qkv_c = [tail[..., t_c + t * C_WIDTH + gi * C_OUT:t_c + t * C_WIDTH + (gi + 1) * C_OUT]
             for gi in range(len(C_GROUPS)) for t in range(3)]
    rwkv_w = _pad_cols(jnp.concatenate(
        [w_in_b[..., b0:b_r_end], _pad_cols(w_in_b[..., b_r_end:b_wd_end], LORA_PAD),
         _pad_cols(w_in_b[..., b_wd_end:b_ad_end], LORA_PAD), tail[..., :t_c]], axis=-1), N_RWKV)
    w_all = jnp.concatenate([w_in_b[..., :q_end]] + qkv_c + [rwkv_w, tail[..., t_g:]], axis=-1)
    proj_blk = q_end
    n_rwkv_blk = N_RWKV // proj_blk
    n_gate_blk = (w_in.shape[-1] - c_end) // proj_blk
    mu_pad = rwkv_cols(jnp.pad(rwkv_mu, ((0, 0), (b0, 0))))
    wup_pad = _pad_rows(rwkv_w_up, LORA_PAD)
    aup_pad = _pad_rows(rwkv_a_up, LORA_PAD)
    r_k = rwkv_r_k.reshape(depth, B_WIDTH)

    head_id = jnp.arange(B_WIDTH) // HEAD_DIM
    e_bd = (head_id[:, None] == head_id[None, :]).astype(BF16)

    pa, pb, pc, wo = (t.astype(BF16) for t in (proj_a, proj_b, proj_c, w_out))
    wg = _pad_cols(ffn_up[..., :d_ff], ffp).astype(BF16)
    wv = ffn_up[..., d_ff:].astype(BF16)
    cw = _pad_cols(ffn_conv, ffp)
    wd = ffn_down.astype(BF16)

    x2 = x.reshape(m, d)
    for l in range(depth):
        p_swa, *p_dil, p_rwkv, gates = _proj(x2, norm1_g[l], w_all, l, n_rwkv_blk, n_gate_blk, tm_proj)

        kvw = A_HKV * HEAD_DIM
        y_a = _band_attention(p_swa.reshape(bsz, seq, -1), bias_a, attn_sinks[l], dil=1, q_blk=0,
                              k_blk=(A_HQ * HEAD_DIM) // kvw, v_blk=(A_HQ * HEAD_DIM) // kvw + 1,
                              q_w=A_HQ * HEAD_DIM, kv_w=kvw, nq=A_HQ, nkv=A_HKV, want_lse=False,
                              out_dtype=BF16, name="attn_swa", nqb=4)
        oc, lc = [], []
        for gi, (win, dil) in enumerate(C_GROUPS):
            pv = p_dil[gi].reshape(bsz, seq // dil, -1)
            o, ls = _band_attention(pv, bias_c[gi], None, dil=dil, q_blk=0, k_blk=1, v_blk=2, q_w=C_OUT,
                                    kv_w=C_OUT, nq=C_HG, nkv=C_HG, want_lse=True, out_dtype=BF16,
                                    name=f"attn_dil{dil}", nqb=min(4, seq // dil // BLK))
            oc.append(o.reshape(m // dil, dil * C_OUT))
            lc.append(ls.reshape(m // dil, dil * C_OUT))

        r, lw, k2, v, kk, bb, g, bonus = _rwkv_prep(
            p_rwkv, seq, mu_pad[l], rwkv_w0[l], wup_pad[l], rwkv_a0[l], aup_pad[l], rwkv_g_up[l],
            rwkv_k_k[l], rwkv_k_a[l], r_k[l], e_bd, tm_tok)
        sh = lambda t: t.reshape(bsz, seq, B_WIDTH)
        y_raw = _wkv_scan(sh(r), sh(lw), sh(k2), sh(v), sh(kk), sh(bb), ts_wkv).reshape(m, B_WIDTH)

        merged = _merge(y_a.reshape(m, A_HQ * HEAD_DIM), oc[0], oc[1], oc[2], lc[0], lc[1], lc[2], y_raw, g,
                        bonus, gates, rwkv_lnx_g[l], rwkv_lnx_b[l], e_bd, pa, pb, pc, l, tm_tok)
        x2 = _out_proj(x2, merged, wo, l, tm_proj)
        x2 = _ffn(x2, seq, norm2_g[l], wg, wv, cw, wd, l, final_g, l == depth - 1, tm_ffn, 512)
    return x2.reshape(bsz, seq, d)
```

```python
import functools
import math

import jax
import jax.numpy as jnp
from jax import lax
from jax.experimental import pallas as pl
from jax.experimental.pallas import tpu as pltpu

F32 = jnp.float32
BF16 = jnp.bfloat16

HEAD_DIM = 64
LANES = 128
BLK = 128
NORM_EPS = 1e-5
A_HQ, A_HKV = 8, 2
B_HEADS = 12
B_WIDTH = B_HEADS * HEAD_DIM
LORA_DECAY, LORA_ICLR, LORA_GATE = 96, 96, 256
LORA_PAD = 128
B_GN_EPS = 64e-5
C_GROUPS = ((128, 1), (512, 4), (2048, 16))
C_HG = 4
C_WIDTH = C_HG * len(C_GROUPS) * HEAD_DIM
C_OUT = C_HG * HEAD_DIM
N_BUCKETS, MAX_EXACT, REL_MAX_DIST = 32, 16, 2048
D_FF_PAD_TO = 512
NEG_BIG = -1e30

N_RWKV = 3 * B_WIDTH + 2 * LORA_PAD + LORA_GATE + 256
OFF_WD = 3 * B_WIDTH
OFF_AD = OFF_WD + LORA_PAD
OFF_GD = OFF_AD + LORA_PAD
WKV_CHUNK = 64
WKV_UNROLL = 2
HALO = 16

VMEM_LIMIT = 56 * 1024 * 1024


def _cparams(sem):
    return pltpu.CompilerParams(dimension_semantics=sem, vmem_limit_bytes=VMEM_LIMIT)


def _dot(a, b):
    return jnp.dot(a, b, preferred_element_type=F32)


def _dot_nt(a, b):
    return lax.dot_general(a, b, (((1,), (1,)), ((), ())), preferred_element_type=F32)


def _split2(x):
    hi = x.astype(BF16)
    lo = (x - hi.astype(F32)).astype(BF16)
    return hi, lo


def _head_sums(a, e_bf16, exact=False):
    if not exact:
        return _dot(a.astype(BF16), e_bf16)
    h, l = _split2(a)
    return _dot(h, e_bf16) + _dot(l, e_bf16)


def _dot_exact_lhs(a_bf16, b):
    h, l = _split2(b)
    return _dot(a_bf16, h) + _dot(a_bf16, l)


def _dot3(a, b):
    ah, al = _split2(a)
    bh, bl = _split2(b)
    return _dot(ah, bh) + _dot(ah, bl) + _dot(al, bh)


def _rms(x, g):
    ms = jnp.mean(x * x, axis=-1, keepdims=True)
    return x * lax.rsqrt(ms + NORM_EPS) * g


def _proj_kernel(x_ref, g_ref, w_ref, pa_ref, p1_ref, p4_ref, p16_ref, pr_ref, pg_ref, h_ref, acc_ref, *,
                 n_rwkv):
    j = pl.program_id(1)
    tm = x_ref.shape[0]
    w = w_ref.shape[1]
    n_attn = 1 + len(C_GROUPS)

    @pl.when(j == 0)
    def _():
        h_ref[...] = _rms(x_ref[...], g_ref[...]).astype(BF16)

    def acc():
        return _dot(h_ref[...], w_ref[...])

    @pl.when(j == 0)
    def _():
        pa_ref[...] = acc().astype(BF16)

    @pl.when(j == 1)
    def _():
        p1_ref[...] = acc().astype(BF16)

    for jj, dil, ref in ((2, C_GROUPS[1][1], p4_ref), (3, C_GROUPS[2][1], p16_ref)):
        @pl.when(j == jj)
        def _(dil=dil, ref=ref):
            a = acc()
            for c in range(w // LANES):
                acc_ref[c] = a[:, c * LANES:(c + 1) * LANES]
            for r in range(dil):
                for c in range(w // LANES):
                    ref[:, r * w + c * LANES:r * w + (c + 1) * LANES] = (
                        acc_ref[c, pl.ds(r, tm // dil, stride=dil), :].astype(BF16))

    @pl.when(jnp.logical_and(j >= n_attn, j < n_attn + n_rwkv))
    def _():
        pr_ref[...] = acc().astype(BF16)

    @pl.when(j >= n_attn + n_rwkv)
    def _():
        pg_ref[...] = jax.nn.sigmoid(acc()).astype(BF16)


def _proj(x2, g, w, layer, n_rwkv, n_gate, tm):
    m, d = x2.shape
    n_attn = 1 + len(C_GROUPS)
    nblk = n_attn + n_rwkv + n_gate
    wq = w.shape[2] // nblk
    d4, d16 = C_GROUPS[1][1], C_GROUPS[2][1]
    return pl.pallas_call(
        functools.partial(_proj_kernel, n_rwkv=n_rwkv),
        grid=(m // tm, nblk),
        in_specs=[
            pl.BlockSpec((tm, d), lambda i, j: (i, 0)),
            pl.BlockSpec((1, d), lambda i, j: (0, 0)),
            pl.BlockSpec((None, d, wq), lambda i, j: (layer, 0, j)),
        ],
        out_specs=[
            pl.BlockSpec((tm, wq), lambda i, j: (i, 0)),
            pl.BlockSpec((tm, wq), lambda i, j: (i, 0)),
            pl.BlockSpec((tm // d4, d4 * wq), lambda i, j: (i, 0)),
            pl.BlockSpec((tm // d16, d16 * wq), lambda i, j: (i, 0)),
            pl.BlockSpec((tm, wq), lambda i, j: (i, jnp.clip(j - n_attn, 0, n_rwkv - 1))),
            pl.BlockSpec((tm, wq), lambda i, j: (i, jnp.clip(j - n_attn - n_rwkv, 0, n_gate - 1))),
        ],
        out_shape=[
            jax.ShapeDtypeStruct((m, wq), BF16),
            jax.ShapeDtypeStruct((m, wq), BF16),
            jax.ShapeDtypeStruct((m // d4, d4 * wq), BF16),
            jax.ShapeDtypeStruct((m // d16, d16 * wq), BF16),
            jax.ShapeDtypeStruct((m, n_rwkv * wq), BF16),
            jax.ShapeDtypeStruct((m, n_gate * wq), BF16),
        ],
        scratch_shapes=[pltpu.VMEM((tm, d), BF16), pltpu.VMEM((wq // LANES, tm, LANES), F32)],
        compiler_params=_cparams(("parallel", "arbitrary")),
        name="proj",
    )(x2, g.reshape(1, d), w)


def _band_attn_kernel(*refs, nq, nkv, has_sink, want_lse):
    it = iter(refs)
    q_ref, kp_ref, kc_ref, vp_ref, vc_ref, bias_ref = (next(it) for _ in range(6))
    sink_ref = next(it) if has_sink else None
    o_ref = next(it)
    lse_ref = next(it) if want_lse else None

    nqb = q_ref.shape[0] // BLK
    first = pl.program_id(2) == 0
    q = q_ref[...] * jnp.asarray(HEAD_DIM ** -0.5, BF16)
    k = jnp.concatenate([kp_ref[...], kc_ref[...]], axis=0)
    v = jnp.concatenate([vp_ref[...], vc_ref[...]], axis=0)
    col = lax.broadcasted_iota(jnp.int32, (BLK, 2 * BLK), 1)
    edge = jnp.where(col < BLK, jnp.where(first, NEG_BIG, 0.0), 0.0)
    rep = nq // nkv
    units = [(i, h) for i in range(nqb) for h in range(nq)]
    hd = lambda t, i: t[:, i * HEAD_DIM:(i + 1) * HEAD_DIM]
    qs = lambda i, h: hd(q[i * BLK:(i + 1) * BLK], h)
    win = lambda t, i, h: hd(t[i * BLK:(i + 2) * BLK], h // rep)
    s = [_dot_nt(qs(i, h), win(k, i, h)) + bias_ref[h] for i, h in units]
    s = [s[u] + edge if i == 0 else s[u] for u, (i, h) in enumerate(units)]
    m = [jnp.max(t, axis=-1, keepdims=True) for t in s]
    if has_sink:
        m = [jnp.maximum(m[u], sink_ref[h]) for u, (i, h) in enumerate(units)]
    p = [jnp.exp(s[u] - m[u]).astype(BF16) for u in range(len(units))]
    ones = jnp.ones((2 * BLK, LANES), BF16)
    l = [_dot(t, ones)[:, :HEAD_DIM] for t in p]
    denom = [l[u] + jnp.exp(sink_ref[h] - m[u]) for u, (i, h) in enumerate(units)] if has_sink else l
    o = [_dot(p[u], win(v, i, h)) / denom[u] for u, (i, h) in enumerate(units)]
    rows = lambda parts: jnp.concatenate(
        [jnp.concatenate(parts[i * nq:(i + 1) * nq], axis=-1) for i in range(nqb)], axis=0)
    o_ref[...] = rows(o).astype(o_ref.dtype)
    if want_lse:
        lse_ref[...] = rows([m[u] + jnp.log(l[u]) for u in range(len(units))])


def _band_attention(pv, bias, sink, *, dil, q_blk, k_blk, v_blk, q_w, kv_w, nq, nkv, want_lse, out_dtype, name,
                    nqb):
    b, lf, nd = pv.shape
    n = nd // dil
    nb = lf // (BLK * nqb)
    qpr, kpr = n // q_w, n // kv_w
    has_sink = sink is not None
    prev = lambda j: jnp.maximum(j * nqb - 1, 0)

    in_specs = [
        pl.BlockSpec((None, nqb * BLK, q_w), lambda bi, r, j: (bi, j, r * qpr + q_blk)),
        pl.BlockSpec((None, BLK, kv_w), lambda bi, r, j: (bi, prev(j), r * kpr + k_blk)),
        pl.BlockSpec((None, nqb * BLK, kv_w), lambda bi, r, j: (bi, j, r * kpr + k_blk)),
        pl.BlockSpec((None, BLK, kv_w), lambda bi, r, j: (bi, prev(j), r * kpr + v_blk)),
        pl.BlockSpec((None, nqb * BLK, kv_w), lambda bi, r, j: (bi, j, r * kpr + v_blk)),
        pl.BlockSpec((nq, BLK, 2 * BLK), lambda bi, r, j: (0, 0, 0)),
    ]
    args = [pv, pv, pv, pv, pv, bias]
    if has_sink:
        in_specs.append(pl.BlockSpec(memory_space=pltpu.SMEM))
        args.append(sink)
    ow = nq * HEAD_DIM
    out_spec = pl.BlockSpec((None, nqb * BLK, ow), lambda bi, r, j: (bi, j, r))
    out_shape = jax.ShapeDtypeStruct((b, lf, dil * ow), out_dtype)
    if want_lse:
        out_specs = [out_spec, out_spec]
        out_shapes = [out_shape, jax.ShapeDtypeStruct((b, lf, dil * ow), F32)]
    else:
        out_specs, out_shapes = out_spec, out_shape
    return pl.pallas_call(
        functools.partial(_band_attn_kernel, nq=nq, nkv=nkv, has_sink=has_sink, want_lse=want_lse),
        grid=(b, dil, nb),
        in_specs=in_specs,
        out_specs=out_specs,
        out_shape=out_shapes,
        compiler_params=_cparams(("parallel", "parallel", "arbitrary")),
        name=name,
    )(*args)


def _rwkv_prep_kernel(ph_ref, p_ref, mu_ref, w0_ref, wup_ref, a0_ref, aup_ref, gup_ref, kk_ref, ka_ref,
                      rk_ref, e_ref, r_o, lw_o, k_o, v_o, kk_o, b_o, g_o, bonus_o, *, tiles_per_seq):
    tm = p_ref.shape[0]
    p = p_ref[...].astype(F32)
    seq_start = pl.program_id(0) % tiles_per_seq == 0
    last = ph_ref[...].astype(F32)[HALO - 1:HALO, :]
    last = jnp.where(seq_start, 0.0, last)
    row = lax.broadcasted_iota(jnp.int32, (tm, 1), 0)
    prev = jnp.where(row == 0, last, pltpu.roll(p, 1, 0))
    pf = p + (prev - p) * mu_ref[...]

    r = pf[:, 0:B_WIDTH]
    k = pf[:, B_WIDTH:2 * B_WIDTH]
    v = pf[:, 2 * B_WIDTH:3 * B_WIDTH]
    wd = pf[:, OFF_WD:OFF_WD + LORA_PAD]
    ad = pf[:, OFF_AD:OFF_AD + LORA_PAD]
    gd = pf[:, OFF_GD:OFF_GD + LORA_GATE]

    z = w0_ref[...] + _dot3(jnp.tanh(wd), wup_ref[...])
    nz = -z
    softplus = jnp.maximum(nz, 0.0) + jnp.log(1.0 + jnp.exp(-jnp.abs(nz)))
    w = -softplus - 0.5
    lw_o[...] = -jnp.exp(w)
    a = jax.nn.sigmoid(a0_ref[...] + _dot3(ad, aup_ref[...]))
    g_o[...] = _dot3(jax.nn.sigmoid(gd), gup_ref[...]).astype(g_o.dtype)

    e = e_ref[...]
    kk = k * kk_ref[...]
    nrm = jnp.sqrt(_head_sums(kk * kk, e))
    kk = kk / jnp.maximum(nrm, 1e-12)
    k2 = k * (1.0 + (a - 1.0) * ka_ref[...])
    r_o[...] = r
    k_o[...] = k2
    v_o[...] = v
    kk_o[...] = kk
    b_o[...] = kk * a
    bonus_o[...] = (_head_sums(r * k2 * rk_ref[...], e) * v).astype(bonus_o.dtype)


def _rwkv_prep(pb2, seq, mu, w0, wup, a0, aup, gup, k_k, k_a, r_k, e_bd, tm):
    m, n = pb2.shape
    tps = seq // tm
    row = lambda a: a.reshape(1, -1)
    full = lambda a: pl.BlockSpec(a.shape, lambda i: (0,) * a.ndim)
    args = [pb2, pb2, row(mu), row(w0), wup, row(a0), aup, gup, row(k_k), row(k_a), row(r_k), e_bd]
    in_specs = [
        pl.BlockSpec((HALO, n), lambda i: (jnp.maximum(i * (tm // HALO) - 1, 0), 0)),
        pl.BlockSpec((tm, n), lambda i: (i, 0)),
    ] + [full(a) for a in args[2:]]
    o_spec = pl.BlockSpec((tm, B_WIDTH), lambda i: (i, 0))
    o_shape = jax.ShapeDtypeStruct((m, B_WIDTH), F32)
    return pl.pallas_call(
        functools.partial(_rwkv_prep_kernel, tiles_per_seq=tps),
        grid=(m // tm,),
        in_specs=in_specs,
        out_specs=[o_spec] * 8,
        out_shape=[o_shape] * 6 + [jax.ShapeDtypeStruct((m, B_WIDTH), BF16)] * 2,
        compiler_params=_cparams(("parallel",)),
        name="rwkv_prep",
    )(*args)


def _wkv_kernel(r_ref, lw_ref, k_ref, v_ref, kk_ref, b_ref, y_ref, st_ref):
    c = WKV_CHUNK
    nb = r_ref.shape[0]
    n_chunks = r_ref.shape[1] // c

    @pl.when(pl.program_id(0) == 0)
    def _():
        st_ref[...] = jnp.zeros_like(st_ref)

    rowi = lax.broadcasted_iota(jnp.int32, (c, c), 0)
    coli = lax.broadcasted_iota(jnp.int32, (c, c), 1)
    incl = rowi >= coli
    strict = rowi > coli
    row2 = lax.broadcasted_iota(jnp.int32, (2 * c, 2 * c), 0)
    col2 = lax.broadcasted_iota(jnp.int32, (2 * c, 2 * c), 1)
    tri2 = row2 - jnp.where(row2 < c, 1, c) >= jnp.where(col2 >= c, col2 - c, 2 * c)
    tri = jnp.where(incl, 1.0, 0.0).astype(BF16)
    eye = jnp.where(rowi == coli, 1.0, 0.0)

    nu = WKV_UNROLL
    heads = range(B_HEADS)
    hsl = [slice(h * HEAD_DIM, (h + 1) * HEAD_DIM) for h in heads]
    segs = [(b, u) for b in range(nb) for u in range(nu)]
    ns = range(len(segs))
    units = [(s, h) for s in ns for h in heads]

    def chunks(ci, carry):
        sls = [pl.ds(pl.multiple_of((ci * nu + u) * c, c), c) for u in range(nu)]
        ld = lambda ref, s: ref[segs[s][0], sls[segs[s][1]], :]
        lw = [ld(lw_ref, s) for s in ns]
        cum = [_dot_exact_lhs(tri, t) for t in lw]
        tot = [t[c - 1:c, :] for t in cum]
        p_inv = [jnp.exp(-t) for t in cum]
        p_rest = [jnp.exp(tot[s] - cum[s]) for s in ns]
        p_tot = [jnp.exp(t) for t in tot]
        rh_all = [(ld(r_ref, s) * jnp.exp(cum[s])).astype(BF16) for s in ns]
        ah_all = [(-ld(kk_ref, s) * jnp.exp(cum[s] - lw[s])).astype(BF16) for s in ns]
        b_all = [ld(b_ref, s) for s in ns]
        k_all = [ld(k_ref, s) for s in ns]
        bh_all = [(b_all[s] * p_inv[s]).astype(BF16) for s in ns]
        kh_all = [(k_all[s] * p_inv[s]).astype(BF16) for s in ns]
        bt_all = [(b_all[s] * p_rest[s]).astype(BF16) for s in ns]
        kt_all = [(k_all[s] * p_rest[s]).astype(BF16) for s in ns]
        v_all = [ld(v_ref, s) for s in ns]

        ah = [ah_all[u][:, hsl[h]] for u, h in units]
        rh = [rh_all[u][:, hsl[h]] for u, h in units]
        vf = [v_all[u][:, hsl[h]] for u, h in units]
        vb = [t.astype(BF16) for t in vf]
        n = range(len(units))
        ar = [jnp.concatenate([ah[i], rh[i]], axis=0) for i in n]
        g = [_dot_nt(ar[i], jnp.concatenate([bh_all[u][:, hsl[h]], kh_all[u][:, hsl[h]]], axis=0))
             for i, (u, h) in enumerate(units)]
        gb = [t[:, :c] for t in g]
        a_rb = [jnp.where(incl, t[c:], 0.0).astype(BF16) for t in gb]
        akrk = [jnp.where(tri2, t, 0.0).astype(BF16) for t in g]
        x0 = [jnp.where(strict, t[:c], 0.0) for t in gb]
        t = [eye + xi for xi in x0]
        xb = [xi.astype(BF16) for xi in x0]
        x = [_dot(xi, xi) for xi in xb]
        for _ in range(int(math.log2(c)) - 2):
            xt = [_dot(jnp.concatenate([x[i], t[i]], axis=0).astype(BF16), x[i].astype(BF16)) for i in n]
            x = [p[:c] for p in xt]
            t = [t[i] + xt[i][c:] for i in n]
        t = [t[i] + _dot(t[i].astype(BF16), x[i].astype(BF16)) for i in n]
        tb = [ti.astype(BF16) for ti in t]
        zv = jnp.zeros((c, HEAD_DIM), BF16)
        w1y2 = [_dot(akrk[i], jnp.concatenate([zv, vb[i]], axis=0)) for i in n]
        w1 = [p[:c] for p in w1y2]
        y2 = [p[c:] for p in w1y2]
        n2 = [_dot(vf[i].T.astype(BF16), kt_all[u][:, hsl[h]]) for i, (u, h) in enumerate(units)]
        a2 = [_dot(tb[i], ah[i]).astype(BF16) for i in n]
        u1t = [_dot(tb[i], w1[i].astype(BF16)).T for i in n]
        chains = [(b, h) for b in range(nb) for h in heads]
        nc = range(len(chains))
        st = [st_ref[i] for i in nc]
        for u in range(nu):
            seg = [b * nu + u for b, h in chains]
            un = [seg[i] * B_HEADS + chains[i][1] for i in nc]
            hs = [hsl[h] for b, h in chains]
            s0b = [t.astype(BF16) for t in st]
            utb = [(_dot_nt(s0b[i], a2[un[i]]) + u1t[un[i]]).astype(BF16) for i in nc]
            ys = [_dot_nt(rh[un[i]], s0b[i]) + _dot_nt(a_rb[un[i]], utb[i]) + y2[un[i]] for i in nc]
            st = [st[i] * p_tot[seg[i]][:, hs[i]] + _dot(utb[i], bt_all[seg[i]][:, hs[i]]) + n2[un[i]] for i in nc]
            for b in range(nb):
                y_ref[b, sls[u], :] = jnp.concatenate(ys[b * B_HEADS:(b + 1) * B_HEADS], axis=-1)
        st_ref[...] = jnp.stack(st, axis=0)
        return carry

    lax.fori_loop(0, n_chunks // nu, chunks, 0)


def _wkv_scan(r, lw, k, v, kk, b, ts):
    bsz, s, w = r.shape
    spec = pl.BlockSpec((bsz, ts, w), lambda j: (0, j, 0))
    return pl.pallas_call(
        _wkv_kernel,
        grid=(s // ts,),
        in_specs=[spec] * 6,
        out_specs=spec,
        out_shape=jax.ShapeDtypeStruct((bsz, s, w), F32),
        scratch_shapes=[pltpu.VMEM((bsz * B_HEADS, HEAD_DIM, HEAD_DIM), F32)],
        compiler_params=_cparams(("arbitrary",)),
        name="wkv_scan",
    )(r, lw, k, v, kk, b)


def _merge_kernel(ya_ref, o1_ref, o2_ref, o3_ref, l1_ref, l2_ref, l3_ref, yr_ref, g_ref, bonus_ref,
                  gates_ref, lng_ref, lnb_ref, e_ref, pa_ref, pb_ref, pc_ref, out_ref, unf_ref):
    tm, d = out_ref.shape

    def unfold(ref, slot, dil):
        nc = C_OUT // LANES
        for r in range(dil):
            for c in range(nc):
                unf_ref[slot * nc + c, pl.ds(r, tm // dil, stride=dil), :] = (
                    ref[:, r * C_OUT + c * LANES:r * C_OUT + (c + 1) * LANES].astype(F32))
        return jnp.concatenate([unf_ref[slot * nc + c] for c in range(nc)], axis=-1)

    d2, d3 = C_GROUPS[1][1], C_GROUPS[2][1]
    o1, l1 = o1_ref[...].astype(F32), l1_ref[...]
    o2, l2 = unfold(o2_ref, 0, d2), unfold(l2_ref, 1, d2)
    o3, l3 = unfold(o3_ref, 2, d3), unfold(l3_ref, 3, d3)
    m = jnp.maximum(jnp.maximum(l1, l2), l3)
    e1, e2, e3 = jnp.exp(l1 - m), jnp.exp(l2 - m), jnp.exp(l3 - m)
    yc = (e1 * o1 + e2 * o2 + e3 * o3) / (e1 + e2 + e3)

    e = e_ref[...]
    y = yr_ref[...]
    mean = _head_sums(y, e, exact=True) * (1.0 / HEAD_DIM)
    dv = y - mean
    var = _head_sums(dv * dv, e) * (1.0 / HEAD_DIM)
    yb = dv * lax.rsqrt(var + B_GN_EPS) * lng_ref[...] + lnb_ref[...] + bonus_ref[...].astype(F32)
    yb = yb * g_ref[...].astype(F32)

    gates = gates_ref[...]
    merged = (gates[:, 0:d].astype(F32) * _dot(ya_ref[...], pa_ref[...])
              + gates[:, d:2 * d].astype(F32) * _dot(yb.astype(BF16), pb_ref[...])
              + gates[:, 2 * d:3 * d].astype(F32) * _dot(yc.astype(BF16), pc_ref[...]))
    out_ref[...] = merged.astype(BF16)


def _merge(ya, o1, o2, o3, l1, l2, l3, yr, g, bonus, gates, lng, lnb, e_bd, pa, pb, pc, layer, tm):
    m, d = ya.shape[0], pa.shape[2]
    tok = lambda a: pl.BlockSpec((tm * a.shape[0] // m, a.shape[1]), lambda i: (i, 0))
    full = lambda a: pl.BlockSpec(a.shape, lambda i: (0,) * a.ndim, pipeline_mode=pl.Buffered(1))
    at_layer = lambda a: pl.BlockSpec((None,) + a.shape[1:], lambda i: (layer, 0, 0), pipeline_mode=pl.Buffered(1))
    toks = [ya, o1, o2, o3, l1, l2, l3, yr, g, bonus, gates]
    consts = [lng.reshape(1, -1), lnb.reshape(1, -1), e_bd]
    stacked = [pa, pb, pc]
    return pl.pallas_call(
        _merge_kernel,
        grid=(m // tm,),
        in_specs=[tok(a) for a in toks] + [full(a) for a in consts] + [at_layer(a) for a in stacked],
        out_specs=pl.BlockSpec((tm, d), lambda i: (i, 0)),
        out_shape=jax.ShapeDtypeStruct((m, d), BF16),
        scratch_shapes=[pltpu.VMEM((4 * C_OUT // LANES, tm, LANES), F32)],
        compiler_params=_cparams(("parallel",)),
        name="merge",
    )(*toks, *consts, *stacked)


def _out_proj_kernel(x_ref, m_ref, w_ref, o_ref):
    o_ref[...] = x_ref[...] + _dot(m_ref[...], w_ref[...])


def _out_proj(x2, merged, wo, layer, tm):
    m, d = x2.shape
    tok = pl.BlockSpec((tm, d), lambda i: (i, 0))
    return pl.pallas_call(
        _out_proj_kernel,
        grid=(m // tm,),
        in_specs=[tok, tok,
                  pl.BlockSpec((None,) + wo.shape[1:], lambda i: (layer, 0, 0), pipeline_mode=pl.Buffered(1))],
        out_specs=tok,
        out_shape=jax.ShapeDtypeStruct((m, d), F32),
        compiler_params=_cparams(("parallel",)),
        name="out_proj",
    )(x2, merged, wo)


def _ffn_kernel(xh_ref, x_ref, g_ref, wg_ref, wv_ref, cw_ref, wd_ref, fg_ref, o_ref, h_ref, *,
                tiles_per_seq, final_norm, d_ff):
    j = pl.program_id(1)
    tm = x_ref.shape[0]
    tf = wg_ref.shape[1]

    @pl.when(j == 0)
    def _():
        x = x_ref[...]
        g = g_ref[...]
        h_ref[HALO:, :] = _rms(x, g).astype(BF16)
        seq_start = pl.program_id(0) % tiles_per_seq == 0
        halo = _rms(xh_ref[...], g)
        h_ref[0:HALO, :] = jnp.where(seq_start, 0.0, halo).astype(BF16)
        o_ref[...] = x

    h = h_ref[...]
    gext = _dot(h, wg_ref[...])
    val = _dot(h[HALO:], wv_ref[...])
    cw = cw_ref[...]
    gate = (cw[0:1, :] * pltpu.roll(gext, 2, 0)[HALO:]
            + cw[1:2, :] * pltpu.roll(gext, 1, 0)[HALO:]
            + cw[2:3, :] * gext[HALO:])
    col = j * tf + lax.broadcasted_iota(jnp.int32, (1, tf), 1)
    row = j * tf + lax.broadcasted_iota(jnp.int32, (tf, 1), 0)
    hid = jnp.where(col < d_ff, gate * jax.nn.sigmoid(gate) * val, 0.0)
    wd = jnp.where(row < d_ff, wd_ref[...], jnp.zeros((), BF16))
    o_ref[...] += _dot(hid.astype(BF16), wd)

    if final_norm:
        @pl.when(j == pl.num_programs(1) - 1)
        def _():
            o_ref[...] = _rms(o_ref[...], fg_ref[...])


def _ffn(x2, seq, g, wg, wv, cw, wd, layer, fg, final_norm, tm, tf):
    m, d = x2.shape
    ffp = wg.shape[2]
    d_ff = wv.shape[2]
    tps = seq // tm
    return pl.pallas_call(
        functools.partial(_ffn_kernel, tiles_per_seq=tps, final_norm=final_norm, d_ff=d_ff),
        grid=(m // tm, ffp // tf),
        in_specs=[
            pl.BlockSpec((HALO, d), lambda i, j: (jnp.maximum(i * (tm // HALO) - 1, 0), 0)),
            pl.BlockSpec((tm, d), lambda i, j: (i, 0)),
            pl.BlockSpec((1, d), lambda i, j: (0, 0)),
            pl.BlockSpec((None, d, tf), lambda i, j: (layer, 0, j)),
            pl.BlockSpec((None, d, tf), lambda i, j: (layer, 0, j)),
            pl.BlockSpec((None, 3, tf), lambda i, j: (layer, 0, j)),
            pl.BlockSpec((None, tf, d), lambda i, j: (layer, j, 0)),
            pl.BlockSpec((1, d), lambda i, j: (0, 0)),
        ],
        out_specs=pl.BlockSpec((tm, d), lambda i, j: (i, 0)),
        out_shape=jax.ShapeDtypeStruct((m, d), F32),
        scratch_shapes=[pltpu.VMEM((HALO + tm, d), BF16)],
        compiler_params=_cparams(("parallel", "arbitrary")),
        name="conv_ffn",
    )(x2, x2, g.reshape(1, d), wg, wv, cw, wd, fg.reshape(1, d))


def _t5_bucket(dist):
    small = dist < MAX_EXACT
    nf = jnp.maximum(dist, 1).astype(F32)
    large = MAX_EXACT + (jnp.log(nf / MAX_EXACT) / math.log(REL_MAX_DIST / MAX_EXACT)
                         * (N_BUCKETS - MAX_EXACT)).astype(jnp.int32)
    return jnp.where(small, dist, jnp.minimum(large, N_BUCKETS - 1))


def _band_bias(table, dilation, max_steps):
    i = jnp.arange(BLK)[:, None]
    j = jnp.arange(2 * BLK)[None, :]
    off = i + BLK - j
    onehot = jax.nn.one_hot(_t5_bucket(jnp.maximum(off, 0) * dilation), N_BUCKETS, dtype=F32)
    bias = jnp.einsum("ijb,bh->hij", onehot, table.astype(F32), precision=lax.Precision.HIGHEST)
    valid = (off >= 0) & (off <= max_steps)
    return jnp.where(valid[None], bias, NEG_BIG)


def _pad_cols(w, n):
    return jnp.pad(w, [(0, 0)] * (w.ndim - 1) + [(0, n - w.shape[-1])])


def _pad_rows(w, n):
    return jnp.pad(w, [(0, 0)] * (w.ndim - 2) + [(0, n - w.shape[-2]), (0, 0)])


def kernel(x, rel_bias, norm1_g, w_in, attn_sinks, rwkv_mu, rwkv_w0, rwkv_w_up, rwkv_a0, rwkv_a_up, rwkv_g_up,
           rwkv_k_k, rwkv_k_a, rwkv_r_k, rwkv_lnx_g, rwkv_lnx_b, proj_a, proj_b, proj_c, w_out, norm2_g,
           ffn_up, ffn_conv, ffn_down, final_g):
    bsz, seq, d = x.shape
    depth = w_in.shape[0]
    m = bsz * seq
    d_ff = ffn_conv.shape[-1]
    ffp = -(-d_ff // D_FF_PAD_TO) * D_FF_PAD_TO

    tm_proj = min(1024, seq)
    tm_tok = min(512, seq)
    tm_ffn = min(1024, seq)
    ts_wkv = min(512, seq)

    na = A_HQ
    bias_a = _band_bias(rel_bias[:, :na], 1, BLK - 1)
    bias_c = [_band_bias(rel_bias[:, na + gi * C_HG:na + (gi + 1) * C_HG], dil, win // dil)
              for gi, (win, dil) in enumerate(C_GROUPS)]

    q_end = (A_HQ + 2 * A_HKV) * HEAD_DIM
    b0 = q_end
    b_r_end = b0 + 3 * B_WIDTH
    b_wd_end = b_r_end + LORA_DECAY
    b_ad_end = b_wd_end + LORA_ICLR
    b_end = b_ad_end + LORA_GATE
    c_end = b_end + 3 * C_WIDTH

    def rwkv_cols(t):
        parts = [t[..., b0:b_r_end], _pad_cols(t[..., b_r_end:b_wd_end], LORA_PAD),
                 _pad_cols(t[..., b_wd_end:b_ad_end], LORA_PAD), t[..., b_ad_end:b_end]]
        return _pad_cols(jnp.concatenate(parts, axis=-1), N_RWKV)

    w_in_b = lax.optimization_barrier(w_in.astype(BF16))
    tail = lax.optimization_barrier(w_in_b[..., b_ad_end:])
    t_c, t_g = b_end - b_ad_end, c_end - b_ad_end
    qkv_c = [tail[..., t_c + t * C_WIDTH + gi * C_OUT:t_c + t * C_WIDTH + (gi + 1) * C_OUT]
             for gi in range(len(C_GROUPS)) for t in range(3)]
    rwkv_w = _pad_cols(jnp.concatenate(
        [w_in_b[..., b0:b_r_end], _pad_cols(w_in_b[..., b_r_end:b_wd_end], LORA_PAD),
         _pad_cols(w_in_b[..., b_wd_end:b_ad_end], LORA_PAD), tail[..., :t_c]], axis=-1), N_RWKV)
    w_all = jnp.concatenate([w_in_b[..., :q_end]] + qkv_c + [rwkv_w, tail[..., t_g:]], axis=-1)
    proj_blk = q_end
    n_rwkv_blk = N_RWKV // proj_blk
    n_gate_blk = (w_in.shape[-1] - c_end) // proj_blk
    mu_pad = rwkv_cols(jnp.pad(rwkv_mu, ((0, 0), (b0, 0))))
    wup_pad = _pad_rows(rwkv_w_up, LORA_PAD)
    aup_pad = _pad_rows(rwkv_a_up, LORA_PAD)
    r_k = rwkv_r_k.reshape(depth, B_WIDTH)

    head_id = jnp.arange(B_WIDTH) // HEAD_DIM
    e_bd = (head_id[:, None] == head_id[None, :]).astype(BF16)

    pa, pb, pc, wo = (t.astype(BF16) for t in (proj_a, proj_b, proj_c, w_out))
    wg = _pad_cols(ffn_up[..., :d_ff], ffp).astype(BF16)
    wv = ffn_up[..., d_ff:].astype(BF16)
    cw = _pad_cols(ffn_conv, ffp)
    wd = ffn_down.astype(BF16)

    x2 = x.reshape(m, d)
    for l in range(depth):
        p_swa, *p_dil, p_rwkv, gates = _proj(x2, norm1_g[l], w_all, l, n_rwkv_blk, n_gate_blk, tm_proj)

        kvw = A_HKV * HEAD_DIM
        y_a = _band_attention(p_swa.reshape(bsz, seq, -1), bias_a, attn_sinks[l], dil=1, q_blk=0,
                              k_blk=(A_HQ * HEAD_DIM) // kvw, v_blk=(A_HQ * HEAD_DIM) // kvw + 1,
                              q_w=A_HQ * HEAD_DIM, kv_w=kvw, nq=A_HQ, nkv=A_HKV, want_lse=False,
                              out_dtype=BF16, name="attn_swa", nqb=4)
        oc, lc = [], []
        for gi, (win, dil) in enumerate(C_GROUPS):
            pv = p_dil[gi].reshape(bsz, seq // dil, -1)
            o, ls = _band_attention(pv, bias_c[gi], None, dil=dil, q_blk=0, k_blk=1, v_blk=2, q_w=C_OUT,
                                    kv_w=C_OUT, nq=C_HG, nkv=C_HG, want_lse=True, out_dtype=BF16,
                                    name=f"attn_dil{dil}", nqb=min(4, seq // dil // BLK))
            oc.append(o.reshape(m // dil, dil * C_OUT))
            lc.append(ls.reshape(m // dil, dil * C_OUT))

        r, lw, k2, v, kk, bb, g, bonus = _rwkv_prep(
            p_rwkv, seq, mu_pad[l], rwkv_w0[l], wup_pad[l], rwkv_a0[l], aup_pad[l], rwkv_g_up[l],
            rwkv_k_k[l], rwkv_k_a[l], r_k[l], e_bd, tm_tok)
        sh = lambda t: t.reshape(bsz, seq, B_WIDTH)
        y_raw = _wkv_scan(sh(r), sh(lw), sh(k2), sh(v), sh(kk), sh(bb), ts_wkv).reshape(m, B_WIDTH)

        merged = _merge(y_a.reshape(m, A_HQ * HEAD_DIM), oc[0], oc[1], oc[2], lc[0], lc[1], lc[2], y_raw, g,
                        bonus, gates, rwkv_lnx_g[l], rwkv_lnx_b[l], e_bd, pa, pb, pc, l, tm_tok)
        x2 = _out_proj(x2, merged, wo, l, tm_proj)
        x2 = _ffn(x2, seq, norm2_g[l], wg, wv, cw, wd, l, final_g, l == depth - 1, tm_ffn, 512)
    return x2.reshape(bsz, seq, d)
```
